```python
import jax, jax.numpy as jnp
from jax import lax
import numpy as np

D_MODEL = 1024
BATCH = 16
SEQ = 2048
DEPTH = 1
DEC_BATCH = 128
DEC_SEQ = 4
PAST_LEN = 16384
PAGE_SIZE = 128

N_META = 16
EPS = 1e-6
NEG_INF = -1e30
A_HEADS = 8
A_KV_HEADS = 2
A_HEAD_DIM = 64
A_GROUP = A_HEADS // A_KV_HEADS
WINDOW = 128
ATTN_BLOCK = 128
ROPE_THETA = 500000.0
ROPE_DIM = A_HEAD_DIM // 4
META_PAD = ATTN_BLOCK - N_META
B_HEADS = 4
B_KEY_DIM = 64
B_VAL_DIM = 128
GATE_RANK = 16
GATE_NORMALIZER = 16.0
GLA_CHUNK = 64
PEER_HEADS = 8
PEER_KEYS = 128
PEER_EXPERTS = PEER_KEYS * PEER_KEYS
PEER_QDIM = 128
PEER_HALF = PEER_QDIM // 2
PEER_TOPK = 16
PEER_BLOCK = 256
A_Q_WIDTH = A_HEADS * A_HEAD_DIM
A_KV_WIDTH = A_KV_HEADS * A_HEAD_DIM
B_QK_WIDTH = B_HEADS * B_KEY_DIM
B_V_WIDTH = B_HEADS * B_VAL_DIM
PROJ_SIZES = (A_Q_WIDTH, A_KV_WIDTH, A_KV_WIDTH, B_QK_WIDTH, B_QK_WIDTH, B_V_WIDTH, B_V_WIDTH, GATE_RANK, D_MODEL, D_MODEL)
PROJ_COLS = sum(PROJ_SIZES)

kernel_name = 'hybrid_swa_gla_peer_step'


def rms_norm(x, g):
    xf = x.astype(jnp.float32)
    y = xf * lax.rsqrt(jnp.mean(xf * xf, axis=-1, keepdims=True) + EPS)
    return (y * g.astype(jnp.float32)).astype(x.dtype)


def rope(x, pos):
    half = ROPE_DIM // 2
    inv = ROPE_THETA ** (-jnp.arange(0, ROPE_DIM, 2, dtype=jnp.float32) / ROPE_DIM)
    ang = pos.astype(jnp.float32)[:, None] * inv[None, :]
    cos = jnp.cos(ang)[:, None, :]
    sin = jnp.sin(ang)[:, None, :]
    xf = x.astype(jnp.float32)
    x1 = xf[..., :half]
    x2 = xf[..., half:ROPE_DIM]
    out = jnp.concatenate([x1 * cos - x2 * sin, x2 * cos + x1 * sin, xf[..., ROPE_DIM:]], axis=-1)
    return out.astype(x.dtype)


def split_columns(z):
    offsets = [int(o) for o in np.cumsum(PROJ_SIZES)[:-1]]
    return jnp.split(z, offsets, axis=-1)


def mixer_inputs(h, pos, w_in, w_gate_up, b_gate):
    lead = h.shape[:-1]
    qa, ka, va, qb, kb, vb, gb, lr, gate_a, gate_b = split_columns(h @ w_in)
    qa = rope(qa.reshape(lead + (A_HEADS, A_HEAD_DIM)), pos)
    ka = rope(ka.reshape(lead + (A_KV_HEADS, A_HEAD_DIM)), pos)
    va = va.reshape(lead + (A_KV_HEADS, A_HEAD_DIM))
    qb = qb.reshape(lead + (B_HEADS, B_KEY_DIM)) * (B_KEY_DIM ** -0.5)
    kb = kb.reshape(lead + (B_HEADS, B_KEY_DIM))
    vb = vb.reshape(lead + (B_HEADS, B_VAL_DIM))
    log_decay = jax.nn.log_sigmoid((lr @ w_gate_up + b_gate).astype(jnp.float32)) / GATE_NORMALIZER
    log_decay = log_decay.reshape(lead + (B_HEADS, B_KEY_DIM))
    return qa, ka, va, qb, kb, vb, gb, log_decay, gate_a, gate_b


def sink_softmax(s, mask, sink):
    sink = sink.astype(jnp.float32)
    s = jnp.where(mask, s, NEG_INF)
    m = jnp.maximum(jnp.max(s, axis=-1, keepdims=True), sink)
    e = jnp.exp(s - m)
    return e / (jnp.sum(e, axis=-1, keepdims=True) + jnp.exp(sink - m))


def swa_prompt(q, k, v, sinks):
    bsz, lp = q.shape[:2]
    nb = lp // ATTN_BLOCK
    qb = q.reshape(bsz, nb, ATTN_BLOCK, A_KV_HEADS, A_GROUP, A_HEAD_DIM)
    kb = k.reshape(bsz, nb, ATTN_BLOCK, A_KV_HEADS, A_HEAD_DIM)
    vb = v.reshape(bsz, nb, ATTN_BLOCK, A_KV_HEADS, A_HEAD_DIM)
    pad = ((0, 0), (1, 0), (0, 0), (0, 0), (0, 0))
    kk = jnp.concatenate([jnp.pad(kb, pad)[:, :-1], kb], axis=2)
    vv = jnp.concatenate([jnp.pad(vb, pad)[:, :-1], vb], axis=2)
    s = jnp.einsum('bnqkgd,bnskd->bnkgqs', qb, kk).astype(jnp.float32) * (A_HEAD_DIM ** -0.5)
    qi = jnp.arange(nb)[:, None] * ATTN_BLOCK + jnp.arange(ATTN_BLOCK)[None, :]
    ki = jnp.arange(nb)[:, None] * ATTN_BLOCK - ATTN_BLOCK + jnp.arange(2 * ATTN_BLOCK)[None, :]
    diff = qi[:, :, None] - ki[:, None, :]
    mask = (diff >= 0) & (diff <= WINDOW) & (ki[:, None, :] >= META_PAD)
    p = sink_softmax(s, mask[None, :, None, None], sinks.reshape(A_KV_HEADS, A_GROUP, 1, 1))
    o = jnp.einsum('bnkgqs,bnskd->bnqkgd', p.astype(vv.dtype), vv)
    return o.reshape(bsz, lp, A_Q_WIDTH)


def swa_sample(q, k_new, v_new, cache_k, cache_v, sinks):
    dbsz, t = q.shape[:2]
    w = cache_k.shape[1]
    kk = jnp.concatenate([cache_k.astype(k_new.dtype), k_new], axis=1)
    vv = jnp.concatenate([cache_v.astype(v_new.dtype), v_new], axis=1)
    qg = q.reshape(dbsz, t, A_KV_HEADS, A_GROUP, A_HEAD_DIM)
    s = jnp.einsum('bqkgd,bskd->bkgqs', qg, kk).astype(jnp.float32) * (A_HEAD_DIM ** -0.5)
    qpos = PAST_LEN + jnp.arange(t)
    kpos = PAST_LEN - w + jnp.arange(w + t)
    diff = qpos[:, None] - kpos[None, :]
    mask = (diff >= 0) & (diff <= WINDOW)
    p = sink_softmax(s, mask, sinks.reshape(A_KV_HEADS, A_GROUP, 1, 1))
    o = jnp.einsum('bkgqs,bskd->bqkgd', p.astype(vv.dtype), vv)
    return o.reshape(dbsz, t, A_Q_WIDTH), kk[:, -w:], vv[:, -w:]


def gla_chunked(q, k, v, g, s0):
    q, k, v, g = (a.astype(jnp.float32) for a in (q, k, v, g))
    b = jnp.cumsum(g, axis=2)
    b_last = b[:, :, -1:]
    q_t = q * jnp.exp(b)
    k_t = k * jnp.exp(-b)
    k_end = k * jnp.exp(b_last - b)
    c = q.shape[2]
    causal = jnp.tril(jnp.ones((c, c), dtype=bool))
    a = jnp.where(causal, jnp.einsum('bnihd,bnjhd->bnhij', q_t, k_t), 0.0)
    o_intra = jnp.einsum('bnhij,bnjhe->bnihe', a, v)
    decay = jnp.exp(b_last[:, :, 0])

    def step(state, inp):
        qn, kn, vn, dn = inp
        o = jnp.einsum('bihd,bhde->bihe', qn, state)
        state = state * dn[..., None] + jnp.einsum('bjhd,bjhe->bhde', kn, vn)
        return state, o

    xs = tuple(jnp.moveaxis(t, 1, 0) for t in (q_t, k_end, v, decay))
    s_fin, o_inter = lax.scan(step, s0.astype(jnp.float32), xs)
    return o_intra + jnp.moveaxis(o_inter, 0, 1), s_fin


def gla_output(o, g_out, g_norm, dtype):
    lead = o.shape[:-2]
    gate = jax.nn.silu(g_out.reshape(lead + (B_HEADS, B_VAL_DIM)).astype(jnp.float32))
    y = rms_norm(o, g_norm) * gate
    return y.reshape(lead + (B_V_WIDTH,)).astype(dtype)


def merge_branches(gate_a, gate_b, ya, yb, w_branch_a, w_branch_b, w_out):
    m = jax.nn.sigmoid(gate_a) * (ya @ w_branch_a) + jax.nn.sigmoid(gate_b) * (yb @ w_branch_b)
    return m @ w_out


def peer(h, w_q, sub_keys, u, v):
    t = h.shape[0]
    nblk = -(-t // PEER_BLOCK)
    hp = jnp.pad(h, ((0, nblk * PEER_BLOCK - t), (0, 0))).reshape(nblk, PEER_BLOCK, D_MODEL)
    kk = PEER_TOPK * PEER_TOPK

    def one_block(hb):
        q = (hb @ w_q).reshape(PEER_BLOCK, PEER_HEADS, 2, PEER_HALF)
        s = jnp.einsum('thcd,hcnd->thcn', q, sub_keys).astype(jnp.float32)
        sv, si = lax.top_k(s, PEER_TOPK)
        cand = (sv[:, :, 0, :, None] + sv[:, :, 1, None, :]).reshape(PEER_BLOCK, PEER_HEADS, kk)
        cidx = (si[:, :, 0, :, None] * PEER_KEYS + si[:, :, 1, None, :]).reshape(PEER_BLOCK, PEER_HEADS, kk)
        fv, fpos = lax.top_k(cand, PEER_TOPK)
        eidx = jnp.take_along_axis(cidx, fpos, axis=-1)
        wts = jax.nn.softmax(fv, axis=-1)
        act = jax.nn.gelu(jnp.einsum('thkd,td->thk', u[eidx], hb).astype(jnp.float32), approximate=False)
        return jnp.einsum('thk,thkd->td', (wts * act).astype(hb.dtype), v[eidx])

    return lax.map(one_block, hp).reshape(-1, D_MODEL)[:t]


def to_chunks(t, bsz, n_chunks):
    return t.reshape((bsz, n_chunks, GLA_CHUNK) + t.shape[2:])


def setup_inputs(seed: int = 0) -> dict:
    key = jax.random.key(seed)
    ks = jax.random.split(key, 21)

    def nrm(k, shape, scale):
        return jax.random.normal(k, shape, jnp.float32) * scale

    return {
        'x_prompt': nrm(ks[0], (BATCH, SEQ, D_MODEL), 1.0),
        'x_sample': nrm(ks[1], (DEC_BATCH, DEC_SEQ, D_MODEL), 1.0),
        'cache_k_window': nrm(ks[2], (DEPTH, DEC_BATCH, WINDOW, A_KV_HEADS, A_HEAD_DIM), 1.0),
        'cache_v_window': nrm(ks[3], (DEPTH, DEC_BATCH, WINDOW, A_KV_HEADS, A_HEAD_DIM), 1.0),
        'state_gla': nrm(ks[4], (DEPTH, DEC_BATCH, B_HEADS, B_KEY_DIM, B_VAL_DIM), 0.5),
        'meta_tokens': nrm(ks[5], (N_META, D_MODEL), 1.0),
        'g_norm_mix': 1.0 + nrm(ks[6], (DEPTH, D_MODEL), 0.05),
        'w_in': nrm(ks[7], (DEPTH, D_MODEL, PROJ_COLS), D_MODEL ** -0.5),
        'w_gate_up': nrm(ks[8], (DEPTH, GATE_RANK, B_QK_WIDTH), GATE_RANK ** -0.5),
        'b_gate': nrm(ks[9], (DEPTH, B_QK_WIDTH), 0.1),
        'attn_sinks': nrm(ks[10], (DEPTH, A_HEADS), 0.5),
        'g_gla_norm': 1.0 + nrm(ks[11], (DEPTH, B_VAL_DIM), 0.05),
        'w_branch_a': nrm(ks[12], (DEPTH, A_Q_WIDTH, D_MODEL), A_Q_WIDTH ** -0.5),
        'w_branch_b': nrm(ks[13], (DEPTH, B_V_WIDTH, D_MODEL), B_V_WIDTH ** -0.5),
        'w_out': nrm(ks[14], (DEPTH, D_MODEL, D_MODEL), D_MODEL ** -0.5),
        'g_norm_ffn': 1.0 + nrm(ks[15], (DEPTH, D_MODEL), 0.05),
        'w_peer_q': nrm(ks[16], (DEPTH, D_MODEL, PEER_HEADS * PEER_QDIM), D_MODEL ** -0.5),
        'peer_sub_keys': nrm(ks[17], (DEPTH, PEER_HEADS, 2, PEER_KEYS, PEER_HALF), PEER_HALF ** -0.5),
        'peer_u': nrm(ks[18], (DEPTH, PEER_EXPERTS, D_MODEL), D_MODEL ** -0.5),
        'peer_v': nrm(ks[19], (DEPTH, PEER_EXPERTS, D_MODEL), PEER_HEADS ** -0.5),
        'g_norm_final': 1.0 + nrm(ks[20], (D_MODEL,), 0.05),
    }


def reference(x_prompt, x_sample, cache_k_window, cache_v_window, state_gla, meta_tokens, g_norm_mix, w_in,
              w_gate_up, b_gate, attn_sinks, g_gla_norm, w_branch_a, w_branch_b, w_out, g_norm_ffn, w_peer_q,
              peer_sub_keys, peer_u, peer_v, g_norm_final):
    bsz = x_prompt.shape[0]
    tdec = x_sample.shape[1]
    meta = jnp.broadcast_to(meta_tokens[None].astype(x_prompt.dtype), (bsz, N_META, D_MODEL))
    xp = jnp.concatenate([meta, x_prompt], axis=1)
    lp_len = xp.shape[1] + META_PAD
    pos_p = jnp.arange(lp_len) - META_PAD
    pos_s = PAST_LEN + jnp.arange(tdec)
    valid_p = (jnp.arange(lp_len) >= META_PAD)[None, :, None, None]
    n_chunks = lp_len // GLA_CHUNK
    xs = x_sample
    kp_list, vp_list, sp_list, ks_list, vs_list, ss_list = [], [], [], [], [], []
    for l in range(DEPTH):
        h = jnp.pad(rms_norm(xp, g_norm_mix[l]), ((0, 0), (META_PAD, 0), (0, 0)))
        qa, ka, va, qb, kb, vb, gb, ld, ga, gbr = mixer_inputs(h, pos_p, w_in[l], w_gate_up[l], b_gate[l])
        ya = swa_prompt(qa, ka, va, attn_sinks[l])
        ld = jnp.where(valid_p, ld, 0.0)
        s0 = jnp.zeros((bsz, B_HEADS, B_KEY_DIM, B_VAL_DIM), jnp.float32)
        o, s_fin = gla_chunked(to_chunks(qb, bsz, n_chunks), to_chunks(kb, bsz, n_chunks),
                               to_chunks(vb, bsz, n_chunks), to_chunks(ld, bsz, n_chunks), s0)
        yb = gla_output(o.reshape(bsz, lp_len, B_HEADS, B_VAL_DIM), gb, g_gla_norm[l], xp.dtype)
        xp = xp + merge_branches(ga[:, META_PAD:], gbr[:, META_PAD:], ya[:, META_PAD:], yb[:, META_PAD:],
                                 w_branch_a[l], w_branch_b[l], w_out[l])
        xp = xp + peer(rms_norm(xp, g_norm_ffn[l]).reshape(-1, D_MODEL), w_peer_q[l], peer_sub_keys[l],
                       peer_u[l], peer_v[l]).reshape(xp.shape)
        kp_list.append(ka[:, -WINDOW:])
        vp_list.append(va[:, -WINDOW:])
        sp_list.append(s_fin.astype(xp.dtype))
        h = rms_norm(xs, g_norm_mix[l])
        qa, ka, va, qb, kb, vb, gb, ld, ga, gbr = mixer_inputs(h, pos_s, w_in[l], w_gate_up[l], b_gate[l])
        ya, nk, nv = swa_sample(qa, ka, va, cache_k_window[l], cache_v_window[l], attn_sinks[l])
        o, s_fin = gla_chunked(qb[:, None], kb[:, None], vb[:, None], ld[:, None], state_gla[l])
        yb = gla_output(o[:, 0], gb, g_gla_norm[l], xs.dtype)
        xs = xs + merge_branches(ga, gbr, ya, yb, w_branch_a[l], w_branch_b[l], w_out[l])
        xs = xs + peer(rms_norm(xs, g_norm_ffn[l]).reshape(-1, D_MODEL), w_peer_q[l], peer_sub_keys[l],
                       peer_u[l], peer_v[l]).reshape(xs.shape)
        ks_list.append(nk)
        vs_list.append(nv)
        ss_list.append(s_fin.astype(xs.dtype))
    y_prompt = rms_norm(xp, g_norm_final)[:, N_META:]
    y_sample = rms_norm(xs, g_norm_final)
    return (y_prompt, y_sample, jnp.stack(kp_list), jnp.stack(vp_list), jnp.stack(sp_list),
            jnp.stack(ks_list), jnp.stack(vs_list), jnp.stack(ss_list))
```

```python
import functools
import math

import jax
import jax.numpy as jnp
from jax import lax
from jax.experimental import pallas as pl
from jax.experimental.pallas import tpu as pltpu

F32 = jnp.float32
BF16 = jnp.bfloat16

EPS = 1e-6
NEG_INF = -1e30
PAST_LEN = 16384
ROPE_THETA = 500000.0
GATE_NORMALIZER = 16.0
PEER_TOPK = 16

LANES = 128
SUBLANES = 8
VMEM_LIMIT_BYTES = 56 * 1024 * 1024

ATTN_BLOCK = 128
GLA_CHUNK = 64
SAMPLE_PAD = 16
PROJ_ROWS = 544
MERGE_ROWS = 128
EXPERT_TOKENS = 16


def _cparams(sem):
    return pltpu.CompilerParams(dimension_semantics=sem, vmem_limit_bytes=VMEM_LIMIT_BYTES)


def _rms(x, g):
    ms = jnp.mean(x * x, axis=-1, keepdims=True)
    return (x * lax.rsqrt(ms + EPS)) * g


def _proj_kernel(x_ref, g_ref, tab_ref, w1_ref, wlr_ref, wgu_ref, bg_ref, w2_ref,
                 qa_ref, kv_ref, gl_ref, gt_ref, *, period, qk_scale):
    i = pl.program_id(0)
    tr = x_ref.shape[0]
    hb = _rms(x_ref[...], g_ref[...]).astype(BF16)
    z1 = jnp.dot(hb, w1_ref[...], preferred_element_type=F32)

    start = pl.multiple_of((i * tr) % period, SUBLANES)
    tab = tab_ref[pl.ds(start, tr), :]
    cosf = tab[:, 0:LANES]
    sin_lo = tab[:, LANES:2 * LANES]
    sin_hi = tab[:, 2 * LANES:3 * LANES]
    valid = tab[:, 3 * LANES:3 * LANES + 1]

    def rope(xg):
        return xg * cosf + pltpu.roll(xg, 8, 1) * sin_lo + pltpu.roll(xg, LANES - 8, 1) * sin_hi

    for gi in range(4):
        sl = slice(gi * LANES, (gi + 1) * LANES)
        qa_ref[:, sl] = rope(z1[:, sl]).astype(BF16)
    kv_ref[:, 0:LANES] = rope(z1[:, 512:640])
    kv_ref[:, LANES:2 * LANES] = z1[:, 640:768]

    lr = jnp.dot(hb, wlr_ref[...], preferred_element_type=F32)
    pre = jnp.dot(lr.astype(BF16), wgu_ref[...], preferred_element_type=F32) + bg_ref[...]
    log_sig = jnp.minimum(pre, 0.0) - jnp.log1p(jnp.exp(-jnp.abs(pre)))
    ld = jnp.where(valid > 0.5, log_sig / GATE_NORMALIZER, 0.0)

    gl_ref[:, 0:256] = z1[:, 768:1024] * qk_scale
    gl_ref[:, 256:512] = z1[:, 1024:1280]
    gl_ref[:, 512:768] = ld
    gl_ref[:, 768:1792] = z1[:, 1280:2304]
    gt_ref[...] = jnp.dot(hb, w2_ref[...], preferred_element_type=F32)


def _project(x, g, tab, w1, wlr, wgu, bg, w2, rows, qk_scale):
    r, d = x.shape
    period = tab.shape[0]
    const = lambda i: (0, 0)
    row = lambda i: (i, 0)
    return pl.pallas_call(
        functools.partial(_proj_kernel, period=period, qk_scale=qk_scale),
        grid=(r // rows,),
        in_specs=[
            pl.BlockSpec((rows, d), row),
            pl.BlockSpec(g.shape, const),
            pl.BlockSpec(tab.shape, const),
            pl.BlockSpec(w1.shape, const),
            pl.BlockSpec(wlr.shape, const),
            pl.BlockSpec(wgu.shape, const),
            pl.BlockSpec(bg.shape, const),
            pl.BlockSpec(w2.shape, const),
        ],
        out_specs=[
            pl.BlockSpec((rows, 512), row),
            pl.BlockSpec((rows, 256), row),
            pl.BlockSpec((rows, 1792), row),
            pl.BlockSpec((rows, 2048), row),
        ],
        out_shape=[
            jax.ShapeDtypeStruct((r, 512), BF16),
            jax.ShapeDtypeStruct((r, 256), F32),
            jax.ShapeDtypeStruct((r, 1792), F32),
            jax.ShapeDtypeStruct((r, 2048), F32),
        ],
        compiler_params=_cparams(("arbitrary",)),
        name="proj",
    )(x, g, tab, w1, wlr, wgu, bg, w2)


def _attn_kernel(sink_ref, q_ref, prev_ref, cur_ref, o_ref, *, first_valid_key, block_offset):
    n = pl.program_id(1)
    qr = q_ref.shape[0]
    kr = cur_ref.shape[0]
    w = prev_ref.shape[0]
    nk = w + kr
    group = 4
    hd = 64

    rows = lax.broadcasted_iota(jnp.int32, (group * qr, nk), 0)
    cols = lax.broadcasted_iota(jnp.int32, (group * qr, nk), 1)
    head_of_row = rows // qr
    diff = (rows - head_of_row * qr) - cols + w
    mask = (diff >= 0) & (diff <= w)
    if first_valid_key is not None:
        blk = n + block_offset
        mask = mask & (cols >= first_valid_key + w - blk * w)

    prev = prev_ref[...]
    cur = cur_ref[...]
    q = q_ref[...]
    row_head = lax.broadcasted_iota(jnp.int32, (group * qr, 1), 0) // qr
    for kh in range(2):
        k = jnp.concatenate([prev[:, kh * hd:(kh + 1) * hd], cur[:, kh * hd:(kh + 1) * hd]], axis=0).astype(BF16)
        v = jnp.concatenate([prev[:, LANES + kh * hd:LANES + (kh + 1) * hd],
                             cur[:, LANES + kh * hd:LANES + (kh + 1) * hd]], axis=0).astype(BF16)
        qs = jnp.concatenate([q[:, (group * kh + g) * hd:(group * kh + g + 1) * hd] for g in range(group)], axis=0)
        s = lax.dot_general(qs, k, (((1,), (1,)), ((), ())), preferred_element_type=F32) * (hd ** -0.5)
        s = jnp.where(mask, s, NEG_INF)
        sink = jnp.zeros((group * qr, 1), F32)
        for g in range(group):
            sink = jnp.where(row_head == g, sink_ref[group * kh + g], sink)
        m = jnp.maximum(jnp.max(s, axis=-1, keepdims=True), sink)
        e = jnp.exp(s - m)
        p = e / (jnp.sum(e, axis=-1, keepdims=True) + jnp.exp(sink - m))
        o = jnp.dot(p.astype(BF16), v, preferred_element_type=F32)
        for g in range(group):
            h = group * kh + g
            o_ref[:, h * hd:(h + 1) * hd] = o[g * qr:(g + 1) * qr].astype(BF16)


def _attention(sinks, q, kv_prev, kv_cur, nb, nblk, qr, q_map, prev_map, cur_map, out_map, out_rows,
               first_valid_key, block_offset):
    w = ATTN_BLOCK
    return pl.pallas_call(
        functools.partial(_attn_kernel, first_valid_key=first_valid_key, block_offset=block_offset),
        grid=(nb, nblk),
        in_specs=[
            pl.BlockSpec(memory_space=pltpu.SMEM),
            pl.BlockSpec((qr, 512), q_map),
            pl.BlockSpec((w, 256), prev_map),
            pl.BlockSpec((qr, 256), cur_map),
        ],
        out_specs=pl.BlockSpec((qr, 512), out_map),
        out_shape=jax.ShapeDtypeStruct((out_rows, 512), BF16),
        compiler_params=_cparams(("arbitrary", "arbitrary")),
        name="swa",
    )(sinks, q, kv_prev, kv_cur)


def _gla_kernel(gl_ref, s0_ref, gn_ref, yb_ref, sfin_ref, st_ref):
    c = pl.program_id(1)
    ch = gl_ref.shape[0]
    nh, dk, dv = 4, 64, 128

    @pl.when(c == 0)
    def _():
        for h in range(nh):
            st_ref[h] = s0_ref[0, h].T

    gl = gl_ref[...]
    q = gl[:, 0:256]
    k = gl[:, 256:512]
    b = gl[:, 512:768]
    row = lax.broadcasted_iota(jnp.int32, (ch, nh * dk), 0)
    sh = 1
    while sh < ch:
        b = b + jnp.where(row >= sh, pltpu.roll(b, sh, 0), 0.0)
        sh *= 2
    b_last = b[ch - 1:ch, :]
    q_t = (q * jnp.exp(b)).astype(BF16)
    k_t = (k * jnp.exp(-b)).astype(BF16)
    k_end = (k * jnp.exp(b_last - b)).astype(BF16)
    decay = jnp.exp(b_last)
    causal = (lax.broadcasted_iota(jnp.int32, (ch, ch), 0) >= lax.broadcasted_iota(jnp.int32, (ch, ch), 1))
    gn = gn_ref[...]
    nt = (((1,), (1,)), ((), ()))
    for h in range(nh):
        ks = slice(h * dk, (h + 1) * dk)
        v = gl[:, 768 + h * dv:768 + (h + 1) * dv]
        vb = v.astype(BF16)
        a = lax.dot_general(q_t[:, ks], k_t[:, ks], nt, preferred_element_type=F32)
        a = jnp.where(causal, a, 0.0)
        s_t = st_ref[h]
        o = jnp.dot(a.astype(BF16), vb, preferred_element_type=F32)
        o = o + lax.dot_general(q_t[:, ks], s_t.astype(BF16), nt, preferred_element_type=F32)
        upd = jnp.dot(v.T.astype(BF16), k_end[:, ks], preferred_element_type=F32)
        st_ref[h] = s_t * decay[:, ks] + upd
        go = gl[:, 1280 + h * dv:1280 + (h + 1) * dv]
        y = _rms(o, gn) * (go * jax.nn.sigmoid(go))
        yb_ref[:, h * dv:(h + 1) * dv] = y.astype(BF16)

    @pl.when(c == pl.num_programs(1) - 1)
    def _():
        for h in range(nh):
            sfin_ref[0, h] = st_ref[h].T


def _gla(gl, s0, gn, nb, nchunks, ch, in_map, out_map, out_rows):
    return pl.pallas_call(
        _gla_kernel,
        grid=(nb, nchunks),
        in_specs=[
            pl.BlockSpec((ch, 1792), in_map),
            pl.BlockSpec((1, 4, 64, 128), lambda b, c: (b, 0, 0, 0)),
            pl.BlockSpec((1, 128), lambda b, c: (0, 0)),
        ],
        out_specs=[
            pl.BlockSpec((ch, 512), out_map),
            pl.BlockSpec((1, 4, 64, 128), lambda b, c: (b, 0, 0, 0)),
        ],
        out_shape=[
            jax.ShapeDtypeStruct((out_rows, 512), BF16),
            jax.ShapeDtypeStruct((nb, 4, 64, 128), F32),
        ],
        scratch_shapes=[pltpu.VMEM((4, 128, 64), F32)],
        compiler_params=_cparams(("arbitrary", "arbitrary")),
        name="gla",
    )(gl, s0, gn)


def _extract_topk(work, nsel, iota0, sentinel):
    vals, idxs = [], []
    for j in range(nsel):
        m = jnp.max(work, axis=0, keepdims=True)
        idx = jnp.min(jnp.where(work == m, iota0, sentinel), axis=0, keepdims=True)
        vals.append(m)
        idxs.append(idx)
        if j + 1 < nsel:
            work = jnp.where(iota0 == idx, -jnp.inf, work)
    return vals, idxs


def _merge_kernel(x_ref, gt_ref, ya_ref, yb_ref, wa_ref, wb_ref, wo_ref, gf_ref, wq_ref, keys_ref,
                  xm_ref, hn_ref, et_ref, wt_ref):
    td = x_ref.shape[0]
    nkeys = keys_ref.shape[1]
    half = keys_ref.shape[2]
    nheads = keys_ref.shape[0] // 2
    topk = PEER_TOPK

    gt = gt_ref[...]
    d = x_ref.shape[1]
    ma = jnp.dot(ya_ref[...], wa_ref[...], preferred_element_type=F32)
    mb = jnp.dot(yb_ref[...], wb_ref[...], preferred_element_type=F32)
    m = jax.nn.sigmoid(gt[:, 0:d]) * ma + jax.nn.sigmoid(gt[:, d:2 * d]) * mb
    xm = x_ref[...] + jnp.dot(m.astype(BF16), wo_ref[...], preferred_element_type=F32)
    xm_ref[...] = xm
    hn = _rms(xm, gf_ref[...])
    hn_ref[...] = hn
    q = jnp.dot(hn.astype(BF16), wq_ref[...], preferred_element_type=F32).astype(BF16)

    nt = (((1,), (1,)), ((), ()))
    iota_k = lax.broadcasted_iota(jnp.int32, (nkeys, td), 0)
    iota_c = lax.broadcasted_iota(jnp.int32, (topk * topk, td), 0)
    for h in range(nheads):
        sv, si = [], []
        for c in range(2):
            gi = 2 * h + c
            s_t = lax.dot_general(keys_ref[gi], q[:, gi * half:(gi + 1) * half], nt,
                                  preferred_element_type=F32)
            vals, idxs = _extract_topk(s_t, topk, iota_k, nkeys)
            sv.append(vals)
            si.append(idxs)
        sv1 = jnp.concatenate(sv[1], axis=0)
        si1 = jnp.concatenate(si[1], axis=0)
        cand = jnp.concatenate([sv[0][a] + sv1 for a in range(topk)], axis=0)
        cidx = jnp.concatenate([si[0][a] * nkeys + si1 for a in range(topk)], axis=0)
        fvals, eids = [], []
        work = cand
        for j in range(topk):
            mx = jnp.max(work, axis=0, keepdims=True)
            pos = jnp.min(jnp.where(work == mx, iota_c, topk * topk), axis=0, keepdims=True)
            hit = iota_c == pos
            eids.append(jnp.max(jnp.where(hit, cidx, -1), axis=0, keepdims=True))
            fvals.append(mx)
            if j + 1 < topk:
                work = jnp.where(hit, -jnp.inf, work)
        fv = jnp.concatenate(fvals, axis=0)
        e = jnp.exp(fv - fvals[0])
        wt_ref[h * topk:(h + 1) * topk, :] = e / jnp.sum(e, axis=0, keepdims=True)
        et_ref[h * topk:(h + 1) * topk, :] = jnp.concatenate(eids, axis=0)


def _merge_route(x, gt, ya, yb, wa, wb, wo, gf, wq, keys, gt_map):
    t, d = x.shape
    td = MERGE_ROWS
    nsel = (keys.shape[0] // 2) * PEER_TOPK
    const2 = lambda i: (0, 0)
    row = lambda i: (i, 0)
    col = lambda i: (0, i)
    return pl.pallas_call(
        _merge_kernel,
        grid=(t // td,),
        in_specs=[
            pl.BlockSpec((td, d), row),
            pl.BlockSpec((td, 2 * d), gt_map),
            pl.BlockSpec((td, ya.shape[1]), row),
            pl.BlockSpec((td, yb.shape[1]), row),
            pl.BlockSpec(wa.shape, const2),
            pl.BlockSpec(wb.shape, const2),
            pl.BlockSpec(wo.shape, const2),
            pl.BlockSpec(gf.shape, const2),
            pl.BlockSpec(wq.shape, const2),
            pl.BlockSpec(keys.shape, lambda i: (0, 0, 0)),
        ],
        out_specs=[
            pl.BlockSpec((td, d), row),
            pl.BlockSpec((td, d), row),
            pl.BlockSpec((nsel, td), col),
            pl.BlockSpec((nsel, td), col),
        ],
        out_shape=[
            jax.ShapeDtypeStruct((t, d), F32),
            jax.ShapeDtypeStruct((t, d), F32),
            jax.ShapeDtypeStruct((nsel, t), jnp.int32),
            jax.ShapeDtypeStruct((nsel, t), F32),
        ],
        compiler_params=_cparams(("arbitrary",)),
        name="merge_route",
    )(x, gt, ya, yb, wa, wb, wo, gf, wq, keys)


def _expert_kernel(xm_ref, hn_ref, wc_ref, ug_ref, vg_ref, gfin_ref, y_ref):
    tt = xm_ref.shape[0]
    wc = wc_ref[0]
    rows = []
    for t in range(tt):
        u = ug_ref[t].astype(F32)
        h = hn_ref[t:t + 1, :]
        act = jnp.sum(u * h, axis=1, keepdims=True)
        gelu = 0.5 * act * (1.0 + lax.erf(act * (2.0 ** -0.5)))
        coef = wc[:, t:t + 1] * gelu
        v = vg_ref[t].astype(F32)
        rows.append(jnp.sum(v * coef, axis=0, keepdims=True))
    out = xm_ref[...] + jnp.concatenate(rows, axis=0)
    y_ref[...] = _rms(out, gfin_ref[...])


def _experts(xm, hn, wc, ug, vg, gfin):
    t, d = xm.shape
    tt = EXPERT_TOKENS
    nsel = ug.shape[1]
    row = lambda i: (i, 0)
    row3 = lambda i: (i, 0, 0)
    return pl.pallas_call(
        _expert_kernel,
        grid=(t // tt,),
        in_specs=[
            pl.BlockSpec((tt, d), row),
            pl.BlockSpec((tt, d), row),
            pl.BlockSpec((1, nsel, tt), row3),
            pl.BlockSpec((tt, nsel, d), row3),
            pl.BlockSpec((tt, nsel, d), row3),
            pl.BlockSpec(gfin.shape, lambda i: (0, 0)),
        ],
        out_specs=pl.BlockSpec((tt, d), row),
        out_shape=jax.ShapeDtypeStruct((t, d), F32),
        compiler_params=_cparams(("arbitrary",)),
        name="experts",
    )(xm, hn, wc, ug, vg, gfin)


def _rope_table(pos, valid, rope_dim, head_dim):
    half = rope_dim // 2
    inv = ROPE_THETA ** (-jnp.arange(0, rope_dim, 2, dtype=F32) / rope_dim)
    ang = pos.astype(F32)[:, None] * inv[None, :]
    cos, sin = jnp.cos(ang), jnp.sin(ang)
    n = pos.shape[0]
    ones = jnp.ones((n, head_dim - rope_dim), F32)
    zeros_h = jnp.zeros((n, half), F32)
    zeros_r = jnp.zeros((n, head_dim - rope_dim), F32)
    reps = LANES // head_dim
    cosf = jnp.tile(jnp.concatenate([cos, cos, ones], axis=1), (1, reps))
    sin_lo = jnp.tile(jnp.concatenate([zeros_h, sin, zeros_r], axis=1), (1, reps))
    sin_hi = jnp.tile(jnp.concatenate([-sin, zeros_h, zeros_r], axis=1), (1, reps))
    vcol = jnp.broadcast_to(valid.astype(F32)[:, None], (n, LANES))
    return jnp.concatenate([cosf, sin_lo, sin_hi, vcol], axis=1)


def _peer_tail(xm, hn, et, wt, u_tab, v_tab, gfin):
    t = xm.shape[0]
    tt = EXPERT_TOKENS
    nsel = et.shape[0]
    eidx = et.T
    wc = wt.reshape(nsel, t // tt, tt).transpose(1, 0, 2)
    ug = jnp.take(u_tab, eidx, axis=0)
    vg = jnp.take(v_tab, eidx, axis=0)
    return _experts(xm, hn, wc, ug, vg, gfin)


def kernel(x_prompt, x_sample, cache_k_window, cache_v_window, state_gla, meta_tokens, g_norm_mix, w_in,
           w_gate_up, b_gate, attn_sinks, g_gla_norm, w_branch_a, w_branch_b, w_out, g_norm_ffn, w_peer_q,
           peer_sub_keys, peer_u, peer_v, g_norm_final):
    bsz, seq, d = x_prompt.shape
    dbsz, tdec, _ = x_sample.shape
    n_meta = meta_tokens.shape[0]
    depth = w_in.shape[0]
    window = cache_k_window.shape[2]
    kv_heads, head_dim = cache_k_window.shape[3], cache_k_window.shape[4]
    gate_rank = w_gate_up.shape[1]
    bqk = w_gate_up.shape[2]
    n_ph, _, n_keys, p_half = peer_sub_keys.shape[1:]
    assert depth == 1 and d == 1024 and window == ATTN_BLOCK and kv_heads == 2 and head_dim == 64
    assert bqk == 256 and state_gla.shape[2:] == (4, 64, 128) and n_meta <= ATTN_BLOCK
    assert seq % ATTN_BLOCK == 0 and tdec <= SAMPLE_PAD and n_keys == 128 and p_half == 64 and n_ph == 8
    rope_dim = head_dim // 4
    meta_pad = ATTN_BLOCK - n_meta
    lp = ATTN_BLOCK + seq
    nblk = lp // ATTN_BLOCK

    w = w_in[0]
    c_lr = 2304
    c_gate = c_lr + gate_rank
    w1 = w[:, :c_lr].astype(BF16)
    wlr = jnp.pad(w[:, c_lr:c_gate], ((0, 0), (0, LANES - gate_rank))).astype(BF16)
    w2 = w[:, c_gate:].astype(BF16)
    wgu = jnp.pad(w_gate_up[0], ((0, LANES - gate_rank), (0, 0))).astype(BF16)
    bg = b_gate[0][None, :]
    gmix = g_norm_mix[0][None, :]
    wa = w_branch_a[0].astype(BF16)
    wb = w_branch_b[0].astype(BF16)
    wo = w_out[0].astype(BF16)
    gffn = g_norm_ffn[0][None, :]
    wq = w_peer_q[0].astype(BF16)
    keys = peer_sub_keys[0].reshape(n_ph * 2, n_keys, p_half).astype(BF16)
    u_tab = peer_u[0].astype(BF16)
    v_tab = peer_v[0].astype(BF16)
    gfin = g_norm_final[None, :]
    gn = g_gla_norm[0][None, :]
    sinks = attn_sinks[0]
    qk_scale = float(bqk // 4) ** -0.5

    meta = jnp.broadcast_to(meta_tokens[None].astype(x_prompt.dtype), (bsz, n_meta, d))
    xpad = jnp.concatenate([jnp.zeros((bsz, meta_pad, d), x_prompt.dtype), meta, x_prompt], axis=1)
    xpad = xpad.reshape(bsz * lp, d)
    rows_p = jnp.arange(lp)
    tab_p = _rope_table(rows_p - meta_pad, rows_p >= meta_pad, rope_dim, head_dim)
    proj_rows = max(r for r in range(16, PROJ_ROWS + 1, 16) if lp % r == 0)
    qa, kv, gl, gt = _project(xpad, gmix, tab_p, w1, wlr, wgu, bg, w2, proj_rows, qk_scale)

    nq = nblk - 1
    ya = _attention(
        sinks, qa, kv, kv, bsz, nq, ATTN_BLOCK,
        lambda b, n: (b * nblk + n + 1, 0), lambda b, n: (b * nblk + n, 0), lambda b, n: (b * nblk + n + 1, 0),
        lambda b, n: (b * nq + n, 0), bsz * seq, first_valid_key=meta_pad, block_offset=1)

    nchunks = lp // GLA_CHUNK
    skip = ATTN_BLOCK // GLA_CHUNK
    ncq = nchunks - skip
    s0_p = jnp.zeros((bsz,) + state_gla.shape[2:], F32)
    yb, s_fin_p = _gla(gl, s0_p, gn, bsz, nchunks, GLA_CHUNK,
                       lambda b, c: (b * nchunks + c, 0),
                       lambda b, c: (b * ncq + jnp.maximum(c - skip, 0), 0), bsz * seq)

    xp_rows = x_prompt.reshape(bsz * seq, d)
    per_seq = seq // MERGE_ROWS
    gt_map_p = lambda i: ((i // per_seq) * nblk + 1 + (i % per_seq), 0)
    xm_p, hn_p, et_p, wt_p = _merge_route(xp_rows, gt, ya, yb, wa, wb, wo, gffn, wq, keys, gt_map_p)
    y_prompt = _peer_tail(xm_p, hn_p, et_p, wt_p, u_tab, v_tab, gfin).reshape(bsz, seq, d)

    kv_p = kv.reshape(bsz, lp, 2, kv_heads, head_dim)[:, lp - window:]
    new_k_p = kv_p[:, :, 0][None]
    new_v_p = kv_p[:, :, 1][None]

    sp = SAMPLE_PAD
    xs_pad = jnp.pad(x_sample, ((0, 0), (0, sp - tdec), (0, 0))).reshape(dbsz * sp, d)
    rows_s = jnp.arange(sp)
    reps = 256 // sp
    tab_s = jnp.tile(_rope_table(PAST_LEN + rows_s, rows_s < tdec, rope_dim, head_dim), (reps, 1))
    qa_s, kv_s, gl_s, gt_s = _project(xs_pad, gmix, tab_s, w1, wlr, wgu, bg, w2, 256, qk_scale)

    cache_kv = jnp.concatenate([cache_k_window[0].reshape(dbsz * window, kv_heads * head_dim),
                                cache_v_window[0].reshape(dbsz * window, kv_heads * head_dim)], axis=1)
    seq_map = lambda b, n: (b, 0)
    ya_s = _attention(sinks, qa_s, cache_kv, kv_s, dbsz, 1, sp, seq_map, seq_map, seq_map, seq_map,
                      dbsz * sp, first_valid_key=None, block_offset=0)
    yb_s, s_fin_s = _gla(gl_s, state_gla[0], gn, dbsz, 1, sp, seq_map, seq_map, dbsz * sp)

    def real_rows(a):
        return a.reshape(dbsz, sp, a.shape[-1])[:, :tdec].reshape(dbsz * tdec, a.shape[-1])

    xs_rows = x_sample.reshape(dbsz * tdec, d)
    xm_s, hn_s, et_s, wt_s = _merge_route(xs_rows, real_rows(gt_s), real_rows(ya_s), real_rows(yb_s),
                                          wa, wb, wo, gffn, wq, keys, lambda i: (i, 0))
    y_sample = _peer_tail(xm_s, hn_s, et_s, wt_s, u_tab, v_tab, gfin).reshape(dbsz, tdec, d)

    kv_new = real_rows(kv_s).reshape(dbsz, tdec, 2, kv_heads, head_dim)
    new_k_s = jnp.concatenate([cache_k_window[0].astype(F32), kv_new[:, :, 0]], axis=1)[:, -window:][None]
    new_v_s = jnp.concatenate([cache_v_window[0].astype(F32), kv_new[:, :, 1]], axis=1)[:, -window:][None]

    return (y_prompt, y_sample, new_k_p, new_v_p, s_fin_p[None], new_k_s, new_v_s, s_fin_s[None])
```

```python
import functools
import math

import jax
import jax.numpy as jnp
from jax import lax
from jax.experimental import pallas as pl
from jax.experimental.pallas import tpu as pltpu
from jax.experimental.pallas import tpu_sc as plsc

F32 = jnp.float32
BF16 = jnp.bfloat16

EPS = 1e-6
NEG_INF = -1e30
PAST_LEN = 16384
ROPE_THETA = 500000.0
GATE_NORMALIZER = 16.0
PEER_TOPK = 16

LANES = 128
SUBLANES = 8
VMEM_LIMIT_BYTES = 56 * 1024 * 1024

ATTN_BLOCK = 128
GLA_CHUNK = 64
SAMPLE_PAD = 16
PROJ_ROWS = 544
MERGE_ROWS = 128
EXPERT_TOKENS = 16
GATHER_WINDOW = 64
GATHER_INDEX_CHUNK = 2048


def _cparams(sem):
    return pltpu.CompilerParams(dimension_semantics=sem, vmem_limit_bytes=VMEM_LIMIT_BYTES)


def _rms(x, g):
    ms = jnp.mean(x * x, axis=-1, keepdims=True)
    return (x * lax.rsqrt(ms + EPS)) * g


def _proj_kernel(x_ref, g_ref, tab_ref, w1_ref, wlr_ref, wgu_ref, bg_ref, w2_ref,
                 qa_ref, kv_ref, gl_ref, gt_ref, *, period, qk_scale):
    i = pl.program_id(0)
    tr = x_ref.shape[0]
    hb = _rms(x_ref[...], g_ref[...]).astype(BF16)
    z1 = jnp.dot(hb, w1_ref[...], preferred_element_type=F32)

    start = pl.multiple_of((i * tr) % period, SUBLANES)
    tab = tab_ref[pl.ds(start, tr), :]
    cosf = tab[:, 0:LANES]
    sin_lo = tab[:, LANES:2 * LANES]
    sin_hi = tab[:, 2 * LANES:3 * LANES]
    valid = tab[:, 3 * LANES:3 * LANES + 1]

    def rope(xg):
        return xg * cosf + pltpu.roll(xg, 8, 1) * sin_lo + pltpu.roll(xg, LANES - 8, 1) * sin_hi

    for gi in range(4):
        sl = slice(gi * LANES, (gi + 1) * LANES)
        qa_ref[:, sl] = rope(z1[:, sl]).astype(BF16)
    kv_ref[:, 0:LANES] = rope(z1[:, 512:640])
    kv_ref[:, LANES:2 * LANES] = z1[:, 640:768]

    lr = jnp.dot(hb, wlr_ref[...], preferred_element_type=F32)
    pre = jnp.dot(lr.astype(BF16), wgu_ref[...], preferred_element_type=F32) + bg_ref[...]
    log_sig = jnp.minimum(pre, 0.0) - jnp.log1p(jnp.exp(-jnp.abs(pre)))
    ld = jnp.where(valid > 0.5, log_sig / GATE_NORMALIZER, 0.0)

    gl_ref[:, 0:256] = z1[:, 768:1024] * qk_scale
    gl_ref[:, 256:512] = z1[:, 1024:1280]
    gl_ref[:, 512:768] = ld
    gl_ref[:, 768:1792] = z1[:, 1280:2304]
    gt_ref[...] = jnp.dot(hb, w2_ref[...], preferred_element_type=F32)


def _project(x, g, tab, w1, wlr, wgu, bg, w2, rows, qk_scale):
    r, d = x.shape
    period = tab.shape[0]
    const = lambda i: (0, 0)
    row = lambda i: (i, 0)
    return pl.pallas_call(
        functools.partial(_proj_kernel, period=period, qk_scale=qk_scale),
        grid=(r // rows,),
        in_specs=[
            pl.BlockSpec((rows, d), row),
            pl.BlockSpec(g.shape, const),
            pl.BlockSpec(tab.shape, const),
            pl.BlockSpec(w1.shape, const),
            pl.BlockSpec(wlr.shape, const),
            pl.BlockSpec(wgu.shape, const),
            pl.BlockSpec(bg.shape, const),
            pl.BlockSpec(w2.shape, const),
        ],
        out_specs=[
            pl.BlockSpec((rows, 512), row),
            pl.BlockSpec((rows, 256), row),
            pl.BlockSpec((rows, 1792), row),
            pl.BlockSpec((rows, 2048), row),
        ],
        out_shape=[
            jax.ShapeDtypeStruct((r, 512), BF16),
            jax.ShapeDtypeStruct((r, 256), F32),
            jax.ShapeDtypeStruct((r, 1792), F32),
            jax.ShapeDtypeStruct((r, 2048), F32),
        ],
        compiler_params=_cparams(("arbitrary",)),
        name="proj",
    )(x, g, tab, w1, wlr, wgu, bg, w2)


def _attn_kernel(sink_ref, q_ref, prev_ref, cur_ref, o_ref, *, first_valid_key, block_offset):
    n = pl.program_id(1)
    qr = q_ref.shape[0]
    kr = cur_ref.shape[0]
    w = prev_ref.shape[0]
    nk = w + kr
    group = 4
    hd = 64

    rows = lax.broadcasted_iota(jnp.int32, (group * qr, nk), 0)
    cols = lax.broadcasted_iota(jnp.int32, (group * qr, nk), 1)
    head_of_row = rows // qr
    diff = (rows - head_of_row * qr) - cols + w
    mask = (diff >= 0) & (diff <= w)
    if first_valid_key is not None:
        blk = n + block_offset
        mask = mask & (cols >= first_valid_key + w - blk * w)

    prev = prev_ref[...]
    cur = cur_ref[...]
    q = q_ref[...]
    row_head = lax.broadcasted_iota(jnp.int32, (group * qr, 1), 0) // qr
    for kh in range(2):
        k = jnp.concatenate([prev[:, kh * hd:(kh + 1) * hd], cur[:, kh * hd:(kh + 1) * hd]], axis=0).astype(BF16)
        v = jnp.concatenate([prev[:, LANES + kh * hd:LANES + (kh + 1) * hd],
                             cur[:, LANES + kh * hd:LANES + (kh + 1) * hd]], axis=0).astype(BF16)
        qs = jnp.concatenate([q[:, (group * kh + g) * hd:(group * kh + g + 1) * hd] for g in range(group)], axis=0)
        s = lax.dot_general(qs, k, (((1,), (1,)), ((), ())), preferred_element_type=F32) * (hd ** -0.5)
        s = jnp.where(mask, s, NEG_INF)
        sink = jnp.zeros((group * qr, 1), F32)
        for g in range(group):
            sink = jnp.where(row_head == g, sink_ref[group * kh + g], sink)
        m = jnp.maximum(jnp.max(s, axis=-1, keepdims=True), sink)
        e = jnp.exp(s - m)
        p = e / (jnp.sum(e, axis=-1, keepdims=True) + jnp.exp(sink - m))
        o = jnp.dot(p.astype(BF16), v, preferred_element_type=F32)
        for g in range(group):
            h = group * kh + g
            o_ref[:, h * hd:(h + 1) * hd] = o[g * qr:(g + 1) * qr].astype(BF16)


def _attention(sinks, q, kv_prev, kv_cur, nb, nblk, qr, q_map, prev_map, cur_map, out_map, out_rows,
               first_valid_key, block_offset):
    w = ATTN_BLOCK
    return pl.pallas_call(
        functools.partial(_attn_kernel, first_valid_key=first_valid_key, block_offset=block_offset),
        grid=(nb, nblk),
        in_specs=[
            pl.BlockSpec(memory_space=pltpu.SMEM),
            pl.BlockSpec((qr, 512), q_map),
            pl.BlockSpec((w, 256), prev_map),
            pl.BlockSpec((qr, 256), cur_map),
        ],
        out_specs=pl.BlockSpec((qr, 512), out_map),
        out_shape=jax.ShapeDtypeStruct((out_rows, 512), BF16),
        compiler_params=_cparams(("arbitrary", "arbitrary")),
        name="swa",
    )(sinks, q, kv_prev, kv_cur)


def _gla_kernel(gl_ref, s0_ref, gn_ref, yb_ref, sfin_ref, st_ref):
    c = pl.program_id(1)
    ch = gl_ref.shape[0]
    nh, dk, dv = 4, 64, 128

    @pl.when(c == 0)
    def _():
        for h in range(nh):
            st_ref[h] = s0_ref[0, h].T

    gl = gl_ref[...]
    q = gl[:, 0:256]
    k = gl[:, 256:512]
    b = gl[:, 512:768]
    row = lax.broadcasted_iota(jnp.int32, (ch, nh * dk), 0)
    sh = 1
    while sh < ch:
        b = b + jnp.where(row >= sh, pltpu.roll(b, sh, 0), 0.0)
        sh *= 2
    b_last = b[ch - 1:ch, :]
    q_t = (q * jnp.exp(b)).astype(BF16)
    k_t = (k * jnp.exp(-b)).astype(BF16)
    k_end = (k * jnp.exp(b_last - b)).astype(BF16)
    decay = jnp.exp(b_last)
    causal = (lax.broadcasted_iota(jnp.int32, (ch, ch), 0) >= lax.broadcasted_iota(jnp.int32, (ch, ch), 1))
    gn = gn_ref[...]
    nt = (((1,), (1,)), ((), ()))
    for h in range(nh):
        ks = slice(h * dk, (h + 1) * dk)
        v = gl[:, 768 + h * dv:768 + (h + 1) * dv]
        vb = v.astype(BF16)
        a = lax.dot_general(q_t[:, ks], k_t[:, ks], nt, preferred_element_type=F32)
        a = jnp.where(causal, a, 0.0)
        s_t = st_ref[h]
        o = jnp.dot(a.astype(BF16), vb, preferred_element_type=F32)
        o = o + lax.dot_general(q_t[:, ks], s_t.astype(BF16), nt, preferred_element_type=F32)
        upd = jnp.dot(v.T.astype(BF16), k_end[:, ks], preferred_element_type=F32)
        st_ref[h] = s_t * decay[:, ks] + upd
        go = gl[:, 1280 + h * dv:1280 + (h + 1) * dv]
        y = _rms(o, gn) * (go * jax.nn.sigmoid(go))
        yb_ref[:, h * dv:(h + 1) * dv] = y.astype(BF16)

    @pl.when(c == pl.num_programs(1) - 1)
    def _():
        for h in range(nh):
            sfin_ref[0, h] = st_ref[h].T


def _gla(gl, s0, gn, nb, nchunks, ch, in_map, out_map, out_rows):
    return pl.pallas_call(
        _gla_kernel,
        grid=(nb, nchunks),
        in_specs=[
            pl.BlockSpec((ch, 1792), in_map),
            pl.BlockSpec((1, 4, 64, 128), lambda b, c: (b, 0, 0, 0)),
            pl.BlockSpec((1, 128), lambda b, c: (0, 0)),
        ],
        out_specs=[
            pl.BlockSpec((ch, 512), out_map),
            pl.BlockSpec((1, 4, 64, 128), lambda b, c: (b, 0, 0, 0)),
        ],
        out_shape=[
            jax.ShapeDtypeStruct((out_rows, 512), BF16),
            jax.ShapeDtypeStruct((nb, 4, 64, 128), F32),
        ],
        scratch_shapes=[pltpu.VMEM((4, 128, 64), F32)],
        compiler_params=_cparams(("arbitrary", "arbitrary")),
        name="gla",
    )(gl, s0, gn)


def _extract_topk(work, nsel, iota0, sentinel):
    vals, idxs = [], []
    for j in range(nsel):
        m = jnp.max(work, axis=0, keepdims=True)
        idx = jnp.min(jnp.where(work == m, iota0, sentinel), axis=0, keepdims=True)
        vals.append(m)
        idxs.append(idx)
        if j + 1 < nsel:
            work = jnp.where(iota0 == idx, -jnp.inf, work)
    return vals, idxs


def _merge_kernel(x_ref, gt_ref, ya_ref, yb_ref, wa_ref, wb_ref, wo_ref, gf_ref, wq_ref, keys_ref,
                  xm_ref, hn_ref, et_ref, wt_ref):
    td = x_ref.shape[0]
    nkeys = keys_ref.shape[1]
    half = keys_ref.shape[2]
    nheads = keys_ref.shape[0] // 2
    topk = PEER_TOPK

    gt = gt_ref[...]
    d = x_ref.shape[1]
    ma = jnp.dot(ya_ref[...], wa_ref[...], preferred_element_type=F32)
    mb = jnp.dot(yb_ref[...], wb_ref[...], preferred_element_type=F32)
    m = jax.nn.sigmoid(gt[:, 0:d]) * ma + jax.nn.sigmoid(gt[:, d:2 * d]) * mb
    xm = x_ref[...] + jnp.dot(m.astype(BF16), wo_ref[...], preferred_element_type=F32)
    xm_ref[...] = xm
    hn = _rms(xm, gf_ref[...])
    hn_ref[...] = hn
    q = jnp.dot(hn.astype(BF16), wq_ref[...], preferred_element_type=F32).astype(BF16)

    nt = (((1,), (1,)), ((), ()))
    iota_k = lax.broadcasted_iota(jnp.int32, (nkeys, td), 0)
    iota_c = lax.broadcasted_iota(jnp.int32, (topk * topk, td), 0)
    for h in range(nheads):
        sv, si = [], []
        for c in range(2):
            gi = 2 * h + c
            s_t = lax.dot_general(keys_ref[gi], q[:, gi * half:(gi + 1) * half], nt,
                                  preferred_element_type=F32)
            vals, idxs = _extract_topk(s_t, topk, iota_k, nkeys)
            sv.append(vals)
            si.append(idxs)
        sv1 = jnp.concatenate(sv[1], axis=0)
        si1 = jnp.concatenate(si[1], axis=0)
        cand = jnp.concatenate([sv[0][a] + sv1 for a in range(topk)], axis=0)
        cidx = jnp.concatenate([si[0][a] * nkeys + si1 for a in range(topk)], axis=0)
        fvals, eids = [], []
        work = cand
        for j in range(topk):
            mx = jnp.max(work, axis=0, keepdims=True)
            pos = jnp.min(jnp.where(work == mx, iota_c, topk * topk), axis=0, keepdims=True)
            hit = iota_c == pos
            eids.append(jnp.max(jnp.where(hit, cidx, -1), axis=0, keepdims=True))
            fvals.append(mx)
            if j + 1 < topk:
                work = jnp.where(hit, -jnp.inf, work)
        fv = jnp.concatenate(fvals, axis=0)
        e = jnp.exp(fv - fvals[0])
        wt_ref[h * topk:(h + 1) * topk, :] = e / jnp.sum(e, axis=0, keepdims=True)
        et_ref[h * topk:(h + 1) * topk, :] = jnp.concatenate(eids, axis=0)


def _merge_route(x, gt, ya, yb, wa, wb, wo, gf, wq, keys, gt_map):
    t, d = x.shape
    td = MERGE_ROWS
    nsel = (keys.shape[0] // 2) * PEER_TOPK
    const2 = lambda i: (0, 0)
    row = lambda i: (i, 0)
    col = lambda i: (0, i)
    return pl.pallas_call(
        _merge_kernel,
        grid=(t // td,),
        in_specs=[
            pl.BlockSpec((td, d), row),
            pl.BlockSpec((td, 2 * d), gt_map),
            pl.BlockSpec((td, ya.shape[1]), row),
            pl.BlockSpec((td, yb.shape[1]), row),
            pl.BlockSpec(wa.shape, const2),
            pl.BlockSpec(wb.shape, const2),
            pl.BlockSpec(wo.shape, const2),
            pl.BlockSpec(gf.shape, const2),
            pl.BlockSpec(wq.shape, const2),
            pl.BlockSpec(keys.shape, lambda i: (0, 0, 0)),
        ],
        out_specs=[
            pl.BlockSpec((td, d), row),
            pl.BlockSpec((td, d), row),
            pl.BlockSpec((nsel, td), col),
            pl.BlockSpec((nsel, td), col),
        ],
        out_shape=[
            jax.ShapeDtypeStruct((t, d), F32),
            jax.ShapeDtypeStruct((t, d), F32),
            jax.ShapeDtypeStruct((nsel, t), jnp.int32),
            jax.ShapeDtypeStruct((nsel, t), F32),
        ],
        compiler_params=_cparams(("arbitrary",)),
        name="merge_route",
    )(x, gt, ya, yb, wa, wb, wo, gf, wq, keys)


def _unpack_pair(w):
    lo = pltpu.bitcast(w << 16, F32)
    hi = pltpu.bitcast(w & jnp.int32(-65536), F32)
    return lo, hi


def _expert_kernel(xm_ref, hn_ref, wc_ref, ug_ref, vg_ref, gfin_ref, y_ref):
    tt = xm_ref.shape[0]
    dh = ug_ref.shape[2]
    wc = wc_ref[0]
    rows = []
    for t in range(tt):
        u_lo, u_hi = _unpack_pair(ug_ref[t])
        h = hn_ref[t:t + 1, :]
        act = jnp.sum(u_lo * h[:, :dh] + u_hi * h[:, dh:], axis=1, keepdims=True)
        gelu = 0.5 * act * (1.0 + lax.erf(act * (2.0 ** -0.5)))
        coef = wc[:, t:t + 1] * gelu
        v_lo, v_hi = _unpack_pair(vg_ref[t])
        rows.append(jnp.concatenate([jnp.sum(v_lo * coef, axis=0, keepdims=True),
                                     jnp.sum(v_hi * coef, axis=0, keepdims=True)], axis=1))
    out = xm_ref[...] + jnp.concatenate(rows, axis=0)
    y_ref[...] = _rms(out, gfin_ref[...])


def _experts(xm, hn, wc, ug, vg, gfin):
    t, d = xm.shape
    tt = EXPERT_TOKENS
    nsel = ug.shape[1]
    row = lambda i: (i, 0)
    row3 = lambda i: (i, 0, 0)
    return pl.pallas_call(
        _expert_kernel,
        grid=(t // tt,),
        in_specs=[
            pl.BlockSpec((tt, d), row),
            pl.BlockSpec((tt, d), row),
            pl.BlockSpec((1, nsel, tt), row3),
            pl.BlockSpec((tt, nsel, d // 2), row3),
            pl.BlockSpec((tt, nsel, d // 2), row3),
            pl.BlockSpec(gfin.shape, lambda i: (0, 0)),
        ],
        out_specs=pl.BlockSpec((tt, d), row),
        out_shape=jax.ShapeDtypeStruct((t, d), F32),
        compiler_params=_cparams(("arbitrary",)),
        name="experts",
    )(xm, hn, wc, ug, vg, gfin)


def _rope_table(pos, valid, rope_dim, head_dim):
    half = rope_dim // 2
    inv = ROPE_THETA ** (-jnp.arange(0, rope_dim, 2, dtype=F32) / rope_dim)
    ang = pos.astype(F32)[:, None] * inv[None, :]
    cos, sin = jnp.cos(ang), jnp.sin(ang)
    n = pos.shape[0]
    ones = jnp.ones((n, head_dim - rope_dim), F32)
    zeros_h = jnp.zeros((n, half), F32)
    zeros_r = jnp.zeros((n, head_dim - rope_dim), F32)
    reps = LANES // head_dim
    cosf = jnp.tile(jnp.concatenate([cos, cos, ones], axis=1), (1, reps))
    sin_lo = jnp.tile(jnp.concatenate([zeros_h, sin, zeros_r], axis=1), (1, reps))
    sin_hi = jnp.tile(jnp.concatenate([-sin, zeros_h, zeros_r], axis=1), (1, reps))
    vcol = jnp.broadcast_to(valid.astype(F32)[:, None], (n, LANES))
    return jnp.concatenate([cosf, sin_lo, sin_hi, vcol], axis=1)


def _pack_table(tab):
    half = tab.shape[1] // 2
    bits = lax.bitcast_convert_type(tab.astype(BF16), jnp.uint16).astype(jnp.uint32)
    return lax.bitcast_convert_type(bits[:, :half] | (bits[:, half:] << 16), jnp.int32)


def _gather_rows(table, idx):
    n = idx.shape[0]
    c = table.shape[1]
    sc = plsc.get_sparse_core_info()
    workers = sc.num_cores * sc.num_subcores
    gw = GATHER_WINDOW
    per_worker = n // workers
    assert n % workers == 0 and per_worker % GATHER_INDEX_CHUNK == 0 and GATHER_INDEX_CHUNK % (2 * gw) == 0
    mesh = plsc.VectorSubcoreMesh(core_axis_name="c", subcore_axis_name="s")

    @functools.partial(
        pl.kernel, out_type=jax.ShapeDtypeStruct((n, c), table.dtype), mesh=mesh, name="row_gather",
        scratch_types=[pltpu.VMEM((GATHER_INDEX_CHUNK,), jnp.int32),
                       pltpu.VMEM((gw, c), table.dtype), pltpu.VMEM((gw, c), table.dtype),
                       pltpu.SemaphoreType.DMA, pltpu.SemaphoreType.DMA,
                       pltpu.SemaphoreType.DMA, pltpu.SemaphoreType.DMA])
    def gather(tab_hbm, idx_hbm, out_hbm, idx_v, rows0, rows1, gsem0, gsem1, wsem0, wsem1):
        wid = lax.axis_index("s") * sc.num_cores + lax.axis_index("c")
        base = wid * per_worker

        @pl.loop(0, per_worker // GATHER_INDEX_CHUNK)
        def _(ic):
            chunk_base = base + ic * GATHER_INDEX_CHUNK
            pltpu.sync_copy(idx_hbm.at[pl.ds(chunk_base, GATHER_INDEX_CHUNK)], idx_v)

            @pl.loop(0, GATHER_INDEX_CHUNK // (2 * gw))
            def _(j):
                off0 = j * (2 * gw)
                off1 = off0 + gw
                g0 = pltpu.async_copy(tab_hbm.at[idx_v.at[pl.ds(off0, gw)]], rows0, gsem0)
                g1 = pltpu.async_copy(tab_hbm.at[idx_v.at[pl.ds(off1, gw)]], rows1, gsem1)
                g0.wait()
                w0 = pltpu.async_copy(rows0, out_hbm.at[pl.ds(chunk_base + off0, gw)], wsem0)
                g1.wait()
                w1 = pltpu.async_copy(rows1, out_hbm.at[pl.ds(chunk_base + off1, gw)], wsem1)
                w0.wait()
                w1.wait()

    return gather(table, idx)


def _peer_tail(xm, hn, et, wt, u_tab, v_tab, gfin):
    t = xm.shape[0]
    tt = EXPERT_TOKENS
    nsel = et.shape[0]
    eidx = et.T.reshape(t * nsel)
    wc = wt.reshape(nsel, t // tt, tt).transpose(1, 0, 2)
    ug = _gather_rows(u_tab, eidx).reshape(t, nsel, u_tab.shape[1])
    vg = _gather_rows(v_tab, eidx).reshape(t, nsel, v_tab.shape[1])
    return _experts(xm, hn, wc, ug, vg, gfin)


def kernel(x_prompt, x_sample, cache_k_window, cache_v_window, state_gla, meta_tokens, g_norm_mix, w_in,
           w_gate_up, b_gate, attn_sinks, g_gla_norm, w_branch_a, w_branch_b, w_out, g_norm_ffn, w_peer_q,
           peer_sub_keys, peer_u, peer_v, g_norm_final):
    bsz, seq, d = x_prompt.shape
    dbsz, tdec, _ = x_sample.shape
    n_meta = meta_tokens.shape[0]
    depth = w_in.shape[0]
    window = cache_k_window.shape[2]
    kv_heads, head_dim = cache_k_window.shape[3], cache_k_window.shape[4]
    gate_rank = w_gate_up.shape[1]
    bqk = w_gate_up.shape[2]
    n_ph, _, n_keys, p_half = peer_sub_keys.shape[1:]
    assert depth == 1 and d == 1024 and window == ATTN_BLOCK and kv_heads == 2 and head_dim == 64
    assert bqk == 256 and state_gla.shape[2:] == (4, 64, 128) and n_meta <= ATTN_BLOCK
    assert seq % ATTN_BLOCK == 0 and tdec <= SAMPLE_PAD and n_keys == 128 and p_half == 64 and n_ph == 8
    rope_dim = head_dim // 4
    meta_pad = ATTN_BLOCK - n_meta
    lp = ATTN_BLOCK + seq
    nblk = lp // ATTN_BLOCK

    w = w_in[0]
    c_lr = 2304
    c_gate = c_lr + gate_rank
    w1 = w[:, :c_lr].astype(BF16)
    wlr = jnp.pad(w[:, c_lr:c_gate], ((0, 0), (0, LANES - gate_rank))).astype(BF16)
    w2 = w[:, c_gate:].astype(BF16)
    wgu = jnp.pad(w_gate_up[0], ((0, LANES - gate_rank), (0, 0))).astype(BF16)
    bg = b_gate[0][None, :]
    gmix = g_norm_mix[0][None, :]
    wa = w_branch_a[0].astype(BF16)
    wb = w_branch_b[0].astype(BF16)
    wo = w_out[0].astype(BF16)
    gffn = g_norm_ffn[0][None, :]
    wq = w_peer_q[0].astype(BF16)
    keys = peer_sub_keys[0].reshape(n_ph * 2, n_keys, p_half).astype(BF16)
    u_tab = _pack_table(peer_u[0])
    v_tab = _pack_table(peer_v[0])
    gfin = g_norm_final[None, :]
    gn = g_gla_norm[0][None, :]
    sinks = attn_sinks[0]
    qk_scale = float(bqk // 4) ** -0.5

    meta = jnp.broadcast_to(meta_tokens[None].astype(x_prompt.dtype), (bsz, n_meta, d))
    xpad = jnp.concatenate([jnp.zeros((bsz, meta_pad, d), x_prompt.dtype), meta, x_prompt], axis=1)
    xpad = xpad.reshape(bsz * lp, d)
    rows_p = jnp.arange(lp)
    tab_p = _rope_table(rows_p - meta_pad, rows_p >= meta_pad, rope_dim, head_dim)
    proj_rows = max(r for r in range(16, PROJ_ROWS + 1, 16) if lp % r == 0)
    qa, kv, gl, gt = _project(xpad, gmix, tab_p, w1, wlr, wgu, bg, w2, proj_rows, qk_scale)

    nq = nblk - 1
    ya = _attention(
        sinks, qa, kv, kv, bsz, nq, ATTN_BLOCK,
        lambda b, n: (b * nblk + n + 1, 0), lambda b, n: (b * nblk + n, 0), lambda b, n: (b * nblk + n + 1, 0),
        lambda b, n: (b * nq + n, 0), bsz * seq, first_valid_key=meta_pad, block_offset=1)

    nchunks = lp // GLA_CHUNK
    skip = ATTN_BLOCK // GLA_CHUNK
    ncq = nchunks - skip
    s0_p = jnp.zeros((bsz,) + state_gla.shape[2:], F32)
    yb, s_fin_p = _gla(gl, s0_p, gn, bsz, nchunks, GLA_CHUNK,
                       lambda b, c: (b * nchunks + c, 0),
                       lambda b, c: (b * ncq + jnp.maximum(c - skip, 0), 0), bsz * seq)

    xp_rows = x_prompt.reshape(bsz * seq, d)
    per_seq = seq // MERGE_ROWS
    gt_map_p = lambda i: ((i // per_seq) * nblk + 1 + (i % per_seq), 0)
    xm_p, hn_p, et_p, wt_p = _merge_route(xp_rows, gt, ya, yb, wa, wb, wo, gffn, wq, keys, gt_map_p)
    y_prompt = _peer_tail(xm_p, hn_p, et_p, wt_p, u_tab, v_tab, gfin).reshape(bsz, seq, d)

    kv_p = kv.reshape(bsz, lp, 2, kv_heads, head_dim)[:, lp - window:]
    new_k_p = kv_p[:, :, 0][None]
    new_v_p = kv_p[:, :, 1][None]

    sp = SAMPLE_PAD
    xs_pad = jnp.pad(x_sample, ((0, 0), (0, sp - tdec), (0, 0))).reshape(dbsz * sp, d)
    rows_s = jnp.arange(sp)
    reps = 256 // sp
    tab_s = jnp.tile(_rope_table(PAST_LEN + rows_s, rows_s < tdec, rope_dim, head_dim), (reps, 1))
    qa_s, kv_s, gl_s, gt_s = _project(xs_pad, gmix, tab_s, w1, wlr, wgu, bg, w2, 256, qk_scale)

    cache_kv = jnp.concatenate([cache_k_window[0].reshape(dbsz * window, kv_heads * head_dim),
                                cache_v_window[0].reshape(dbsz * window, kv_heads * head_dim)], axis=1)
    seq_map = lambda b, n: (b, 0)
    ya_s = _attention(sinks, qa_s, cache_kv, kv_s, dbsz, 1, sp, seq_map, seq_map, seq_map, seq_map,
                      dbsz * sp, first_valid_key=None, block_offset=0)
    yb_s, s_fin_s = _gla(gl_s, state_gla[0], gn, dbsz, 1, sp, seq_map, seq_map, dbsz * sp)

    def real_rows(a):
        return a.reshape(dbsz, sp, a.shape[-1])[:, :tdec].reshape(dbsz * tdec, a.shape[-1])

    xs_rows = x_sample.reshape(dbsz * tdec, d)
    xm_s, hn_s, et_s, wt_s = _merge_route(xs_rows, real_rows(gt_s), real_rows(ya_s), real_rows(yb_s),
                                          wa, wb, wo, gffn, wq, keys, lambda i: (i, 0))
    y_sample = _peer_tail(xm_s, hn_s, et_s, wt_s, u_tab, v_tab, gfin).reshape(dbsz, tdec, d)

    kv_new = real_rows(kv_s).reshape(dbsz, tdec, 2, kv_heads, head_dim)
    new_k_s = jnp.concatenate([cache_k_window[0].astype(F32), kv_new[:, :, 0]], axis=1)[:, -window:][None]
    new_v_s = jnp.concatenate([cache_v_window[0].astype(F32), kv_new[:, :, 1]], axis=1)[:, -window:][None]

    return (y_prompt, y_sample, new_k_p, new_v_p, s_fin_p[None], new_k_s, new_v_s, s_fin_s[None])
```

```python
import functools
import math

import jax
import jax.numpy as jnp
from jax import lax
from jax.experimental import pallas as pl
from jax.experimental.pallas import tpu as pltpu
from jax.experimental.pallas import tpu_sc as plsc

F32 = jnp.float32
BF16 = jnp.bfloat16

EPS = 1e-6
NEG_INF = -1e30
PAST_LEN = 16384
ROPE_THETA = 500000.0
GATE_NORMALIZER = 16.0
PEER_TOPK = 16

LANES = 128
SUBLANES = 8
VMEM_LIMIT_BYTES = 56 * 1024 * 1024

ATTN_BLOCK = 128
GLA_CHUNK = 64
SAMPLE_PAD = 16
PROJ_ROWS = 544
MERGE_ROWS = 128
EXPERT_TOKENS = 16
GATHER_WINDOW = 64
GATHER_INDEX_CHUNK = 2048
GATHER_ROW_QUANTUM = 512
PROMPT_CHUNKS = 4


def _cparams(sem):
    return pltpu.CompilerParams(dimension_semantics=sem, vmem_limit_bytes=VMEM_LIMIT_BYTES)


def _rms(x, g):
    ms = jnp.mean(x * x, axis=-1, keepdims=True)
    return (x * lax.rsqrt(ms + EPS)) * g


def _proj_kernel(x_ref, g_ref, tab_ref, w1_ref, wlr_ref, wgu_ref, bg_ref, w2_ref,
                 qa_ref, kv_ref, gl_ref, gt_ref, *, period, qk_scale):
    i = pl.program_id(0)
    tr = x_ref.shape[0]
    hb = _rms(x_ref[...], g_ref[...]).astype(BF16)
    z1 = jnp.dot(hb, w1_ref[...], preferred_element_type=F32)

    start = pl.multiple_of((i * tr) % period, SUBLANES)
    tab = tab_ref[pl.ds(start, tr), :]
    cosf = tab[:, 0:LANES]
    sin_lo = tab[:, LANES:2 * LANES]
    sin_hi = tab[:, 2 * LANES:3 * LANES]
    valid = tab[:, 3 * LANES:3 * LANES + 1]

    def rope(xg):
        return xg * cosf + pltpu.roll(xg, 8, 1) * sin_lo + pltpu.roll(xg, LANES - 8, 1) * sin_hi

    for gi in range(4):
        sl = slice(gi * LANES, (gi + 1) * LANES)
        qa_ref[:, sl] = rope(z1[:, sl]).astype(BF16)
    kv_ref[:, 0:LANES] = rope(z1[:, 512:640])
    kv_ref[:, LANES:2 * LANES] = z1[:, 640:768]

    lr = jnp.dot(hb, wlr_ref[...], preferred_element_type=F32)
    pre = jnp.dot(lr.astype(BF16), wgu_ref[...], preferred_element_type=F32) + bg_ref[...]
    log_sig = jnp.minimum(pre, 0.0) - jnp.log1p(jnp.exp(-jnp.abs(pre)))
    ld = jnp.where(valid > 0.5, log_sig / GATE_NORMALIZER, 0.0)

    gl_ref[:, 0:256] = z1[:, 768:1024] * qk_scale
    gl_ref[:, 256:512] = z1[:, 1024:1280]
    gl_ref[:, 512:768] = ld
    gl_ref[:, 768:1792] = z1[:, 1280:2304]
    gt_ref[...] = jnp.dot(hb, w2_ref[...], preferred_element_type=F32)


def _project(x, g, tab, w1, wlr, wgu, bg, w2, rows, qk_scale):
    r, d = x.shape
    period = tab.shape[0]
    const = lambda i: (0, 0)
    row = lambda i: (i, 0)
    return pl.pallas_call(
        functools.partial(_proj_kernel, period=period, qk_scale=qk_scale),
        grid=(r // rows,),
        in_specs=[
            pl.BlockSpec((rows, d), row),
            pl.BlockSpec(g.shape, const),
            pl.BlockSpec(tab.shape, const),
            pl.BlockSpec(w1.shape, const),
            pl.BlockSpec(wlr.shape, const),
            pl.BlockSpec(wgu.shape, const),
            pl.BlockSpec(bg.shape, const),
            pl.BlockSpec(w2.shape, const),
        ],
        out_specs=[
            pl.BlockSpec((rows, 512), row),
            pl.BlockSpec((rows, 256), row),
            pl.BlockSpec((rows, 1792), row),
            pl.BlockSpec((rows, 2048), row),
        ],
        out_shape=[
            jax.ShapeDtypeStruct((r, 512), BF16),
            jax.ShapeDtypeStruct((r, 256), F32),
            jax.ShapeDtypeStruct((r, 1792), F32),
            jax.ShapeDtypeStruct((r, 2048), F32),
        ],
        compiler_params=_cparams(("arbitrary",)),
        name="proj",
    )(x, g, tab, w1, wlr, wgu, bg, w2)


def _attn_kernel(sink_ref, q_ref, prev_ref, cur_ref, o_ref, *, first_valid_key, block_offset):
    n = pl.program_id(1)
    qr = q_ref.shape[0]
    kr = cur_ref.shape[0]
    w = prev_ref.shape[0]
    nk = w + kr
    group = 4
    hd = 64

    rows = lax.broadcasted_iota(jnp.int32, (group * qr, nk), 0)
    cols = lax.broadcasted_iota(jnp.int32, (group * qr, nk), 1)
    head_of_row = rows // qr
    diff = (rows - head_of_row * qr) - cols + w
    mask = (diff >= 0) & (diff <= w)
    if first_valid_key is not None:
        blk = n + block_offset
        mask = mask & (cols >= first_valid_key + w - blk * w)

    prev = prev_ref[...]
    cur = cur_ref[...]
    q = q_ref[...]
    row_head = lax.broadcasted_iota(jnp.int32, (group * qr, 1), 0) // qr
    for kh in range(2):
        k = jnp.concatenate([prev[:, kh * hd:(kh + 1) * hd], cur[:, kh * hd:(kh + 1) * hd]], axis=0).astype(BF16)
        v = jnp.concatenate([prev[:, LANES + kh * hd:LANES + (kh + 1) * hd],
                             cur[:, LANES + kh * hd:LANES + (kh + 1) * hd]], axis=0).astype(BF16)
        qs = jnp.concatenate([q[:, (group * kh + g) * hd:(group * kh + g + 1) * hd] for g in range(group)], axis=0)
        s = lax.dot_general(qs, k, (((1,), (1,)), ((), ())), preferred_element_type=F32) * (hd ** -0.5)
        s = jnp.where(mask, s, NEG_INF)
        sink = jnp.zeros((group * qr, 1), F32)
        for g in range(group):
            sink = jnp.where(row_head == g, sink_ref[group * kh + g], sink)
        m = jnp.maximum(jnp.max(s, axis=-1, keepdims=True), sink)
        e = jnp.exp(s - m)
        p = e / (jnp.sum(e, axis=-1, keepdims=True) + jnp.exp(sink - m))
        o = jnp.dot(p.astype(BF16), v, preferred_element_type=F32)
        for g in range(group):
            h = group * kh + g
            o_ref[:, h * hd:(h + 1) * hd] = o[g * qr:(g + 1) * qr].astype(BF16)


def _attention(sinks, q, kv_prev, kv_cur, nb, nblk, qr, q_map, prev_map, cur_map, out_map, out_rows,
               first_valid_key, block_offset):
    w = ATTN_BLOCK
    return pl.pallas_call(
        functools.partial(_attn_kernel, first_valid_key=first_valid_key, block_offset=block_offset),
        grid=(nb, nblk),
        in_specs=[
            pl.BlockSpec(memory_space=pltpu.SMEM),
            pl.BlockSpec((qr, 512), q_map),
            pl.BlockSpec((w, 256), prev_map),
            pl.BlockSpec((qr, 256), cur_map),
        ],
        out_specs=pl.BlockSpec((qr, 512), out_map),
        out_shape=jax.ShapeDtypeStruct((out_rows, 512), BF16),
        compiler_params=_cparams(("arbitrary", "arbitrary")),
        name="swa",
    )(sinks, q, kv_prev, kv_cur)


def _gla_kernel(gl_ref, s0_ref, gn_ref, yb_ref, sfin_ref, st_ref):
    c = pl.program_id(1)
    ch = gl_ref.shape[0]
    nh, dk, dv = 4, 64, 128

    @pl.when(c == 0)
    def _():
        for h in range(nh):
            st_ref[h] = s0_ref[0, h].T

    gl = gl_ref[...]
    q = gl[:, 0:256]
    k = gl[:, 256:512]
    b = gl[:, 512:768]
    row = lax.broadcasted_iota(jnp.int32, (ch, nh * dk), 0)
    sh = 1
    while sh < ch:
        b = b + jnp.where(row >= sh, pltpu.roll(b, sh, 0), 0.0)
        sh *= 2
    b_last = b[ch - 1:ch, :]
    q_t = (q * jnp.exp(b)).astype(BF16)
    k_t = (k * jnp.exp(-b)).astype(BF16)
    k_end = (k * jnp.exp(b_last - b)).astype(BF16)
    decay = jnp.exp(b_last)
    causal = (lax.broadcasted_iota(jnp.int32, (ch, ch), 0) >= lax.broadcasted_iota(jnp.int32, (ch, ch), 1))
    gn = gn_ref[...]
    nt = (((1,), (1,)), ((), ()))
    for h in range(nh):
        ks = slice(h * dk, (h + 1) * dk)
        v = gl[:, 768 + h * dv:768 + (h + 1) * dv]
        vb = v.astype(BF16)
        a = lax.dot_general(q_t[:, ks], k_t[:, ks], nt, preferred_element_type=F32)
        a = jnp.where(causal, a, 0.0)
        s_t = st_ref[h]
        o = jnp.dot(a.astype(BF16), vb, preferred_element_type=F32)
        o = o + lax.dot_general(q_t[:, ks], s_t.astype(BF16), nt, preferred_element_type=F32)
        upd = jnp.dot(v.T.astype(BF16), k_end[:, ks], preferred_element_type=F32)
        st_ref[h] = s_t * decay[:, ks] + upd
        go = gl[:, 1280 + h * dv:1280 + (h + 1) * dv]
        y = _rms(o, gn) * (go * jax.nn.sigmoid(go))
        yb_ref[:, h * dv:(h + 1) * dv] = y.astype(BF16)

    @pl.when(c == pl.num_programs(1) - 1)
    def _():
        for h in range(nh):
            sfin_ref[0, h] = st_ref[h].T


def _gla(gl, s0, gn, nb, nchunks, ch, in_map, out_map, out_rows):
    return pl.pallas_call(
        _gla_kernel,
        grid=(nb, nchunks),
        in_specs=[
            pl.BlockSpec((ch, 1792), in_map),
            pl.BlockSpec((1, 4, 64, 128), lambda b, c: (b, 0, 0, 0)),
            pl.BlockSpec((1, 128), lambda b, c: (0, 0)),
        ],
        out_specs=[
            pl.BlockSpec((ch, 512), out_map),
            pl.BlockSpec((1, 4, 64, 128), lambda b, c: (b, 0, 0, 0)),
        ],
        out_shape=[
            jax.ShapeDtypeStruct((out_rows, 512), BF16),
            jax.ShapeDtypeStruct((nb, 4, 64, 128), F32),
        ],
        scratch_shapes=[pltpu.VMEM((4, 128, 64), F32)],
        compiler_params=_cparams(("arbitrary", "arbitrary")),
        name="gla",
    )(gl, s0, gn)


def _extract_topk(work, nsel, iota0, sentinel):
    vals, idxs = [], []
    for j in range(nsel):
        m = jnp.max(work, axis=0, keepdims=True)
        idx = jnp.min(jnp.where(work == m, iota0, sentinel), axis=0, keepdims=True)
        vals.append(m)
        idxs.append(idx)
        if j + 1 < nsel:
            work = jnp.where(iota0 == idx, -jnp.inf, work)
    return vals, idxs


def _merge_kernel(x_ref, gt_ref, ya_ref, yb_ref, wa_ref, wb_ref, wo_ref, gf_ref, wq_ref, keys_ref,
                  xm_ref, hn_ref, et_ref, wt_ref):
    td = x_ref.shape[0]
    nkeys = keys_ref.shape[1]
    half = keys_ref.shape[2]
    nheads = keys_ref.shape[0] // 2
    topk = PEER_TOPK

    gt = gt_ref[...]
    d = x_ref.shape[1]
    ma = jnp.dot(ya_ref[...], wa_ref[...], preferred_element_type=F32)
    mb = jnp.dot(yb_ref[...], wb_ref[...], preferred_element_type=F32)
    m = jax.nn.sigmoid(gt[:, 0:d]) * ma + jax.nn.sigmoid(gt[:, d:2 * d]) * mb
    xm = x_ref[...] + jnp.dot(m.astype(BF16), wo_ref[...], preferred_element_type=F32)
    xm_ref[...] = xm
    hn = _rms(xm, gf_ref[...])
    hn_ref[...] = hn
    q = jnp.dot(hn.astype(BF16), wq_ref[...], preferred_element_type=F32).astype(BF16)

    nt = (((1,), (1,)), ((), ()))
    iota_k = lax.broadcasted_iota(jnp.int32, (nkeys, td), 0)
    iota_c = lax.broadcasted_iota(jnp.int32, (topk * topk, td), 0)
    for h in range(nheads):
        sv, si = [], []
        for c in range(2):
            gi = 2 * h + c
            s_t = lax.dot_general(keys_ref[gi], q[:, gi * half:(gi + 1) * half], nt,
                                  preferred_element_type=F32)
            vals, idxs = _extract_topk(s_t, topk, iota_k, nkeys)
            sv.append(vals)
            si.append(idxs)
        sv1 = jnp.concatenate(sv[1], axis=0)
        si1 = jnp.concatenate(si[1], axis=0)
        cand = jnp.concatenate([sv[0][a] + sv1 for a in range(topk)], axis=0)
        cidx = jnp.concatenate([si[0][a] * nkeys + si1 for a in range(topk)], axis=0)
        fvals, eids = [], []
        work = cand
        for j in range(topk):
            mx = jnp.max(work, axis=0, keepdims=True)
            pos = jnp.min(jnp.where(work == mx, iota_c, topk * topk), axis=0, keepdims=True)
            hit = iota_c == pos
            eids.append(jnp.max(jnp.where(hit, cidx, -1), axis=0, keepdims=True))
            fvals.append(mx)
            if j + 1 < topk:
                work = jnp.where(hit, -jnp.inf, work)
        fv = jnp.concatenate(fvals, axis=0)
        e = jnp.exp(fv - fvals[0])
        wt_ref[h * topk:(h + 1) * topk, :] = e / jnp.sum(e, axis=0, keepdims=True)
        et_ref[h * topk:(h + 1) * topk, :] = jnp.concatenate(eids, axis=0)


def _merge_route(x, gt, ya, yb, wa, wb, wo, gf, wq, keys, gt_map, first_block, t):
    d = x.shape[1]
    td = MERGE_ROWS
    nsel = (keys.shape[0] // 2) * PEER_TOPK
    const2 = lambda i: (0, 0)
    row = lambda i: (i, 0)
    col = lambda i: (0, i)
    row_in = lambda i: (i + first_block, 0)
    return pl.pallas_call(
        _merge_kernel,
        grid=(t // td,),
        in_specs=[
            pl.BlockSpec((td, d), row_in),
            pl.BlockSpec((td, 2 * d), lambda i: gt_map(i + first_block)),
            pl.BlockSpec((td, ya.shape[1]), row_in),
            pl.BlockSpec((td, yb.shape[1]), row_in),
            pl.BlockSpec(wa.shape, const2),
            pl.BlockSpec(wb.shape, const2),
            pl.BlockSpec(wo.shape, const2),
            pl.BlockSpec(gf.shape, const2),
            pl.BlockSpec(wq.shape, const2),
            pl.BlockSpec(keys.shape, lambda i: (0, 0, 0)),
        ],
        out_specs=[
            pl.BlockSpec((td, d), row),
            pl.BlockSpec((td, d), row),
            pl.BlockSpec((nsel, td), col),
            pl.BlockSpec((nsel, td), col),
        ],
        out_shape=[
            jax.ShapeDtypeStruct((t, d), F32),
            jax.ShapeDtypeStruct((t, d), F32),
            jax.ShapeDtypeStruct((nsel, t), jnp.int32),
            jax.ShapeDtypeStruct((nsel, t), F32),
        ],
        compiler_params=_cparams(("arbitrary",)),
        name="merge_route",
    )(x, gt, ya, yb, wa, wb, wo, gf, wq, keys)


def _unpack_pair(w):
    lo = pltpu.bitcast(w << 16, F32)
    hi = pltpu.bitcast(w & jnp.int32(-65536), F32)
    return lo, hi


def _expert_kernel(xm_ref, hn_ref, wc_ref, ug_ref, vg_ref, gfin_ref, y_ref):
    tt = xm_ref.shape[0]
    dh = ug_ref.shape[2]
    wc = wc_ref[0]
    rows = []
    for t in range(tt):
        u_lo, u_hi = _unpack_pair(ug_ref[t])
        h = hn_ref[t:t + 1, :]
        act = jnp.sum(u_lo * h[:, :dh] + u_hi * h[:, dh:], axis=1, keepdims=True)
        gelu = 0.5 * act * (1.0 + lax.erf(act * (2.0 ** -0.5)))
        coef = wc[:, t:t + 1] * gelu
        v_lo, v_hi = _unpack_pair(vg_ref[t])
        rows.append(jnp.concatenate([jnp.sum(v_lo * coef, axis=0, keepdims=True),
                                     jnp.sum(v_hi * coef, axis=0, keepdims=True)], axis=1))
    out = xm_ref[...] + jnp.concatenate(rows, axis=0)
    y_ref[...] = _rms(out, gfin_ref[...])


def _experts(xm, hn, wc, ug, vg, gfin):
    t, d = xm.shape
    tt = EXPERT_TOKENS
    nsel = ug.shape[1]
    row = lambda i: (i, 0)
    row3 = lambda i: (i, 0, 0)
    return pl.pallas_call(
        _expert_kernel,
        grid=(t // tt,),
        in_specs=[
            pl.BlockSpec((tt, d), row),
            pl.BlockSpec((tt, d), row),
            pl.BlockSpec((1, nsel, tt), row3),
            pl.BlockSpec((tt, nsel, d // 2), row3),
            pl.BlockSpec((tt, nsel, d // 2), row3),
            pl.BlockSpec(gfin.shape, lambda i: (0, 0)),
        ],
        out_specs=pl.BlockSpec((tt, d), row),
        out_shape=jax.ShapeDtypeStruct((t, d), F32),
        compiler_params=_cparams(("arbitrary",)),
        name="experts",
    )(xm, hn, wc, ug, vg, gfin)


def _rope_table(pos, valid, rope_dim, head_dim):
    half = rope_dim // 2
    inv = ROPE_THETA ** (-jnp.arange(0, rope_dim, 2, dtype=F32) / rope_dim)
    ang = pos.astype(F32)[:, None] * inv[None, :]
    cos, sin = jnp.cos(ang), jnp.sin(ang)
    n = pos.shape[0]
    ones = jnp.ones((n, head_dim - rope_dim), F32)
    zeros_h = jnp.zeros((n, half), F32)
    zeros_r = jnp.zeros((n, head_dim - rope_dim), F32)
    reps = LANES // head_dim
    cosf = jnp.tile(jnp.concatenate([cos, cos, ones], axis=1), (1, reps))
    sin_lo = jnp.tile(jnp.concatenate([zeros_h, sin, zeros_r], axis=1), (1, reps))
    sin_hi = jnp.tile(jnp.concatenate([-sin, zeros_h, zeros_r], axis=1), (1, reps))
    vcol = jnp.broadcast_to(valid.astype(F32)[:, None], (n, LANES))
    return jnp.concatenate([cosf, sin_lo, sin_hi, vcol], axis=1)


def _pack_table(tab):
    half = tab.shape[1] // 2
    bits = lax.bitcast_convert_type(tab.astype(BF16), jnp.uint16).astype(jnp.uint32)
    return lax.bitcast_convert_type(bits[:, :half] | (bits[:, half:] << 16), jnp.int32)


def _gather_rows(table, idx):
    n = idx.shape[0]
    c = table.shape[1]
    sc = plsc.get_sparse_core_info()
    workers = sc.num_cores * sc.num_subcores
    gw = GATHER_WINDOW
    per_worker = n // workers
    assert n % workers == 0 and per_worker % GATHER_INDEX_CHUNK == 0 and GATHER_INDEX_CHUNK % (2 * gw) == 0
    mesh = plsc.VectorSubcoreMesh(core_axis_name="c", subcore_axis_name="s")

    @functools.partial(
        pl.kernel, out_type=jax.ShapeDtypeStruct((n, c), table.dtype), mesh=mesh, name="row_gather",
        scratch_types=[pltpu.VMEM((GATHER_INDEX_CHUNK,), jnp.int32),
                       pltpu.VMEM((gw, c), table.dtype), pltpu.VMEM((gw, c), table.dtype),
                       pltpu.SemaphoreType.DMA, pltpu.SemaphoreType.DMA,
                       pltpu.SemaphoreType.DMA, pltpu.SemaphoreType.DMA])
    def gather(tab_hbm, idx_hbm, out_hbm, idx_v, rows0, rows1, gsem0, gsem1, wsem0, wsem1):
        wid = lax.axis_index("s") * sc.num_cores + lax.axis_index("c")
        base = wid * per_worker

        @pl.loop(0, per_worker // GATHER_INDEX_CHUNK)
        def _(ic):
            chunk_base = base + ic * GATHER_INDEX_CHUNK
            pltpu.sync_copy(idx_hbm.at[pl.ds(chunk_base, GATHER_INDEX_CHUNK)], idx_v)

            @pl.loop(0, GATHER_INDEX_CHUNK // (2 * gw))
            def _(j):
                off0 = j * (2 * gw)
                off1 = off0 + gw
                g0 = pltpu.async_copy(tab_hbm.at[idx_v.at[pl.ds(off0, gw)]], rows0, gsem0)
                g1 = pltpu.async_copy(tab_hbm.at[idx_v.at[pl.ds(off1, gw)]], rows1, gsem1)
                g0.wait()
                w0 = pltpu.async_copy(rows0, out_hbm.at[pl.ds(chunk_base + off0, gw)], wsem0)
                g1.wait()
                w1 = pltpu.async_copy(rows1, out_hbm.at[pl.ds(chunk_base + off1, gw)], wsem1)
                w0.wait()
                w1.wait()

    return gather(table, idx)


def _peer_tail(xm, hn, et, wt, u_tab, v_tab, gfin):
    t = xm.shape[0]
    tt = EXPERT_TOKENS
    nsel = et.shape[0]
    eidx = et.T.reshape(t * nsel)
    wc = wt.reshape(nsel, t // tt, tt).transpose(1, 0, 2)
    ug = _gather_rows(u_tab, eidx).reshape(t, nsel, u_tab.shape[1])
    vg = _gather_rows(v_tab, eidx).reshape(t, nsel, v_tab.shape[1])
    return _experts(xm, hn, wc, ug, vg, gfin)


def kernel(x_prompt, x_sample, cache_k_window, cache_v_window, state_gla, meta_tokens, g_norm_mix, w_in,
           w_gate_up, b_gate, attn_sinks, g_gla_norm, w_branch_a, w_branch_b, w_out, g_norm_ffn, w_peer_q,
           peer_sub_keys, peer_u, peer_v, g_norm_final):
    bsz, seq, d = x_prompt.shape
    dbsz, tdec, _ = x_sample.shape
    n_meta = meta_tokens.shape[0]
    depth = w_in.shape[0]
    window = cache_k_window.shape[2]
    kv_heads, head_dim = cache_k_window.shape[3], cache_k_window.shape[4]
    gate_rank = w_gate_up.shape[1]
    bqk = w_gate_up.shape[2]
    n_ph, _, n_keys, p_half = peer_sub_keys.shape[1:]
    assert depth == 1 and d == 1024 and window == ATTN_BLOCK and kv_heads == 2 and head_dim == 64
    assert bqk == 256 and state_gla.shape[2:] == (4, 64, 128) and n_meta <= ATTN_BLOCK
    assert seq % ATTN_BLOCK == 0 and tdec <= SAMPLE_PAD and n_keys == 128 and p_half == 64 and n_ph == 8
    rope_dim = head_dim // 4
    meta_pad = ATTN_BLOCK - n_meta
    lp = ATTN_BLOCK + seq
    nblk = lp // ATTN_BLOCK

    w = w_in[0]
    c_lr = 2304
    c_gate = c_lr + gate_rank
    w1 = w[:, :c_lr].astype(BF16)
    wlr = jnp.pad(w[:, c_lr:c_gate], ((0, 0), (0, LANES - gate_rank))).astype(BF16)
    w2 = w[:, c_gate:].astype(BF16)
    wgu = jnp.pad(w_gate_up[0], ((0, LANES - gate_rank), (0, 0))).astype(BF16)
    bg = b_gate[0][None, :]
    gmix = g_norm_mix[0][None, :]
    wa = w_branch_a[0].astype(BF16)
    wb = w_branch_b[0].astype(BF16)
    wo = w_out[0].astype(BF16)
    gffn = g_norm_ffn[0][None, :]
    wq = w_peer_q[0].astype(BF16)
    keys = peer_sub_keys[0].reshape(n_ph * 2, n_keys, p_half).astype(BF16)
    u_tab = _pack_table(peer_u[0])
    v_tab = _pack_table(peer_v[0])
    gfin = g_norm_final[None, :]
    gn = g_gla_norm[0][None, :]
    sinks = attn_sinks[0]
    qk_scale = float(bqk // 4) ** -0.5

    meta = jnp.broadcast_to(meta_tokens[None].astype(x_prompt.dtype), (bsz, n_meta, d))
    xpad = jnp.concatenate([jnp.zeros((bsz, meta_pad, d), x_prompt.dtype), meta, x_prompt], axis=1)
    xpad = xpad.reshape(bsz * lp, d)
    rows_p = jnp.arange(lp)
    tab_p = _rope_table(rows_p - meta_pad, rows_p >= meta_pad, rope_dim, head_dim)
    proj_rows = max(r for r in range(16, PROJ_ROWS + 1, 16) if lp % r == 0)
    qa, kv, gl, gt = _project(xpad, gmix, tab_p, w1, wlr, wgu, bg, w2, proj_rows, qk_scale)

    nq = nblk - 1
    ya = _attention(
        sinks, qa, kv, kv, bsz, nq, ATTN_BLOCK,
        lambda b, n: (b * nblk + n + 1, 0), lambda b, n: (b * nblk + n, 0), lambda b, n: (b * nblk + n + 1, 0),
        lambda b, n: (b * nq + n, 0), bsz * seq, first_valid_key=meta_pad, block_offset=1)

    nchunks = lp // GLA_CHUNK
    skip = ATTN_BLOCK // GLA_CHUNK
    ncq = nchunks - skip
    s0_p = jnp.zeros((bsz,) + state_gla.shape[2:], F32)
    yb, s_fin_p = _gla(gl, s0_p, gn, bsz, nchunks, GLA_CHUNK,
                       lambda b, c: (b * nchunks + c, 0),
                       lambda b, c: (b * ncq + jnp.maximum(c - skip, 0), 0), bsz * seq)

    xp_rows = x_prompt.reshape(bsz * seq, d)
    per_seq = seq // MERGE_ROWS
    gt_map_p = lambda i: ((i // per_seq) * nblk + 1 + (i % per_seq), 0)
    n_rows = bsz * seq
    chunk = n_rows // PROMPT_CHUNKS if n_rows % (PROMPT_CHUNKS * GATHER_ROW_QUANTUM) == 0 else n_rows
    y_chunks = []
    for c0 in range(0, n_rows, chunk):
        xm_c, hn_c, et_c, wt_c = _merge_route(xp_rows, gt, ya, yb, wa, wb, wo, gffn, wq, keys, gt_map_p,
                                              c0 // MERGE_ROWS, chunk)
        y_chunks.append(_peer_tail(xm_c, hn_c, et_c, wt_c, u_tab, v_tab, gfin))
    y_prompt = jnp.concatenate(y_chunks, axis=0).reshape(bsz, seq, d)

    kv_p = kv.reshape(bsz, lp, 2, kv_heads, head_dim)[:, lp - window:]
    new_k_p = kv_p[:, :, 0][None]
    new_v_p = kv_p[:, :, 1][None]

    sp = SAMPLE_PAD
    xs_pad = jnp.pad(x_sample, ((0, 0), (0, sp - tdec), (0, 0))).reshape(dbsz * sp, d)
    rows_s = jnp.arange(sp)
    reps = 256 // sp
    tab_s = jnp.tile(_rope_table(PAST_LEN + rows_s, rows_s < tdec, rope_dim, head_dim), (reps, 1))
    qa_s, kv_s, gl_s, gt_s = _project(xs_pad, gmix, tab_s, w1, wlr, wgu, bg, w2, 256, qk_scale)

    cache_kv = jnp.concatenate([cache_k_window[0].reshape(dbsz * window, kv_heads * head_dim),
                                cache_v_window[0].reshape(dbsz * window, kv_heads * head_dim)], axis=1)
    seq_map = lambda b, n: (b, 0)
    ya_s = _attention(sinks, qa_s, cache_kv, kv_s, dbsz, 1, sp, seq_map, seq_map, seq_map, seq_map,
                      dbsz * sp, first_valid_key=None, block_offset=0)
    yb_s, s_fin_s = _gla(gl_s, state_gla[0], gn, dbsz, 1, sp, seq_map, seq_map, dbsz * sp)

    def real_rows(a):
        return a.reshape(dbsz, sp, a.shape[-1])[:, :tdec].reshape(dbsz * tdec, a.shape[-1])

    xs_rows = x_sample.reshape(dbsz * tdec, d)
    xm_s, hn_s, et_s, wt_s = _merge_route(xs_rows, real_rows(gt_s), real_rows(ya_s), real_rows(yb_s),
                                          wa, wb, wo, gffn, wq, keys, lambda i: (i, 0), 0, dbsz * tdec)
    y_sample = _peer_tail(xm_s, hn_s, et_s, wt_s, u_tab, v_tab, gfin).reshape(dbsz, tdec, d)

    kv_new = real_rows(kv_s).reshape(dbsz, tdec, 2, kv_heads, head_dim)
    new_k_s = jnp.concatenate([cache_k_window[0].astype(F32), kv_new[:, :, 0]], axis=1)[:, -window:][None]
    new_v_s = jnp.concatenate([cache_v_window[0].astype(F32), kv_new[:, :, 1]], axis=1)[:, -window:][None]

    return (y_prompt, y_sample, new_k_p, new_v_p, s_fin_p[None], new_k_s, new_v_s, s_fin_s[None])
```

```python
import functools
import math

import jax
import jax.numpy as jnp
from jax import lax
from jax.experimental import pallas as pl
from jax.experimental.pallas import tpu as pltpu
from jax.experimental.pallas import tpu_sc as plsc

F32 = jnp.float32
BF16 = jnp.bfloat16

EPS = 1e-6
NEG_INF = -1e30
PAST_LEN = 16384
ROPE_THETA = 500000.0
GATE_NORMALIZER = 16.0
PEER_TOPK = 16

LANES = 128
SUBLANES = 8
VMEM_LIMIT_BYTES = 56 * 1024 * 1024

ATTN_BLOCK = 128
GLA_CHUNK = 64
SAMPLE_PAD = 16
PROJ_ROWS = 544
MERGE_ROWS = 128
EXPERT_TOKENS = 16
GATHER_WINDOW = 64
GATHER_INDEX_CHUNK = 2048
GATHER_ROW_QUANTUM = 512
PROMPT_CHUNKS = 4


def _cparams(sem):
    return pltpu.CompilerParams(dimension_semantics=sem, vmem_limit_bytes=VMEM_LIMIT_BYTES)


def _rms(x, g):
    ms = jnp.mean(x * x, axis=-1, keepdims=True)
    return (x * lax.rsqrt(ms + EPS)) * g


def _proj_kernel(x_ref, g_ref, tab_ref, w1_ref, wlr_ref, wgu_ref, bg_ref, w2_ref,
                 qa_ref, kv_ref, gl_ref, gt_ref, *, period, qk_scale):
    i = pl.program_id(0)
    tr = x_ref.shape[0]
    hb = _rms(x_ref[...], g_ref[...]).astype(BF16)
    z1 = jnp.dot(hb, w1_ref[...], preferred_element_type=F32)

    start = pl.multiple_of((i * tr) % period, SUBLANES)
    tab = tab_ref[pl.ds(start, tr), :]
    cosf = tab[:, 0:LANES]
    sin_lo = tab[:, LANES:2 * LANES]
    sin_hi = tab[:, 2 * LANES:3 * LANES]
    valid = tab[:, 3 * LANES:3 * LANES + 1]

    def rope(xg):
        return xg * cosf + pltpu.roll(xg, 8, 1) * sin_lo + pltpu.roll(xg, LANES - 8, 1) * sin_hi

    for gi in range(4):
        sl = slice(gi * LANES, (gi + 1) * LANES)
        qa_ref[:, sl] = rope(z1[:, sl]).astype(BF16)
    kv_ref[:, 0:LANES] = rope(z1[:, 512:640])
    kv_ref[:, LANES:2 * LANES] = z1[:, 640:768]

    lr = jnp.dot(hb, wlr_ref[...], preferred_element_type=F32)
    pre = jnp.dot(lr.astype(BF16), wgu_ref[...], preferred_element_type=F32) + bg_ref[...]
    log_sig = jnp.minimum(pre, 0.0) - jnp.log1p(jnp.exp(-jnp.abs(pre)))
    ld = jnp.where(valid > 0.5, log_sig / GATE_NORMALIZER, 0.0)

    gl_ref[:, 0:256] = z1[:, 768:1024] * qk_scale
    gl_ref[:, 256:512] = z1[:, 1024:1280]
    gl_ref[:, 512:768] = ld
    gl_ref[:, 768:1792] = z1[:, 1280:2304]
    gt_ref[...] = jnp.dot(hb, w2_ref[...], preferred_element_type=F32)


def _project(x, g, tab, w1, wlr, wgu, bg, w2, rows, qk_scale):
    r, d = x.shape
    period = tab.shape[0]
    const = lambda i: (0, 0)
    row = lambda i: (i, 0)
    return pl.pallas_call(
        functools.partial(_proj_kernel, period=period, qk_scale=qk_scale),
        grid=(r // rows,),
        in_specs=[
            pl.BlockSpec((rows, d), row),
            pl.BlockSpec(g.shape, const),
            pl.BlockSpec(tab.shape, const),
            pl.BlockSpec(w1.shape, const),
            pl.BlockSpec(wlr.shape, const),
            pl.BlockSpec(wgu.shape, const),
            pl.BlockSpec(bg.shape, const),
            pl.BlockSpec(w2.shape, const),
        ],
        out_specs=[
            pl.BlockSpec((rows, 512), row),
            pl.BlockSpec((rows, 256), row),
            pl.BlockSpec((rows, 1792), row),
            pl.BlockSpec((rows, 2048), row),
        ],
        out_shape=[
            jax.ShapeDtypeStruct((r, 512), BF16),
            jax.ShapeDtypeStruct((r, 256), F32),
            jax.ShapeDtypeStruct((r, 1792), F32),
            jax.ShapeDtypeStruct((r, 2048), F32),
        ],
        compiler_params=_cparams(("arbitrary",)),
        name="proj",
    )(x, g, tab, w1, wlr, wgu, bg, w2)


def _attn_kernel(sink_ref, q_ref, prev_ref, cur_ref, o_ref, *, first_valid_key, block_offset):
    n = pl.program_id(1)
    qr = q_ref.shape[0]
    kr = cur_ref.shape[0]
    w = prev_ref.shape[0]
    nk = w + kr
    group = 4
    hd = 64

    rows = lax.broadcasted_iota(jnp.int32, (group * qr, nk), 0)
    cols = lax.broadcasted_iota(jnp.int32, (group * qr, nk), 1)
    head_of_row = rows // qr
    diff = (rows - head_of_row * qr) - cols + w
    mask = (diff >= 0) & (diff <= w)
    if first_valid_key is not None:
        blk = n + block_offset
        mask = mask & (cols >= first_valid_key + w - blk * w)

    prev = prev_ref[...]
    cur = cur_ref[...]
    q = q_ref[...]
    row_head = lax.broadcasted_iota(jnp.int32, (group * qr, 1), 0) // qr
    for kh in range(2):
        k = jnp.concatenate([prev[:, kh * hd:(kh + 1) * hd], cur[:, kh * hd:(kh + 1) * hd]], axis=0).astype(BF16)
        v = jnp.concatenate([prev[:, LANES + kh * hd:LANES + (kh + 1) * hd],
                             cur[:, LANES + kh * hd:LANES + (kh + 1) * hd]], axis=0).astype(BF16)
        qs = jnp.concatenate([q[:, (group * kh + g) * hd:(group * kh + g + 1) * hd] for g in range(group)], axis=0)
        s = lax.dot_general(qs, k, (((1,), (1,)), ((), ())), preferred_element_type=F32) * (hd ** -0.5)
        s = jnp.where(mask, s, NEG_INF)
        sink = jnp.zeros((group * qr, 1), F32)
        for g in range(group):
            sink = jnp.where(row_head == g, sink_ref[group * kh + g], sink)
        m = jnp.maximum(jnp.max(s, axis=-1, keepdims=True), sink)
        e = jnp.exp(s - m)
        p = e / (jnp.sum(e, axis=-1, keepdims=True) + jnp.exp(sink - m))
        o = jnp.dot(p.astype(BF16), v, preferred_element_type=F32)
        for g in range(group):
            h = group * kh + g
            o_ref[:, h * hd:(h + 1) * hd] = o[g * qr:(g + 1) * qr].astype(BF16)


def _attention(sinks, q, kv_prev, kv_cur, nb, nblk, qr, q_map, prev_map, cur_map, out_map, out_rows,
               first_valid_key, block_offset):
    w = ATTN_BLOCK
    return pl.pallas_call(
        functools.partial(_attn_kernel, first_valid_key=first_valid_key, block_offset=block_offset),
        grid=(nb, nblk),
        in_specs=[
            pl.BlockSpec(memory_space=pltpu.SMEM),
            pl.BlockSpec((qr, 512), q_map),
            pl.BlockSpec((w, 256), prev_map),
            pl.BlockSpec((qr, 256), cur_map),
        ],
        out_specs=pl.BlockSpec((qr, 512), out_map),
        out_shape=jax.ShapeDtypeStruct((out_rows, 512), BF16),
        compiler_params=_cparams(("arbitrary", "arbitrary")),
        name="swa",
    )(sinks, q, kv_prev, kv_cur)


def _gla_kernel(gl_ref, s0_ref, gn_ref, yb_ref, sfin_ref, st_ref):
    c = pl.program_id(1)
    ch = gl_ref.shape[0]
    nh, dk, dv = 4, 64, 128

    @pl.when(c == 0)
    def _():
        for h in range(nh):
            st_ref[h] = s0_ref[0, h].T

    gl = gl_ref[...]
    q = gl[:, 0:256]
    k = gl[:, 256:512]
    b = gl[:, 512:768]
    row = lax.broadcasted_iota(jnp.int32, (ch, nh * dk), 0)
    sh = 1
    while sh < ch:
        b = b + jnp.where(row >= sh, pltpu.roll(b, sh, 0), 0.0)
        sh *= 2
    b_last = b[ch - 1:ch, :]
    q_t = (q * jnp.exp(b)).astype(BF16)
    k_t = (k * jnp.exp(-b)).astype(BF16)
    k_end = (k * jnp.exp(b_last - b)).astype(BF16)
    decay = jnp.exp(b_last)
    causal = (lax.broadcasted_iota(jnp.int32, (ch, ch), 0) >= lax.broadcasted_iota(jnp.int32, (ch, ch), 1))
    gn = gn_ref[...]
    nt = (((1,), (1,)), ((), ()))
    for h in range(nh):
        ks = slice(h * dk, (h + 1) * dk)
        v = gl[:, 768 + h * dv:768 + (h + 1) * dv]
        vb = v.astype(BF16)
        a = lax.dot_general(q_t[:, ks], k_t[:, ks], nt, preferred_element_type=F32)
        a = jnp.where(causal, a, 0.0)
        s_t = st_ref[h]
        o = jnp.dot(a.astype(BF16), vb, preferred_element_type=F32)
        o = o + lax.dot_general(q_t[:, ks], s_t.astype(BF16), nt, preferred_element_type=F32)
        upd = jnp.dot(v.T.astype(BF16), k_end[:, ks], preferred_element_type=F32)
        st_ref[h] = s_t * decay[:, ks] + upd
        go = gl[:, 1280 + h * dv:1280 + (h + 1) * dv]
        y = _rms(o, gn) * (go * jax.nn.sigmoid(go))
        yb_ref[:, h * dv:(h + 1) * dv] = y.astype(BF16)

    @pl.when(c == pl.num_programs(1) - 1)
    def _():
        for h in range(nh):
            sfin_ref[0, h] = st_ref[h].T


def _gla(gl, s0, gn, nb, nchunks, ch, in_map, out_map, out_rows):
    return pl.pallas_call(
        _gla_kernel,
        grid=(nb, nchunks),
        in_specs=[
            pl.BlockSpec((ch, 1792), in_map),
            pl.BlockSpec((1, 4, 64, 128), lambda b, c: (b, 0, 0, 0)),
            pl.BlockSpec((1, 128), lambda b, c: (0, 0)),
        ],
        out_specs=[
            pl.BlockSpec((ch, 512), out_map),
            pl.BlockSpec((1, 4, 64, 128), lambda b, c: (b, 0, 0, 0)),
        ],
        out_shape=[
            jax.ShapeDtypeStruct((out_rows, 512), BF16),
            jax.ShapeDtypeStruct((nb, 4, 64, 128), F32),
        ],
        scratch_shapes=[pltpu.VMEM((4, 128, 64), F32)],
        compiler_params=_cparams(("arbitrary", "arbitrary")),
        name="gla",
    )(gl, s0, gn)


def _extract_topk(work, nsel, iota0, sentinel):
    vals, idxs = [], []
    for j in range(nsel):
        m = jnp.max(work, axis=0, keepdims=True)
        idx = jnp.min(jnp.where(work == m, iota0, sentinel), axis=0, keepdims=True)
        vals.append(m)
        idxs.append(idx)
        if j + 1 < nsel:
            work = jnp.where(iota0 == idx, -jnp.inf, work)
    return vals, idxs


def _merge_kernel(x_ref, gt_ref, ya_ref, yb_ref, wa_ref, wb_ref, wo_ref, gf_ref, wq_ref, keys_ref,
                  xm_ref, hn_ref, et_ref, wt_ref):
    td = x_ref.shape[0]
    nkeys = keys_ref.shape[1]
    half = keys_ref.shape[2]
    nheads = keys_ref.shape[0] // 2
    topk = PEER_TOPK

    gt = gt_ref[...]
    d = x_ref.shape[1]
    ma = jnp.dot(ya_ref[...], wa_ref[...], preferred_element_type=F32)
    mb = jnp.dot(yb_ref[...], wb_ref[...], preferred_element_type=F32)
    m = jax.nn.sigmoid(gt[:, 0:d]) * ma + jax.nn.sigmoid(gt[:, d:2 * d]) * mb
    xm = x_ref[...] + jnp.dot(m.astype(BF16), wo_ref[...], preferred_element_type=F32)
    xm_ref[...] = xm
    hn = _rms(xm, gf_ref[...])
    hn_ref[...] = hn
    q = jnp.dot(hn.astype(BF16), wq_ref[...], preferred_element_type=F32).astype(BF16)

    nt = (((1,), (1,)), ((), ()))
    iota_k = lax.broadcasted_iota(jnp.int32, (nkeys, td), 0)
    iota_c = lax.broadcasted_iota(jnp.int32, (topk * topk, td), 0)
    for h in range(nheads):
        sv, si = [], []
        for c in range(2):
            gi = 2 * h + c
            s_t = lax.dot_general(keys_ref[gi], q[:, gi * half:(gi + 1) * half], nt,
                                  preferred_element_type=F32)
            vals, idxs = _extract_topk(s_t, topk, iota_k, nkeys)
            sv.append(vals)
            si.append(idxs)
        sv1 = jnp.concatenate(sv[1], axis=0)
        si1 = jnp.concatenate(si[1], axis=0)
        cand = jnp.concatenate([sv[0][a] + sv1 for a in range(topk)], axis=0)
        cidx = jnp.concatenate([si[0][a] * nkeys + si1 for a in range(topk)], axis=0)
        fvals, eids = [], []
        work = cand
        for j in range(topk):
            mx = jnp.max(work, axis=0, keepdims=True)
            pos = jnp.min(jnp.where(work == mx, iota_c, topk * topk), axis=0, keepdims=True)
            hit = iota_c == pos
            eids.append(jnp.max(jnp.where(hit, cidx, -1), axis=0, keepdims=True))
            fvals.append(mx)
            if j + 1 < topk:
                work = jnp.where(hit, -jnp.inf, work)
        fv = jnp.concatenate(fvals, axis=0)
        e = jnp.exp(fv - fvals[0])
        wt_ref[h * topk:(h + 1) * topk, :] = e / jnp.sum(e, axis=0, keepdims=True)
        et_ref[h * topk:(h + 1) * topk, :] = jnp.concatenate(eids, axis=0)


def _merge_route(x, gt, ya, yb, wa, wb, wo, gf, wq, keys, gt_map, first_block, t):
    d = x.shape[1]
    td = MERGE_ROWS
    nsel = (keys.shape[0] // 2) * PEER_TOPK
    const2 = lambda i: (0, 0)
    row = lambda i: (i, 0)
    col = lambda i: (0, i)
    row_in = lambda i: (i + first_block, 0)
    return pl.pallas_call(
        _merge_kernel,
        grid=(t // td,),
        in_specs=[
            pl.BlockSpec((td, d), row_in),
            pl.BlockSpec((td, 2 * d), lambda i: gt_map(i + first_block)),
            pl.BlockSpec((td, ya.shape[1]), row_in),
            pl.BlockSpec((td, yb.shape[1]), row_in),
            pl.BlockSpec(wa.shape, const2),
            pl.BlockSpec(wb.shape, const2),
            pl.BlockSpec(wo.shape, const2),
            pl.BlockSpec(gf.shape, const2),
            pl.BlockSpec(wq.shape, const2),
            pl.BlockSpec(keys.shape, lambda i: (0, 0, 0)),
        ],
        out_specs=[
            pl.BlockSpec((td, d), row),
            pl.BlockSpec((td, d), row),
            pl.BlockSpec((nsel, td), col),
            pl.BlockSpec((nsel, td), col),
        ],
        out_shape=[
            jax.ShapeDtypeStruct((t, d), F32),
            jax.ShapeDtypeStruct((t, d), F32),
            jax.ShapeDtypeStruct((nsel, t), jnp.int32),
            jax.ShapeDtypeStruct((nsel, t), F32),
        ],
        compiler_params=_cparams(("arbitrary",)),
        name="merge_route",
    )(x, gt, ya, yb, wa, wb, wo, gf, wq, keys)


def _unpack_pair(w):
    lo = pltpu.bitcast(w << 16, F32)
    hi = pltpu.bitcast(w & jnp.int32(-65536), F32)
    return lo, hi


def _expert_kernel(xm_ref, wc_ref, ac_ref, vg_ref, gfin_ref, y_ref):
    tt = xm_ref.shape[0]
    act = ac_ref[0]
    gelu = 0.5 * act * (1.0 + lax.erf(act * (2.0 ** -0.5)))
    coef_all = wc_ref[0] * gelu
    rows = []
    for t in range(tt):
        coef = coef_all[:, t:t + 1]
        v_lo, v_hi = _unpack_pair(vg_ref[t])
        rows.append(jnp.concatenate([jnp.sum(v_lo * coef, axis=0, keepdims=True),
                                     jnp.sum(v_hi * coef, axis=0, keepdims=True)], axis=1))
    out = xm_ref[...] + jnp.concatenate(rows, axis=0)
    y_ref[...] = _rms(out, gfin_ref[...])


def _experts(xm, wc, ac, vg, gfin):
    t, d = xm.shape
    tt = EXPERT_TOKENS
    nsel = vg.shape[1]
    row = lambda i: (i, 0)
    row3 = lambda i: (i, 0, 0)
    return pl.pallas_call(
        _expert_kernel,
        grid=(t // tt,),
        in_specs=[
            pl.BlockSpec((tt, d), row),
            pl.BlockSpec((1, nsel, tt), row3),
            pl.BlockSpec((1, nsel, tt), row3),
            pl.BlockSpec((tt, nsel, d // 2), row3),
            pl.BlockSpec(gfin.shape, lambda i: (0, 0)),
        ],
        out_specs=pl.BlockSpec((tt, d), row),
        out_shape=jax.ShapeDtypeStruct((t, d), F32),
        compiler_params=_cparams(("arbitrary",)),
        name="experts",
    )(xm, wc, ac, vg, gfin)


def _rope_table(pos, valid, rope_dim, head_dim):
    half = rope_dim // 2
    inv = ROPE_THETA ** (-jnp.arange(0, rope_dim, 2, dtype=F32) / rope_dim)
    ang = pos.astype(F32)[:, None] * inv[None, :]
    cos, sin = jnp.cos(ang), jnp.sin(ang)
    n = pos.shape[0]
    ones = jnp.ones((n, head_dim - rope_dim), F32)
    zeros_h = jnp.zeros((n, half), F32)
    zeros_r = jnp.zeros((n, head_dim - rope_dim), F32)
    reps = LANES // head_dim
    cosf = jnp.tile(jnp.concatenate([cos, cos, ones], axis=1), (1, reps))
    sin_lo = jnp.tile(jnp.concatenate([zeros_h, sin, zeros_r], axis=1), (1, reps))
    sin_hi = jnp.tile(jnp.concatenate([-sin, zeros_h, zeros_r], axis=1), (1, reps))
    vcol = jnp.broadcast_to(valid.astype(F32)[:, None], (n, LANES))
    return jnp.concatenate([cosf, sin_lo, sin_hi, vcol], axis=1)


def _pack_table(tab):
    half = tab.shape[1] // 2
    bits = lax.bitcast_convert_type(tab.astype(BF16), jnp.uint16).astype(jnp.uint32)
    return lax.bitcast_convert_type(bits[:, :half] | (bits[:, half:] << 16), jnp.int32)


def _gather_rows(table, idx):
    n = idx.shape[0]
    c = table.shape[1]
    sc = plsc.get_sparse_core_info()
    workers = sc.num_cores * sc.num_subcores
    gw = GATHER_WINDOW
    per_worker = n // workers
    assert n % workers == 0 and per_worker % GATHER_INDEX_CHUNK == 0 and GATHER_INDEX_CHUNK % (2 * gw) == 0
    mesh = plsc.VectorSubcoreMesh(core_axis_name="c", subcore_axis_name="s")

    @functools.partial(
        pl.kernel, out_type=jax.ShapeDtypeStruct((n, c), table.dtype), mesh=mesh, name="row_gather",
        scratch_types=[pltpu.VMEM((GATHER_INDEX_CHUNK,), jnp.int32),
                       pltpu.VMEM((gw, c), table.dtype), pltpu.VMEM((gw, c), table.dtype),
                       pltpu.SemaphoreType.DMA, pltpu.SemaphoreType.DMA,
                       pltpu.SemaphoreType.DMA, pltpu.SemaphoreType.DMA])
    def gather(tab_hbm, idx_hbm, out_hbm, idx_v, rows0, rows1, gsem0, gsem1, wsem0, wsem1):
        wid = lax.axis_index("s") * sc.num_cores + lax.axis_index("c")
        base = wid * per_worker

        @pl.loop(0, per_worker // GATHER_INDEX_CHUNK)
        def _(ic):
            chunk_base = base + ic * GATHER_INDEX_CHUNK
            pltpu.sync_copy(idx_hbm.at[pl.ds(chunk_base, GATHER_INDEX_CHUNK)], idx_v)

            @pl.loop(0, GATHER_INDEX_CHUNK // (2 * gw))
            def _(j):
                off0 = j * (2 * gw)
                off1 = off0 + gw
                g0 = pltpu.async_copy(tab_hbm.at[idx_v.at[pl.ds(off0, gw)]], rows0, gsem0)
                g1 = pltpu.async_copy(tab_hbm.at[idx_v.at[pl.ds(off1, gw)]], rows1, gsem1)
                g0.wait()
                w0 = pltpu.async_copy(rows0, out_hbm.at[pl.ds(chunk_base + off0, gw)], wsem0)
                g1.wait()
                w1 = pltpu.async_copy(rows1, out_hbm.at[pl.ds(chunk_base + off1, gw)], wsem1)
                w0.wait()
                w1.wait()

    return gather(table, idx)


def _expert_dots(table, idx, hn, nsel):
    n = idx.shape[0]
    c = table.shape[1]
    d = hn.shape[1]
    sc = plsc.get_sparse_core_info()
    lanes = sc.num_lanes
    workers = sc.num_cores * sc.num_subcores
    gw = GATHER_WINDOW
    ich = GATHER_INDEX_CHUNK
    tok = ich // nsel
    per_worker = n // workers
    assert n % workers == 0 and per_worker % ich == 0 and nsel == 2 * gw and d == 2 * c and gw % lanes == 0
    mesh = plsc.VectorSubcoreMesh(core_axis_name="c", subcore_axis_name="s")

    @functools.partial(
        pl.kernel, out_type=jax.ShapeDtypeStruct((n,), F32), mesh=mesh, name="expert_dots",
        compiler_params=pltpu.CompilerParams(needs_layout_passes=False),
        scratch_types=[pltpu.VMEM((ich,), jnp.int32),
                       pltpu.VMEM((gw, c), jnp.int32), pltpu.VMEM((gw, c), jnp.int32),
                       pltpu.VMEM((tok, d), F32), pltpu.VMEM((ich,), F32), pltpu.VMEM((lanes * lanes,), F32),
                       pltpu.SemaphoreType.DMA, pltpu.SemaphoreType.DMA])
    def dots(tab_hbm, idx_hbm, hn_hbm, act_hbm, idx_v, rows0, rows1, h_v, act_v, scr, sem0, sem1):
        wid = lax.axis_index("s") * sc.num_cores + lax.axis_index("c")
        base = wid * per_worker
        lane = lax.iota(jnp.int32, lanes)

        def gather(win, buf, sem):
            return pltpu.make_async_copy(tab_hbm.at[idx_v.at[pl.ds(win * gw, gw)]], buf, sem)

        def reduce_window(buf, t_loc, out_off):
            for rb in range(gw // lanes):
                def kbody(k, accs):
                    h_lo = h_v[t_loc, pl.ds(k * lanes, lanes)]
                    h_hi = h_v[t_loc, pl.ds(c + k * lanes, lanes)]
                    out = []
                    for r in range(lanes):
                        w = buf[rb * lanes + r, pl.ds(k * lanes, lanes)]
                        lo = lax.bitcast_convert_type(w << 16, F32)
                        hi = lax.bitcast_convert_type(w & jnp.int32(-65536), F32)
                        out.append(accs[r] + lo * h_lo + hi * h_hi)
                    return tuple(out)

                accs = lax.fori_loop(0, c // lanes, kbody,
                                     tuple(jnp.zeros((lanes,), F32) for _ in range(lanes)))
                for r in range(lanes):
                    scr[pl.ds(r * lanes, lanes)] = accs[r]
                tot = plsc.load_gather(scr, [lane * lanes])
                for l in range(1, lanes):
                    tot = tot + plsc.load_gather(scr, [lane * lanes + l])
                act_v[pl.ds(out_off + rb * lanes, lanes)] = tot

        @pl.loop(0, per_worker // ich)
        def _(g):
            cb = base + g * ich
            pltpu.sync_copy(idx_hbm.at[pl.ds(cb, ich)], idx_v)
            tok_base = pl.multiple_of(wid * (per_worker // nsel) + g * tok, tok)
            pltpu.sync_copy(hn_hbm.at[pl.ds(tok_base, tok)], h_v)
            gather(0, rows0, sem0).start()
            gather(1, rows1, sem1).start()

            @pl.loop(0, tok)
            def _(j):
                gather(2 * j, rows0, sem0).wait()
                reduce_window(rows0, j, j * nsel)

                @pl.when(j + 1 < tok)
                def _():
                    gather(2 * j + 2, rows0, sem0).start()

                gather(2 * j + 1, rows1, sem1).wait()
                reduce_window(rows1, j, j * nsel + gw)

                @pl.when(j + 1 < tok)
                def _():
                    gather(2 * j + 3, rows1, sem1).start()

            pltpu.sync_copy(act_v, act_hbm.at[pl.ds(cb, ich)])

    return dots(table, idx, hn)


def _peer_tail(xm, hn, et, wt, u_tab, v_tab, gfin):
    t = xm.shape[0]
    tt = EXPERT_TOKENS
    nsel = et.shape[0]
    eidx = et.T.reshape(t * nsel)
    wc = wt.reshape(nsel, t // tt, tt).transpose(1, 0, 2)
    act = _expert_dots(u_tab, eidx, hn, nsel)
    ac = act.reshape(t // tt, tt, nsel).transpose(0, 2, 1)
    vg = _gather_rows(v_tab, eidx).reshape(t, nsel, v_tab.shape[1])
    return _experts(xm, wc, ac, vg, gfin)


def kernel(x_prompt, x_sample, cache_k_window, cache_v_window, state_gla, meta_tokens, g_norm_mix, w_in,
           w_gate_up, b_gate, attn_sinks, g_gla_norm, w_branch_a, w_branch_b, w_out, g_norm_ffn, w_peer_q,
           peer_sub_keys, peer_u, peer_v, g_norm_final):
    bsz, seq, d = x_prompt.shape
    dbsz, tdec, _ = x_sample.shape
    n_meta = meta_tokens.shape[0]
    depth = w_in.shape[0]
    window = cache_k_window.shape[2]
    kv_heads, head_dim = cache_k_window.shape[3], cache_k_window.shape[4]
    gate_rank = w_gate_up.shape[1]
    bqk = w_gate_up.shape[2]
    n_ph, _, n_keys, p_half = peer_sub_keys.shape[1:]
    assert depth == 1 and d == 1024 and window == ATTN_BLOCK and kv_heads == 2 and head_dim == 64
    assert bqk == 256 and state_gla.shape[2:] == (4, 64, 128) and n_meta <= ATTN_BLOCK
    assert seq % ATTN_BLOCK == 0 and tdec <= SAMPLE_PAD and n_keys == 128 and p_half == 64 and n_ph == 8
    rope_dim = head_dim // 4
    meta_pad = ATTN_BLOCK - n_meta
    lp = ATTN_BLOCK + seq
    nblk = lp // ATTN_BLOCK

    w = w_in[0]
    c_lr = 2304
    c_gate = c_lr + gate_rank
    w1 = w[:, :c_lr].astype(BF16)
    wlr = jnp.pad(w[:, c_lr:c_gate], ((0, 0), (0, LANES - gate_rank))).astype(BF16)
    w2 = w[:, c_gate:].astype(BF16)
    wgu = jnp.pad(w_gate_up[0], ((0, LANES - gate_rank), (0, 0))).astype(BF16)
    bg = b_gate[0][None, :]
    gmix = g_norm_mix[0][None, :]
    wa = w_branch_a[0].astype(BF16)
    wb = w_branch_b[0].astype(BF16)
    wo = w_out[0].astype(BF16)
    gffn = g_norm_ffn[0][None, :]
    wq = w_peer_q[0].astype(BF16)
    keys = peer_sub_keys[0].reshape(n_ph * 2, n_keys, p_half).astype(BF16)
    u_tab = _pack_table(peer_u[0])
    v_tab = _pack_table(peer_v[0])
    gfin = g_norm_final[None, :]
    gn = g_gla_norm[0][None, :]
    sinks = attn_sinks[0]
    qk_scale = float(bqk // 4) ** -0.5

    meta = jnp.broadcast_to(meta_tokens[None].astype(x_prompt.dtype), (bsz, n_meta, d))
    xpad = jnp.concatenate([jnp.zeros((bsz, meta_pad, d), x_prompt.dtype), meta, x_prompt], axis=1)
    xpad = xpad.reshape(bsz * lp, d)
    rows_p = jnp.arange(lp)
    tab_p = _rope_table(rows_p - meta_pad, rows_p >= meta_pad, rope_dim, head_dim)
    proj_rows = max(r for r in range(16, PROJ_ROWS + 1, 16) if lp % r == 0)
    qa, kv, gl, gt = _project(xpad, gmix, tab_p, w1, wlr, wgu, bg, w2, proj_rows, qk_scale)

    nq = nblk - 1
    ya = _attention(
        sinks, qa, kv, kv, bsz, nq, ATTN_BLOCK,
        lambda b, n: (b * nblk + n + 1, 0), lambda b, n: (b * nblk + n, 0), lambda b, n: (b * nblk + n + 1, 0),
        lambda b, n: (b * nq + n, 0), bsz * seq, first_valid_key=meta_pad, block_offset=1)

    nchunks = lp // GLA_CHUNK
    skip = ATTN_BLOCK // GLA_CHUNK
    ncq = nchunks - skip
    s0_p = jnp.zeros((bsz,) + state_gla.shape[2:], F32)
    yb, s_fin_p = _gla(gl, s0_p, gn, bsz, nchunks, GLA_CHUNK,
                       lambda b, c: (b * nchunks + c, 0),
                       lambda b, c: (b * ncq + jnp.maximum(c - skip, 0), 0), bsz * seq)

    xp_rows = x_prompt.reshape(bsz * seq, d)
    per_seq = seq // MERGE_ROWS
    gt_map_p = lambda i: ((i // per_seq) * nblk + 1 + (i % per_seq), 0)
    n_rows = bsz * seq
    chunk = n_rows // PROMPT_CHUNKS if n_rows % (PROMPT_CHUNKS * GATHER_ROW_QUANTUM) == 0 else n_rows
    y_chunks = []
    for c0 in range(0, n_rows, chunk):
        xm_c, hn_c, et_c, wt_c = _merge_route(xp_rows, gt, ya, yb, wa, wb, wo, gffn, wq, keys, gt_map_p,
                                              c0 // MERGE_ROWS, chunk)
        y_chunks.append(_peer_tail(xm_c, hn_c, et_c, wt_c, u_tab, v_tab, gfin))
    y_prompt = jnp.concatenate(y_chunks, axis=0).reshape(bsz, seq, d)

    kv_p = kv.reshape(bsz, lp, 2, kv_heads, head_dim)[:, lp - window:]
    new_k_p = kv_p[:, :, 0][None]
    new_v_p = kv_p[:, :, 1][None]

    sp = SAMPLE_PAD
    xs_pad = jnp.pad(x_sample, ((0, 0), (0, sp - tdec), (0, 0))).reshape(dbsz * sp, d)
    rows_s = jnp.arange(sp)
    reps = 256 // sp
    tab_s = jnp.tile(_rope_table(PAST_LEN + rows_s, rows_s < tdec, rope_dim, head_dim), (reps, 1))
    qa_s, kv_s, gl_s, gt_s = _project(xs_pad, gmix, tab_s, w1, wlr, wgu, bg, w2, 256, qk_scale)

    cache_kv = jnp.concatenate([cache_k_window[0].reshape(dbsz * window, kv_heads * head_dim),
                                cache_v_window[0].reshape(dbsz * window, kv_heads * head_dim)], axis=1)
    seq_map = lambda b, n: (b, 0)
    ya_s = _attention(sinks, qa_s, cache_kv, kv_s, dbsz, 1, sp, seq_map, seq_map, seq_map, seq_map,
                      dbsz * sp, first_valid_key=None, block_offset=0)
    yb_s, s_fin_s = _gla(gl_s, state_gla[0], gn, dbsz, 1, sp, seq_map, seq_map, dbsz * sp)

    def real_rows(a):
        return a.reshape(dbsz, sp, a.shape[-1])[:, :tdec].reshape(dbsz * tdec, a.shape[-1])

    xs_rows = x_sample.reshape(dbsz * tdec, d)
    xm_s, hn_s, et_s, wt_s = _merge_route(xs_rows, real_rows(gt_s), real_rows(ya_s), real_rows(yb_s),
                                          wa, wb, wo, gffn, wq, keys, lambda i: (i, 0), 0, dbsz * tdec)
    y_sample = _peer_tail(xm_s, hn_s, et_s, wt_s, u_tab, v_tab, gfin).reshape(dbsz, tdec, d)

    kv_new = real_rows(kv_s).reshape(dbsz, tdec, 2, kv_heads, head_dim)
    new_k_s = jnp.concatenate([cache_k_window[0].astype(F32), kv_new[:, :, 0]], axis=1)[:, -window:][None]
    new_v_s = jnp.concatenate([cache_v_window[0].astype(F32), kv_new[:, :, 1]], axis=1)[:, -window:][None]

    return (y_prompt, y_sample, new_k_p, new_v_p, s_fin_p[None], new_k_s, new_v_s, s_fin_s[None])
```

```python
import functools
import math

import jax
import jax.numpy as jnp
from jax import lax
from jax.experimental import pallas as pl
from jax.experimental.pallas import tpu as pltpu
from jax.experimental.pallas import tpu_sc as plsc

F32 = jnp.float32
BF16 = jnp.bfloat16

EPS = 1e-6
NEG_INF = -1e30
PAST_LEN = 16384
ROPE_THETA = 500000.0
GATE_NORMALIZER = 16.0
PEER_TOPK = 16

LANES = 128
SUBLANES = 8
VMEM_LIMIT_BYTES = 56 * 1024 * 1024

ATTN_BLOCK = 128
GLA_CHUNK = 64
SAMPLE_PAD = 16
PROJ_ROWS = 544
MERGE_ROWS = 128
GATHER_WINDOW = 64
GATHER_INDEX_CHUNK = 2048
GATHER_ROW_QUANTUM = 512
PROMPT_CHUNKS = 4


def _cparams(sem):
    return pltpu.CompilerParams(dimension_semantics=sem, vmem_limit_bytes=VMEM_LIMIT_BYTES)


def _rms(x, g):
    ms = jnp.mean(x * x, axis=-1, keepdims=True)
    return (x * lax.rsqrt(ms + EPS)) * g


def _proj_kernel(x_ref, g_ref, tab_ref, w1_ref, wlr_ref, wgu_ref, bg_ref, w2_ref,
                 qa_ref, kv_ref, gl_ref, gt_ref, *, period, qk_scale):
    i = pl.program_id(0)
    tr = x_ref.shape[0]
    hb = _rms(x_ref[...], g_ref[...]).astype(BF16)
    z1 = jnp.dot(hb, w1_ref[...], preferred_element_type=F32)

    start = pl.multiple_of((i * tr) % period, SUBLANES)
    tab = tab_ref[pl.ds(start, tr), :]
    cosf = tab[:, 0:LANES]
    sin_lo = tab[:, LANES:2 * LANES]
    sin_hi = tab[:, 2 * LANES:3 * LANES]
    valid = tab[:, 3 * LANES:3 * LANES + 1]

    def rope(xg):
        return xg * cosf + pltpu.roll(xg, 8, 1) * sin_lo + pltpu.roll(xg, LANES - 8, 1) * sin_hi

    for gi in range(4):
        sl = slice(gi * LANES, (gi + 1) * LANES)
        qa_ref[:, sl] = rope(z1[:, sl]).astype(BF16)
    kv_ref[:, 0:LANES] = rope(z1[:, 512:640])
    kv_ref[:, LANES:2 * LANES] = z1[:, 640:768]

    lr = jnp.dot(hb, wlr_ref[...], preferred_element_type=F32)
    pre = jnp.dot(lr.astype(BF16), wgu_ref[...], preferred_element_type=F32) + bg_ref[...]
    log_sig = jnp.minimum(pre, 0.0) - jnp.log1p(jnp.exp(-jnp.abs(pre)))
    ld = jnp.where(valid > 0.5, log_sig / GATE_NORMALIZER, 0.0)

    gl_ref[:, 0:256] = z1[:, 768:1024] * qk_scale
    gl_ref[:, 256:512] = z1[:, 1024:1280]
    gl_ref[:, 512:768] = ld
    gl_ref[:, 768:1792] = z1[:, 1280:2304]
    gt_ref[...] = jnp.dot(hb, w2_ref[...], preferred_element_type=F32)


def _project(x, g, tab, w1, wlr, wgu, bg, w2, rows, qk_scale):
    r, d = x.shape
    period = tab.shape[0]
    const = lambda i: (0, 0)
    row = lambda i: (i, 0)
    return pl.pallas_call(
        functools.partial(_proj_kernel, period=period, qk_scale=qk_scale),
        grid=(r // rows,),
        in_specs=[
            pl.BlockSpec((rows, d), row),
            pl.BlockSpec(g.shape, const),
            pl.BlockSpec(tab.shape, const),
            pl.BlockSpec(w1.shape, const),
            pl.BlockSpec(wlr.shape, const),
            pl.BlockSpec(wgu.shape, const),
            pl.BlockSpec(bg.shape, const),
            pl.BlockSpec(w2.shape, const),
        ],
        out_specs=[
            pl.BlockSpec((rows, 512), row),
            pl.BlockSpec((rows, 256), row),
            pl.BlockSpec((rows, 1792), row),
            pl.BlockSpec((rows, 2048), row),
        ],
        out_shape=[
            jax.ShapeDtypeStruct((r, 512), BF16),
            jax.ShapeDtypeStruct((r, 256), F32),
            jax.ShapeDtypeStruct((r, 1792), F32),
            jax.ShapeDtypeStruct((r, 2048), F32),
        ],
        compiler_params=_cparams(("arbitrary",)),
        name="proj",
    )(x, g, tab, w1, wlr, wgu, bg, w2)


def _attn_kernel(sink_ref, q_ref, prev_ref, cur_ref, o_ref, *, first_valid_key, block_offset):
    n = pl.program_id(1)
    qr = q_ref.shape[0]
    kr = cur_ref.shape[0]
    w = prev_ref.shape[0]
    nk = w + kr
    group = 4
    hd = 64

    rows = lax.broadcasted_iota(jnp.int32, (group * qr, nk), 0)
    cols = lax.broadcasted_iota(jnp.int32, (group * qr, nk), 1)
    head_of_row = rows // qr
    diff = (rows - head_of_row * qr) - cols + w
    mask = (diff >= 0) & (diff <= w)
    if first_valid_key is not None:
        blk = n + block_offset
        mask = mask & (cols >= first_valid_key + w - blk * w)

    prev = prev_ref[...]
    cur = cur_ref[...]
    q = q_ref[...]
    row_head = lax.broadcasted_iota(jnp.int32, (group * qr, 1), 0) // qr
    for kh in range(2):
        k = jnp.concatenate([prev[:, kh * hd:(kh + 1) * hd], cur[:, kh * hd:(kh + 1) * hd]], axis=0).astype(BF16)
        v = jnp.concatenate([prev[:, LANES + kh * hd:LANES + (kh + 1) * hd],
                             cur[:, LANES + kh * hd:LANES + (kh + 1) * hd]], axis=0).astype(BF16)
        qs = jnp.concatenate([q[:, (group * kh + g) * hd:(group * kh + g + 1) * hd] for g in range(group)], axis=0)
        s = lax.dot_general(qs, k, (((1,), (1,)), ((), ())), preferred_element_type=F32) * (hd ** -0.5)
        s = jnp.where(mask, s, NEG_INF)
        sink = jnp.zeros((group * qr, 1), F32)
        for g in range(group):
            sink = jnp.where(row_head == g, sink_ref[group * kh + g], sink)
        m = jnp.maximum(jnp.max(s, axis=-1, keepdims=True), sink)
        e = jnp.exp(s - m)
        p = e / (jnp.sum(e, axis=-1, keepdims=True) + jnp.exp(sink - m))
        o = jnp.dot(p.astype(BF16), v, preferred_element_type=F32)
        for g in range(group):
            h = group * kh + g
            o_ref[:, h * hd:(h + 1) * hd] = o[g * qr:(g + 1) * qr].astype(BF16)


def _attention(sinks, q, kv_prev, kv_cur, nb, nblk, qr, q_map, prev_map, cur_map, out_map, out_rows,
               first_valid_key, block_offset):
    w = ATTN_BLOCK
    return pl.pallas_call(
        functools.partial(_attn_kernel, first_valid_key=first_valid_key, block_offset=block_offset),
        grid=(nb, nblk),
        in_specs=[
            pl.BlockSpec(memory_space=pltpu.SMEM),
            pl.BlockSpec((qr, 512), q_map),
            pl.BlockSpec((w, 256), prev_map),
            pl.BlockSpec((qr, 256), cur_map),
        ],
        out_specs=pl.BlockSpec((qr, 512), out_map),
        out_shape=jax.ShapeDtypeStruct((out_rows, 512), BF16),
        compiler_params=_cparams(("arbitrary", "arbitrary")),
        name="swa",
    )(sinks, q, kv_prev, kv_cur)


def _gla_kernel(gl_ref, s0_ref, gn_ref, yb_ref, sfin_ref, st_ref):
    c = pl.program_id(1)
    ch = gl_ref.shape[0]
    nh, dk, dv = 4, 64, 128

    @pl.when(c == 0)
    def _():
        for h in range(nh):
            st_ref[h] = s0_ref[0, h].T

    gl = gl_ref[...]
    q = gl[:, 0:256]
    k = gl[:, 256:512]
    b = gl[:, 512:768]
    row = lax.broadcasted_iota(jnp.int32, (ch, nh * dk), 0)
    sh = 1
    while sh < ch:
        b = b + jnp.where(row >= sh, pltpu.roll(b, sh, 0), 0.0)
        sh *= 2
    b_last = b[ch - 1:ch, :]
    q_t = (q * jnp.exp(b)).astype(BF16)
    k_t = (k * jnp.exp(-b)).astype(BF16)
    k_end = (k * jnp.exp(b_last - b)).astype(BF16)
    decay = jnp.exp(b_last)
    causal = (lax.broadcasted_iota(jnp.int32, (ch, ch), 0) >= lax.broadcasted_iota(jnp.int32, (ch, ch), 1))
    gn = gn_ref[...]
    nt = (((1,), (1,)), ((), ()))
    for h in range(nh):
        ks = slice(h * dk, (h + 1) * dk)
        v = gl[:, 768 + h * dv:768 + (h + 1) * dv]
        vb = v.astype(BF16)
        a = lax.dot_general(q_t[:, ks], k_t[:, ks], nt, preferred_element_type=F32)
        a = jnp.where(causal, a, 0.0)
        s_t = st_ref[h]
        o = jnp.dot(a.astype(BF16), vb, preferred_element_type=F32)
        o = o + lax.dot_general(q_t[:, ks], s_t.astype(BF16), nt, preferred_element_type=F32)
        upd = jnp.dot(v.T.astype(BF16), k_end[:, ks], preferred_element_type=F32)
        st_ref[h] = s_t * decay[:, ks] + upd
        go = gl[:, 1280 + h * dv:1280 + (h + 1) * dv]
        y = _rms(o, gn) * (go * jax.nn.sigmoid(go))
        yb_ref[:, h * dv:(h + 1) * dv] = y.astype(BF16)

    @pl.when(c == pl.num_programs(1) - 1)
    def _():
        for h in range(nh):
            sfin_ref[0, h] = st_ref[h].T


def _gla(gl, s0, gn, nb, nchunks, ch, in_map, out_map, out_rows):
    return pl.pallas_call(
        _gla_kernel,
        grid=(nb, nchunks),
        in_specs=[
            pl.BlockSpec((ch, 1792), in_map),
            pl.BlockSpec((1, 4, 64, 128), lambda b, c: (b, 0, 0, 0)),
            pl.BlockSpec((1, 128), lambda b, c: (0, 0)),
        ],
        out_specs=[
            pl.BlockSpec((ch, 512), out_map),
            pl.BlockSpec((1, 4, 64, 128), lambda b, c: (b, 0, 0, 0)),
        ],
        out_shape=[
            jax.ShapeDtypeStruct((out_rows, 512), BF16),
            jax.ShapeDtypeStruct((nb, 4, 64, 128), F32),
        ],
        scratch_shapes=[pltpu.VMEM((4, 128, 64), F32)],
        compiler_params=_cparams(("arbitrary", "arbitrary")),
        name="gla",
    )(gl, s0, gn)


def _extract_topk(work, nsel, iota0, sentinel):
    vals, idxs = [], []
    for j in range(nsel):
        m = jnp.max(work, axis=0, keepdims=True)
        idx = jnp.min(jnp.where(work == m, iota0, sentinel), axis=0, keepdims=True)
        vals.append(m)
        idxs.append(idx)
        if j + 1 < nsel:
            work = jnp.where(iota0 == idx, -jnp.inf, work)
    return vals, idxs


def _merge_kernel(x_ref, gt_ref, ya_ref, yb_ref, wa_ref, wb_ref, wo_ref, gf_ref, wq_ref, keys_ref,
                  xm_ref, hn_ref, et_ref, wt_ref):
    td = x_ref.shape[0]
    nkeys = keys_ref.shape[1]
    half = keys_ref.shape[2]
    nheads = keys_ref.shape[0] // 2
    topk = PEER_TOPK

    gt = gt_ref[...]
    d = x_ref.shape[1]
    ma = jnp.dot(ya_ref[...], wa_ref[...], preferred_element_type=F32)
    mb = jnp.dot(yb_ref[...], wb_ref[...], preferred_element_type=F32)
    m = jax.nn.sigmoid(gt[:, 0:d]) * ma + jax.nn.sigmoid(gt[:, d:2 * d]) * mb
    xm = x_ref[...] + jnp.dot(m.astype(BF16), wo_ref[...], preferred_element_type=F32)
    xm_ref[...] = xm
    hn = _rms(xm, gf_ref[...])
    hn_ref[...] = hn
    q = jnp.dot(hn.astype(BF16), wq_ref[...], preferred_element_type=F32).astype(BF16)

    nt = (((1,), (1,)), ((), ()))
    iota_k = lax.broadcasted_iota(jnp.int32, (nkeys, td), 0)
    iota_c = lax.broadcasted_iota(jnp.int32, (topk * topk, td), 0)
    wts, ids = [], []
    for h in range(nheads):
        sv, si = [], []
        for c in range(2):
            gi = 2 * h + c
            s_t = lax.dot_general(keys_ref[gi], q[:, gi * half:(gi + 1) * half], nt,
                                  preferred_element_type=F32)
            vals, idxs = _extract_topk(s_t, topk, iota_k, nkeys)
            sv.append(vals)
            si.append(idxs)
        sv1 = jnp.concatenate(sv[1], axis=0)
        si1 = jnp.concatenate(si[1], axis=0)
        cand = jnp.concatenate([sv[0][a] + sv1 for a in range(topk)], axis=0)
        cidx = jnp.concatenate([si[0][a] * nkeys + si1 for a in range(topk)], axis=0)
        fvals, eids = [], []
        work = cand
        for j in range(topk):
            mx = jnp.max(work, axis=0, keepdims=True)
            pos = jnp.min(jnp.where(work == mx, iota_c, topk * topk), axis=0, keepdims=True)
            hit = iota_c == pos
            eids.append(jnp.max(jnp.where(hit, cidx, -1), axis=0, keepdims=True))
            fvals.append(mx)
            if j + 1 < topk:
                work = jnp.where(hit, -jnp.inf, work)
        fv = jnp.concatenate(fvals, axis=0)
        e = jnp.exp(fv - fvals[0])
        wts.append(e / jnp.sum(e, axis=0, keepdims=True))
        ids.extend(eids)
    wt_ref[...] = jnp.concatenate(wts, axis=0).T
    et_ref[...] = jnp.concatenate(ids, axis=0).T


def _merge_route(x, gt, ya, yb, wa, wb, wo, gf, wq, keys, gt_map, first_block, t):
    d = x.shape[1]
    td = MERGE_ROWS
    nsel = (keys.shape[0] // 2) * PEER_TOPK
    const2 = lambda i: (0, 0)
    row = lambda i: (i, 0)
    row_in = lambda i: (i + first_block, 0)
    return pl.pallas_call(
        _merge_kernel,
        grid=(t // td,),
        in_specs=[
            pl.BlockSpec((td, d), row_in),
            pl.BlockSpec((td, 2 * d), lambda i: gt_map(i + first_block)),
            pl.BlockSpec((td, ya.shape[1]), row_in),
            pl.BlockSpec((td, yb.shape[1]), row_in),
            pl.BlockSpec(wa.shape, const2),
            pl.BlockSpec(wb.shape, const2),
            pl.BlockSpec(wo.shape, const2),
            pl.BlockSpec(gf.shape, const2),
            pl.BlockSpec(wq.shape, const2),
            pl.BlockSpec(keys.shape, lambda i: (0, 0, 0)),
        ],
        out_specs=[
            pl.BlockSpec((td, d), row),
            pl.BlockSpec((td, d), row),
            pl.BlockSpec((td, nsel), row),
            pl.BlockSpec((td, nsel), row),
        ],
        out_shape=[
            jax.ShapeDtypeStruct((t, d), F32),
            jax.ShapeDtypeStruct((t, d), F32),
            jax.ShapeDtypeStruct((t, nsel), jnp.int32),
            jax.ShapeDtypeStruct((t, nsel), F32),
        ],
        compiler_params=_cparams(("arbitrary",)),
        name="merge_route",
    )(x, gt, ya, yb, wa, wb, wo, gf, wq, keys)


def _coef_kernel(act_ref, wt_ref, o_ref):
    act = act_ref[...]
    gelu = 0.5 * act * (1.0 + lax.erf(act * (2.0 ** -0.5)))
    o_ref[...] = wt_ref[...] * gelu


def _expert_coefs(act, wt):
    t, nsel = wt.shape
    rows = math.gcd(t, 512)
    row = lambda i: (i, 0)
    return pl.pallas_call(
        _coef_kernel,
        grid=(t // rows,),
        in_specs=[pl.BlockSpec((rows, nsel), row), pl.BlockSpec((rows, nsel), row)],
        out_specs=pl.BlockSpec((rows, nsel), row),
        out_shape=jax.ShapeDtypeStruct((t, nsel), F32),
        compiler_params=_cparams(("arbitrary",)),
        name="expert_coefs",
    )(act, wt)


def _finish_kernel(xm_ref, o_ref, gfin_ref, y_ref):
    y_ref[...] = _rms(xm_ref[...] + o_ref[...], gfin_ref[...])


def _finish(xm, o, gfin):
    t, d = xm.shape
    rows = math.gcd(t, 512)
    row = lambda i: (i, 0)
    return pl.pallas_call(
        _finish_kernel,
        grid=(t // rows,),
        in_specs=[pl.BlockSpec((rows, d), row), pl.BlockSpec((rows, d), row),
                  pl.BlockSpec(gfin.shape, lambda i: (0, 0))],
        out_specs=pl.BlockSpec((rows, d), row),
        out_shape=jax.ShapeDtypeStruct((t, d), F32),
        compiler_params=_cparams(("arbitrary",)),
        name="finish",
    )(xm, o, gfin)


def _rope_table(pos, valid, rope_dim, head_dim):
    half = rope_dim // 2
    inv = ROPE_THETA ** (-jnp.arange(0, rope_dim, 2, dtype=F32) / rope_dim)
    ang = pos.astype(F32)[:, None] * inv[None, :]
    cos, sin = jnp.cos(ang), jnp.sin(ang)
    n = pos.shape[0]
    ones = jnp.ones((n, head_dim - rope_dim), F32)
    zeros_h = jnp.zeros((n, half), F32)
    zeros_r = jnp.zeros((n, head_dim - rope_dim), F32)
    reps = LANES // head_dim
    cosf = jnp.tile(jnp.concatenate([cos, cos, ones], axis=1), (1, reps))
    sin_lo = jnp.tile(jnp.concatenate([zeros_h, sin, zeros_r], axis=1), (1, reps))
    sin_hi = jnp.tile(jnp.concatenate([-sin, zeros_h, zeros_r], axis=1), (1, reps))
    vcol = jnp.broadcast_to(valid.astype(F32)[:, None], (n, LANES))
    return jnp.concatenate([cosf, sin_lo, sin_hi, vcol], axis=1)


def _pack_table(tab):
    half = tab.shape[1] // 2
    lo = lax.bitcast_convert_type(tab[:, :half].astype(BF16), jnp.uint16).astype(jnp.uint32)
    xb = lax.bitcast_convert_type(tab[:, half:], jnp.uint32)
    sign = xb & jnp.uint32(0x80000000)
    mag = xb & jnp.uint32(0x7FFFFFFF)
    hi = (jnp.maximum(mag + jnp.uint32(0x8000), lo) - lo) >> 16
    return lax.bitcast_convert_type(sign | (hi << 16) | lo, jnp.int32)


def _expert_dots(table, idx, hn, nsel):
    n = idx.shape[0]
    c = table.shape[1]
    d = hn.shape[1]
    sc = plsc.get_sparse_core_info()
    lanes = sc.num_lanes
    workers = sc.num_cores * sc.num_subcores
    gw = GATHER_WINDOW
    ich = GATHER_INDEX_CHUNK
    tok = ich // nsel
    per_worker = n // workers
    assert n % workers == 0 and per_worker % ich == 0 and nsel == 2 * gw and d == 2 * c and gw % lanes == 0
    mesh = plsc.VectorSubcoreMesh(core_axis_name="c", subcore_axis_name="s")

    @functools.partial(
        pl.kernel, out_type=jax.ShapeDtypeStruct((n,), F32), mesh=mesh, name="expert_dots",
        compiler_params=pltpu.CompilerParams(needs_layout_passes=False),
        scratch_types=[pltpu.VMEM((ich,), jnp.int32),
                       pltpu.VMEM((gw, c), jnp.int32), pltpu.VMEM((gw, c), jnp.int32),
                       pltpu.VMEM((tok, d), F32), pltpu.VMEM((ich,), F32), pltpu.VMEM((lanes * lanes,), F32),
                       pltpu.SemaphoreType.DMA, pltpu.SemaphoreType.DMA])
    def dots(tab_hbm, idx_hbm, hn_hbm, act_hbm, idx_v, rows0, rows1, h_v, act_v, scr, sem0, sem1):
        wid = lax.axis_index("s") * sc.num_cores + lax.axis_index("c")
        base = wid * per_worker
        lane = lax.iota(jnp.int32, lanes)

        def gather(win, buf, sem):
            return pltpu.make_async_copy(tab_hbm.at[idx_v.at[pl.ds(win * gw, gw)]], buf, sem)

        def reduce_window(buf, t_loc, out_off):
            for rb in range(gw // lanes):
                def kbody(k, accs):
                    h_lo = h_v[t_loc, pl.ds(k * lanes, lanes)]
                    h_hi = h_v[t_loc, pl.ds(c + k * lanes, lanes)]
                    out = []
                    for r in range(lanes):
                        w = buf[rb * lanes + r, pl.ds(k * lanes, lanes)]
                        lo = lax.bitcast_convert_type(w << 16, F32)
                        hi = lax.bitcast_convert_type(w, F32)
                        out.append(accs[r] + lo * h_lo + hi * h_hi)
                    return tuple(out)

                accs = lax.fori_loop(0, c // lanes, kbody,
                                     tuple(jnp.zeros((lanes,), F32) for _ in range(lanes)))
                for r in range(lanes):
                    scr[pl.ds(r * lanes, lanes)] = accs[r]
                tot = plsc.load_gather(scr, [lane * lanes])
                for l in range(1, lanes):
                    tot = tot + plsc.load_gather(scr, [lane * lanes + l])
                act_v[pl.ds(out_off + rb * lanes, lanes)] = tot

        @pl.loop(0, per_worker // ich)
        def _(g):
            cb = base + g * ich
            pltpu.sync_copy(idx_hbm.at[pl.ds(cb, ich)], idx_v)
            tok_base = pl.multiple_of(wid * (per_worker // nsel) + g * tok, tok)
            pltpu.sync_copy(hn_hbm.at[pl.ds(tok_base, tok)], h_v)
            gather(0, rows0, sem0).start()
            gather(1, rows1, sem1).start()

            @pl.loop(0, tok)
            def _(j):
                gather(2 * j, rows0, sem0).wait()
                reduce_window(rows0, j, j * nsel)

                @pl.when(j + 1 < tok)
                def _():
                    gather(2 * j + 2, rows0, sem0).start()

                gather(2 * j + 1, rows1, sem1).wait()
                reduce_window(rows1, j, j * nsel + gw)

                @pl.when(j + 1 < tok)
                def _():
                    gather(2 * j + 3, rows1, sem1).start()

            pltpu.sync_copy(act_v, act_hbm.at[pl.ds(cb, ich)])

    return dots(table, idx, hn)


def _expert_mix(table, idx, coef, nsel):
    n = idx.shape[0]
    c = table.shape[1]
    d = 2 * c
    sc = plsc.get_sparse_core_info()
    lanes = sc.num_lanes
    workers = sc.num_cores * sc.num_subcores
    gw = GATHER_WINDOW
    ich = GATHER_INDEX_CHUNK
    tok = ich // nsel
    per_worker = n // workers
    kblock = 8
    assert n % workers == 0 and per_worker % ich == 0 and nsel == 2 * gw and c % (kblock * lanes) == 0
    mesh = plsc.VectorSubcoreMesh(core_axis_name="c", subcore_axis_name="s")

    @functools.partial(
        pl.kernel, out_type=jax.ShapeDtypeStruct((n // nsel, d), F32), mesh=mesh, name="expert_mix",
        compiler_params=pltpu.CompilerParams(needs_layout_passes=False),
        scratch_types=[pltpu.VMEM((ich,), jnp.int32), pltpu.VMEM((ich,), F32),
                       pltpu.VMEM((gw, c), jnp.int32), pltpu.VMEM((gw, c), jnp.int32),
                       pltpu.VMEM((tok, d), F32),
                       pltpu.SemaphoreType.DMA, pltpu.SemaphoreType.DMA])
    def mix(tab_hbm, idx_hbm, coef_hbm, out_hbm, idx_v, coef_v, rows0, rows1, out_v, sem0, sem1):
        wid = lax.axis_index("s") * sc.num_cores + lax.axis_index("c")
        base = wid * per_worker
        zero_idx = jnp.zeros((lanes,), jnp.int32)

        def gather(win, buf, sem):
            return pltpu.make_async_copy(tab_hbm.at[idx_v.at[pl.ds(win * gw, gw)]], buf, sem)

        def accumulate_window(buf, t_loc, coef_off, first):
            for kb in range(c // (kblock * lanes)):
                col0 = kb * kblock * lanes
                if first:
                    init = tuple(jnp.zeros((lanes,), F32) for _ in range(2 * kblock))
                else:
                    init = tuple(out_v[t_loc, pl.ds(col0 + i * lanes, lanes)] for i in range(kblock)) + \
                           tuple(out_v[t_loc, pl.ds(c + col0 + i * lanes, lanes)] for i in range(kblock))

                def rbody(r2, accs):
                    accs = list(accs)
                    for rr in range(2):
                        r = 2 * r2 + rr
                        cvec = plsc.load_gather(coef_v, [zero_idx + (coef_off + r)])
                        for i in range(kblock):
                            w = buf[r, pl.ds(col0 + i * lanes, lanes)]
                            lo = lax.bitcast_convert_type(w << 16, F32)
                            hi = lax.bitcast_convert_type(w, F32)
                            accs[i] = accs[i] + lo * cvec
                            accs[kblock + i] = accs[kblock + i] + hi * cvec
                    return tuple(accs)

                accs = lax.fori_loop(0, gw // 2, rbody, init)
                for i in range(kblock):
                    out_v[t_loc, pl.ds(col0 + i * lanes, lanes)] = accs[i]
                    out_v[t_loc, pl.ds(c + col0 + i * lanes, lanes)] = accs[kblock + i]

        @pl.loop(0, per_worker // ich)
        def _(g):
            cb = base + g * ich
            pltpu.sync_copy(idx_hbm.at[pl.ds(cb, ich)], idx_v)
            pltpu.sync_copy(coef_hbm.at[pl.ds(cb, ich)], coef_v)
            gather(0, rows0, sem0).start()
            gather(1, rows1, sem1).start()

            @pl.loop(0, tok)
            def _(j):
                gather(2 * j, rows0, sem0).wait()
                accumulate_window(rows0, j, j * nsel, True)

                @pl.when(j + 1 < tok)
                def _():
                    gather(2 * j + 2, rows0, sem0).start()

                gather(2 * j + 1, rows1, sem1).wait()
                accumulate_window(rows1, j, j * nsel + gw, False)

                @pl.when(j + 1 < tok)
                def _():
                    gather(2 * j + 3, rows1, sem1).start()

            tok_base = pl.multiple_of(wid * (per_worker // nsel) + g * tok, tok)
            pltpu.sync_copy(out_v, out_hbm.at[pl.ds(tok_base, tok)])

    return mix(table, idx, coef)


def _peer_tail(xm, hn, et, wt, u_tab, v_tab, gfin):
    t, nsel = et.shape
    eidx = et.reshape(t * nsel)
    act = _expert_dots(u_tab, eidx, hn, nsel).reshape(t, nsel)
    coef = _expert_coefs(act, wt).reshape(t * nsel)
    mixed = _expert_mix(v_tab, eidx, coef, nsel)
    return _finish(xm, mixed, gfin)


def kernel(x_prompt, x_sample, cache_k_window, cache_v_window, state_gla, meta_tokens, g_norm_mix, w_in,
           w_gate_up, b_gate, attn_sinks, g_gla_norm, w_branch_a, w_branch_b, w_out, g_norm_ffn, w_peer_q,
           peer_sub_keys, peer_u, peer_v, g_norm_final):
    bsz, seq, d = x_prompt.shape
    dbsz, tdec, _ = x_sample.shape
    n_meta = meta_tokens.shape[0]
    depth = w_in.shape[0]
    window = cache_k_window.shape[2]
    kv_heads, head_dim = cache_k_window.shape[3], cache_k_window.shape[4]
    gate_rank = w_gate_up.shape[1]
    bqk = w_gate_up.shape[2]
    n_ph, _, n_keys, p_half = peer_sub_keys.shape[1:]
    assert depth == 1 and d == 1024 and window == ATTN_BLOCK and kv_heads == 2 and head_dim == 64
    assert bqk == 256 and state_gla.shape[2:] == (4, 64, 128) and n_meta <= ATTN_BLOCK
    assert seq % ATTN_BLOCK == 0 and tdec <= SAMPLE_PAD and n_keys == 128 and p_half == 64 and n_ph == 8
    rope_dim = head_dim // 4
    meta_pad = ATTN_BLOCK - n_meta
    lp = ATTN_BLOCK + seq
    nblk = lp // ATTN_BLOCK

    w = w_in[0]
    c_lr = 2304
    c_gate = c_lr + gate_rank
    w1 = w[:, :c_lr].astype(BF16)
    wlr = jnp.pad(w[:, c_lr:c_gate], ((0, 0), (0, LANES - gate_rank))).astype(BF16)
    w2 = w[:, c_gate:].astype(BF16)
    wgu = jnp.pad(w_gate_up[0], ((0, LANES - gate_rank), (0, 0))).astype(BF16)
    bg = b_gate[0][None, :]
    gmix = g_norm_mix[0][None, :]
    wa = w_branch_a[0].astype(BF16)
    wb = w_branch_b[0].astype(BF16)
    wo = w_out[0].astype(BF16)
    gffn = g_norm_ffn[0][None, :]
    wq = w_peer_q[0].astype(BF16)
    keys = peer_sub_keys[0].reshape(n_ph * 2, n_keys, p_half).astype(BF16)
    u_tab = _pack_table(peer_u[0])
    v_tab = _pack_table(peer_v[0])
    gfin = g_norm_final[None, :]
    gn = g_gla_norm[0][None, :]
    sinks = attn_sinks[0]
    qk_scale = float(bqk // 4) ** -0.5

    meta = jnp.broadcast_to(meta_tokens[None].astype(x_prompt.dtype), (bsz, n_meta, d))
    xpad = jnp.concatenate([jnp.zeros((bsz, meta_pad, d), x_prompt.dtype), meta, x_prompt], axis=1)
    xpad = xpad.reshape(bsz * lp, d)
    rows_p = jnp.arange(lp)
    tab_p = _rope_table(rows_p - meta_pad, rows_p >= meta_pad, rope_dim, head_dim)
    proj_rows = max(r for r in range(16, PROJ_ROWS + 1, 16) if lp % r == 0)
    qa, kv, gl, gt = _project(xpad, gmix, tab_p, w1, wlr, wgu, bg, w2, proj_rows, qk_scale)

    nq = nblk - 1
    ya = _attention(
        sinks, qa, kv, kv, bsz, nq, ATTN_BLOCK,
        lambda b, n: (b * nblk + n + 1, 0), lambda b, n: (b * nblk + n, 0), lambda b, n: (b * nblk + n + 1, 0),
        lambda b, n: (b * nq + n, 0), bsz * seq, first_valid_key=meta_pad, block_offset=1)

    nchunks = lp // GLA_CHUNK
    skip = ATTN_BLOCK // GLA_CHUNK
    ncq = nchunks - skip
    s0_p = jnp.zeros((bsz,) + state_gla.shape[2:], F32)
    yb, s_fin_p = _gla(gl, s0_p, gn, bsz, nchunks, GLA_CHUNK,
                       lambda b, c: (b * nchunks + c, 0),
                       lambda b, c: (b * ncq + jnp.maximum(c - skip, 0), 0), bsz * seq)

    xp_rows = x_prompt.reshape(bsz * seq, d)
    per_seq = seq // MERGE_ROWS
    gt_map_p = lambda i: ((i // per_seq) * nblk + 1 + (i % per_seq), 0)
    n_rows = bsz * seq
    chunk = n_rows // PROMPT_CHUNKS if n_rows % (PROMPT_CHUNKS * GATHER_ROW_QUANTUM) == 0 else n_rows
    y_chunks = []
    for c0 in range(0, n_rows, chunk):
        xm_c, hn_c, et_c, wt_c = _merge_route(xp_rows, gt, ya, yb, wa, wb, wo, gffn, wq, keys, gt_map_p,
                                              c0 // MERGE_ROWS, chunk)
        y_chunks.append(_peer_tail(xm_c, hn_c, et_c, wt_c, u_tab, v_tab, gfin))
    y_prompt = jnp.concatenate(y_chunks, axis=0).reshape(bsz, seq, d)

    kv_p = kv.reshape(bsz, lp, 2, kv_heads, head_dim)[:, lp - window:]
    new_k_p = kv_p[:, :, 0][None]
    new_v_p = kv_p[:, :, 1][None]

    sp = SAMPLE_PAD
    xs_pad = jnp.pad(x_sample, ((0, 0), (0, sp - tdec), (0, 0))).reshape(dbsz * sp, d)
    rows_s = jnp.arange(sp)
    reps = 256 // sp
    tab_s = jnp.tile(_rope_table(PAST_LEN + rows_s, rows_s < tdec, rope_dim, head_dim), (reps, 1))
    qa_s, kv_s, gl_s, gt_s = _project(xs_pad, gmix, tab_s, w1, wlr, wgu, bg, w2, 256, qk_scale)

    cache_kv = jnp.concatenate([cache_k_window[0].reshape(dbsz * window, kv_heads * head_dim),
                                cache_v_window[0].reshape(dbsz * window, kv_heads * head_dim)], axis=1)
    seq_map = lambda b, n: (b, 0)
    ya_s = _attention(sinks, qa_s, cache_kv, kv_s, dbsz, 1, sp, seq_map, seq_map, seq_map, seq_map,
                      dbsz * sp, first_valid_key=None, block_offset=0)
    yb_s, s_fin_s = _gla(gl_s, state_gla[0], gn, dbsz, 1, sp, seq_map, seq_map, dbsz * sp)

    def real_rows(a):
        return a.reshape(dbsz, sp, a.shape[-1])[:, :tdec].reshape(dbsz * tdec, a.shape[-1])

    xs_rows = x_sample.reshape(dbsz * tdec, d)
    xm_s, hn_s, et_s, wt_s = _merge_route(xs_rows, real_rows(gt_s), real_rows(ya_s), real_rows(yb_s),
                                          wa, wb, wo, gffn, wq, keys, lambda i: (i, 0), 0, dbsz * tdec)
    y_sample = _peer_tail(xm_s, hn_s, et_s, wt_s, u_tab, v_tab, gfin).reshape(dbsz, tdec, d)

    kv_new = real_rows(kv_s).reshape(dbsz, tdec, 2, kv_heads, head_dim)
    new_k_s = jnp.concatenate([cache_k_window[0].astype(F32), kv_new[:, :, 0]], axis=1)[:, -window:][None]
    new_v_s = jnp.concatenate([cache_v_window[0].astype(F32), kv_new[:, :, 1]], axis=1)[:, -window:][None]

    return (y_prompt, y_sample, new_k_p, new_v_p, s_fin_p[None], new_k_s, new_v_s, s_fin_s[None])
```

```python
import functools
import math

import jax
import jax.numpy as jnp
from jax import lax
from jax.experimental import pallas as pl
from jax.experimental.pallas import tpu as pltpu
from jax.experimental.pallas import tpu_sc as plsc

F32 = jnp.float32
BF16 = jnp.bfloat16

EPS = 1e-6
NEG_INF = -1e30
PAST_LEN = 16384
ROPE_THETA = 500000.0
GATE_NORMALIZER = 16.0
PEER_TOPK = 16

LANES = 128
SUBLANES = 8
VMEM_LIMIT_BYTES = 56 * 1024 * 1024

ATTN_BLOCK = 128
GLA_CHUNK = 64
SAMPLE_PAD = 16
PROJ_ROWS = 544
MERGE_ROWS = 128
GATHER_WINDOW = 64
GATHER_INDEX_CHUNK = 2048
GATHER_ROW_QUANTUM = 512
PROMPT_GROUP = 2


def _cparams(sem):
    return pltpu.CompilerParams(dimension_semantics=sem, vmem_limit_bytes=VMEM_LIMIT_BYTES)


def _rms(x, g):
    ms = jnp.mean(x * x, axis=-1, keepdims=True)
    return (x * lax.rsqrt(ms + EPS)) * g


def _proj_kernel(x_ref, g_ref, tab_ref, w1_ref, wlr_ref, wgu_ref, bg_ref, w2_ref,
                 qa_ref, kv_ref, gl_ref, gt_ref, *, period, qk_scale):
    i = pl.program_id(0)
    tr = x_ref.shape[0]
    hb = _rms(x_ref[...], g_ref[...]).astype(BF16)
    z1 = jnp.dot(hb, w1_ref[...], preferred_element_type=F32)

    start = pl.multiple_of((i * tr) % period, SUBLANES)
    tab = tab_ref[pl.ds(start, tr), :]
    cosf = tab[:, 0:LANES]
    sin_lo = tab[:, LANES:2 * LANES]
    sin_hi = tab[:, 2 * LANES:3 * LANES]
    valid = tab[:, 3 * LANES:3 * LANES + 1]

    def rope(xg):
        return xg * cosf + pltpu.roll(xg, 8, 1) * sin_lo + pltpu.roll(xg, LANES - 8, 1) * sin_hi

    for gi in range(4):
        sl = slice(gi * LANES, (gi + 1) * LANES)
        qa_ref[:, sl] = rope(z1[:, sl]).astype(BF16)
    kv_ref[:, 0:LANES] = rope(z1[:, 512:640])
    kv_ref[:, LANES:2 * LANES] = z1[:, 640:768]

    lr = jnp.dot(hb, wlr_ref[...], preferred_element_type=F32)
    pre = jnp.dot(lr.astype(BF16), wgu_ref[...], preferred_element_type=F32) + bg_ref[...]
    log_sig = jnp.minimum(pre, 0.0) - jnp.log1p(jnp.exp(-jnp.abs(pre)))
    ld = jnp.where(valid > 0.5, log_sig / GATE_NORMALIZER, 0.0)

    gl_ref[:, 0:256] = z1[:, 768:1024] * qk_scale
    gl_ref[:, 256:512] = z1[:, 1024:1280]
    gl_ref[:, 512:768] = ld
    gl_ref[:, 768:1792] = z1[:, 1280:2304]
    gt_ref[...] = jnp.dot(hb, w2_ref[...], preferred_element_type=F32)


def _project(x, g, tab, w1, wlr, wgu, bg, w2, rows, qk_scale):
    r, d = x.shape
    period = tab.shape[0]
    const = lambda i: (0, 0)
    row = lambda i: (i, 0)
    return pl.pallas_call(
        functools.partial(_proj_kernel, period=period, qk_scale=qk_scale),
        grid=(r // rows,),
        in_specs=[
            pl.BlockSpec((rows, d), row),
            pl.BlockSpec(g.shape, const),
            pl.BlockSpec(tab.shape, const),
            pl.BlockSpec(w1.shape, const),
            pl.BlockSpec(wlr.shape, const),
            pl.BlockSpec(wgu.shape, const),
            pl.BlockSpec(bg.shape, const),
            pl.BlockSpec(w2.shape, const),
        ],
        out_specs=[
            pl.BlockSpec((rows, 512), row),
            pl.BlockSpec((rows, 256), row),
            pl.BlockSpec((rows, 1792), row),
            pl.BlockSpec((rows, 2048), row),
        ],
        out_shape=[
            jax.ShapeDtypeStruct((r, 512), BF16),
            jax.ShapeDtypeStruct((r, 256), F32),
            jax.ShapeDtypeStruct((r, 1792), F32),
            jax.ShapeDtypeStruct((r, 2048), F32),
        ],
        compiler_params=_cparams(("arbitrary",)),
        name="proj",
    )(x, g, tab, w1, wlr, wgu, bg, w2)


def _attn_kernel(sink_ref, q_ref, prev_ref, cur_ref, o_ref, *, first_valid_key, block_offset):
    n = pl.program_id(1)
    qr = q_ref.shape[0]
    kr = cur_ref.shape[0]
    w = prev_ref.shape[0]
    nk = w + kr
    group = 4
    hd = 64

    rows = lax.broadcasted_iota(jnp.int32, (group * qr, nk), 0)
    cols = lax.broadcasted_iota(jnp.int32, (group * qr, nk), 1)
    head_of_row = rows // qr
    diff = (rows - head_of_row * qr) - cols + w
    mask = (diff >= 0) & (diff <= w)
    if first_valid_key is not None:
        blk = n + block_offset
        mask = mask & (cols >= first_valid_key + w - blk * w)

    prev = prev_ref[...]
    cur = cur_ref[...]
    q = q_ref[...]
    row_head = lax.broadcasted_iota(jnp.int32, (group * qr, 1), 0) // qr
    for kh in range(2):
        k = jnp.concatenate([prev[:, kh * hd:(kh + 1) * hd], cur[:, kh * hd:(kh + 1) * hd]], axis=0).astype(BF16)
        v = jnp.concatenate([prev[:, LANES + kh * hd:LANES + (kh + 1) * hd],
                             cur[:, LANES + kh * hd:LANES + (kh + 1) * hd]], axis=0).astype(BF16)
        qs = jnp.concatenate([q[:, (group * kh + g) * hd:(group * kh + g + 1) * hd] for g in range(group)], axis=0)
        s = lax.dot_general(qs, k, (((1,), (1,)), ((), ())), preferred_element_type=F32) * (hd ** -0.5)
        s = jnp.where(mask, s, NEG_INF)
        sink = jnp.zeros((group * qr, 1), F32)
        for g in range(group):
            sink = jnp.where(row_head == g, sink_ref[group * kh + g], sink)
        m = jnp.maximum(jnp.max(s, axis=-1, keepdims=True), sink)
        e = jnp.exp(s - m)
        p = e / (jnp.sum(e, axis=-1, keepdims=True) + jnp.exp(sink - m))
        o = jnp.dot(p.astype(BF16), v, preferred_element_type=F32)
        for g in range(group):
            h = group * kh + g
            o_ref[:, h * hd:(h + 1) * hd] = o[g * qr:(g + 1) * qr].astype(BF16)


def _attention(sinks, q, kv_prev, kv_cur, nb, nblk, qr, q_map, prev_map, cur_map, out_map, out_rows,
               first_valid_key, block_offset):
    w = ATTN_BLOCK
    return pl.pallas_call(
        functools.partial(_attn_kernel, first_valid_key=first_valid_key, block_offset=block_offset),
        grid=(nb, nblk),
        in_specs=[
            pl.BlockSpec(memory_space=pltpu.SMEM),
            pl.BlockSpec((qr, 512), q_map),
            pl.BlockSpec((w, 256), prev_map),
            pl.BlockSpec((qr, 256), cur_map),
        ],
        out_specs=pl.BlockSpec((qr, 512), out_map),
        out_shape=jax.ShapeDtypeStruct((out_rows, 512), BF16),
        compiler_params=_cparams(("arbitrary", "arbitrary")),
        name="swa",
    )(sinks, q, kv_prev, kv_cur)


def _gla_kernel(gl_ref, s0_ref, gn_ref, yb_ref, sfin_ref, st_ref):
    c = pl.program_id(1)
    ch = gl_ref.shape[0]
    nh, dk, dv = 4, 64, 128

    @pl.when(c == 0)
    def _():
        for h in range(nh):
            st_ref[h] = s0_ref[0, h].T

    gl = gl_ref[...]
    q = gl[:, 0:256]
    k = gl[:, 256:512]
    b = gl[:, 512:768]
    row = lax.broadcasted_iota(jnp.int32, (ch, nh * dk), 0)
    sh = 1
    while sh < ch:
        b = b + jnp.where(row >= sh, pltpu.roll(b, sh, 0), 0.0)
        sh *= 2
    b_last = b[ch - 1:ch, :]
    q_t = (q * jnp.exp(b)).astype(BF16)
    k_t = (k * jnp.exp(-b)).astype(BF16)
    k_end = (k * jnp.exp(b_last - b)).astype(BF16)
    decay = jnp.exp(b_last)
    causal = (lax.broadcasted_iota(jnp.int32, (ch, ch), 0) >= lax.broadcasted_iota(jnp.int32, (ch, ch), 1))
    gn = gn_ref[...]
    nt = (((1,), (1,)), ((), ()))
    for h in range(nh):
        ks = slice(h * dk, (h + 1) * dk)
        v = gl[:, 768 + h * dv:768 + (h + 1) * dv]
        vb = v.astype(BF16)
        a = lax.dot_general(q_t[:, ks], k_t[:, ks], nt, preferred_element_type=F32)
        a = jnp.where(causal, a, 0.0)
        s_t = st_ref[h]
        o = jnp.dot(a.astype(BF16), vb, preferred_element_type=F32)
        o = o + lax.dot_general(q_t[:, ks], s_t.astype(BF16), nt, preferred_element_type=F32)
        upd = jnp.dot(v.T.astype(BF16), k_end[:, ks], preferred_element_type=F32)
        st_ref[h] = s_t * decay[:, ks] + upd
        go = gl[:, 1280 + h * dv:1280 + (h + 1) * dv]
        y = _rms(o, gn) * (go * jax.nn.sigmoid(go))
        yb_ref[:, h * dv:(h + 1) * dv] = y.astype(BF16)

    @pl.when(c == pl.num_programs(1) - 1)
    def _():
        for h in range(nh):
            sfin_ref[0, h] = st_ref[h].T


def _gla(gl, s0, gn, nb, nchunks, ch, in_map, out_map, out_rows):
    return pl.pallas_call(
        _gla_kernel,
        grid=(nb, nchunks),
        in_specs=[
            pl.BlockSpec((ch, 1792), in_map),
            pl.BlockSpec((1, 4, 64, 128), lambda b, c: (b, 0, 0, 0)),
            pl.BlockSpec((1, 128), lambda b, c: (0, 0)),
        ],
        out_specs=[
            pl.BlockSpec((ch, 512), out_map),
            pl.BlockSpec((1, 4, 64, 128), lambda b, c: (b, 0, 0, 0)),
        ],
        out_shape=[
            jax.ShapeDtypeStruct((out_rows, 512), BF16),
            jax.ShapeDtypeStruct((nb, 4, 64, 128), F32),
        ],
        scratch_shapes=[pltpu.VMEM((4, 128, 64), F32)],
        compiler_params=_cparams(("arbitrary", "arbitrary")),
        name="gla",
    )(gl, s0, gn)


def _extract_topk(work, nsel, iota0, sentinel):
    vals, idxs = [], []
    for j in range(nsel):
        m = jnp.max(work, axis=0, keepdims=True)
        idx = jnp.min(jnp.where(work == m, iota0, sentinel), axis=0, keepdims=True)
        vals.append(m)
        idxs.append(idx)
        if j + 1 < nsel:
            work = jnp.where(iota0 == idx, -jnp.inf, work)
    return vals, idxs


def _merge_kernel(x_ref, gt_ref, ya_ref, yb_ref, wa_ref, wb_ref, wo_ref, gf_ref, wq_ref, keys_ref,
                  xm_ref, hn_ref, et_ref, wt_ref):
    td = x_ref.shape[0]
    nkeys = keys_ref.shape[1]
    half = keys_ref.shape[2]
    nheads = keys_ref.shape[0] // 2
    topk = PEER_TOPK

    gt = gt_ref[...]
    d = x_ref.shape[1]
    ma = jnp.dot(ya_ref[...], wa_ref[...], preferred_element_type=F32)
    mb = jnp.dot(yb_ref[...], wb_ref[...], preferred_element_type=F32)
    m = jax.nn.sigmoid(gt[:, 0:d]) * ma + jax.nn.sigmoid(gt[:, d:2 * d]) * mb
    xm = x_ref[...] + jnp.dot(m.astype(BF16), wo_ref[...], preferred_element_type=F32)
    xm_ref[...] = xm
    hn = _rms(xm, gf_ref[...])
    hn_ref[...] = hn
    q = jnp.dot(hn.astype(BF16), wq_ref[...], preferred_element_type=F32).astype(BF16)

    nt = (((1,), (1,)), ((), ()))
    iota_k = lax.broadcasted_iota(jnp.int32, (nkeys, td), 0)
    iota_c = lax.broadcasted_iota(jnp.int32, (topk * topk, td), 0)
    wts, ids = [], []
    for h in range(nheads):
        sv, si = [], []
        for c in range(2):
            gi = 2 * h + c
            s_t = lax.dot_general(keys_ref[gi], q[:, gi * half:(gi + 1) * half], nt,
                                  preferred_element_type=F32)
            vals, idxs = _extract_topk(s_t, topk, iota_k, nkeys)
            sv.append(vals)
            si.append(idxs)
        sv1 = jnp.concatenate(sv[1], axis=0)
        si1 = jnp.concatenate(si[1], axis=0)
        cand = jnp.concatenate([sv[0][a] + sv1 for a in range(topk)], axis=0)
        cidx = jnp.concatenate([si[0][a] * nkeys + si1 for a in range(topk)], axis=0)
        fvals, eids = [], []
        work = cand
        for j in range(topk):
            mx = jnp.max(work, axis=0, keepdims=True)
            pos = jnp.min(jnp.where(work == mx, iota_c, topk * topk), axis=0, keepdims=True)
            hit = iota_c == pos
            eids.append(jnp.max(jnp.where(hit, cidx, -1), axis=0, keepdims=True))
            fvals.append(mx)
            if j + 1 < topk:
                work = jnp.where(hit, -jnp.inf, work)
        fv = jnp.concatenate(fvals, axis=0)
        e = jnp.exp(fv - fvals[0])
        wts.append(e / jnp.sum(e, axis=0, keepdims=True))
        ids.extend(eids)
    wt_ref[...] = jnp.concatenate(wts, axis=0).T
    et_ref[...] = jnp.concatenate(ids, axis=0).T


def _merge_route(x, gt, ya, yb, wa, wb, wo, gf, wq, keys, gt_map, first_block, t):
    d = x.shape[1]
    td = MERGE_ROWS
    nsel = (keys.shape[0] // 2) * PEER_TOPK
    const2 = lambda i: (0, 0)
    row = lambda i: (i, 0)
    row_in = lambda i: (i + first_block, 0)
    return pl.pallas_call(
        _merge_kernel,
        grid=(t // td,),
        in_specs=[
            pl.BlockSpec((td, d), row_in),
            pl.BlockSpec((td, 2 * d), lambda i: gt_map(i + first_block)),
            pl.BlockSpec((td, ya.shape[1]), row_in),
            pl.BlockSpec((td, yb.shape[1]), row_in),
            pl.BlockSpec(wa.shape, const2),
            pl.BlockSpec(wb.shape, const2),
            pl.BlockSpec(wo.shape, const2),
            pl.BlockSpec(gf.shape, const2),
            pl.BlockSpec(wq.shape, const2),
            pl.BlockSpec(keys.shape, lambda i: (0, 0, 0)),
        ],
        out_specs=[
            pl.BlockSpec((td, d), row),
            pl.BlockSpec((td, d), row),
            pl.BlockSpec((td, nsel), row),
            pl.BlockSpec((td, nsel), row),
        ],
        out_shape=[
            jax.ShapeDtypeStruct((t, d), F32),
            jax.ShapeDtypeStruct((t, d), F32),
            jax.ShapeDtypeStruct((t, nsel), jnp.int32),
            jax.ShapeDtypeStruct((t, nsel), F32),
        ],
        compiler_params=_cparams(("arbitrary",)),
        name="merge_route",
    )(x, gt, ya, yb, wa, wb, wo, gf, wq, keys)


def _coef_kernel(act_ref, wt_ref, o_ref):
    act = act_ref[...]
    gelu = 0.5 * act * (1.0 + lax.erf(act * (2.0 ** -0.5)))
    o_ref[...] = wt_ref[...] * gelu


def _expert_coefs(act, wt):
    t, nsel = wt.shape
    rows = math.gcd(t, 512)
    row = lambda i: (i, 0)
    return pl.pallas_call(
        _coef_kernel,
        grid=(t // rows,),
        in_specs=[pl.BlockSpec((rows, nsel), row), pl.BlockSpec((rows, nsel), row)],
        out_specs=pl.BlockSpec((rows, nsel), row),
        out_shape=jax.ShapeDtypeStruct((t, nsel), F32),
        compiler_params=_cparams(("arbitrary",)),
        name="expert_coefs",
    )(act, wt)


def _finish_kernel(xm_ref, o_ref, gfin_ref, y_ref):
    y_ref[...] = _rms(xm_ref[...] + o_ref[...], gfin_ref[...])


def _finish(xm, o, gfin):
    t, d = xm.shape
    rows = math.gcd(t, 512)
    row = lambda i: (i, 0)
    return pl.pallas_call(
        _finish_kernel,
        grid=(t // rows,),
        in_specs=[pl.BlockSpec((rows, d), row), pl.BlockSpec((rows, d), row),
                  pl.BlockSpec(gfin.shape, lambda i: (0, 0))],
        out_specs=pl.BlockSpec((rows, d), row),
        out_shape=jax.ShapeDtypeStruct((t, d), F32),
        compiler_params=_cparams(("arbitrary",)),
        name="finish",
    )(xm, o, gfin)


def _rope_table(pos, valid, rope_dim, head_dim):
    half = rope_dim // 2
    inv = ROPE_THETA ** (-jnp.arange(0, rope_dim, 2, dtype=F32) / rope_dim)
    ang = pos.astype(F32)[:, None] * inv[None, :]
    cos, sin = jnp.cos(ang), jnp.sin(ang)
    n = pos.shape[0]
    ones = jnp.ones((n, head_dim - rope_dim), F32)
    zeros_h = jnp.zeros((n, half), F32)
    zeros_r = jnp.zeros((n, head_dim - rope_dim), F32)
    reps = LANES // head_dim
    cosf = jnp.tile(jnp.concatenate([cos, cos, ones], axis=1), (1, reps))
    sin_lo = jnp.tile(jnp.concatenate([zeros_h, sin, zeros_r], axis=1), (1, reps))
    sin_hi = jnp.tile(jnp.concatenate([-sin, zeros_h, zeros_r], axis=1), (1, reps))
    vcol = jnp.broadcast_to(valid.astype(F32)[:, None], (n, LANES))
    return jnp.concatenate([cosf, sin_lo, sin_hi, vcol], axis=1)


def _pack_table(tab):
    half = tab.shape[1] // 2
    lo = lax.bitcast_convert_type(tab[:, :half].astype(BF16), jnp.uint16).astype(jnp.uint32)
    xb = lax.bitcast_convert_type(tab[:, half:], jnp.uint32)
    sign = xb & jnp.uint32(0x80000000)
    mag = xb & jnp.uint32(0x7FFFFFFF)
    hi = (jnp.maximum(mag + jnp.uint32(0x8000), lo) - lo) >> 16
    return lax.bitcast_convert_type(sign | (hi << 16) | lo, jnp.int32)


def _expert_dots(table, idx, hn, nsel):
    n = idx.shape[0]
    c = table.shape[1]
    d = hn.shape[1]
    sc = plsc.get_sparse_core_info()
    lanes = sc.num_lanes
    workers = sc.num_cores * sc.num_subcores
    gw = GATHER_WINDOW
    ich = GATHER_INDEX_CHUNK
    tok = ich // nsel
    per_worker = n // workers
    assert n % workers == 0 and per_worker % ich == 0 and nsel == 2 * gw and d == 2 * c and gw % lanes == 0
    mesh = plsc.VectorSubcoreMesh(core_axis_name="c", subcore_axis_name="s")

    @functools.partial(
        pl.kernel, out_type=jax.ShapeDtypeStruct((n,), F32), mesh=mesh, name="expert_dots",
        compiler_params=pltpu.CompilerParams(needs_layout_passes=False),
        scratch_types=[pltpu.VMEM((ich,), jnp.int32),
                       pltpu.VMEM((gw, c), jnp.int32), pltpu.VMEM((gw, c), jnp.int32),
                       pltpu.VMEM((tok, d), F32), pltpu.VMEM((ich,), F32), pltpu.VMEM((lanes * lanes,), F32),
                       pltpu.SemaphoreType.DMA, pltpu.SemaphoreType.DMA])
    def dots(tab_hbm, idx_hbm, hn_hbm, act_hbm, idx_v, rows0, rows1, h_v, act_v, scr, sem0, sem1):
        wid = lax.axis_index("s") * sc.num_cores + lax.axis_index("c")
        base = wid * per_worker
        lane = lax.iota(jnp.int32, lanes)

        def gather(win, buf, sem):
            return pltpu.make_async_copy(tab_hbm.at[idx_v.at[pl.ds(win * gw, gw)]], buf, sem)

        def reduce_window(buf, t_loc, out_off):
            for rb in range(gw // lanes):
                def kbody(k, accs):
                    h_lo = h_v[t_loc, pl.ds(k * lanes, lanes)]
                    h_hi = h_v[t_loc, pl.ds(c + k * lanes, lanes)]
                    out = []
                    for r in range(lanes):
                        w = buf[rb * lanes + r, pl.ds(k * lanes, lanes)]
                        lo = lax.bitcast_convert_type(w << 16, F32)
                        hi = lax.bitcast_convert_type(w, F32)
                        out.append(accs[r] + lo * h_lo + hi * h_hi)
                    return tuple(out)

                accs = lax.fori_loop(0, c // lanes, kbody,
                                     tuple(jnp.zeros((lanes,), F32) for _ in range(lanes)))
                for r in range(lanes):
                    scr[pl.ds(r * lanes, lanes)] = accs[r]
                tot = plsc.load_gather(scr, [lane * lanes])
                for l in range(1, lanes):
                    tot = tot + plsc.load_gather(scr, [lane * lanes + l])
                act_v[pl.ds(out_off + rb * lanes, lanes)] = tot

        @pl.loop(0, per_worker // ich)
        def _(g):
            cb = base + g * ich
            pltpu.sync_copy(idx_hbm.at[pl.ds(cb, ich)], idx_v)
            tok_base = pl.multiple_of(wid * (per_worker // nsel) + g * tok, tok)
            pltpu.sync_copy(hn_hbm.at[pl.ds(tok_base, tok)], h_v)
            gather(0, rows0, sem0).start()
            gather(1, rows1, sem1).start()

            @pl.loop(0, tok)
            def _(j):
                gather(2 * j, rows0, sem0).wait()
                reduce_window(rows0, j, j * nsel)

                @pl.when(j + 1 < tok)
                def _():
                    gather(2 * j + 2, rows0, sem0).start()

                gather(2 * j + 1, rows1, sem1).wait()
                reduce_window(rows1, j, j * nsel + gw)

                @pl.when(j + 1 < tok)
                def _():
                    gather(2 * j + 3, rows1, sem1).start()

            pltpu.sync_copy(act_v, act_hbm.at[pl.ds(cb, ich)])

    return dots(table, idx, hn)


def _expert_mix(table, idx, coef, nsel):
    n = idx.shape[0]
    c = table.shape[1]
    d = 2 * c
    sc = plsc.get_sparse_core_info()
    lanes = sc.num_lanes
    workers = sc.num_cores * sc.num_subcores
    gw = GATHER_WINDOW
    ich = GATHER_INDEX_CHUNK
    tok = ich // nsel
    per_worker = n // workers
    kblock = 8
    assert n % workers == 0 and per_worker % ich == 0 and nsel == 2 * gw and c % (kblock * lanes) == 0
    mesh = plsc.VectorSubcoreMesh(core_axis_name="c", subcore_axis_name="s")

    @functools.partial(
        pl.kernel, out_type=jax.ShapeDtypeStruct((n // nsel, d), F32), mesh=mesh, name="expert_mix",
        compiler_params=pltpu.CompilerParams(needs_layout_passes=False),
        scratch_types=[pltpu.VMEM((ich,), jnp.int32), pltpu.VMEM((ich,), F32),
                       pltpu.VMEM((gw, c), jnp.int32), pltpu.VMEM((gw, c), jnp.int32),
                       pltpu.VMEM((tok, d), F32),
                       pltpu.SemaphoreType.DMA, pltpu.SemaphoreType.DMA])
    def mix(tab_hbm, idx_hbm, coef_hbm, out_hbm, idx_v, coef_v, rows0, rows1, out_v, sem0, sem1):
        wid = lax.axis_index("s") * sc.num_cores + lax.axis_index("c")
        base = wid * per_worker
        zero_idx = jnp.zeros((lanes,), jnp.int32)

        def gather(win, buf, sem):
            return pltpu.make_async_copy(tab_hbm.at[idx_v.at[pl.ds(win * gw, gw)]], buf, sem)

        def accumulate_window(buf, t_loc, coef_off, first):
            for kb in range(c // (kblock * lanes)):
                col0 = kb * kblock * lanes
                if first:
                    init = tuple(jnp.zeros((lanes,), F32) for _ in range(2 * kblock))
                else:
                    init = tuple(out_v[t_loc, pl.ds(col0 + i * lanes, lanes)] for i in range(kblock)) + \
                           tuple(out_v[t_loc, pl.ds(c + col0 + i * lanes, lanes)] for i in range(kblock))

                def rbody(r2, accs):
                    accs = list(accs)
                    for rr in range(2):
                        r = 2 * r2 + rr
                        cvec = plsc.load_gather(coef_v, [zero_idx + (coef_off + r)])
                        for i in range(kblock):
                            w = buf[r, pl.ds(col0 + i * lanes, lanes)]
                            lo = lax.bitcast_convert_type(w << 16, F32)
                            hi = lax.bitcast_convert_type(w, F32)
                            accs[i] = accs[i] + lo * cvec
                            accs[kblock + i] = accs[kblock + i] + hi * cvec
                    return tuple(accs)

                accs = lax.fori_loop(0, gw // 2, rbody, init)
                for i in range(kblock):
                    out_v[t_loc, pl.ds(col0 + i * lanes, lanes)] = accs[i]
                    out_v[t_loc, pl.ds(c + col0 + i * lanes, lanes)] = accs[kblock + i]

        @pl.loop(0, per_worker // ich)
        def _(g):
            cb = base + g * ich
            pltpu.sync_copy(idx_hbm.at[pl.ds(cb, ich)], idx_v)
            pltpu.sync_copy(coef_hbm.at[pl.ds(cb, ich)], coef_v)
            gather(0, rows0, sem0).start()
            gather(1, rows1, sem1).start()

            @pl.loop(0, tok)
            def _(j):
                gather(2 * j, rows0, sem0).wait()
                accumulate_window(rows0, j, j * nsel, True)

                @pl.when(j + 1 < tok)
                def _():
                    gather(2 * j + 2, rows0, sem0).start()

                gather(2 * j + 1, rows1, sem1).wait()
                accumulate_window(rows1, j, j * nsel + gw, False)

                @pl.when(j + 1 < tok)
                def _():
                    gather(2 * j + 3, rows1, sem1).start()

            tok_base = pl.multiple_of(wid * (per_worker // nsel) + g * tok, tok)
            pltpu.sync_copy(out_v, out_hbm.at[pl.ds(tok_base, tok)])

    return mix(table, idx, coef)


def _peer_tail(xm, hn, et, wt, u_tab, v_tab, gfin):
    t, nsel = et.shape
    eidx = et.reshape(t * nsel)
    act = _expert_dots(u_tab, eidx, hn, nsel).reshape(t, nsel)
    coef = _expert_coefs(act, wt).reshape(t * nsel)
    mixed = _expert_mix(v_tab, eidx, coef, nsel)
    return _finish(xm, mixed, gfin)


def kernel(x_prompt, x_sample, cache_k_window, cache_v_window, state_gla, meta_tokens, g_norm_mix, w_in,
           w_gate_up, b_gate, attn_sinks, g_gla_norm, w_branch_a, w_branch_b, w_out, g_norm_ffn, w_peer_q,
           peer_sub_keys, peer_u, peer_v, g_norm_final):
    bsz, seq, d = x_prompt.shape
    dbsz, tdec, _ = x_sample.shape
    n_meta = meta_tokens.shape[0]
    depth = w_in.shape[0]
    window = cache_k_window.shape[2]
    kv_heads, head_dim = cache_k_window.shape[3], cache_k_window.shape[4]
    gate_rank = w_gate_up.shape[1]
    bqk = w_gate_up.shape[2]
    n_ph, _, n_keys, p_half = peer_sub_keys.shape[1:]
    assert depth == 1 and d == 1024 and window == ATTN_BLOCK and kv_heads == 2 and head_dim == 64
    assert bqk == 256 and state_gla.shape[2:] == (4, 64, 128) and n_meta <= ATTN_BLOCK
    assert seq % ATTN_BLOCK == 0 and tdec <= SAMPLE_PAD and n_keys == 128 and p_half == 64 and n_ph == 8
    rope_dim = head_dim // 4
    meta_pad = ATTN_BLOCK - n_meta
    lp = ATTN_BLOCK + seq
    nblk = lp // ATTN_BLOCK

    w = w_in[0]
    c_lr = 2304
    c_gate = c_lr + gate_rank
    w1 = w[:, :c_lr].astype(BF16)
    wlr = jnp.pad(w[:, c_lr:c_gate], ((0, 0), (0, LANES - gate_rank))).astype(BF16)
    w2 = w[:, c_gate:].astype(BF16)
    wgu = jnp.pad(w_gate_up[0], ((0, LANES - gate_rank), (0, 0))).astype(BF16)
    bg = b_gate[0][None, :]
    gmix = g_norm_mix[0][None, :]
    wa = w_branch_a[0].astype(BF16)
    wb = w_branch_b[0].astype(BF16)
    wo = w_out[0].astype(BF16)
    gffn = g_norm_ffn[0][None, :]
    wq = w_peer_q[0].astype(BF16)
    keys = peer_sub_keys[0].reshape(n_ph * 2, n_keys, p_half).astype(BF16)
    u_tab = _pack_table(peer_u[0])
    v_tab = _pack_table(peer_v[0])
    gfin = g_norm_final[None, :]
    gn = g_gla_norm[0][None, :]
    sinks = attn_sinks[0]
    qk_scale = float(bqk // 4) ** -0.5

    rows_p = jnp.arange(lp)
    tab_p = _rope_table(rows_p - meta_pad, rows_p >= meta_pad, rope_dim, head_dim)
    proj_rows = max(r for r in range(16, PROJ_ROWS + 1, 16) if lp % r == 0)
    nq = nblk - 1
    nchunks = lp // GLA_CHUNK
    skip = ATTN_BLOCK // GLA_CHUNK
    ncq = nchunks - skip
    per_seq = seq // MERGE_ROWS
    gt_map_p = lambda i: ((i // per_seq) * nblk + 1 + (i % per_seq), 0)

    def prompt_sequences(xg):
        gb = xg.shape[0]
        meta = jnp.broadcast_to(meta_tokens[None].astype(xg.dtype), (gb, n_meta, d))
        xpad = jnp.concatenate([jnp.zeros((gb, meta_pad, d), xg.dtype), meta, xg], axis=1).reshape(gb * lp, d)
        qa, kv, gl, gt = _project(xpad, gmix, tab_p, w1, wlr, wgu, bg, w2, proj_rows, qk_scale)
        ya = _attention(
            sinks, qa, kv, kv, gb, nq, ATTN_BLOCK,
            lambda b, n: (b * nblk + n + 1, 0), lambda b, n: (b * nblk + n, 0), lambda b, n: (b * nblk + n + 1, 0),
            lambda b, n: (b * nq + n, 0), gb * seq, first_valid_key=meta_pad, block_offset=1)
        s0 = jnp.zeros((gb,) + state_gla.shape[2:], F32)
        yb, s_fin = _gla(gl, s0, gn, gb, nchunks, GLA_CHUNK,
                         lambda b, c: (b * nchunks + c, 0),
                         lambda b, c: (b * ncq + jnp.maximum(c - skip, 0), 0), gb * seq)
        xm, hn, et, wt = _merge_route(xg.reshape(gb * seq, d), gt, ya, yb, wa, wb, wo, gffn, wq, keys,
                                      gt_map_p, 0, gb * seq)
        y = _peer_tail(xm, hn, et, wt, u_tab, v_tab, gfin)
        kv_w = kv.reshape(gb, lp, 2, kv_heads, head_dim)[:, lp - window:]
        return y, kv_w, s_fin

    group = PROMPT_GROUP if (bsz % PROMPT_GROUP == 0 and (PROMPT_GROUP * seq) % GATHER_ROW_QUANTUM == 0) else bsz
    parts = [prompt_sequences(x_prompt[b0:b0 + group]) for b0 in range(0, bsz, group)]
    y_prompt = jnp.concatenate([p[0] for p in parts], axis=0).reshape(bsz, seq, d)
    kv_p = jnp.concatenate([p[1] for p in parts], axis=0)
    s_fin_p = jnp.concatenate([p[2] for p in parts], axis=0)
    new_k_p = kv_p[:, :, 0][None]
    new_v_p = kv_p[:, :, 1][None]

    sp = SAMPLE_PAD
    xs_pad = jnp.pad(x_sample, ((0, 0), (0, sp - tdec), (0, 0))).reshape(dbsz * sp, d)
    rows_s = jnp.arange(sp)
    reps = 256 // sp
    tab_s = jnp.tile(_rope_table(PAST_LEN + rows_s, rows_s < tdec, rope_dim, head_dim), (reps, 1))
    qa_s, kv_s, gl_s, gt_s = _project(xs_pad, gmix, tab_s, w1, wlr, wgu, bg, w2, 256, qk_scale)

    cache_kv = jnp.concatenate([cache_k_window[0].reshape(dbsz * window, kv_heads * head_dim),
                                cache_v_window[0].reshape(dbsz * window, kv_heads * head_dim)], axis=1)
    seq_map = lambda b, n: (b, 0)
    ya_s = _attention(sinks, qa_s, cache_kv, kv_s, dbsz, 1, sp, seq_map, seq_map, seq_map, seq_map,
                      dbsz * sp, first_valid_key=None, block_offset=0)
    yb_s, s_fin_s = _gla(gl_s, state_gla[0], gn, dbsz, 1, sp, seq_map, seq_map, dbsz * sp)

    def real_rows(a):
        return a.reshape(dbsz, sp, a.shape[-1])[:, :tdec].reshape(dbsz * tdec, a.shape[-1])

    xs_rows = x_sample.reshape(dbsz * tdec, d)
    xm_s, hn_s, et_s, wt_s = _merge_route(xs_rows, real_rows(gt_s), real_rows(ya_s), real_rows(yb_s),
                                          wa, wb, wo, gffn, wq, keys, lambda i: (i, 0), 0, dbsz * tdec)
    y_sample = _peer_tail(xm_s, hn_s, et_s, wt_s, u_tab, v_tab, gfin).reshape(dbsz, tdec, d)

    kv_new = real_rows(kv_s).reshape(dbsz, tdec, 2, kv_heads, head_dim)
    new_k_s = jnp.concatenate([cache_k_window[0].astype(F32), kv_new[:, :, 0]], axis=1)[:, -window:][None]
    new_v_s = jnp.concatenate([cache_v_window[0].astype(F32), kv_new[:, :, 1]], axis=1)[:, -window:][None]

    return (y_prompt, y_sample, new_k_p, new_v_p, s_fin_p[None], new_k_s, new_v_s, s_fin_s[None])
```

```python
import functools
import math

import jax
import jax.numpy as jnp
from jax import lax
from jax.experimental import pallas as pl
from jax.experimental.pallas import tpu as pltpu
from jax.experimental.pallas import tpu_sc as plsc

F32 = jnp.float32
BF16 = jnp.bfloat16

EPS = 1e-6
NEG_INF = -1e30
PAST_LEN = 16384
ROPE_THETA = 500000.0
GATE_NORMALIZER = 16.0
PEER_TOPK = 16

LANES = 128
SUBLANES = 8
VMEM_LIMIT_BYTES = 56 * 1024 * 1024

ATTN_BLOCK = 128
GLA_CHUNK = 64
SAMPLE_PAD = 16
PROJ_ROWS = 544
MERGE_ROWS = 128
GATHER_WINDOW = 64
GATHER_INDEX_CHUNK = 2048
GATHER_ROW_QUANTUM = 512
PROMPT_GROUP = 1


def _cparams(sem):
    return pltpu.CompilerParams(dimension_semantics=sem, vmem_limit_bytes=VMEM_LIMIT_BYTES)


def _rms(x, g):
    ms = jnp.mean(x * x, axis=-1, keepdims=True)
    return (x * lax.rsqrt(ms + EPS)) * g


def _proj_kernel(x_ref, g_ref, tab_ref, w1_ref, wlr_ref, wgu_ref, bg_ref, w2_ref,
                 qa_ref, kv_ref, gl_ref, gt_ref, *, period, qk_scale):
    i = pl.program_id(0)
    tr = x_ref.shape[0]
    hb = _rms(x_ref[...], g_ref[...]).astype(BF16)
    z1 = jnp.dot(hb, w1_ref[...], preferred_element_type=F32)

    start = pl.multiple_of((i * tr) % period, SUBLANES)
    tab = tab_ref[pl.ds(start, tr), :]
    cosf = tab[:, 0:LANES]
    sin_lo = tab[:, LANES:2 * LANES]
    sin_hi = tab[:, 2 * LANES:3 * LANES]
    valid = tab[:, 3 * LANES:3 * LANES + 1]

    def rope(xg):
        return xg * cosf + pltpu.roll(xg, 8, 1) * sin_lo + pltpu.roll(xg, LANES - 8, 1) * sin_hi

    for gi in range(4):
        sl = slice(gi * LANES, (gi + 1) * LANES)
        qa_ref[:, sl] = rope(z1[:, sl]).astype(BF16)
    kv_ref[:, 0:LANES] = rope(z1[:, 512:640])
    kv_ref[:, LANES:2 * LANES] = z1[:, 640:768]

    lr = jnp.dot(hb, wlr_ref[...], preferred_element_type=F32)
    pre = jnp.dot(lr.astype(BF16), wgu_ref[...], preferred_element_type=F32) + bg_ref[...]
    log_sig = jnp.minimum(pre, 0.0) - jnp.log1p(jnp.exp(-jnp.abs(pre)))
    ld = jnp.where(valid > 0.5, log_sig / GATE_NORMALIZER, 0.0)

    gl_ref[:, 0:256] = z1[:, 768:1024] * qk_scale
    gl_ref[:, 256:512] = z1[:, 1024:1280]
    gl_ref[:, 512:768] = ld
    gl_ref[:, 768:1792] = z1[:, 1280:2304]
    gt_ref[...] = jnp.dot(hb, w2_ref[...], preferred_element_type=F32)


def _project(x, g, tab, w1, wlr, wgu, bg, w2, rows, qk_scale):
    r, d = x.shape
    period = tab.shape[0]
    const = lambda i: (0, 0)
    row = lambda i: (i, 0)
    return pl.pallas_call(
        functools.partial(_proj_kernel, period=period, qk_scale=qk_scale),
        grid=(r // rows,),
        in_specs=[
            pl.BlockSpec((rows, d), row),
            pl.BlockSpec(g.shape, const),
            pl.BlockSpec(tab.shape, const),
            pl.BlockSpec(w1.shape, const),
            pl.BlockSpec(wlr.shape, const),
            pl.BlockSpec(wgu.shape, const),
            pl.BlockSpec(bg.shape, const),
            pl.BlockSpec(w2.shape, const),
        ],
        out_specs=[
            pl.BlockSpec((rows, 512), row),
            pl.BlockSpec((rows, 256), row),
            pl.BlockSpec((rows, 1792), row),
            pl.BlockSpec((rows, 2048), row),
        ],
        out_shape=[
            jax.ShapeDtypeStruct((r, 512), BF16),
            jax.ShapeDtypeStruct((r, 256), F32),
            jax.ShapeDtypeStruct((r, 1792), F32),
            jax.ShapeDtypeStruct((r, 2048), F32),
        ],
        compiler_params=_cparams(("arbitrary",)),
        name="proj",
    )(x, g, tab, w1, wlr, wgu, bg, w2)


def _attn_kernel(sink_ref, q_ref, prev_ref, cur_ref, o_ref, *, first_valid_key, block_offset):
    n = pl.program_id(1)
    qr = q_ref.shape[0]
    kr = cur_ref.shape[0]
    w = prev_ref.shape[0]
    nk = w + kr
    group = 4
    hd = 64

    rows = lax.broadcasted_iota(jnp.int32, (group * qr, nk), 0)
    cols = lax.broadcasted_iota(jnp.int32, (group * qr, nk), 1)
    head_of_row = rows // qr
    diff = (rows - head_of_row * qr) - cols + w
    mask = (diff >= 0) & (diff <= w)
    if first_valid_key is not None:
        blk = n + block_offset
        mask = mask & (cols >= first_valid_key + w - blk * w)

    prev = prev_ref[...]
    cur = cur_ref[...]
    q = q_ref[...]
    row_head = lax.broadcasted_iota(jnp.int32, (group * qr, 1), 0) // qr
    for kh in range(2):
        k = jnp.concatenate([prev[:, kh * hd:(kh + 1) * hd], cur[:, kh * hd:(kh + 1) * hd]], axis=0).astype(BF16)
        v = jnp.concatenate([prev[:, LANES + kh * hd:LANES + (kh + 1) * hd],
                             cur[:, LANES + kh * hd:LANES + (kh + 1) * hd]], axis=0).astype(BF16)
        qs = jnp.concatenate([q[:, (group * kh + g) * hd:(group * kh + g + 1) * hd] for g in range(group)], axis=0)
        s = lax.dot_general(qs, k, (((1,), (1,)), ((), ())), preferred_element_type=F32) * (hd ** -0.5)
        s = jnp.where(mask, s, NEG_INF)
        sink = jnp.zeros((group * qr, 1), F32)
        for g in range(group):
            sink = jnp.where(row_head == g, sink_ref[group * kh + g], sink)
        m = jnp.maximum(jnp.max(s, axis=-1, keepdims=True), sink)
        e = jnp.exp(s - m)
        p = e / (jnp.sum(e, axis=-1, keepdims=True) + jnp.exp(sink - m))
        o = jnp.dot(p.astype(BF16), v, preferred_element_type=F32)
        for g in range(group):
            h = group * kh + g
            o_ref[:, h * hd:(h + 1) * hd] = o[g * qr:(g + 1) * qr].astype(BF16)


def _attention(sinks, q, kv_prev, kv_cur, nb, nblk, qr, q_map, prev_map, cur_map, out_map, out_rows,
               first_valid_key, block_offset):
    w = ATTN_BLOCK
    return pl.pallas_call(
        functools.partial(_attn_kernel, first_valid_key=first_valid_key, block_offset=block_offset),
        grid=(nb, nblk),
        in_specs=[
            pl.BlockSpec(memory_space=pltpu.SMEM),
            pl.BlockSpec((qr, 512), q_map),
            pl.BlockSpec((w, 256), prev_map),
            pl.BlockSpec((qr, 256), cur_map),
        ],
        out_specs=pl.BlockSpec((qr, 512), out_map),
        out_shape=jax.ShapeDtypeStruct((out_rows, 512), BF16),
        compiler_params=_cparams(("arbitrary", "arbitrary")),
        name="swa",
    )(sinks, q, kv_prev, kv_cur)


def _gla_kernel(gl_ref, s0_ref, gn_ref, yb_ref, sfin_ref, st_ref):
    c = pl.program_id(1)
    ch = gl_ref.shape[0]
    nh, dk, dv = 4, 64, 128

    @pl.when(c == 0)
    def _():
        for h in range(nh):
            st_ref[h] = s0_ref[0, h].T

    gl = gl_ref[...]
    q = gl[:, 0:256]
    k = gl[:, 256:512]
    b = gl[:, 512:768]
    row = lax.broadcasted_iota(jnp.int32, (ch, nh * dk), 0)
    sh = 1
    while sh < ch:
        b = b + jnp.where(row >= sh, pltpu.roll(b, sh, 0), 0.0)
        sh *= 2
    b_last = b[ch - 1:ch, :]
    q_t = (q * jnp.exp(b)).astype(BF16)
    k_t = (k * jnp.exp(-b)).astype(BF16)
    k_end = (k * jnp.exp(b_last - b)).astype(BF16)
    decay = jnp.exp(b_last)
    causal = (lax.broadcasted_iota(jnp.int32, (ch, ch), 0) >= lax.broadcasted_iota(jnp.int32, (ch, ch), 1))
    gn = gn_ref[...]
    nt = (((1,), (1,)), ((), ()))
    for h in range(nh):
        ks = slice(h * dk, (h + 1) * dk)
        v = gl[:, 768 + h * dv:768 + (h + 1) * dv]
        vb = v.astype(BF16)
        a = lax.dot_general(q_t[:, ks], k_t[:, ks], nt, preferred_element_type=F32)
        a = jnp.where(causal, a, 0.0)
        s_t = st_ref[h]
        o = jnp.dot(a.astype(BF16), vb, preferred_element_type=F32)
        o = o + lax.dot_general(q_t[:, ks], s_t.astype(BF16), nt, preferred_element_type=F32)
        upd = jnp.dot(v.T.astype(BF16), k_end[:, ks], preferred_element_type=F32)
        st_ref[h] = s_t * decay[:, ks] + upd
        go = gl[:, 1280 + h * dv:1280 + (h + 1) * dv]
        y = _rms(o, gn) * (go * jax.nn.sigmoid(go))
        yb_ref[:, h * dv:(h + 1) * dv] = y.astype(BF16)

    @pl.when(c == pl.num_programs(1) - 1)
    def _():
        for h in range(nh):
            sfin_ref[0, h] = st_ref[h].T


def _gla(gl, s0, gn, nb, nchunks, ch, in_map, out_map, out_rows):
    return pl.pallas_call(
        _gla_kernel,
        grid=(nb, nchunks),
        in_specs=[
            pl.BlockSpec((ch, 1792), in_map),
            pl.BlockSpec((1, 4, 64, 128), lambda b, c: (b, 0, 0, 0)),
            pl.BlockSpec((1, 128), lambda b, c: (0, 0)),
        ],
        out_specs=[
            pl.BlockSpec((ch, 512), out_map),
            pl.BlockSpec((1, 4, 64, 128), lambda b, c: (b, 0, 0, 0)),
        ],
        out_shape=[
            jax.ShapeDtypeStruct((out_rows, 512), BF16),
            jax.ShapeDtypeStruct((nb, 4, 64, 128), F32),
        ],
        scratch_shapes=[pltpu.VMEM((4, 128, 64), F32)],
        compiler_params=_cparams(("arbitrary", "arbitrary")),
        name="gla",
    )(gl, s0, gn)


def _extract_topk(work, nsel, iota0, sentinel):
    vals, idxs = [], []
    for j in range(nsel):
        m = jnp.max(work, axis=0, keepdims=True)
        idx = jnp.min(jnp.where(work == m, iota0, sentinel), axis=0, keepdims=True)
        vals.append(m)
        idxs.append(idx)
        if j + 1 < nsel:
            work = jnp.where(iota0 == idx, -jnp.inf, work)
    return vals, idxs


def _merge_kernel(x_ref, gt_ref, ya_ref, yb_ref, wa_ref, wb_ref, wo_ref, gf_ref, wq_ref, keys_ref,
                  xm_ref, hn_ref, et_ref, wt_ref):
    td = x_ref.shape[0]
    nkeys = keys_ref.shape[1]
    half = keys_ref.shape[2]
    nheads = keys_ref.shape[0] // 2
    topk = PEER_TOPK

    gt = gt_ref[...]
    d = x_ref.shape[1]
    ma = jnp.dot(ya_ref[...], wa_ref[...], preferred_element_type=F32)
    mb = jnp.dot(yb_ref[...], wb_ref[...], preferred_element_type=F32)
    m = jax.nn.sigmoid(gt[:, 0:d]) * ma + jax.nn.sigmoid(gt[:, d:2 * d]) * mb
    xm = x_ref[...] + jnp.dot(m.astype(BF16), wo_ref[...], preferred_element_type=F32)
    xm_ref[...] = xm
    hn = _rms(xm, gf_ref[...])
    hn_ref[...] = hn
    q = jnp.dot(hn.astype(BF16), wq_ref[...], preferred_element_type=F32).astype(BF16)

    nt = (((1,), (1,)), ((), ()))
    iota_k = lax.broadcasted_iota(jnp.int32, (nkeys, td), 0)
    iota_c = lax.broadcasted_iota(jnp.int32, (topk * topk, td), 0)
    wts, ids = [], []
    for h in range(nheads):
        sv, si = [], []
        for c in range(2):
            gi = 2 * h + c
            s_t = lax.dot_general(keys_ref[gi], q[:, gi * half:(gi + 1) * half], nt,
                                  preferred_element_type=F32)
            vals, idxs = _extract_topk(s_t, topk, iota_k, nkeys)
            sv.append(vals)
            si.append(idxs)
        sv1 = jnp.concatenate(sv[1], axis=0)
        si1 = jnp.concatenate(si[1], axis=0)
        cand = jnp.concatenate([sv[0][a] + sv1 for a in range(topk)], axis=0)
        cidx = jnp.concatenate([si[0][a] * nkeys + si1 for a in range(topk)], axis=0)
        fvals, eids = [], []
        work = cand
        for j in range(topk):
            mx = jnp.max(work, axis=0, keepdims=True)
            pos = jnp.min(jnp.where(work == mx, iota_c, topk * topk), axis=0, keepdims=True)
            hit = iota_c == pos
            eids.append(jnp.max(jnp.where(hit, cidx, -1), axis=0, keepdims=True))
            fvals.append(mx)
            if j + 1 < topk:
                work = jnp.where(hit, -jnp.inf, work)
        fv = jnp.concatenate(fvals, axis=0)
        e = jnp.exp(fv - fvals[0])
        wts.append(e / jnp.sum(e, axis=0, keepdims=True))
        ids.extend(eids)
    wt_ref[...] = jnp.concatenate(wts, axis=0).T
    et_ref[...] = jnp.concatenate(ids, axis=0).T


def _merge_route(x, gt, ya, yb, wa, wb, wo, gf, wq, keys, gt_map):
    t, d = x.shape
    td = MERGE_ROWS
    nsel = (keys.shape[0] // 2) * PEER_TOPK
    const2 = lambda i: (0, 0)
    row = lambda i: (i, 0)
    return pl.pallas_call(
        _merge_kernel,
        grid=(t // td,),
        in_specs=[
            pl.BlockSpec((td, d), row),
            pl.BlockSpec((td, 2 * d), gt_map),
            pl.BlockSpec((td, ya.shape[1]), row),
            pl.BlockSpec((td, yb.shape[1]), row),
            pl.BlockSpec(wa.shape, const2),
            pl.BlockSpec(wb.shape, const2),
            pl.BlockSpec(wo.shape, const2),
            pl.BlockSpec(gf.shape, const2),
            pl.BlockSpec(wq.shape, const2),
            pl.BlockSpec(keys.shape, lambda i: (0, 0, 0)),
        ],
        out_specs=[
            pl.BlockSpec((td, d), row),
            pl.BlockSpec((td, d), row),
            pl.BlockSpec((td, nsel), row),
            pl.BlockSpec((td, nsel), row),
        ],
        out_shape=[
            jax.ShapeDtypeStruct((t, d), F32),
            jax.ShapeDtypeStruct((t, d), F32),
            jax.ShapeDtypeStruct((t, nsel), jnp.int32),
            jax.ShapeDtypeStruct((t, nsel), F32),
        ],
        compiler_params=_cparams(("arbitrary",)),
        name="merge_route",
    )(x, gt, ya, yb, wa, wb, wo, gf, wq, keys)


def _coef_kernel(act_ref, wt_ref, o_ref):
    act = act_ref[...]
    gelu = 0.5 * act * (1.0 + lax.erf(act * (2.0 ** -0.5)))
    o_ref[...] = wt_ref[...] * gelu


def _expert_coefs(act, wt):
    t, nsel = wt.shape
    rows = math.gcd(t, 512)
    row = lambda i: (i, 0)
    return pl.pallas_call(
        _coef_kernel,
        grid=(t // rows,),
        in_specs=[pl.BlockSpec((rows, nsel), row), pl.BlockSpec((rows, nsel), row)],
        out_specs=pl.BlockSpec((rows, nsel), row),
        out_shape=jax.ShapeDtypeStruct((t, nsel), F32),
        compiler_params=_cparams(("arbitrary",)),
        name="expert_coefs",
    )(act, wt)


def _finish_kernel(xm_ref, o_ref, gfin_ref, *rest):
    y_ref = rest[-1]
    y_ref[...] = _rms(xm_ref[...] + o_ref[...], gfin_ref[...])


def _finish(xm, o, gfin, y_acc, row_off, total_rows):
    t, d = xm.shape
    rows = math.gcd(t, 512)
    first = row_off // rows
    row = lambda i: (i, 0)
    in_specs = [pl.BlockSpec((rows, d), row), pl.BlockSpec((rows, d), row),
                pl.BlockSpec(gfin.shape, lambda i: (0, 0))]
    args = [xm, o, gfin]
    if y_acc is not None:
        in_specs.append(pl.BlockSpec(memory_space=pl.ANY))
        args.append(y_acc)
    return pl.pallas_call(
        _finish_kernel,
        grid=(t // rows,),
        in_specs=in_specs,
        out_specs=pl.BlockSpec((rows, d), lambda i: (i + first, 0)),
        out_shape=jax.ShapeDtypeStruct((total_rows, d), F32),
        input_output_aliases={} if y_acc is None else {3: 0},
        compiler_params=_cparams(("arbitrary",)),
        name="finish",
    )(*args)


def _rope_table(pos, valid, rope_dim, head_dim):
    half = rope_dim // 2
    inv = ROPE_THETA ** (-jnp.arange(0, rope_dim, 2, dtype=F32) / rope_dim)
    ang = pos.astype(F32)[:, None] * inv[None, :]
    cos, sin = jnp.cos(ang), jnp.sin(ang)
    n = pos.shape[0]
    ones = jnp.ones((n, head_dim - rope_dim), F32)
    zeros_h = jnp.zeros((n, half), F32)
    zeros_r = jnp.zeros((n, head_dim - rope_dim), F32)
    reps = LANES // head_dim
    cosf = jnp.tile(jnp.concatenate([cos, cos, ones], axis=1), (1, reps))
    sin_lo = jnp.tile(jnp.concatenate([zeros_h, sin, zeros_r], axis=1), (1, reps))
    sin_hi = jnp.tile(jnp.concatenate([-sin, zeros_h, zeros_r], axis=1), (1, reps))
    vcol = jnp.broadcast_to(valid.astype(F32)[:, None], (n, LANES))
    return jnp.concatenate([cosf, sin_lo, sin_hi, vcol], axis=1)


def _pack_table(tab):
    half = tab.shape[1] // 2
    lo = lax.bitcast_convert_type(tab[:, :half].astype(BF16), jnp.uint16).astype(jnp.uint32)
    xb = lax.bitcast_convert_type(tab[:, half:], jnp.uint32)
    sign = xb & jnp.uint32(0x80000000)
    mag = xb & jnp.uint32(0x7FFFFFFF)
    hi = (jnp.maximum(mag + jnp.uint32(0x8000), lo) - lo) >> 16
    return lax.bitcast_convert_type(sign | (hi << 16) | lo, jnp.int32)


def _expert_dots(table, idx, hn, nsel):
    n = idx.shape[0]
    c = table.shape[1]
    d = hn.shape[1]
    sc = plsc.get_sparse_core_info()
    lanes = sc.num_lanes
    workers = sc.num_cores * sc.num_subcores
    gw = GATHER_WINDOW
    ich = GATHER_INDEX_CHUNK
    tok = ich // nsel
    per_worker = n // workers
    assert n % workers == 0 and per_worker % ich == 0 and nsel == 2 * gw and d == 2 * c and gw % lanes == 0
    mesh = plsc.VectorSubcoreMesh(core_axis_name="c", subcore_axis_name="s")

    @functools.partial(
        pl.kernel, out_type=jax.ShapeDtypeStruct((n,), F32), mesh=mesh, name="expert_dots",
        compiler_params=pltpu.CompilerParams(needs_layout_passes=False),
        scratch_types=[pltpu.VMEM((ich,), jnp.int32),
                       pltpu.VMEM((gw, c), jnp.int32), pltpu.VMEM((gw, c), jnp.int32),
                       pltpu.VMEM((tok, d), F32), pltpu.VMEM((ich,), F32), pltpu.VMEM((lanes * lanes,), F32),
                       pltpu.SemaphoreType.DMA, pltpu.SemaphoreType.DMA])
    def dots(tab_hbm, idx_hbm, hn_hbm, act_hbm, idx_v, rows0, rows1, h_v, act_v, scr, sem0, sem1):
        wid = lax.axis_index("s") * sc.num_cores + lax.axis_index("c")
        base = wid * per_worker
        lane = lax.iota(jnp.int32, lanes)

        def gather(win, buf, sem):
            return pltpu.make_async_copy(tab_hbm.at[idx_v.at[pl.ds(win * gw, gw)]], buf, sem)

        def reduce_window(buf, t_loc, out_off):
            for rb in range(gw // lanes):
                def kbody(k, accs):
                    h_lo = h_v[t_loc, pl.ds(k * lanes, lanes)]
                    h_hi = h_v[t_loc, pl.ds(c + k * lanes, lanes)]
                    out = []
                    for r in range(lanes):
                        w = buf[rb * lanes + r, pl.ds(k * lanes, lanes)]
                        lo = lax.bitcast_convert_type(w << 16, F32)
                        hi = lax.bitcast_convert_type(w, F32)
                        out.append(accs[r] + lo * h_lo + hi * h_hi)
                    return tuple(out)

                accs = lax.fori_loop(0, c // lanes, kbody,
                                     tuple(jnp.zeros((lanes,), F32) for _ in range(lanes)))
                for r in range(lanes):
                    scr[pl.ds(r * lanes, lanes)] = accs[r]
                tot = plsc.load_gather(scr, [lane * lanes])
                for l in range(1, lanes):
                    tot = tot + plsc.load_gather(scr, [lane * lanes + l])
                act_v[pl.ds(out_off + rb * lanes, lanes)] = tot

        @pl.loop(0, per_worker // ich)
        def _(g):
            cb = base + g * ich
            pltpu.sync_copy(idx_hbm.at[pl.ds(cb, ich)], idx_v)
            tok_base = pl.multiple_of(wid * (per_worker // nsel) + g * tok, tok)
            pltpu.sync_copy(hn_hbm.at[pl.ds(tok_base, tok)], h_v)
            gather(0, rows0, sem0).start()
            gather(1, rows1, sem1).start()

            @pl.loop(0, tok)
            def _(j):
                gather(2 * j, rows0, sem0).wait()
                reduce_window(rows0, j, j * nsel)

                @pl.when(j + 1 < tok)
                def _():
                    gather(2 * j + 2, rows0, sem0).start()

                gather(2 * j + 1, rows1, sem1).wait()
                reduce_window(rows1, j, j * nsel + gw)

                @pl.when(j + 1 < tok)
                def _():
                    gather(2 * j + 3, rows1, sem1).start()

            pltpu.sync_copy(act_v, act_hbm.at[pl.ds(cb, ich)])

    return dots(table, idx, hn)


def _expert_mix(table, idx, coef, nsel):
    n = idx.shape[0]
    c = table.shape[1]
    d = 2 * c
    sc = plsc.get_sparse_core_info()
    lanes = sc.num_lanes
    workers = sc.num_cores * sc.num_subcores
    gw = GATHER_WINDOW
    ich = GATHER_INDEX_CHUNK
    tok = ich // nsel
    per_worker = n // workers
    kblock = 8
    assert n % workers == 0 and per_worker % ich == 0 and nsel == 2 * gw and c % (kblock * lanes) == 0
    mesh = plsc.VectorSubcoreMesh(core_axis_name="c", subcore_axis_name="s")

    @functools.partial(
        pl.kernel, out_type=jax.ShapeDtypeStruct((n // nsel, d), F32), mesh=mesh, name="expert_mix",
        compiler_params=pltpu.CompilerParams(needs_layout_passes=False),
        scratch_types=[pltpu.VMEM((ich,), jnp.int32), pltpu.VMEM((ich,), F32),
                       pltpu.VMEM((gw, c), jnp.int32), pltpu.VMEM((gw, c), jnp.int32),
                       pltpu.VMEM((tok, d), F32),
                       pltpu.SemaphoreType.DMA, pltpu.SemaphoreType.DMA])
    def mix(tab_hbm, idx_hbm, coef_hbm, out_hbm, idx_v, coef_v, rows0, rows1, out_v, sem0, sem1):
        wid = lax.axis_index("s") * sc.num_cores + lax.axis_index("c")
        base = wid * per_worker
        zero_idx = jnp.zeros((lanes,), jnp.int32)

        def gather(win, buf, sem):
            return pltpu.make_async_copy(tab_hbm.at[idx_v.at[pl.ds(win * gw, gw)]], buf, sem)

        def accumulate_window(buf, t_loc, coef_off, first):
            for kb in range(c // (kblock * lanes)):
                col0 = kb * kblock * lanes
                if first:
                    init = tuple(jnp.zeros((lanes,), F32) for _ in range(2 * kblock))
                else:
                    init = tuple(out_v[t_loc, pl.ds(col0 + i * lanes, lanes)] for i in range(kblock)) + \
                           tuple(out_v[t_loc, pl.ds(c + col0 + i * lanes, lanes)] for i in range(kblock))

                def rbody(r2, accs):
                    accs = list(accs)
                    for rr in range(2):
                        r = 2 * r2 + rr
                        cvec = plsc.load_gather(coef_v, [zero_idx + (coef_off + r)])
                        for i in range(kblock):
                            w = buf[r, pl.ds(col0 + i * lanes, lanes)]
                            lo = lax.bitcast_convert_type(w << 16, F32)
                            hi = lax.bitcast_convert_type(w, F32)
                            accs[i] = accs[i] + lo * cvec
                            accs[kblock + i] = accs[kblock + i] + hi * cvec
                    return tuple(accs)

                accs = lax.fori_loop(0, gw // 2, rbody, init)
                for i in range(kblock):
                    out_v[t_loc, pl.ds(col0 + i * lanes, lanes)] = accs[i]
                    out_v[t_loc, pl.ds(c + col0 + i * lanes, lanes)] = accs[kblock + i]

        @pl.loop(0, per_worker // ich)
        def _(g):
            cb = base + g * ich
            pltpu.sync_copy(idx_hbm.at[pl.ds(cb, ich)], idx_v)
            pltpu.sync_copy(coef_hbm.at[pl.ds(cb, ich)], coef_v)
            gather(0, rows0, sem0).start()
            gather(1, rows1, sem1).start()

            @pl.loop(0, tok)
            def _(j):
                gather(2 * j, rows0, sem0).wait()
                accumulate_window(rows0, j, j * nsel, True)

                @pl.when(j + 1 < tok)
                def _():
                    gather(2 * j + 2, rows0, sem0).start()

                gather(2 * j + 1, rows1, sem1).wait()
                accumulate_window(rows1, j, j * nsel + gw, False)

                @pl.when(j + 1 < tok)
                def _():
                    gather(2 * j + 3, rows1, sem1).start()

            tok_base = pl.multiple_of(wid * (per_worker // nsel) + g * tok, tok)
            pltpu.sync_copy(out_v, out_hbm.at[pl.ds(tok_base, tok)])

    return mix(table, idx, coef)


def _peer_tail(xm, hn, et, wt, u_tab, v_tab, gfin, y_acc, row_off, total_rows):
    t, nsel = et.shape
    eidx = et.reshape(t * nsel)
    act = _expert_dots(u_tab, eidx, hn, nsel).reshape(t, nsel)
    coef = _expert_coefs(act, wt).reshape(t * nsel)
    mixed = _expert_mix(v_tab, eidx, coef, nsel)
    return _finish(xm, mixed, gfin, y_acc, row_off, total_rows)


def kernel(x_prompt, x_sample, cache_k_window, cache_v_window, state_gla, meta_tokens, g_norm_mix, w_in,
           w_gate_up, b_gate, attn_sinks, g_gla_norm, w_branch_a, w_branch_b, w_out, g_norm_ffn, w_peer_q,
           peer_sub_keys, peer_u, peer_v, g_norm_final):
    bsz, seq, d = x_prompt.shape
    dbsz, tdec, _ = x_sample.shape
    n_meta = meta_tokens.shape[0]
    depth = w_in.shape[0]
    window = cache_k_window.shape[2]
    kv_heads, head_dim = cache_k_window.shape[3], cache_k_window.shape[4]
    gate_rank = w_gate_up.shape[1]
    bqk = w_gate_up.shape[2]
    n_ph, _, n_keys, p_half = peer_sub_keys.shape[1:]
    assert depth == 1 and d == 1024 and window == ATTN_BLOCK and kv_heads == 2 and head_dim == 64
    assert bqk == 256 and state_gla.shape[2:] == (4, 64, 128) and n_meta <= ATTN_BLOCK
    assert seq % ATTN_BLOCK == 0 and tdec <= SAMPLE_PAD and n_keys == 128 and p_half == 64 and n_ph == 8
    rope_dim = head_dim // 4
    meta_pad = ATTN_BLOCK - n_meta
    lp = ATTN_BLOCK + seq
    nblk = lp // ATTN_BLOCK

    w = w_in[0]
    c_lr = 2304
    c_gate = c_lr + gate_rank
    w1 = w[:, :c_lr].astype(BF16)
    wlr = jnp.pad(w[:, c_lr:c_gate], ((0, 0), (0, LANES - gate_rank))).astype(BF16)
    w2 = w[:, c_gate:].astype(BF16)
    wgu = jnp.pad(w_gate_up[0], ((0, LANES - gate_rank), (0, 0))).astype(BF16)
    bg = b_gate[0][None, :]
    gmix = g_norm_mix[0][None, :]
    wa = w_branch_a[0].astype(BF16)
    wb = w_branch_b[0].astype(BF16)
    wo = w_out[0].astype(BF16)
    gffn = g_norm_ffn[0][None, :]
    wq = w_peer_q[0].astype(BF16)
    keys = peer_sub_keys[0].reshape(n_ph * 2, n_keys, p_half).astype(BF16)
    u_tab = _pack_table(peer_u[0])
    v_tab = _pack_table(peer_v[0])
    gfin = g_norm_final[None, :]
    gn = g_gla_norm[0][None, :]
    sinks = attn_sinks[0]
    qk_scale = float(bqk // 4) ** -0.5

    rows_p = jnp.arange(lp)
    tab_p = _rope_table(rows_p - meta_pad, rows_p >= meta_pad, rope_dim, head_dim)
    proj_rows = max(r for r in range(16, PROJ_ROWS + 1, 16) if lp % r == 0)
    nq = nblk - 1
    nchunks = lp // GLA_CHUNK
    skip = ATTN_BLOCK // GLA_CHUNK
    ncq = nchunks - skip
    per_seq = seq // MERGE_ROWS
    gt_map_p = lambda i: ((i // per_seq) * nblk + 1 + (i % per_seq), 0)

    def prompt_sequences(xg, y_acc, row_off):
        gb = xg.shape[0]
        meta = jnp.broadcast_to(meta_tokens[None].astype(xg.dtype), (gb, n_meta, d))
        xpad = jnp.concatenate([jnp.zeros((gb, meta_pad, d), xg.dtype), meta, xg], axis=1).reshape(gb * lp, d)
        qa, kv, gl, gt = _project(xpad, gmix, tab_p, w1, wlr, wgu, bg, w2, proj_rows, qk_scale)
        ya = _attention(
            sinks, qa, kv, kv, gb, nq, ATTN_BLOCK,
            lambda b, n: (b * nblk + n + 1, 0), lambda b, n: (b * nblk + n, 0), lambda b, n: (b * nblk + n + 1, 0),
            lambda b, n: (b * nq + n, 0), gb * seq, first_valid_key=meta_pad, block_offset=1)
        s0 = jnp.zeros((gb,) + state_gla.shape[2:], F32)
        yb, s_fin = _gla(gl, s0, gn, gb, nchunks, GLA_CHUNK,
                         lambda b, c: (b * nchunks + c, 0),
                         lambda b, c: (b * ncq + jnp.maximum(c - skip, 0), 0), gb * seq)
        xm, hn, et, wt = _merge_route(xg.reshape(gb * seq, d), gt, ya, yb, wa, wb, wo, gffn, wq, keys,
                                      gt_map_p)
        y_acc = _peer_tail(xm, hn, et, wt, u_tab, v_tab, gfin, y_acc, row_off, bsz * seq)
        kv_w = kv.reshape(gb, lp, 2, kv_heads, head_dim)[:, lp - window:]
        return y_acc, kv_w, s_fin

    group = PROMPT_GROUP if (bsz % PROMPT_GROUP == 0 and (PROMPT_GROUP * seq) % GATHER_ROW_QUANTUM == 0) else bsz
    y_acc, kv_parts, s_parts = None, [], []
    for b0 in range(0, bsz, group):
        y_acc, kv_w, s_fin = prompt_sequences(x_prompt[b0:b0 + group], y_acc, b0 * seq)
        kv_parts.append(kv_w)
        s_parts.append(s_fin)
    y_prompt = y_acc.reshape(bsz, seq, d)
    kv_p = jnp.concatenate(kv_parts, axis=0)
    s_fin_p = jnp.concatenate(s_parts, axis=0)
    new_k_p = kv_p[:, :, 0][None]
    new_v_p = kv_p[:, :, 1][None]

    sp = SAMPLE_PAD
    xs_pad = jnp.pad(x_sample, ((0, 0), (0, sp - tdec), (0, 0))).reshape(dbsz * sp, d)
    rows_s = jnp.arange(sp)
    reps = 256 // sp
    tab_s = jnp.tile(_rope_table(PAST_LEN + rows_s, rows_s < tdec, rope_dim, head_dim), (reps, 1))
    qa_s, kv_s, gl_s, gt_s = _project(xs_pad, gmix, tab_s, w1, wlr, wgu, bg, w2, 256, qk_scale)

    cache_kv = jnp.concatenate([cache_k_window[0].reshape(dbsz * window, kv_heads * head_dim),
                                cache_v_window[0].reshape(dbsz * window, kv_heads * head_dim)], axis=1)
    seq_map = lambda b, n: (b, 0)
    ya_s = _attention(sinks, qa_s, cache_kv, kv_s, dbsz, 1, sp, seq_map, seq_map, seq_map, seq_map,
                      dbsz * sp, first_valid_key=None, block_offset=0)
    yb_s, s_fin_s = _gla(gl_s, state_gla[0], gn, dbsz, 1, sp, seq_map, seq_map, dbsz * sp)

    def real_rows(a):
        return a.reshape(dbsz, sp, a.shape[-1])[:, :tdec].reshape(dbsz * tdec, a.shape[-1])

    xs_rows = x_sample.reshape(dbsz * tdec, d)
    xm_s, hn_s, et_s, wt_s = _merge_route(xs_rows, real_rows(gt_s), real_rows(ya_s), real_rows(yb_s),
                                          wa, wb, wo, gffn, wq, keys, lambda i: (i, 0))
    y_sample = _peer_tail(xm_s, hn_s, et_s, wt_s, u_tab, v_tab, gfin, None, 0, dbsz * tdec).reshape(dbsz, tdec, d)

    kv_new = real_rows(kv_s).reshape(dbsz, tdec, 2, kv_heads, head_dim)
    new_k_s = jnp.concatenate([cache_k_window[0].astype(F32), kv_new[:, :, 0]], axis=1)[:, -window:][None]
    new_v_s = jnp.concatenate([cache_v_window[0].astype(F32), kv_new[:, :, 1]], axis=1)[:, -window:][None]

    return (y_prompt, y_sample, new_k_p, new_v_p, s_fin_p[None], new_k_s, new_v_s, s_fin_s[None])
```

```python
import functools
import math

import jax
import jax.numpy as jnp
from jax import lax
from jax.experimental import pallas as pl
from jax.experimental.pallas import tpu as pltpu
from jax.experimental.pallas import tpu_sc as plsc

F32 = jnp.float32
BF16 = jnp.bfloat16

EPS = 1e-6
NEG_INF = -1e30
PAST_LEN = 16384
ROPE_THETA = 500000.0
GATE_NORMALIZER = 16.0
PEER_TOPK = 16

LANES = 128
SUBLANES = 8
VMEM_LIMIT_BYTES = 56 * 1024 * 1024

ATTN_BLOCK = 128
GLA_CHUNK = 64
SAMPLE_PAD = 16
PROJ_ROWS = 544
MERGE_ROWS = 128
GATHER_WINDOW = 64
GATHER_INDEX_CHUNK = 2048
GATHER_ROW_QUANTUM = 512
PROMPT_GROUP = 1
TC_EXPERT_TOKENS = 8
TC_EXPERT_GROUPS = 2


def _cparams(sem):
    return pltpu.CompilerParams(dimension_semantics=sem, vmem_limit_bytes=VMEM_LIMIT_BYTES)


def _rms(x, g):
    ms = jnp.mean(x * x, axis=-1, keepdims=True)
    return (x * lax.rsqrt(ms + EPS)) * g


def _proj_kernel(x_ref, g_ref, tab_ref, w1_ref, wlr_ref, wgu_ref, bg_ref, w2_ref,
                 qa_ref, kv_ref, gl_ref, gt_ref, *, period, qk_scale):
    i = pl.program_id(0)
    tr = x_ref.shape[0]
    hb = _rms(x_ref[...], g_ref[...]).astype(BF16)
    z1 = jnp.dot(hb, w1_ref[...], preferred_element_type=F32)

    start = pl.multiple_of((i * tr) % period, SUBLANES)
    tab = tab_ref[pl.ds(start, tr), :]
    cosf = tab[:, 0:LANES]
    sin_lo = tab[:, LANES:2 * LANES]
    sin_hi = tab[:, 2 * LANES:3 * LANES]
    valid = tab[:, 3 * LANES:3 * LANES + 1]

    def rope(xg):
        return xg * cosf + pltpu.roll(xg, 8, 1) * sin_lo + pltpu.roll(xg, LANES - 8, 1) * sin_hi

    for gi in range(4):
        sl = slice(gi * LANES, (gi + 1) * LANES)
        qa_ref[:, sl] = rope(z1[:, sl]).astype(BF16)
    kv_ref[:, 0:LANES] = rope(z1[:, 512:640])
    kv_ref[:, LANES:2 * LANES] = z1[:, 640:768]

    lr = jnp.dot(hb, wlr_ref[...], preferred_element_type=F32)
    pre = jnp.dot(lr.astype(BF16), wgu_ref[...], preferred_element_type=F32) + bg_ref[...]
    log_sig = jnp.minimum(pre, 0.0) - jnp.log1p(jnp.exp(-jnp.abs(pre)))
    ld = jnp.where(valid > 0.5, log_sig / GATE_NORMALIZER, 0.0)

    gl_ref[:, 0:256] = z1[:, 768:1024] * qk_scale
    gl_ref[:, 256:512] = z1[:, 1024:1280]
    gl_ref[:, 512:768] = ld
    gl_ref[:, 768:1792] = z1[:, 1280:2304]
    gt_ref[...] = jnp.dot(hb, w2_ref[...], preferred_element_type=F32)


def _project(x, g, tab, w1, wlr, wgu, bg, w2, rows, qk_scale):
    r, d = x.shape
    period = tab.shape[0]
    const = lambda i: (0, 0)
    row = lambda i: (i, 0)
    return pl.pallas_call(
        functools.partial(_proj_kernel, period=period, qk_scale=qk_scale),
        grid=(r // rows,),
        in_specs=[
            pl.BlockSpec((rows, d), row),
            pl.BlockSpec(g.shape, const),
            pl.BlockSpec(tab.shape, const),
            pl.BlockSpec(w1.shape, const),
            pl.BlockSpec(wlr.shape, const),
            pl.BlockSpec(wgu.shape, const),
            pl.BlockSpec(bg.shape, const),
            pl.BlockSpec(w2.shape, const),
        ],
        out_specs=[
            pl.BlockSpec((rows, 512), row),
            pl.BlockSpec((rows, 256), row),
            pl.BlockSpec((rows, 1792), row),
            pl.BlockSpec((rows, 2048), row),
        ],
        out_shape=[
            jax.ShapeDtypeStruct((r, 512), BF16),
            jax.ShapeDtypeStruct((r, 256), F32),
            jax.ShapeDtypeStruct((r, 1792), F32),
            jax.ShapeDtypeStruct((r, 2048), F32),
        ],
        compiler_params=_cparams(("arbitrary",)),
        name="proj",
    )(x, g, tab, w1, wlr, wgu, bg, w2)


def _attn_kernel(sink_ref, q_ref, prev_ref, cur_ref, o_ref, *, first_valid_key, block_offset):
    n = pl.program_id(1)
    qr = q_ref.shape[0]
    kr = cur_ref.shape[0]
    w = prev_ref.shape[0]
    nk = w + kr
    group = 4
    hd = 64

    rows = lax.broadcasted_iota(jnp.int32, (group * qr, nk), 0)
    cols = lax.broadcasted_iota(jnp.int32, (group * qr, nk), 1)
    head_of_row = rows // qr
    diff = (rows - head_of_row * qr) - cols + w
    mask = (diff >= 0) & (diff <= w)
    if first_valid_key is not None:
        blk = n + block_offset
        mask = mask & (cols >= first_valid_key + w - blk * w)

    prev = prev_ref[...]
    cur = cur_ref[...]
    q = q_ref[...]
    row_head = lax.broadcasted_iota(jnp.int32, (group * qr, 1), 0) // qr
    for kh in range(2):
        k = jnp.concatenate([prev[:, kh * hd:(kh + 1) * hd], cur[:, kh * hd:(kh + 1) * hd]], axis=0).astype(BF16)
        v = jnp.concatenate([prev[:, LANES + kh * hd:LANES + (kh + 1) * hd],
                             cur[:, LANES + kh * hd:LANES + (kh + 1) * hd]], axis=0).astype(BF16)
        qs = jnp.concatenate([q[:, (group * kh + g) * hd:(group * kh + g + 1) * hd] for g in range(group)], axis=0)
        s = lax.dot_general(qs, k, (((1,), (1,)), ((), ())), preferred_element_type=F32) * (hd ** -0.5)
        s = jnp.where(mask, s, NEG_INF)
        sink = jnp.zeros((group * qr, 1), F32)
        for g in range(group):
            sink = jnp.where(row_head == g, sink_ref[group * kh + g], sink)
        m = jnp.maximum(jnp.max(s, axis=-1, keepdims=True), sink)
        e = jnp.exp(s - m)
        p = e / (jnp.sum(e, axis=-1, keepdims=True) + jnp.exp(sink - m))
        o = jnp.dot(p.astype(BF16), v, preferred_element_type=F32)
        for g in range(group):
            h = group * kh + g
            o_ref[:, h * hd:(h + 1) * hd] = o[g * qr:(g + 1) * qr].astype(BF16)


def _attention(sinks, q, kv_prev, kv_cur, nb, nblk, qr, q_map, prev_map, cur_map, out_map, out_rows,
               first_valid_key, block_offset):
    w = ATTN_BLOCK
    return pl.pallas_call(
        functools.partial(_attn_kernel, first_valid_key=first_valid_key, block_offset=block_offset),
        grid=(nb, nblk),
        in_specs=[
            pl.BlockSpec(memory_space=pltpu.SMEM),
            pl.BlockSpec((qr, 512), q_map),
            pl.BlockSpec((w, 256), prev_map),
            pl.BlockSpec((qr, 256), cur_map),
        ],
        out_specs=pl.BlockSpec((qr, 512), out_map),
        out_shape=jax.ShapeDtypeStruct((out_rows, 512), BF16),
        compiler_params=_cparams(("arbitrary", "arbitrary")),
        name="swa",
    )(sinks, q, kv_prev, kv_cur)


def _gla_kernel(gl_ref, s0_ref, gn_ref, yb_ref, sfin_ref, st_ref):
    c = pl.program_id(1)
    ch = gl_ref.shape[0]
    nh, dk, dv = 4, 64, 128

    @pl.when(c == 0)
    def _():
        for h in range(nh):
            st_ref[h] = s0_ref[0, h].T

    gl = gl_ref[...]
    q = gl[:, 0:256]
    k = gl[:, 256:512]
    b = gl[:, 512:768]
    row = lax.broadcasted_iota(jnp.int32, (ch, nh * dk), 0)
    sh = 1
    while sh < ch:
        b = b + jnp.where(row >= sh, pltpu.roll(b, sh, 0), 0.0)
        sh *= 2
    b_last = b[ch - 1:ch, :]
    q_t = (q * jnp.exp(b)).astype(BF16)
    k_t = (k * jnp.exp(-b)).astype(BF16)
    k_end = (k * jnp.exp(b_last - b)).astype(BF16)
    decay = jnp.exp(b_last)
    causal = (lax.broadcasted_iota(jnp.int32, (ch, ch), 0) >= lax.broadcasted_iota(jnp.int32, (ch, ch), 1))
    gn = gn_ref[...]
    nt = (((1,), (1,)), ((), ()))
    for h in range(nh):
        ks = slice(h * dk, (h + 1) * dk)
        v = gl[:, 768 + h * dv:768 + (h + 1) * dv]
        vb = v.astype(BF16)
        a = lax.dot_general(q_t[:, ks], k_t[:, ks], nt, preferred_element_type=F32)
        a = jnp.where(causal, a, 0.0)
        s_t = st_ref[h]
        o = jnp.dot(a.astype(BF16), vb, preferred_element_type=F32)
        o = o + lax.dot_general(q_t[:, ks], s_t.astype(BF16), nt, preferred_element_type=F32)
        upd = jnp.dot(v.T.astype(BF16), k_end[:, ks], preferred_element_type=F32)
        st_ref[h] = s_t * decay[:, ks] + upd
        go = gl[:, 1280 + h * dv:1280 + (h + 1) * dv]
        y = _rms(o, gn) * (go * jax.nn.sigmoid(go))
        yb_ref[:, h * dv:(h + 1) * dv] = y.astype(BF16)

    @pl.when(c == pl.num_programs(1) - 1)
    def _():
        for h in range(nh):
            sfin_ref[0, h] = st_ref[h].T


def _gla(gl, s0, gn, nb, nchunks, ch, in_map, out_map, out_rows):
    return pl.pallas_call(
        _gla_kernel,
        grid=(nb, nchunks),
        in_specs=[
            pl.BlockSpec((ch, 1792), in_map),
            pl.BlockSpec((1, 4, 64, 128), lambda b, c: (b, 0, 0, 0)),
            pl.BlockSpec((1, 128), lambda b, c: (0, 0)),
        ],
        out_specs=[
            pl.BlockSpec((ch, 512), out_map),
            pl.BlockSpec((1, 4, 64, 128), lambda b, c: (b, 0, 0, 0)),
        ],
        out_shape=[
            jax.ShapeDtypeStruct((out_rows, 512), BF16),
            jax.ShapeDtypeStruct((nb, 4, 64, 128), F32),
        ],
        scratch_shapes=[pltpu.VMEM((4, 128, 64), F32)],
        compiler_params=_cparams(("arbitrary", "arbitrary")),
        name="gla",
    )(gl, s0, gn)


def _extract_topk(work, nsel, iota0, sentinel):
    vals, idxs = [], []
    for j in range(nsel):
        m = jnp.max(work, axis=0, keepdims=True)
        idx = jnp.min(jnp.where(work == m, iota0, sentinel), axis=0, keepdims=True)
        vals.append(m)
        idxs.append(idx)
        if j + 1 < nsel:
            work = jnp.where(iota0 == idx, -jnp.inf, work)
    return vals, idxs


def _merge_kernel(x_ref, gt_ref, ya_ref, yb_ref, wa_ref, wb_ref, wo_ref, gf_ref, wq_ref, keys_ref,
                  xm_ref, hn_ref, et_ref, wt_ref):
    td = x_ref.shape[0]
    nkeys = keys_ref.shape[1]
    half = keys_ref.shape[2]
    nheads = keys_ref.shape[0] // 2
    topk = PEER_TOPK

    gt = gt_ref[...]
    d = x_ref.shape[1]
    ma = jnp.dot(ya_ref[...], wa_ref[...], preferred_element_type=F32)
    mb = jnp.dot(yb_ref[...], wb_ref[...], preferred_element_type=F32)
    m = jax.nn.sigmoid(gt[:, 0:d]) * ma + jax.nn.sigmoid(gt[:, d:2 * d]) * mb
    xm = x_ref[...] + jnp.dot(m.astype(BF16), wo_ref[...], preferred_element_type=F32)
    xm_ref[...] = xm
    hn = _rms(xm, gf_ref[...])
    hn_ref[...] = hn
    q = jnp.dot(hn.astype(BF16), wq_ref[...], preferred_element_type=F32).astype(BF16)

    nt = (((1,), (1,)), ((), ()))
    iota_k = lax.broadcasted_iota(jnp.int32, (nkeys, td), 0)
    iota_c = lax.broadcasted_iota(jnp.int32, (topk * topk, td), 0)
    wts, ids = [], []
    for h in range(nheads):
        sv, si = [], []
        for c in range(2):
            gi = 2 * h + c
            s_t = lax.dot_general(keys_ref[gi], q[:, gi * half:(gi + 1) * half], nt,
                                  preferred_element_type=F32)
            vals, idxs = _extract_topk(s_t, topk, iota_k, nkeys)
            sv.append(vals)
            si.append(idxs)
        sv1 = jnp.concatenate(sv[1], axis=0)
        si1 = jnp.concatenate(si[1], axis=0)
        cand = jnp.concatenate([sv[0][a] + sv1 for a in range(topk)], axis=0)
        cidx = jnp.concatenate([si[0][a] * nkeys + si1 for a in range(topk)], axis=0)
        fvals, eids = [], []
        work = cand
        for j in range(topk):
            mx = jnp.max(work, axis=0, keepdims=True)
            pos = jnp.min(jnp.where(work == mx, iota_c, topk * topk), axis=0, keepdims=True)
            hit = iota_c == pos
            eids.append(jnp.max(jnp.where(hit, cidx, -1), axis=0, keepdims=True))
            fvals.append(mx)
            if j + 1 < topk:
                work = jnp.where(hit, -jnp.inf, work)
        fv = jnp.concatenate(fvals, axis=0)
        e = jnp.exp(fv - fvals[0])
        wts.append(e / jnp.sum(e, axis=0, keepdims=True))
        ids.extend(eids)
    wt_ref[...] = jnp.concatenate(wts, axis=0).T
    et_ref[...] = jnp.concatenate(ids, axis=0).T


def _merge_route(x, gt, ya, yb, wa, wb, wo, gf, wq, keys, gt_map):
    t, d = x.shape
    td = MERGE_ROWS
    nsel = (keys.shape[0] // 2) * PEER_TOPK
    const2 = lambda i: (0, 0)
    row = lambda i: (i, 0)
    return pl.pallas_call(
        _merge_kernel,
        grid=(t // td,),
        in_specs=[
            pl.BlockSpec((td, d), row),
            pl.BlockSpec((td, 2 * d), gt_map),
            pl.BlockSpec((td, ya.shape[1]), row),
            pl.BlockSpec((td, yb.shape[1]), row),
            pl.BlockSpec(wa.shape, const2),
            pl.BlockSpec(wb.shape, const2),
            pl.BlockSpec(wo.shape, const2),
            pl.BlockSpec(gf.shape, const2),
            pl.BlockSpec(wq.shape, const2),
            pl.BlockSpec(keys.shape, lambda i: (0, 0, 0)),
        ],
        out_specs=[
            pl.BlockSpec((td, d), row),
            pl.BlockSpec((td, d), row),
            pl.BlockSpec((td, nsel), row),
            pl.BlockSpec((td, nsel), row),
        ],
        out_shape=[
            jax.ShapeDtypeStruct((t, d), F32),
            jax.ShapeDtypeStruct((t, d), F32),
            jax.ShapeDtypeStruct((t, nsel), jnp.int32),
            jax.ShapeDtypeStruct((t, nsel), F32),
        ],
        compiler_params=_cparams(("arbitrary",)),
        name="merge_route",
    )(x, gt, ya, yb, wa, wb, wo, gf, wq, keys)


def _coef_kernel(act_ref, wt_ref, o_ref):
    act = act_ref[...]
    gelu = 0.5 * act * (1.0 + lax.erf(act * (2.0 ** -0.5)))
    o_ref[...] = wt_ref[...] * gelu


def _expert_coefs(act, wt):
    t, nsel = wt.shape
    rows = math.gcd(t, 512)
    row = lambda i: (i, 0)
    return pl.pallas_call(
        _coef_kernel,
        grid=(t // rows,),
        in_specs=[pl.BlockSpec((rows, nsel), row), pl.BlockSpec((rows, nsel), row)],
        out_specs=pl.BlockSpec((rows, nsel), row),
        out_shape=jax.ShapeDtypeStruct((t, nsel), F32),
        compiler_params=_cparams(("arbitrary",)),
        name="expert_coefs",
    )(act, wt)


def _finish_kernel(xm_ref, o_ref, gfin_ref, yacc_hbm, y_ref):
    del yacc_hbm
    y_ref[...] = _rms(xm_ref[...] + o_ref[...], gfin_ref[...])


def _finish(xm, o, gfin, y_acc, row_off):
    t, d = xm.shape
    rows = math.gcd(t, 512)
    first = row_off // rows
    row = lambda i: (i, 0)
    return pl.pallas_call(
        _finish_kernel,
        grid=(t // rows,),
        in_specs=[pl.BlockSpec((rows, d), row), pl.BlockSpec((rows, d), row),
                  pl.BlockSpec(gfin.shape, lambda i: (0, 0)), pl.BlockSpec(memory_space=pl.ANY)],
        out_specs=pl.BlockSpec((rows, d), lambda i: (i + first, 0)),
        out_shape=jax.ShapeDtypeStruct(y_acc.shape, F32),
        input_output_aliases={3: 0},
        compiler_params=_cparams(("arbitrary",)),
        name="finish",
    )(xm, o, gfin, y_acc)


def _rope_table(pos, valid, rope_dim, head_dim):
    half = rope_dim // 2
    inv = ROPE_THETA ** (-jnp.arange(0, rope_dim, 2, dtype=F32) / rope_dim)
    ang = pos.astype(F32)[:, None] * inv[None, :]
    cos, sin = jnp.cos(ang), jnp.sin(ang)
    n = pos.shape[0]
    ones = jnp.ones((n, head_dim - rope_dim), F32)
    zeros_h = jnp.zeros((n, half), F32)
    zeros_r = jnp.zeros((n, head_dim - rope_dim), F32)
    reps = LANES // head_dim
    cosf = jnp.tile(jnp.concatenate([cos, cos, ones], axis=1), (1, reps))
    sin_lo = jnp.tile(jnp.concatenate([zeros_h, sin, zeros_r], axis=1), (1, reps))
    sin_hi = jnp.tile(jnp.concatenate([-sin, zeros_h, zeros_r], axis=1), (1, reps))
    vcol = jnp.broadcast_to(valid.astype(F32)[:, None], (n, LANES))
    return jnp.concatenate([cosf, sin_lo, sin_hi, vcol], axis=1)


def _pack_table(tab):
    half = tab.shape[1] // 2
    lo = lax.bitcast_convert_type(tab[:, :half].astype(BF16), jnp.uint16).astype(jnp.uint32)
    xb = lax.bitcast_convert_type(tab[:, half:], jnp.uint32)
    sign = xb & jnp.uint32(0x80000000)
    mag = xb & jnp.uint32(0x7FFFFFFF)
    hi = (jnp.maximum(mag + jnp.uint32(0x8000), lo) - lo) >> 16
    return lax.bitcast_convert_type(sign | (hi << 16) | lo, jnp.int32)


def _expert_dots(table, idx, hn, nsel):
    n = idx.shape[0]
    c = table.shape[1]
    d = hn.shape[1]
    sc = plsc.get_sparse_core_info()
    lanes = sc.num_lanes
    workers = sc.num_cores * sc.num_subcores
    gw = GATHER_WINDOW
    ich = GATHER_INDEX_CHUNK
    tok = ich // nsel
    per_worker = n // workers
    assert n % workers == 0 and per_worker % ich == 0 and nsel == 2 * gw and d == 2 * c and gw % lanes == 0
    mesh = plsc.VectorSubcoreMesh(core_axis_name="c", subcore_axis_name="s")

    @functools.partial(
        pl.kernel, out_type=jax.ShapeDtypeStruct((n,), F32), mesh=mesh, name="expert_dots",
        compiler_params=pltpu.CompilerParams(needs_layout_passes=False),
        scratch_types=[pltpu.VMEM((ich,), jnp.int32),
                       pltpu.VMEM((gw, c), jnp.int32), pltpu.VMEM((gw, c), jnp.int32),
                       pltpu.VMEM((tok, d), F32), pltpu.VMEM((ich,), F32), pltpu.VMEM((lanes * lanes,), F32),
                       pltpu.SemaphoreType.DMA, pltpu.SemaphoreType.DMA])
    def dots(tab_hbm, idx_hbm, hn_hbm, act_hbm, idx_v, rows0, rows1, h_v, act_v, scr, sem0, sem1):
        wid = lax.axis_index("s") * sc.num_cores + lax.axis_index("c")
        base = wid * per_worker
        lane = lax.iota(jnp.int32, lanes)

        def gather(win, buf, sem):
            return pltpu.make_async_copy(tab_hbm.at[idx_v.at[pl.ds(win * gw, gw)]], buf, sem)

        def reduce_window(buf, t_loc, out_off):
            for rb in range(gw // lanes):
                def kbody(k, accs):
                    h_lo = h_v[t_loc, pl.ds(k * lanes, lanes)]
                    h_hi = h_v[t_loc, pl.ds(c + k * lanes, lanes)]
                    out = []
                    for r in range(lanes):
                        w = buf[rb * lanes + r, pl.ds(k * lanes, lanes)]
                        lo = lax.bitcast_convert_type(w << 16, F32)
                        hi = lax.bitcast_convert_type(w, F32)
                        out.append(accs[r] + lo * h_lo + hi * h_hi)
                    return tuple(out)

                accs = lax.fori_loop(0, c // lanes, kbody,
                                     tuple(jnp.zeros((lanes,), F32) for _ in range(lanes)))
                for r in range(lanes):
                    scr[pl.ds(r * lanes, lanes)] = accs[r]
                tot = plsc.load_gather(scr, [lane * lanes])
                for l in range(1, lanes):
                    tot = tot + plsc.load_gather(scr, [lane * lanes + l])
                act_v[pl.ds(out_off + rb * lanes, lanes)] = tot

        @pl.loop(0, per_worker // ich)
        def _(g):
            cb = base + g * ich
            pltpu.sync_copy(idx_hbm.at[pl.ds(cb, ich)], idx_v)
            tok_base = pl.multiple_of(wid * (per_worker // nsel) + g * tok, tok)
            pltpu.sync_copy(hn_hbm.at[pl.ds(tok_base, tok)], h_v)
            gather(0, rows0, sem0).start()
            gather(1, rows1, sem1).start()

            @pl.loop(0, tok)
            def _(j):
                gather(2 * j, rows0, sem0).wait()
                reduce_window(rows0, j, j * nsel)

                @pl.when(j + 1 < tok)
                def _():
                    gather(2 * j + 2, rows0, sem0).start()

                gather(2 * j + 1, rows1, sem1).wait()
                reduce_window(rows1, j, j * nsel + gw)

                @pl.when(j + 1 < tok)
                def _():
                    gather(2 * j + 3, rows1, sem1).start()

            pltpu.sync_copy(act_v, act_hbm.at[pl.ds(cb, ich)])

    return dots(table, idx, hn)


def _expert_mix(table, idx, coef, nsel):
    n = idx.shape[0]
    c = table.shape[1]
    d = 2 * c
    sc = plsc.get_sparse_core_info()
    lanes = sc.num_lanes
    workers = sc.num_cores * sc.num_subcores
    gw = GATHER_WINDOW
    ich = GATHER_INDEX_CHUNK
    tok = ich // nsel
    per_worker = n // workers
    kblock = 8
    assert n % workers == 0 and per_worker % ich == 0 and nsel == 2 * gw and c % (kblock * lanes) == 0
    mesh = plsc.VectorSubcoreMesh(core_axis_name="c", subcore_axis_name="s")

    @functools.partial(
        pl.kernel, out_type=jax.ShapeDtypeStruct((n // nsel, d), F32), mesh=mesh, name="expert_mix",
        compiler_params=pltpu.CompilerParams(needs_layout_passes=False),
        scratch_types=[pltpu.VMEM((ich,), jnp.int32), pltpu.VMEM((ich,), F32),
                       pltpu.VMEM((gw, c), jnp.int32), pltpu.VMEM((gw, c), jnp.int32),
                       pltpu.VMEM((tok, d), F32),
                       pltpu.SemaphoreType.DMA, pltpu.SemaphoreType.DMA])
    def mix(tab_hbm, idx_hbm, coef_hbm, out_hbm, idx_v, coef_v, rows0, rows1, out_v, sem0, sem1):
        wid = lax.axis_index("s") * sc.num_cores + lax.axis_index("c")
        base = wid * per_worker
        zero_idx = jnp.zeros((lanes,), jnp.int32)

        def gather(win, buf, sem):
            return pltpu.make_async_copy(tab_hbm.at[idx_v.at[pl.ds(win * gw, gw)]], buf, sem)

        def accumulate_window(buf, t_loc, coef_off, first):
            for kb in range(c // (kblock * lanes)):
                col0 = kb * kblock * lanes
                if first:
                    init = tuple(jnp.zeros((lanes,), F32) for _ in range(2 * kblock))
                else:
                    init = tuple(out_v[t_loc, pl.ds(col0 + i * lanes, lanes)] for i in range(kblock)) + \
                           tuple(out_v[t_loc, pl.ds(c + col0 + i * lanes, lanes)] for i in range(kblock))

                def rbody(r2, accs):
                    accs = list(accs)
                    for rr in range(2):
                        r = 2 * r2 + rr
                        cvec = plsc.load_gather(coef_v, [zero_idx + (coef_off + r)])
                        for i in range(kblock):
                            w = buf[r, pl.ds(col0 + i * lanes, lanes)]
                            lo = lax.bitcast_convert_type(w << 16, F32)
                            hi = lax.bitcast_convert_type(w, F32)
                            accs[i] = accs[i] + lo * cvec
                            accs[kblock + i] = accs[kblock + i] + hi * cvec
                    return tuple(accs)

                accs = lax.fori_loop(0, gw // 2, rbody, init)
                for i in range(kblock):
                    out_v[t_loc, pl.ds(col0 + i * lanes, lanes)] = accs[i]
                    out_v[t_loc, pl.ds(c + col0 + i * lanes, lanes)] = accs[kblock + i]

        @pl.loop(0, per_worker // ich)
        def _(g):
            cb = base + g * ich
            pltpu.sync_copy(idx_hbm.at[pl.ds(cb, ich)], idx_v)
            pltpu.sync_copy(coef_hbm.at[pl.ds(cb, ich)], coef_v)
            gather(0, rows0, sem0).start()
            gather(1, rows1, sem1).start()

            @pl.loop(0, tok)
            def _(j):
                gather(2 * j, rows0, sem0).wait()
                accumulate_window(rows0, j, j * nsel, True)

                @pl.when(j + 1 < tok)
                def _():
                    gather(2 * j + 2, rows0, sem0).start()

                gather(2 * j + 1, rows1, sem1).wait()
                accumulate_window(rows1, j, j * nsel + gw, False)

                @pl.when(j + 1 < tok)
                def _():
                    gather(2 * j + 3, rows1, sem1).start()

            tok_base = pl.multiple_of(wid * (per_worker // nsel) + g * tok, tok)
            pltpu.sync_copy(out_v, out_hbm.at[pl.ds(tok_base, tok)])

    return mix(table, idx, coef)


def _experts_tc_kernel(idx_first_ref, idx_next_ref, xm_ref, hn_ref, wt_ref, gfin_ref, uv_hbm, yacc_hbm,
                       y_ref, buf, sem):
    del yacc_hbm
    i = pl.program_id(0)
    n = pl.num_programs(0)
    tg, nsel = wt_ref.shape
    rows = tg * nsel
    dh = buf.shape[2] // 2
    sel_bits = nsel.bit_length() - 1
    assert nsel == 1 << sel_bits

    def start_rows(idx_ref, slot):
        def body(jj, carry):
            for p in range(2):
                j = 2 * jj + p
                e = idx_ref[lax.shift_right_logical(j, sel_bits), j & (nsel - 1)]
                pltpu.make_async_copy(uv_hbm.at[pl.ds(e, 1)], buf.at[slot, pl.ds(j, 1)],
                                      sem.at[slot]).start(priority=p)
            return carry
        lax.fori_loop(0, rows // 2, body, 0, unroll=4)

    @pl.when(i == 0)
    def _():
        start_rows(idx_first_ref, 0)

    @pl.when(i + 1 < n)
    def _():
        start_rows(idx_next_ref, (i + 1) % 2)

    slot = i % 2
    pltpu.make_async_copy(uv_hbm.at[pl.ds(0, rows)], buf.at[slot], sem.at[slot]).wait()

    def unpack(w):
        return pltpu.bitcast(w << 16, F32), pltpu.bitcast(w, F32)

    wc = wt_ref[...].T
    out_rows = []
    for t in range(tg):
        tok = slice(t * nsel, (t + 1) * nsel)
        u_lo, u_hi = unpack(buf[slot, tok, 0:dh])
        h = hn_ref[t:t + 1, :]
        act = jnp.sum(u_lo * h[:, :dh] + u_hi * h[:, dh:], axis=1, keepdims=True)
        gelu = 0.5 * act * (1.0 + lax.erf(act * (2.0 ** -0.5)))
        coef = wc[:, t:t + 1] * gelu
        v_lo, v_hi = unpack(buf[slot, tok, dh:2 * dh])
        out_rows.append(jnp.concatenate([jnp.sum(v_lo * coef, axis=0, keepdims=True),
                                         jnp.sum(v_hi * coef, axis=0, keepdims=True)], axis=1))
    y_ref[...] = _rms(xm_ref[...] + jnp.concatenate(out_rows, axis=0), gfin_ref[...])


def _experts_tc(xm, hn, et, wt, uv_tab, gfin, y_acc, row_off):
    t, d = xm.shape
    nsel = et.shape[1]
    tg = TC_EXPERT_TOKENS
    n = t // tg
    first = row_off // tg
    row = lambda i: (i, 0)
    in_specs = [
        pl.BlockSpec((tg, nsel), lambda i: (0, 0), memory_space=pltpu.SMEM),
        pl.BlockSpec((tg, nsel), lambda i: (jnp.minimum(i + 1, n - 1), 0), memory_space=pltpu.SMEM),
        pl.BlockSpec((tg, d), row),
        pl.BlockSpec((tg, d), row),
        pl.BlockSpec((tg, nsel), row),
        pl.BlockSpec(gfin.shape, lambda i: (0, 0)),
        pl.BlockSpec(memory_space=pl.ANY),
        pl.BlockSpec(memory_space=pl.ANY),
    ]
    return pl.pallas_call(
        _experts_tc_kernel,
        grid=(n,),
        in_specs=in_specs,
        out_specs=pl.BlockSpec((tg, d), lambda i: (i + first, 0)),
        out_shape=jax.ShapeDtypeStruct(y_acc.shape, F32),
        scratch_shapes=[pltpu.VMEM((2, tg * nsel, d), jnp.int32), pltpu.SemaphoreType.DMA((2,))],
        input_output_aliases={7: 0},
        compiler_params=_cparams(("arbitrary",)),
        name="experts_tc",
    )(et, et, xm, hn, wt, gfin, uv_tab, y_acc)


def _peer_tail_tc(xm, hn, et, wt, tabs, gfin, y_acc, row_off):
    return _experts_tc(xm, hn, et, wt, tabs[2], gfin, y_acc, row_off)


def _peer_tail(xm, hn, et, wt, tabs, gfin, y_acc, row_off):
    u_tab, v_tab, _ = tabs
    t, nsel = et.shape
    eidx = et.reshape(t * nsel)
    act = _expert_dots(u_tab, eidx, hn, nsel).reshape(t, nsel)
    coef = _expert_coefs(act, wt).reshape(t * nsel)
    mixed = _expert_mix(v_tab, eidx, coef, nsel)
    return _finish(xm, mixed, gfin, y_acc, row_off)


def kernel(x_prompt, x_sample, cache_k_window, cache_v_window, state_gla, meta_tokens, g_norm_mix, w_in,
           w_gate_up, b_gate, attn_sinks, g_gla_norm, w_branch_a, w_branch_b, w_out, g_norm_ffn, w_peer_q,
           peer_sub_keys, peer_u, peer_v, g_norm_final):
    bsz, seq, d = x_prompt.shape
    dbsz, tdec, _ = x_sample.shape
    n_meta = meta_tokens.shape[0]
    depth = w_in.shape[0]
    window = cache_k_window.shape[2]
    kv_heads, head_dim = cache_k_window.shape[3], cache_k_window.shape[4]
    gate_rank = w_gate_up.shape[1]
    bqk = w_gate_up.shape[2]
    n_ph, _, n_keys, p_half = peer_sub_keys.shape[1:]
    assert depth == 1 and d == 1024 and window == ATTN_BLOCK and kv_heads == 2 and head_dim == 64
    assert bqk == 256 and state_gla.shape[2:] == (4, 64, 128) and n_meta <= ATTN_BLOCK
    assert seq % ATTN_BLOCK == 0 and tdec <= SAMPLE_PAD and n_keys == 128 and p_half == 64 and n_ph == 8
    rope_dim = head_dim // 4
    meta_pad = ATTN_BLOCK - n_meta
    lp = ATTN_BLOCK + seq
    nblk = lp // ATTN_BLOCK

    w = w_in[0]
    c_lr = 2304
    c_gate = c_lr + gate_rank
    w1 = w[:, :c_lr].astype(BF16)
    wlr = jnp.pad(w[:, c_lr:c_gate], ((0, 0), (0, LANES - gate_rank))).astype(BF16)
    w2 = w[:, c_gate:].astype(BF16)
    wgu = jnp.pad(w_gate_up[0], ((0, LANES - gate_rank), (0, 0))).astype(BF16)
    bg = b_gate[0][None, :]
    gmix = g_norm_mix[0][None, :]
    wa = w_branch_a[0].astype(BF16)
    wb = w_branch_b[0].astype(BF16)
    wo = w_out[0].astype(BF16)
    gffn = g_norm_ffn[0][None, :]
    wq = w_peer_q[0].astype(BF16)
    keys = peer_sub_keys[0].reshape(n_ph * 2, n_keys, p_half).astype(BF16)
    u_tab = _pack_table(peer_u[0])
    v_tab = _pack_table(peer_v[0])
    tabs = (u_tab, v_tab, jnp.concatenate([u_tab, v_tab], axis=1))
    gfin = g_norm_final[None, :]
    gn = g_gla_norm[0][None, :]
    sinks = attn_sinks[0]
    qk_scale = float(bqk // 4) ** -0.5

    rows_p = jnp.arange(lp)
    tab_p = _rope_table(rows_p - meta_pad, rows_p >= meta_pad, rope_dim, head_dim)
    proj_rows = max(r for r in range(16, PROJ_ROWS + 1, 16) if lp % r == 0)
    nq = nblk - 1
    nchunks = lp // GLA_CHUNK
    skip = ATTN_BLOCK // GLA_CHUNK
    ncq = nchunks - skip
    per_seq = seq // MERGE_ROWS
    gt_map_p = lambda i: ((i // per_seq) * nblk + 1 + (i % per_seq), 0)

    def prompt_sequences(xg, y_acc, row_off, experts):
        gb = xg.shape[0]
        meta = jnp.broadcast_to(meta_tokens[None].astype(xg.dtype), (gb, n_meta, d))
        xpad = jnp.concatenate([jnp.zeros((gb, meta_pad, d), xg.dtype), meta, xg], axis=1).reshape(gb * lp, d)
        qa, kv, gl, gt = _project(xpad, gmix, tab_p, w1, wlr, wgu, bg, w2, proj_rows, qk_scale)
        ya = _attention(
            sinks, qa, kv, kv, gb, nq, ATTN_BLOCK,
            lambda b, n: (b * nblk + n + 1, 0), lambda b, n: (b * nblk + n, 0), lambda b, n: (b * nblk + n + 1, 0),
            lambda b, n: (b * nq + n, 0), gb * seq, first_valid_key=meta_pad, block_offset=1)
        s0 = jnp.zeros((gb,) + state_gla.shape[2:], F32)
        yb, s_fin = _gla(gl, s0, gn, gb, nchunks, GLA_CHUNK,
                         lambda b, c: (b * nchunks + c, 0),
                         lambda b, c: (b * ncq + jnp.maximum(c - skip, 0), 0), gb * seq)
        xm, hn, et, wt = _merge_route(xg.reshape(gb * seq, d), gt, ya, yb, wa, wb, wo, gffn, wq, keys,
                                      gt_map_p)
        y_acc = experts(xm, hn, et, wt, tabs, gfin, y_acc, row_off)
        kv_w = kv.reshape(gb, lp, 2, kv_heads, head_dim)[:, lp - window:]
        return y_acc, kv_w, s_fin

    group = PROMPT_GROUP if (bsz % PROMPT_GROUP == 0 and (PROMPT_GROUP * seq) % GATHER_ROW_QUANTUM == 0) else bsz
    starts = list(range(0, bsz, group))
    tc_starts = starts[len(starts) - TC_EXPERT_GROUPS:] if len(starts) > TC_EXPERT_GROUPS else []
    y_acc, kv_parts, s_parts = jnp.zeros((bsz * seq, d), F32), {}, {}
    for b0 in starts:
        experts = _peer_tail_tc if b0 in tc_starts else _peer_tail
        y_acc, kv_parts[b0], s_parts[b0] = prompt_sequences(x_prompt[b0:b0 + group], y_acc, b0 * seq, experts)
    y_prompt = y_acc.reshape(bsz, seq, d)
    kv_p = jnp.concatenate([kv_parts[b0] for b0 in starts], axis=0)
    s_fin_p = jnp.concatenate([s_parts[b0] for b0 in starts], axis=0)
    new_k_p = kv_p[:, :, 0][None]
    new_v_p = kv_p[:, :, 1][None]

    sp = SAMPLE_PAD
    xs_pad = jnp.pad(x_sample, ((0, 0), (0, sp - tdec), (0, 0))).reshape(dbsz * sp, d)
    rows_s = jnp.arange(sp)
    reps = 256 // sp
    tab_s = jnp.tile(_rope_table(PAST_LEN + rows_s, rows_s < tdec, rope_dim, head_dim), (reps, 1))
    qa_s, kv_s, gl_s, gt_s = _project(xs_pad, gmix, tab_s, w1, wlr, wgu, bg, w2, 256, qk_scale)

    cache_kv = jnp.concatenate([cache_k_window[0].reshape(dbsz * window, kv_heads * head_dim),
                                cache_v_window[0].reshape(dbsz * window, kv_heads * head_dim)], axis=1)
    seq_map = lambda b, n: (b, 0)
    ya_s = _attention(sinks, qa_s, cache_kv, kv_s, dbsz, 1, sp, seq_map, seq_map, seq_map, seq_map,
                      dbsz * sp, first_valid_key=None, block_offset=0)
    yb_s, s_fin_s = _gla(gl_s, state_gla[0], gn, dbsz, 1, sp, seq_map, seq_map, dbsz * sp)

    def real_rows(a):
        return a.reshape(dbsz, sp, a.shape[-1])[:, :tdec].reshape(dbsz * tdec, a.shape[-1])

    xs_rows = x_sample.reshape(dbsz * tdec, d)
    xm_s, hn_s, et_s, wt_s = _merge_route(xs_rows, real_rows(gt_s), real_rows(ya_s), real_rows(yb_s),
                                          wa, wb, wo, gffn, wq, keys, lambda i: (i, 0))
    y_sample = _peer_tail(xm_s, hn_s, et_s, wt_s, tabs, gfin, jnp.zeros((dbsz * tdec, d), F32), 0).reshape(dbsz, tdec, d)

    kv_new = real_rows(kv_s).reshape(dbsz, tdec, 2, kv_heads, head_dim)
    new_k_s = jnp.concatenate([cache_k_window[0].astype(F32), kv_new[:, :, 0]], axis=1)[:, -window:][None]
    new_v_s = jnp.concatenate([cache_v_window[0].astype(F32), kv_new[:, :, 1]], axis=1)[:, -window:][None]

    return (y_prompt, y_sample, new_k_p, new_v_p, s_fin_p[None], new_k_s, new_v_s, s_fin_s[None])
```

```python
import functools
import math

import jax
import jax.numpy as jnp
from jax import lax
from jax.experimental import pallas as pl
from jax.experimental.pallas import tpu as pltpu
from jax.experimental.pallas import tpu_sc as plsc

F32 = jnp.float32
BF16 = jnp.bfloat16

EPS = 1e-6
NEG_INF = -1e30
PAST_LEN = 16384
ROPE_THETA = 500000.0
GATE_NORMALIZER = 16.0
PEER_TOPK = 16

LANES = 128
SUBLANES = 8
VMEM_LIMIT_BYTES = 56 * 1024 * 1024

ATTN_BLOCK = 128
GLA_CHUNK = 64
SAMPLE_PAD = 16
PROJ_ROWS = 544
MERGE_ROWS = 128
GATHER_WINDOW = 64
GATHER_INDEX_CHUNK = 2048
GATHER_ROW_QUANTUM = 512
PROMPT_GROUP = 1


def _cparams(sem):
    return pltpu.CompilerParams(dimension_semantics=sem, vmem_limit_bytes=VMEM_LIMIT_BYTES)


def _rms(x, g):
    ms = jnp.mean(x * x, axis=-1, keepdims=True)
    return (x * lax.rsqrt(ms + EPS)) * g


def _proj_kernel(x_ref, g_ref, tab_ref, w1_ref, wlr_ref, wgu_ref, bg_ref, w2_ref,
                 qa_ref, kv_ref, gl_ref, gt_ref, *, period, qk_scale):
    i = pl.program_id(0)
    tr = x_ref.shape[0]
    hb = _rms(x_ref[...], g_ref[...]).astype(BF16)
    z1 = jnp.dot(hb, w1_ref[...], preferred_element_type=F32)

    start = pl.multiple_of((i * tr) % period, SUBLANES)
    tab = tab_ref[pl.ds(start, tr), :]
    cosf = tab[:, 0:LANES]
    sin_lo = tab[:, LANES:2 * LANES]
    sin_hi = tab[:, 2 * LANES:3 * LANES]
    valid = tab[:, 3 * LANES:3 * LANES + 1]

    def rope(xg):
        return xg * cosf + pltpu.roll(xg, 8, 1) * sin_lo + pltpu.roll(xg, LANES - 8, 1) * sin_hi

    for gi in range(4):
        sl = slice(gi * LANES, (gi + 1) * LANES)
        qa_ref[:, sl] = rope(z1[:, sl]).astype(BF16)
    kv_ref[:, 0:LANES] = rope(z1[:, 512:640])
    kv_ref[:, LANES:2 * LANES] = z1[:, 640:768]

    lr = jnp.dot(hb, wlr_ref[...], preferred_element_type=F32)
    pre = jnp.dot(lr.astype(BF16), wgu_ref[...], preferred_element_type=F32) + bg_ref[...]
    log_sig = jnp.minimum(pre, 0.0) - jnp.log1p(jnp.exp(-jnp.abs(pre)))
    ld = jnp.where(valid > 0.5, log_sig / GATE_NORMALIZER, 0.0)

    gl_ref[:, 0:256] = z1[:, 768:1024] * qk_scale
    gl_ref[:, 256:512] = z1[:, 1024:1280]
    gl_ref[:, 512:768] = ld
    gl_ref[:, 768:1792] = z1[:, 1280:2304]
    gt_ref[...] = jnp.dot(hb, w2_ref[...], preferred_element_type=F32)


def _project(x, g, tab, w1, wlr, wgu, bg, w2, rows, qk_scale):
    r, d = x.shape
    period = tab.shape[0]
    const = lambda i: (0, 0)
    row = lambda i: (i, 0)
    return pl.pallas_call(
        functools.partial(_proj_kernel, period=period, qk_scale=qk_scale),
        grid=(r // rows,),
        in_specs=[
            pl.BlockSpec((rows, d), row),
            pl.BlockSpec(g.shape, const),
            pl.BlockSpec(tab.shape, const),
            pl.BlockSpec(w1.shape, const),
            pl.BlockSpec(wlr.shape, const),
            pl.BlockSpec(wgu.shape, const),
            pl.BlockSpec(bg.shape, const),
            pl.BlockSpec(w2.shape, const),
        ],
        out_specs=[
            pl.BlockSpec((rows, 512), row),
            pl.BlockSpec((rows, 256), row),
            pl.BlockSpec((rows, 1792), row),
            pl.BlockSpec((rows, 2048), row),
        ],
        out_shape=[
            jax.ShapeDtypeStruct((r, 512), BF16),
            jax.ShapeDtypeStruct((r, 256), F32),
            jax.ShapeDtypeStruct((r, 1792), F32),
            jax.ShapeDtypeStruct((r, 2048), F32),
        ],
        compiler_params=_cparams(("arbitrary",)),
        name="proj",
    )(x, g, tab, w1, wlr, wgu, bg, w2)


def _attn_kernel(sink_ref, q_ref, prev_ref, cur_ref, o_ref, *, first_valid_key, block_offset):
    n = pl.program_id(1)
    qr = q_ref.shape[0]
    kr = cur_ref.shape[0]
    w = prev_ref.shape[0]
    nk = w + kr
    group = 4
    hd = 64

    rows = lax.broadcasted_iota(jnp.int32, (group * qr, nk), 0)
    cols = lax.broadcasted_iota(jnp.int32, (group * qr, nk), 1)
    head_of_row = rows // qr
    diff = (rows - head_of_row * qr) - cols + w
    mask = (diff >= 0) & (diff <= w)
    if first_valid_key is not None:
        blk = n + block_offset
        mask = mask & (cols >= first_valid_key + w - blk * w)

    prev = prev_ref[...]
    cur = cur_ref[...]
    q = q_ref[...]
    row_head = lax.broadcasted_iota(jnp.int32, (group * qr, 1), 0) // qr
    for kh in range(2):
        k = jnp.concatenate([prev[:, kh * hd:(kh + 1) * hd], cur[:, kh * hd:(kh + 1) * hd]], axis=0).astype(BF16)
        v = jnp.concatenate([prev[:, LANES + kh * hd:LANES + (kh + 1) * hd],
                             cur[:, LANES + kh * hd:LANES + (kh + 1) * hd]], axis=0).astype(BF16)
        qs = jnp.concatenate([q[:, (group * kh + g) * hd:(group * kh + g + 1) * hd] for g in range(group)], axis=0)
        s = lax.dot_general(qs, k, (((1,), (1,)), ((), ())), preferred_element_type=F32) * (hd ** -0.5)
        s = jnp.where(mask, s, NEG_INF)
        sink = jnp.zeros((group * qr, 1), F32)
        for g in range(group):
            sink = jnp.where(row_head == g, sink_ref[group * kh + g], sink)
        m = jnp.maximum(jnp.max(s, axis=-1, keepdims=True), sink)
        e = jnp.exp(s - m)
        p = e / (jnp.sum(e, axis=-1, keepdims=True) + jnp.exp(sink - m))
        o = jnp.dot(p.astype(BF16), v, preferred_element_type=F32)
        for g in range(group):
            h = group * kh + g
            o_ref[:, h * hd:(h + 1) * hd] = o[g * qr:(g + 1) * qr].astype(BF16)


def _attention(sinks, q, kv_prev, kv_cur, nb, nblk, qr, q_map, prev_map, cur_map, out_map, out_rows,
               first_valid_key, block_offset):
    w = ATTN_BLOCK
    return pl.pallas_call(
        functools.partial(_attn_kernel, first_valid_key=first_valid_key, block_offset=block_offset),
        grid=(nb, nblk),
        in_specs=[
            pl.BlockSpec(memory_space=pltpu.SMEM),
            pl.BlockSpec((qr, 512), q_map),
            pl.BlockSpec((w, 256), prev_map),
            pl.BlockSpec((qr, 256), cur_map),
        ],
        out_specs=pl.BlockSpec((qr, 512), out_map),
        out_shape=jax.ShapeDtypeStruct((out_rows, 512), BF16),
        compiler_params=_cparams(("arbitrary", "arbitrary")),
        name="swa",
    )(sinks, q, kv_prev, kv_cur)


def _gla_kernel(gl_ref, s0_ref, gn_ref, yb_ref, sfin_ref, st_ref):
    c = pl.program_id(1)
    ch = gl_ref.shape[0]
    nh, dk, dv = 4, 64, 128

    @pl.when(c == 0)
    def _():
        for h in range(nh):
            st_ref[h] = s0_ref[0, h].T

    gl = gl_ref[...]
    q = gl[:, 0:256]
    k = gl[:, 256:512]
    b = gl[:, 512:768]
    row = lax.broadcasted_iota(jnp.int32, (ch, nh * dk), 0)
    sh = 1
    while sh < ch:
        b = b + jnp.where(row >= sh, pltpu.roll(b, sh, 0), 0.0)
        sh *= 2
    b_last = b[ch - 1:ch, :]
    q_t = (q * jnp.exp(b)).astype(BF16)
    k_t = (k * jnp.exp(-b)).astype(BF16)
    k_end = (k * jnp.exp(b_last - b)).astype(BF16)
    decay = jnp.exp(b_last)
    causal = (lax.broadcasted_iota(jnp.int32, (ch, ch), 0) >= lax.broadcasted_iota(jnp.int32, (ch, ch), 1))
    gn = gn_ref[...]
    nt = (((1,), (1,)), ((), ()))
    for h in range(nh):
        ks = slice(h * dk, (h + 1) * dk)
        v = gl[:, 768 + h * dv:768 + (h + 1) * dv]
        vb = v.astype(BF16)
        a = lax.dot_general(q_t[:, ks], k_t[:, ks], nt, preferred_element_type=F32)
        a = jnp.where(causal, a, 0.0)
        s_t = st_ref[h]
        o = jnp.dot(a.astype(BF16), vb, preferred_element_type=F32)
        o = o + lax.dot_general(q_t[:, ks], s_t.astype(BF16), nt, preferred_element_type=F32)
        upd = jnp.dot(v.T.astype(BF16), k_end[:, ks], preferred_element_type=F32)
        st_ref[h] = s_t * decay[:, ks] + upd
        go = gl[:, 1280 + h * dv:1280 + (h + 1) * dv]
        y = _rms(o, gn) * (go * jax.nn.sigmoid(go))
        yb_ref[:, h * dv:(h + 1) * dv] = y.astype(BF16)

    @pl.when(c == pl.num_programs(1) - 1)
    def _():
        for h in range(nh):
            sfin_ref[0, h] = st_ref[h].T


def _gla(gl, s0, gn, nb, nchunks, ch, in_map, out_map, out_rows):
    return pl.pallas_call(
        _gla_kernel,
        grid=(nb, nchunks),
        in_specs=[
            pl.BlockSpec((ch, 1792), in_map),
            pl.BlockSpec((1, 4, 64, 128), lambda b, c: (b, 0, 0, 0)),
            pl.BlockSpec((1, 128), lambda b, c: (0, 0)),
        ],
        out_specs=[
            pl.BlockSpec((ch, 512), out_map),
            pl.BlockSpec((1, 4, 64, 128), lambda b, c: (b, 0, 0, 0)),
        ],
        out_shape=[
            jax.ShapeDtypeStruct((out_rows, 512), BF16),
            jax.ShapeDtypeStruct((nb, 4, 64, 128), F32),
        ],
        scratch_shapes=[pltpu.VMEM((4, 128, 64), F32)],
        compiler_params=_cparams(("arbitrary", "arbitrary")),
        name="gla",
    )(gl, s0, gn)


def _extract_topk(work, nsel, iota0, sentinel):
    vals, idxs = [], []
    for j in range(nsel):
        m = jnp.max(work, axis=0, keepdims=True)
        idx = jnp.min(jnp.where(work == m, iota0, sentinel), axis=0, keepdims=True)
        vals.append(m)
        idxs.append(idx)
        if j + 1 < nsel:
            work = jnp.where(iota0 == idx, -jnp.inf, work)
    return vals, idxs


def _merge_kernel(x_ref, gt_ref, ya_ref, yb_ref, wa_ref, wb_ref, wo_ref, gf_ref, wq_ref, keys_ref,
                  xm_ref, hn_ref, et_ref, wt_ref):
    td = x_ref.shape[0]
    nkeys = keys_ref.shape[1]
    half = keys_ref.shape[2]
    nheads = keys_ref.shape[0] // 2
    topk = PEER_TOPK

    gt = gt_ref[...]
    d = x_ref.shape[1]
    ma = jnp.dot(ya_ref[...], wa_ref[...], preferred_element_type=F32)
    mb = jnp.dot(yb_ref[...], wb_ref[...], preferred_element_type=F32)
    m = jax.nn.sigmoid(gt[:, 0:d]) * ma + jax.nn.sigmoid(gt[:, d:2 * d]) * mb
    xm = x_ref[...] + jnp.dot(m.astype(BF16), wo_ref[...], preferred_element_type=F32)
    xm_ref[...] = xm
    hn = _rms(xm, gf_ref[...])
    hn_ref[...] = hn
    q = jnp.dot(hn.astype(BF16), wq_ref[...], preferred_element_type=F32).astype(BF16)

    nt = (((1,), (1,)), ((), ()))
    iota_k = lax.broadcasted_iota(jnp.int32, (nkeys, td), 0)
    iota_c = lax.broadcasted_iota(jnp.int32, (topk * topk, td), 0)
    wts, ids = [], []
    for h in range(nheads):
        sv, si = [], []
        for c in range(2):
            gi = 2 * h + c
            s_t = lax.dot_general(keys_ref[gi], q[:, gi * half:(gi + 1) * half], nt,
                                  preferred_element_type=F32)
            vals, idxs = _extract_topk(s_t, topk, iota_k, nkeys)
            sv.append(vals)
            si.append(idxs)
        sv1 = jnp.concatenate(sv[1], axis=0)
        si1 = jnp.concatenate(si[1], axis=0)
        cand = jnp.concatenate([sv[0][a] + sv1 for a in range(topk)], axis=0)
        cidx = jnp.concatenate([si[0][a] * nkeys + si1 for a in range(topk)], axis=0)
        fvals, eids = [], []
        work = cand
        for j in range(topk):
            mx = jnp.max(work, axis=0, keepdims=True)
            pos = jnp.min(jnp.where(work == mx, iota_c, topk * topk), axis=0, keepdims=True)
            hit = iota_c == pos
            eids.append(jnp.max(jnp.where(hit, cidx, -1), axis=0, keepdims=True))
            fvals.append(mx)
            if j + 1 < topk:
                work = jnp.where(hit, -jnp.inf, work)
        fv = jnp.concatenate(fvals, axis=0)
        e = jnp.exp(fv - fvals[0])
        wts.append(e / jnp.sum(e, axis=0, keepdims=True))
        ids.extend(eids)
    wt_ref[...] = jnp.concatenate(wts, axis=0).T
    et_ref[...] = jnp.concatenate(ids, axis=0).T


def _merge_route(x, gt, ya, yb, wa, wb, wo, gf, wq, keys, gt_map):
    t, d = x.shape
    td = MERGE_ROWS
    nsel = (keys.shape[0] // 2) * PEER_TOPK
    const2 = lambda i: (0, 0)
    row = lambda i: (i, 0)
    return pl.pallas_call(
        _merge_kernel,
        grid=(t // td,),
        in_specs=[
            pl.BlockSpec((td, d), row),
            pl.BlockSpec((td, 2 * d), gt_map),
            pl.BlockSpec((td, ya.shape[1]), row),
            pl.BlockSpec((td, yb.shape[1]), row),
            pl.BlockSpec(wa.shape, const2),
            pl.BlockSpec(wb.shape, const2),
            pl.BlockSpec(wo.shape, const2),
            pl.BlockSpec(gf.shape, const2),
            pl.BlockSpec(wq.shape, const2),
            pl.BlockSpec(keys.shape, lambda i: (0, 0, 0)),
        ],
        out_specs=[
            pl.BlockSpec((td, d), row),
            pl.BlockSpec((td, d), row),
            pl.BlockSpec((td, nsel), row),
            pl.BlockSpec((td, nsel), row),
        ],
        out_shape=[
            jax.ShapeDtypeStruct((t, d), F32),
            jax.ShapeDtypeStruct((t, d), F32),
            jax.ShapeDtypeStruct((t, nsel), jnp.int32),
            jax.ShapeDtypeStruct((t, nsel), F32),
        ],
        compiler_params=_cparams(("arbitrary",)),
        name="merge_route",
    )(x, gt, ya, yb, wa, wb, wo, gf, wq, keys)


def _coef_kernel(act_ref, wt_ref, o_ref):
    act = act_ref[...]
    gelu = 0.5 * act * (1.0 + lax.erf(act * (2.0 ** -0.5)))
    bits = pltpu.bitcast((wt_ref[...] * gelu).astype(BF16).astype(F32), jnp.int32)
    o_ref[...] = bits | lax.shift_right_logical(bits, 16)


def _expert_coefs(act, wt):
    t, nsel = wt.shape
    rows = math.gcd(t, 512)
    row = lambda i: (i, 0)
    return pl.pallas_call(
        _coef_kernel,
        grid=(t // rows,),
        in_specs=[pl.BlockSpec((rows, nsel), row), pl.BlockSpec((rows, nsel), row)],
        out_specs=pl.BlockSpec((rows, nsel), row),
        out_shape=jax.ShapeDtypeStruct((t, nsel), jnp.int32),
        compiler_params=_cparams(("arbitrary",)),
        name="expert_coefs",
    )(act, wt)


def _finish_kernel(xm_ref, o_ref, gfin_ref, yacc_hbm, y_ref):
    del yacc_hbm
    y_ref[...] = _rms(xm_ref[...] + o_ref[...], gfin_ref[...])


def _finish(xm, o, gfin, y_acc, row_off):
    t, d = xm.shape
    rows = math.gcd(t, 512)
    first = row_off // rows
    row = lambda i: (i, 0)
    return pl.pallas_call(
        _finish_kernel,
        grid=(t // rows,),
        in_specs=[pl.BlockSpec((rows, d), row), pl.BlockSpec((rows, d), row),
                  pl.BlockSpec(gfin.shape, lambda i: (0, 0)), pl.BlockSpec(memory_space=pl.ANY)],
        out_specs=pl.BlockSpec((rows, d), lambda i: (i + first, 0)),
        out_shape=jax.ShapeDtypeStruct(y_acc.shape, F32),
        input_output_aliases={3: 0},
        compiler_params=_cparams(("arbitrary",)),
        name="finish",
    )(xm, o, gfin, y_acc)


def _rope_table(pos, valid, rope_dim, head_dim):
    half = rope_dim // 2
    inv = ROPE_THETA ** (-jnp.arange(0, rope_dim, 2, dtype=F32) / rope_dim)
    ang = pos.astype(F32)[:, None] * inv[None, :]
    cos, sin = jnp.cos(ang), jnp.sin(ang)
    n = pos.shape[0]
    ones = jnp.ones((n, head_dim - rope_dim), F32)
    zeros_h = jnp.zeros((n, half), F32)
    zeros_r = jnp.zeros((n, head_dim - rope_dim), F32)
    reps = LANES // head_dim
    cosf = jnp.tile(jnp.concatenate([cos, cos, ones], axis=1), (1, reps))
    sin_lo = jnp.tile(jnp.concatenate([zeros_h, sin, zeros_r], axis=1), (1, reps))
    sin_hi = jnp.tile(jnp.concatenate([-sin, zeros_h, zeros_r], axis=1), (1, reps))
    vcol = jnp.broadcast_to(valid.astype(F32)[:, None], (n, LANES))
    return jnp.concatenate([cosf, sin_lo, sin_hi, vcol], axis=1)


def _pack_pairs(x):
    half = x.shape[1] // 2
    bits = lax.bitcast_convert_type(x.astype(BF16), jnp.uint16).astype(jnp.uint32)
    return lax.bitcast_convert_type(bits[:, :half] | (bits[:, half:] << 16), jnp.int32)


def _expert_dots(table, idx, hn, nsel):
    n = idx.shape[0]
    c = table.shape[1]
    sc = plsc.get_sparse_core_info()
    lanes = sc.num_lanes
    workers = sc.num_cores * sc.num_subcores
    gw = GATHER_WINDOW
    ich = GATHER_INDEX_CHUNK
    tok = ich // nsel
    per_worker = n // workers
    assert n % workers == 0 and per_worker % ich == 0 and nsel == 2 * gw and hn.shape[1] == c and gw % lanes == 0
    mesh = plsc.VectorSubcoreMesh(core_axis_name="c", subcore_axis_name="s")

    @functools.partial(
        pl.kernel, out_type=jax.ShapeDtypeStruct((n,), F32), mesh=mesh, name="expert_dots",
        compiler_params=pltpu.CompilerParams(needs_layout_passes=False),
        scratch_types=[pltpu.VMEM((ich,), jnp.int32),
                       pltpu.VMEM((gw, c), jnp.int32), pltpu.VMEM((gw, c), jnp.int32),
                       pltpu.VMEM((tok, c), jnp.int32), pltpu.VMEM((ich,), F32), pltpu.VMEM((lanes * lanes,), F32),
                       pltpu.SemaphoreType.DMA, pltpu.SemaphoreType.DMA])
    def dots(tab_hbm, idx_hbm, hn_hbm, act_hbm, idx_v, rows0, rows1, h_v, act_v, scr, sem0, sem1):
        wid = lax.axis_index("s") * sc.num_cores + lax.axis_index("c")
        base = wid * per_worker
        lane = lax.iota(jnp.int32, lanes)

        def gather(win, buf, sem):
            return pltpu.make_async_copy(tab_hbm.at[idx_v.at[pl.ds(win * gw, gw)]], buf, sem)

        def reduce_window(buf, t_loc, out_off):
            for rb in range(gw // lanes):
                def kbody(k, accs):
                    hw = plsc.bitcast(h_v[t_loc, pl.ds(k * lanes, lanes)], BF16)
                    out = []
                    for r in range(lanes):
                        w = buf[rb * lanes + r, pl.ds(k * lanes, lanes)]
                        p = plsc.bitcast(plsc.bitcast(w, BF16) * hw, jnp.int32)
                        out.append(accs[r] + lax.bitcast_convert_type(p << 16, F32)
                                   + lax.bitcast_convert_type(p, F32))
                    return tuple(out)

                accs = lax.fori_loop(0, c // lanes, kbody,
                                     tuple(jnp.zeros((lanes,), F32) for _ in range(lanes)))
                for r in range(lanes):
                    scr[pl.ds(r * lanes, lanes)] = accs[r]
                tot = plsc.load_gather(scr, [lane * lanes])
                for l in range(1, lanes):
                    tot = tot + plsc.load_gather(scr, [lane * lanes + l])
                act_v[pl.ds(out_off + rb * lanes, lanes)] = tot

        @pl.loop(0, per_worker // ich)
        def _(g):
            cb = base + g * ich
            pltpu.sync_copy(idx_hbm.at[pl.ds(cb, ich)], idx_v)
            tok_base = pl.multiple_of(wid * (per_worker // nsel) + g * tok, tok)
            pltpu.sync_copy(hn_hbm.at[pl.ds(tok_base, tok)], h_v)
            gather(0, rows0, sem0).start()
            gather(1, rows1, sem1).start()

            @pl.loop(0, tok)
            def _(j):
                gather(2 * j, rows0, sem0).wait()
                reduce_window(rows0, j, j * nsel)

                @pl.when(j + 1 < tok)
                def _():
                    gather(2 * j + 2, rows0, sem0).start()

                gather(2 * j + 1, rows1, sem1).wait()
                reduce_window(rows1, j, j * nsel + gw)

                @pl.when(j + 1 < tok)
                def _():
                    gather(2 * j + 3, rows1, sem1).start()

            pltpu.sync_copy(act_v, act_hbm.at[pl.ds(cb, ich)])

    return dots(table, idx, hn)


def _expert_mix(table, idx, coef, nsel):
    n = idx.shape[0]
    c = table.shape[1]
    d = 2 * c
    sc = plsc.get_sparse_core_info()
    lanes = sc.num_lanes
    workers = sc.num_cores * sc.num_subcores
    gw = GATHER_WINDOW
    ich = GATHER_INDEX_CHUNK
    tok = ich // nsel
    per_worker = n // workers
    kblock = 8
    assert n % workers == 0 and per_worker % ich == 0 and nsel == 2 * gw and c % (kblock * lanes) == 0
    mesh = plsc.VectorSubcoreMesh(core_axis_name="c", subcore_axis_name="s")

    @functools.partial(
        pl.kernel, out_type=jax.ShapeDtypeStruct((n // nsel, d), F32), mesh=mesh, name="expert_mix",
        compiler_params=pltpu.CompilerParams(needs_layout_passes=False),
        scratch_types=[pltpu.VMEM((ich,), jnp.int32), pltpu.VMEM((ich,), jnp.int32),
                       pltpu.VMEM((gw, c), jnp.int32), pltpu.VMEM((gw, c), jnp.int32),
                       pltpu.VMEM((tok, d), F32),
                       pltpu.SemaphoreType.DMA, pltpu.SemaphoreType.DMA])
    def mix(tab_hbm, idx_hbm, coef_hbm, out_hbm, idx_v, coef_v, rows0, rows1, out_v, sem0, sem1):
        wid = lax.axis_index("s") * sc.num_cores + lax.axis_index("c")
        base = wid * per_worker
        zero_idx = jnp.zeros((lanes,), jnp.int32)

        def gather(win, buf, sem):
            return pltpu.make_async_copy(tab_hbm.at[idx_v.at[pl.ds(win * gw, gw)]], buf, sem)

        def accumulate_window(buf, t_loc, coef_off, first):
            for kb in range(c // (kblock * lanes)):
                col0 = kb * kblock * lanes
                if first:
                    init = tuple(jnp.zeros((lanes,), F32) for _ in range(2 * kblock))
                else:
                    init = tuple(out_v[t_loc, pl.ds(col0 + i * lanes, lanes)] for i in range(kblock)) + \
                           tuple(out_v[t_loc, pl.ds(c + col0 + i * lanes, lanes)] for i in range(kblock))

                def rbody(r2, accs):
                    accs = list(accs)
                    for rr in range(2):
                        r = 2 * r2 + rr
                        cw = plsc.bitcast(plsc.load_gather(coef_v, [zero_idx + (coef_off + r)]), BF16)
                        for i in range(kblock):
                            w = buf[r, pl.ds(col0 + i * lanes, lanes)]
                            p = plsc.bitcast(plsc.bitcast(w, BF16) * cw, jnp.int32)
                            accs[i] = accs[i] + lax.bitcast_convert_type(p << 16, F32)
                            accs[kblock + i] = accs[kblock + i] + lax.bitcast_convert_type(p, F32)
                    return tuple(accs)

                accs = lax.fori_loop(0, gw // 2, rbody, init)
                for i in range(kblock):
                    out_v[t_loc, pl.ds(col0 + i * lanes, lanes)] = accs[i]
                    out_v[t_loc, pl.ds(c + col0 + i * lanes, lanes)] = accs[kblock + i]

        @pl.loop(0, per_worker // ich)
        def _(g):
            cb = base + g * ich
            pltpu.sync_copy(idx_hbm.at[pl.ds(cb, ich)], idx_v)
            pltpu.sync_copy(coef_hbm.at[pl.ds(cb, ich)], coef_v)
            gather(0, rows0, sem0).start()
            gather(1, rows1, sem1).start()

            @pl.loop(0, tok)
            def _(j):
                gather(2 * j, rows0, sem0).wait()
                accumulate_window(rows0, j, j * nsel, True)

                @pl.when(j + 1 < tok)
                def _():
                    gather(2 * j + 2, rows0, sem0).start()

                gather(2 * j + 1, rows1, sem1).wait()
                accumulate_window(rows1, j, j * nsel + gw, False)

                @pl.when(j + 1 < tok)
                def _():
                    gather(2 * j + 3, rows1, sem1).start()

            tok_base = pl.multiple_of(wid * (per_worker // nsel) + g * tok, tok)
            pltpu.sync_copy(out_v, out_hbm.at[pl.ds(tok_base, tok)])

    return mix(table, idx, coef)


def _peer_tail(xm, hn, et, wt, tabs, gfin, y_acc, row_off):
    u_tab, v_tab = tabs
    t, nsel = et.shape
    eidx = et.reshape(t * nsel)
    act = _expert_dots(u_tab, eidx, _pack_pairs(hn), nsel).reshape(t, nsel)
    coef = _expert_coefs(act, wt).reshape(t * nsel)
    mixed = _expert_mix(v_tab, eidx, coef, nsel)
    return _finish(xm, mixed, gfin, y_acc, row_off)


def kernel(x_prompt, x_sample, cache_k_window, cache_v_window, state_gla, meta_tokens, g_norm_mix, w_in,
           w_gate_up, b_gate, attn_sinks, g_gla_norm, w_branch_a, w_branch_b, w_out, g_norm_ffn, w_peer_q,
           peer_sub_keys, peer_u, peer_v, g_norm_final):
    bsz, seq, d = x_prompt.shape
    dbsz, tdec, _ = x_sample.shape
    n_meta = meta_tokens.shape[0]
    depth = w_in.shape[0]
    window = cache_k_window.shape[2]
    kv_heads, head_dim = cache_k_window.shape[3], cache_k_window.shape[4]
    gate_rank = w_gate_up.shape[1]
    bqk = w_gate_up.shape[2]
    n_ph, _, n_keys, p_half = peer_sub_keys.shape[1:]
    assert depth == 1 and d == 1024 and window == ATTN_BLOCK and kv_heads == 2 and head_dim == 64
    assert bqk == 256 and state_gla.shape[2:] == (4, 64, 128) and n_meta <= ATTN_BLOCK
    assert seq % ATTN_BLOCK == 0 and tdec <= SAMPLE_PAD and n_keys == 128 and p_half == 64 and n_ph == 8
    rope_dim = head_dim // 4
    meta_pad = ATTN_BLOCK - n_meta
    lp = ATTN_BLOCK + seq
    nblk = lp // ATTN_BLOCK

    w = w_in[0]
    c_lr = 2304
    c_gate = c_lr + gate_rank
    w1 = w[:, :c_lr].astype(BF16)
    wlr = jnp.pad(w[:, c_lr:c_gate], ((0, 0), (0, LANES - gate_rank))).astype(BF16)
    w2 = w[:, c_gate:].astype(BF16)
    wgu = jnp.pad(w_gate_up[0], ((0, LANES - gate_rank), (0, 0))).astype(BF16)
    bg = b_gate[0][None, :]
    gmix = g_norm_mix[0][None, :]
    wa = w_branch_a[0].astype(BF16)
    wb = w_branch_b[0].astype(BF16)
    wo = w_out[0].astype(BF16)
    gffn = g_norm_ffn[0][None, :]
    wq = w_peer_q[0].astype(BF16)
    keys = peer_sub_keys[0].reshape(n_ph * 2, n_keys, p_half).astype(BF16)
    u_tab = _pack_pairs(peer_u[0])
    v_tab = _pack_pairs(peer_v[0])
    tabs = (u_tab, v_tab)
    gfin = g_norm_final[None, :]
    gn = g_gla_norm[0][None, :]
    sinks = attn_sinks[0]
    qk_scale = float(bqk // 4) ** -0.5

    rows_p = jnp.arange(lp)
    tab_p = _rope_table(rows_p - meta_pad, rows_p >= meta_pad, rope_dim, head_dim)
    proj_rows = max(r for r in range(16, PROJ_ROWS + 1, 16) if lp % r == 0)
    nq = nblk - 1
    nchunks = lp // GLA_CHUNK
    skip = ATTN_BLOCK // GLA_CHUNK
    ncq = nchunks - skip
    per_seq = seq // MERGE_ROWS
    gt_map_p = lambda i: ((i // per_seq) * nblk + 1 + (i % per_seq), 0)

    def prompt_sequences(xg, y_acc, row_off):
        gb = xg.shape[0]
        meta = jnp.broadcast_to(meta_tokens[None].astype(xg.dtype), (gb, n_meta, d))
        xpad = jnp.concatenate([jnp.zeros((gb, meta_pad, d), xg.dtype), meta, xg], axis=1).reshape(gb * lp, d)
        qa, kv, gl, gt = _project(xpad, gmix, tab_p, w1, wlr, wgu, bg, w2, proj_rows, qk_scale)
        ya = _attention(
            sinks, qa, kv, kv, gb, nq, ATTN_BLOCK,
            lambda b, n: (b * nblk + n + 1, 0), lambda b, n: (b * nblk + n, 0), lambda b, n: (b * nblk + n + 1, 0),
            lambda b, n: (b * nq + n, 0), gb * seq, first_valid_key=meta_pad, block_offset=1)
        s0 = jnp.zeros((gb,) + state_gla.shape[2:], F32)
        yb, s_fin = _gla(gl, s0, gn, gb, nchunks, GLA_CHUNK,
                         lambda b, c: (b * nchunks + c, 0),
                         lambda b, c: (b * ncq + jnp.maximum(c - skip, 0), 0), gb * seq)
        xm, hn, et, wt = _merge_route(xg.reshape(gb * seq, d), gt, ya, yb, wa, wb, wo, gffn, wq, keys,
                                      gt_map_p)
        y_acc = _peer_tail(xm, hn, et, wt, tabs, gfin, y_acc, row_off)
        kv_w = kv.reshape(gb, lp, 2, kv_heads, head_dim)[:, lp - window:]
        return y_acc, kv_w, s_fin

    group = PROMPT_GROUP if (bsz % PROMPT_GROUP == 0 and (PROMPT_GROUP * seq) % GATHER_ROW_QUANTUM == 0) else bsz
    y_acc, kv_parts, s_parts = jnp.zeros((bsz * seq, d), F32), [], []
    for b0 in range(0, bsz, group):
        y_acc, kv_w, s_fin = prompt_sequences(x_prompt[b0:b0 + group], y_acc, b0 * seq)
        kv_parts.append(kv_w)
        s_parts.append(s_fin)
    y_prompt = y_acc.reshape(bsz, seq, d)
    kv_p = jnp.concatenate(kv_parts, axis=0)
    s_fin_p = jnp.concatenate(s_parts, axis=0)
    new_k_p = kv_p[:, :, 0][None]
    new_v_p = kv_p[:, :, 1][None]

    sp = SAMPLE_PAD
    xs_pad = jnp.pad(x_sample, ((0, 0), (0, sp - tdec), (0, 0))).reshape(dbsz * sp, d)
    rows_s = jnp.arange(sp)
    reps = 256 // sp
    tab_s = jnp.tile(_rope_table(PAST_LEN + rows_s, rows_s < tdec, rope_dim, head_dim), (reps, 1))
    qa_s, kv_s, gl_s, gt_s = _project(xs_pad, gmix, tab_s, w1, wlr, wgu, bg, w2, 256, qk_scale)

    cache_kv = jnp.concatenate([cache_k_window[0].reshape(dbsz * window, kv_heads * head_dim),
                                cache_v_window[0].reshape(dbsz * window, kv_heads * head_dim)], axis=1)
    seq_map = lambda b, n: (b, 0)
    ya_s = _attention(sinks, qa_s, cache_kv, kv_s, dbsz, 1, sp, seq_map, seq_map, seq_map, seq_map,
                      dbsz * sp, first_valid_key=None, block_offset=0)
    yb_s, s_fin_s = _gla(gl_s, state_gla[0], gn, dbsz, 1, sp, seq_map, seq_map, dbsz * sp)

    def real_rows(a):
        return a.reshape(dbsz, sp, a.shape[-1])[:, :tdec].reshape(dbsz * tdec, a.shape[-1])

    xs_rows = x_sample.reshape(dbsz * tdec, d)
    xm_s, hn_s, et_s, wt_s = _merge_route(xs_rows, real_rows(gt_s), real_rows(ya_s), real_rows(yb_s),
                                          wa, wb, wo, gffn, wq, keys, lambda i: (i, 0))
    y_sample = _peer_tail(xm_s, hn_s, et_s, wt_s, tabs, gfin, jnp.zeros((dbsz * tdec, d), F32), 0).reshape(dbsz, tdec, d)

    kv_new = real_rows(kv_s).reshape(dbsz, tdec, 2, kv_heads, head_dim)
    new_k_s = jnp.concatenate([cache_k_window[0].astype(F32), kv_new[:, :, 0]], axis=1)[:, -window:][None]
    new_v_s = jnp.concatenate([cache_v_window[0].astype(F32), kv_new[:, :, 1]], axis=1)[:, -window:][None]

    return (y_prompt, y_sample, new_k_p, new_v_p, s_fin_p[None], new_k_s, new_v_s, s_fin_s[None])
```

```python
import functools
import math

import jax
import jax.numpy as jnp
from jax import lax
from jax.experimental import pallas as pl
from jax.experimental.pallas import tpu as pltpu
from jax.experimental.pallas import tpu_sc as plsc

F32 = jnp.float32
BF16 = jnp.bfloat16

EPS = 1e-6
NEG_INF = -1e30
PAST_LEN = 16384
ROPE_THETA = 500000.0
GATE_NORMALIZER = 16.0
PEER_TOPK = 16

LANES = 128
SUBLANES = 8
VMEM_LIMIT_BYTES = 56 * 1024 * 1024

ATTN_BLOCK = 128
GLA_CHUNK = 64
SAMPLE_PAD = 16
PROJ_ROWS = 544
MERGE_ROWS = 128
GATHER_WINDOW = 64
GATHER_INDEX_CHUNK = 2048
GATHER_ROW_QUANTUM = 512
PROMPT_GROUP = 1


def _cparams(sem):
    return pltpu.CompilerParams(dimension_semantics=sem, vmem_limit_bytes=VMEM_LIMIT_BYTES)


def _rms(x, g):
    ms = jnp.mean(x * x, axis=-1, keepdims=True)
    return (x * lax.rsqrt(ms + EPS)) * g


def _proj_kernel(x_ref, g_ref, tab_ref, w1_ref, wlr_ref, wgu_ref, bg_ref, w2_ref,
                 qa_ref, kv_ref, gl_ref, gt_ref, *, period, qk_scale):
    i = pl.program_id(0)
    tr = x_ref.shape[0]
    hb = _rms(x_ref[...], g_ref[...]).astype(BF16)
    z1 = jnp.dot(hb, w1_ref[...], preferred_element_type=F32)

    start = pl.multiple_of((i * tr) % period, SUBLANES)
    tab = tab_ref[pl.ds(start, tr), :]
    cosf = tab[:, 0:LANES]
    sin_lo = tab[:, LANES:2 * LANES]
    sin_hi = tab[:, 2 * LANES:3 * LANES]
    valid = tab[:, 3 * LANES:3 * LANES + 1]

    def rope(xg):
        return xg * cosf + pltpu.roll(xg, 8, 1) * sin_lo + pltpu.roll(xg, LANES - 8, 1) * sin_hi

    for gi in range(4):
        sl = slice(gi * LANES, (gi + 1) * LANES)
        qa_ref[:, sl] = rope(z1[:, sl]).astype(BF16)
    kv_ref[:, 0:LANES] = rope(z1[:, 512:640])
    kv_ref[:, LANES:2 * LANES] = z1[:, 640:768]

    lr = jnp.dot(hb, wlr_ref[...], preferred_element_type=F32)
    pre = jnp.dot(lr.astype(BF16), wgu_ref[...], preferred_element_type=F32) + bg_ref[...]
    log_sig = jnp.minimum(pre, 0.0) - jnp.log1p(jnp.exp(-jnp.abs(pre)))
    ld = jnp.where(valid > 0.5, log_sig / GATE_NORMALIZER, 0.0)

    gl_ref[:, 0:256] = z1[:, 768:1024] * qk_scale
    gl_ref[:, 256:512] = z1[:, 1024:1280]
    gl_ref[:, 512:768] = ld
    gl_ref[:, 768:1792] = z1[:, 1280:2304]
    gt_ref[...] = jnp.dot(hb, w2_ref[...], preferred_element_type=F32)


def _project(x, g, tab, w1, wlr, wgu, bg, w2, rows, qk_scale):
    r, d = x.shape
    period = tab.shape[0]
    const = lambda i: (0, 0)
    row = lambda i: (i, 0)
    return pl.pallas_call(
        functools.partial(_proj_kernel, period=period, qk_scale=qk_scale),
        grid=(r // rows,),
        in_specs=[
            pl.BlockSpec((rows, d), row),
            pl.BlockSpec(g.shape, const),
            pl.BlockSpec(tab.shape, const),
            pl.BlockSpec(w1.shape, const),
            pl.BlockSpec(wlr.shape, const),
            pl.BlockSpec(wgu.shape, const),
            pl.BlockSpec(bg.shape, const),
            pl.BlockSpec(w2.shape, const),
        ],
        out_specs=[
            pl.BlockSpec((rows, 512), row),
            pl.BlockSpec((rows, 256), row),
            pl.BlockSpec((rows, 1792), row),
            pl.BlockSpec((rows, 2048), row),
        ],
        out_shape=[
            jax.ShapeDtypeStruct((r, 512), BF16),
            jax.ShapeDtypeStruct((r, 256), F32),
            jax.ShapeDtypeStruct((r, 1792), F32),
            jax.ShapeDtypeStruct((r, 2048), F32),
        ],
        compiler_params=_cparams(("arbitrary",)),
        name="proj",
    )(x, g, tab, w1, wlr, wgu, bg, w2)


def _attn_kernel(sink_ref, q_ref, prev_ref, cur_ref, o_ref, *, first_valid_key, block_offset):
    n = pl.program_id(1)
    qr = q_ref.shape[0]
    kr = cur_ref.shape[0]
    w = prev_ref.shape[0]
    nk = w + kr
    group = 4
    hd = 64

    rows = lax.broadcasted_iota(jnp.int32, (group * qr, nk), 0)
    cols = lax.broadcasted_iota(jnp.int32, (group * qr, nk), 1)
    head_of_row = rows // qr
    diff = (rows - head_of_row * qr) - cols + w
    mask = (diff >= 0) & (diff <= w)
    if first_valid_key is not None:
        blk = n + block_offset
        mask = mask & (cols >= first_valid_key + w - blk * w)

    prev = prev_ref[...]
    cur = cur_ref[...]
    q = q_ref[...]
    row_head = lax.broadcasted_iota(jnp.int32, (group * qr, 1), 0) // qr
    for kh in range(2):
        k = jnp.concatenate([prev[:, kh * hd:(kh + 1) * hd], cur[:, kh * hd:(kh + 1) * hd]], axis=0).astype(BF16)
        v = jnp.concatenate([prev[:, LANES + kh * hd:LANES + (kh + 1) * hd],
                             cur[:, LANES + kh * hd:LANES + (kh + 1) * hd]], axis=0).astype(BF16)
        qs = jnp.concatenate([q[:, (group * kh + g) * hd:(group * kh + g + 1) * hd] for g in range(group)], axis=0)
        s = lax.dot_general(qs, k, (((1,), (1,)), ((), ())), preferred_element_type=F32) * (hd ** -0.5)
        s = jnp.where(mask, s, NEG_INF)
        sink = jnp.zeros((group * qr, 1), F32)
        for g in range(group):
            sink = jnp.where(row_head == g, sink_ref[group * kh + g], sink)
        m = jnp.maximum(jnp.max(s, axis=-1, keepdims=True), sink)
        e = jnp.exp(s - m)
        p = e / (jnp.sum(e, axis=-1, keepdims=True) + jnp.exp(sink - m))
        o = jnp.dot(p.astype(BF16), v, preferred_element_type=F32)
        for g in range(group):
            h = group * kh + g
            o_ref[:, h * hd:(h + 1) * hd] = o[g * qr:(g + 1) * qr].astype(BF16)


def _attention(sinks, q, kv_prev, kv_cur, nb, nblk, qr, q_map, prev_map, cur_map, out_map, out_rows,
               first_valid_key, block_offset):
    w = ATTN_BLOCK
    return pl.pallas_call(
        functools.partial(_attn_kernel, first_valid_key=first_valid_key, block_offset=block_offset),
        grid=(nb, nblk),
        in_specs=[
            pl.BlockSpec(memory_space=pltpu.SMEM),
            pl.BlockSpec((qr, 512), q_map),
            pl.BlockSpec((w, 256), prev_map),
            pl.BlockSpec((qr, 256), cur_map),
        ],
        out_specs=pl.BlockSpec((qr, 512), out_map),
        out_shape=jax.ShapeDtypeStruct((out_rows, 512), BF16),
        compiler_params=_cparams(("arbitrary", "arbitrary")),
        name="swa",
    )(sinks, q, kv_prev, kv_cur)


def _gla_kernel(gl_ref, s0_ref, gn_ref, yb_ref, sfin_ref, st_ref):
    c = pl.program_id(1)
    ch = gl_ref.shape[0]
    nh, dk, dv = 4, 64, 128

    @pl.when(c == 0)
    def _():
        for h in range(nh):
            st_ref[h] = s0_ref[0, h].T

    gl = gl_ref[...]
    q = gl[:, 0:256]
    k = gl[:, 256:512]
    b = gl[:, 512:768]
    row = lax.broadcasted_iota(jnp.int32, (ch, nh * dk), 0)
    sh = 1
    while sh < ch:
        b = b + jnp.where(row >= sh, pltpu.roll(b, sh, 0), 0.0)
        sh *= 2
    b_last = b[ch - 1:ch, :]
    q_t = (q * jnp.exp(b)).astype(BF16)
    k_t = (k * jnp.exp(-b)).astype(BF16)
    k_end = (k * jnp.exp(b_last - b)).astype(BF16)
    decay = jnp.exp(b_last)
    causal = (lax.broadcasted_iota(jnp.int32, (ch, ch), 0) >= lax.broadcasted_iota(jnp.int32, (ch, ch), 1))
    gn = gn_ref[...]
    nt = (((1,), (1,)), ((), ()))
    for h in range(nh):
        ks = slice(h * dk, (h + 1) * dk)
        v = gl[:, 768 + h * dv:768 + (h + 1) * dv]
        vb = v.astype(BF16)
        a = lax.dot_general(q_t[:, ks], k_t[:, ks], nt, preferred_element_type=F32)
        a = jnp.where(causal, a, 0.0)
        s_t = st_ref[h]
        o = jnp.dot(a.astype(BF16), vb, preferred_element_type=F32)
        o = o + lax.dot_general(q_t[:, ks], s_t.astype(BF16), nt, preferred_element_type=F32)
        upd = jnp.dot(v.T.astype(BF16), k_end[:, ks], preferred_element_type=F32)
        st_ref[h] = s_t * decay[:, ks] + upd
        go = gl[:, 1280 + h * dv:1280 + (h + 1) * dv]
        y = _rms(o, gn) * (go * jax.nn.sigmoid(go))
        yb_ref[:, h * dv:(h + 1) * dv] = y.astype(BF16)

    @pl.when(c == pl.num_programs(1) - 1)
    def _():
        for h in range(nh):
            sfin_ref[0, h] = st_ref[h].T


def _gla(gl, s0, gn, nb, nchunks, ch, in_map, out_map, out_rows):
    return pl.pallas_call(
        _gla_kernel,
        grid=(nb, nchunks),
        in_specs=[
            pl.BlockSpec((ch, 1792), in_map),
            pl.BlockSpec((1, 4, 64, 128), lambda b, c: (b, 0, 0, 0)),
            pl.BlockSpec((1, 128), lambda b, c: (0, 0)),
        ],
        out_specs=[
            pl.BlockSpec((ch, 512), out_map),
            pl.BlockSpec((1, 4, 64, 128), lambda b, c: (b, 0, 0, 0)),
        ],
        out_shape=[
            jax.ShapeDtypeStruct((out_rows, 512), BF16),
            jax.ShapeDtypeStruct((nb, 4, 64, 128), F32),
        ],
        scratch_shapes=[pltpu.VMEM((4, 128, 64), F32)],
        compiler_params=_cparams(("arbitrary", "arbitrary")),
        name="gla",
    )(gl, s0, gn)


def _extract_topk(work, nsel, iota0, sentinel):
    vals, idxs = [], []
    for j in range(nsel):
        m = jnp.max(work, axis=0, keepdims=True)
        idx = jnp.min(jnp.where(work == m, iota0, sentinel), axis=0, keepdims=True)
        vals.append(m)
        idxs.append(idx)
        if j + 1 < nsel:
            work = jnp.where(iota0 == idx, -jnp.inf, work)
    return vals, idxs


def _merge_kernel(x_ref, gt_ref, ya_ref, yb_ref, wa_ref, wb_ref, wo_ref, gf_ref, wq_ref, keys_ref,
                  xm_ref, hn_ref, et_ref, wt_ref):
    td = x_ref.shape[0]
    nkeys = keys_ref.shape[1]
    half = keys_ref.shape[2]
    nheads = keys_ref.shape[0] // 2
    topk = PEER_TOPK

    gt = gt_ref[...]
    d = x_ref.shape[1]
    ma = jnp.dot(ya_ref[...], wa_ref[...], preferred_element_type=F32)
    mb = jnp.dot(yb_ref[...], wb_ref[...], preferred_element_type=F32)
    m = jax.nn.sigmoid(gt[:, 0:d]) * ma + jax.nn.sigmoid(gt[:, d:2 * d]) * mb
    xm = x_ref[...] + jnp.dot(m.astype(BF16), wo_ref[...], preferred_element_type=F32)
    xm_ref[...] = xm
    hn = _rms(xm, gf_ref[...])
    hn_ref[...] = hn
    q = jnp.dot(hn.astype(BF16), wq_ref[...], preferred_element_type=F32).astype(BF16)

    nt = (((1,), (1,)), ((), ()))
    iota_k = lax.broadcasted_iota(jnp.int32, (nkeys, td), 0)
    pair_rows = [topk // (a + 1) for a in range(topk)]
    cand_pad = -sum(pair_rows) % SUBLANES
    ncand = sum(pair_rows) + cand_pad
    iota_c = lax.broadcasted_iota(jnp.int32, (ncand, td), 0)
    wts, ids = [], []
    for h in range(nheads):
        sv, si = [], []
        for c in range(2):
            gi = 2 * h + c
            s_t = lax.dot_general(keys_ref[gi], q[:, gi * half:(gi + 1) * half], nt,
                                  preferred_element_type=F32)
            vals, idxs = _extract_topk(s_t, topk, iota_k, nkeys)
            sv.append(vals)
            si.append(idxs)
        sv1 = jnp.concatenate(sv[1], axis=0)
        si1 = jnp.concatenate(si[1], axis=0)
        cand = jnp.concatenate([sv[0][a] + sv1[0:nb] for a, nb in enumerate(pair_rows)]
                               + [jnp.full((cand_pad, td), -jnp.inf, F32)], axis=0)
        cidx = jnp.concatenate([si[0][a] * nkeys + si1[0:nb] for a, nb in enumerate(pair_rows)]
                               + [jnp.full((cand_pad, td), -1, jnp.int32)], axis=0)
        fvals, eids = [], []
        work = cand
        for j in range(topk):
            mx = jnp.max(work, axis=0, keepdims=True)
            pos = jnp.min(jnp.where(work == mx, iota_c, ncand), axis=0, keepdims=True)
            hit = iota_c == pos
            eids.append(jnp.max(jnp.where(hit, cidx, -1), axis=0, keepdims=True))
            fvals.append(mx)
            if j + 1 < topk:
                work = jnp.where(hit, -jnp.inf, work)
        fv = jnp.concatenate(fvals, axis=0)
        e = jnp.exp(fv - fvals[0])
        wts.append(e / jnp.sum(e, axis=0, keepdims=True))
        ids.extend(eids)
    wt_ref[...] = jnp.concatenate(wts, axis=0).T
    et_ref[...] = jnp.concatenate(ids, axis=0).T


def _merge_route(x, gt, ya, yb, wa, wb, wo, gf, wq, keys, gt_map):
    t, d = x.shape
    td = MERGE_ROWS
    nsel = (keys.shape[0] // 2) * PEER_TOPK
    const2 = lambda i: (0, 0)
    row = lambda i: (i, 0)
    return pl.pallas_call(
        _merge_kernel,
        grid=(t // td,),
        in_specs=[
            pl.BlockSpec((td, d), row),
            pl.BlockSpec((td, 2 * d), gt_map),
            pl.BlockSpec((td, ya.shape[1]), row),
            pl.BlockSpec((td, yb.shape[1]), row),
            pl.BlockSpec(wa.shape, const2),
            pl.BlockSpec(wb.shape, const2),
            pl.BlockSpec(wo.shape, const2),
            pl.BlockSpec(gf.shape, const2),
            pl.BlockSpec(wq.shape, const2),
            pl.BlockSpec(keys.shape, lambda i: (0, 0, 0)),
        ],
        out_specs=[
            pl.BlockSpec((td, d), row),
            pl.BlockSpec((td, d), row),
            pl.BlockSpec((td, nsel), row),
            pl.BlockSpec((td, nsel), row),
        ],
        out_shape=[
            jax.ShapeDtypeStruct((t, d), F32),
            jax.ShapeDtypeStruct((t, d), F32),
            jax.ShapeDtypeStruct((t, nsel), jnp.int32),
            jax.ShapeDtypeStruct((t, nsel), F32),
        ],
        compiler_params=_cparams(("arbitrary",)),
        name="merge_route",
    )(x, gt, ya, yb, wa, wb, wo, gf, wq, keys)


def _coef_kernel(act_ref, wt_ref, o_ref):
    act = act_ref[...]
    gelu = 0.5 * act * (1.0 + lax.erf(act * (2.0 ** -0.5)))
    bits = pltpu.bitcast((wt_ref[...] * gelu).astype(BF16).astype(F32), jnp.int32)
    o_ref[...] = bits | lax.shift_right_logical(bits, 16)


def _expert_coefs(act, wt):
    t, nsel = wt.shape
    rows = math.gcd(t, 512)
    row = lambda i: (i, 0)
    return pl.pallas_call(
        _coef_kernel,
        grid=(t // rows,),
        in_specs=[pl.BlockSpec((rows, nsel), row), pl.BlockSpec((rows, nsel), row)],
        out_specs=pl.BlockSpec((rows, nsel), row),
        out_shape=jax.ShapeDtypeStruct((t, nsel), jnp.int32),
        compiler_params=_cparams(("arbitrary",)),
        name="expert_coefs",
    )(act, wt)


def _finish_kernel(xm_ref, o_ref, gfin_ref, yacc_hbm, y_ref):
    del yacc_hbm
    y_ref[...] = _rms(xm_ref[...] + o_ref[...], gfin_ref[...])


def _finish(xm, o, gfin, y_acc, row_off):
    t, d = xm.shape
    rows = math.gcd(t, 512)
    first = row_off // rows
    row = lambda i: (i, 0)
    return pl.pallas_call(
        _finish_kernel,
        grid=(t // rows,),
        in_specs=[pl.BlockSpec((rows, d), row), pl.BlockSpec((rows, d), row),
                  pl.BlockSpec(gfin.shape, lambda i: (0, 0)), pl.BlockSpec(memory_space=pl.ANY)],
        out_specs=pl.BlockSpec((rows, d), lambda i: (i + first, 0)),
        out_shape=jax.ShapeDtypeStruct(y_acc.shape, F32),
        input_output_aliases={3: 0},
        compiler_params=_cparams(("arbitrary",)),
        name="finish",
    )(xm, o, gfin, y_acc)


def _rope_table(pos, valid, rope_dim, head_dim):
    half = rope_dim // 2
    inv = ROPE_THETA ** (-jnp.arange(0, rope_dim, 2, dtype=F32) / rope_dim)
    ang = pos.astype(F32)[:, None] * inv[None, :]
    cos, sin = jnp.cos(ang), jnp.sin(ang)
    n = pos.shape[0]
    ones = jnp.ones((n, head_dim - rope_dim), F32)
    zeros_h = jnp.zeros((n, half), F32)
    zeros_r = jnp.zeros((n, head_dim - rope_dim), F32)
    reps = LANES // head_dim
    cosf = jnp.tile(jnp.concatenate([cos, cos, ones], axis=1), (1, reps))
    sin_lo = jnp.tile(jnp.concatenate([zeros_h, sin, zeros_r], axis=1), (1, reps))
    sin_hi = jnp.tile(jnp.concatenate([-sin, zeros_h, zeros_r], axis=1), (1, reps))
    vcol = jnp.broadcast_to(valid.astype(F32)[:, None], (n, LANES))
    return jnp.concatenate([cosf, sin_lo, sin_hi, vcol], axis=1)


def _pack_pairs(x):
    half = x.shape[1] // 2
    bits = lax.bitcast_convert_type(x.astype(BF16), jnp.uint16).astype(jnp.uint32)
    return lax.bitcast_convert_type(bits[:, :half] | (bits[:, half:] << 16), jnp.int32)


def _expert_dots(table, idx, hn, nsel):
    n = idx.shape[0]
    c = table.shape[1]
    sc = plsc.get_sparse_core_info()
    lanes = sc.num_lanes
    workers = sc.num_cores * sc.num_subcores
    gw = GATHER_WINDOW
    ich = GATHER_INDEX_CHUNK
    tok = ich // nsel
    per_worker = n // workers
    assert n % workers == 0 and per_worker % ich == 0 and nsel == 2 * gw and hn.shape[1] == c and gw % lanes == 0
    mesh = plsc.VectorSubcoreMesh(core_axis_name="c", subcore_axis_name="s")

    @functools.partial(
        pl.kernel, out_type=jax.ShapeDtypeStruct((n,), F32), mesh=mesh, name="expert_dots",
        compiler_params=pltpu.CompilerParams(needs_layout_passes=False),
        scratch_types=[pltpu.VMEM((ich,), jnp.int32),
                       pltpu.VMEM((gw, c), jnp.int32), pltpu.VMEM((gw, c), jnp.int32),
                       pltpu.VMEM((tok, c), jnp.int32), pltpu.VMEM((ich,), F32), pltpu.VMEM((lanes * lanes,), F32),
                       pltpu.SemaphoreType.DMA, pltpu.SemaphoreType.DMA])
    def dots(tab_hbm, idx_hbm, hn_hbm, act_hbm, idx_v, rows0, rows1, h_v, act_v, scr, sem0, sem1):
        wid = lax.axis_index("s") * sc.num_cores + lax.axis_index("c")
        base = wid * per_worker
        lane = lax.iota(jnp.int32, lanes)

        def gather(win, buf, sem):
            return pltpu.make_async_copy(tab_hbm.at[idx_v.at[pl.ds(win * gw, gw)]], buf, sem)

        def reduce_window(buf, t_loc, out_off):
            for rb in range(gw // lanes):
                def kbody(k, accs):
                    hw = plsc.bitcast(h_v[t_loc, pl.ds(k * lanes, lanes)], BF16)
                    out = []
                    for r in range(lanes):
                        w = buf[rb * lanes + r, pl.ds(k * lanes, lanes)]
                        p = plsc.bitcast(plsc.bitcast(w, BF16) * hw, jnp.int32)
                        out.append(accs[r] + lax.bitcast_convert_type(p << 16, F32)
                                   + lax.bitcast_convert_type(p, F32))
                    return tuple(out)

                accs = lax.fori_loop(0, c // lanes, kbody,
                                     tuple(jnp.zeros((lanes,), F32) for _ in range(lanes)))
                for r in range(lanes):
                    scr[pl.ds(r * lanes, lanes)] = accs[r]
                tot = plsc.load_gather(scr, [lane * lanes])
                for l in range(1, lanes):
                    tot = tot + plsc.load_gather(scr, [lane * lanes + l])
                act_v[pl.ds(out_off + rb * lanes, lanes)] = tot

        @pl.loop(0, per_worker // ich)
        def _(g):
            cb = base + g * ich
            pltpu.sync_copy(idx_hbm.at[pl.ds(cb, ich)], idx_v)
            tok_base = pl.multiple_of(wid * (per_worker // nsel) + g * tok, tok)
            pltpu.sync_copy(hn_hbm.at[pl.ds(tok_base, tok)], h_v)
            gather(0, rows0, sem0).start()
            gather(1, rows1, sem1).start()

            @pl.loop(0, tok)
            def _(j):
                gather(2 * j, rows0, sem0).wait()
                reduce_window(rows0, j, j * nsel)

                @pl.when(j + 1 < tok)
                def _():
                    gather(2 * j + 2, rows0, sem0).start()

                gather(2 * j + 1, rows1, sem1).wait()
                reduce_window(rows1, j, j * nsel + gw)

                @pl.when(j + 1 < tok)
                def _():
                    gather(2 * j + 3, rows1, sem1).start()

            pltpu.sync_copy(act_v, act_hbm.at[pl.ds(cb, ich)])

    return dots(table, idx, hn)


def _expert_mix(table, idx, coef, nsel):
    n = idx.shape[0]
    c = table.shape[1]
    d = 2 * c
    sc = plsc.get_sparse_core_info()
    lanes = sc.num_lanes
    workers = sc.num_cores * sc.num_subcores
    gw = GATHER_WINDOW
    ich = GATHER_INDEX_CHUNK
    tok = ich // nsel
    per_worker = n // workers
    kblock = 8
    assert n % workers == 0 and per_worker % ich == 0 and nsel == 2 * gw and c % (kblock * lanes) == 0
    mesh = plsc.VectorSubcoreMesh(core_axis_name="c", subcore_axis_name="s")

    @functools.partial(
        pl.kernel, out_type=jax.ShapeDtypeStruct((n // nsel, d), F32), mesh=mesh, name="expert_mix",
        compiler_params=pltpu.CompilerParams(needs_layout_passes=False),
        scratch_types=[pltpu.VMEM((ich,), jnp.int32), pltpu.VMEM((ich,), jnp.int32),
                       pltpu.VMEM((gw, c), jnp.int32), pltpu.VMEM((gw, c), jnp.int32),
                       pltpu.VMEM((tok, d), F32),
                       pltpu.SemaphoreType.DMA, pltpu.SemaphoreType.DMA])
    def mix(tab_hbm, idx_hbm, coef_hbm, out_hbm, idx_v, coef_v, rows0, rows1, out_v, sem0, sem1):
        wid = lax.axis_index("s") * sc.num_cores + lax.axis_index("c")
        base = wid * per_worker
        zero_idx = jnp.zeros((lanes,), jnp.int32)

        def gather(win, buf, sem):
            return pltpu.make_async_copy(tab_hbm.at[idx_v.at[pl.ds(win * gw, gw)]], buf, sem)

        def accumulate_window(buf, t_loc, coef_off, first):
            for kb in range(c // (kblock * lanes)):
                col0 = kb * kblock * lanes
                if first:
                    init = tuple(jnp.zeros((lanes,), F32) for _ in range(2 * kblock))
                else:
                    init = tuple(out_v[t_loc, pl.ds(col0 + i * lanes, lanes)] for i in range(kblock)) + \
                           tuple(out_v[t_loc, pl.ds(c + col0 + i * lanes, lanes)] for i in range(kblock))

                def rbody(r2, accs):
                    accs = list(accs)
                    for rr in range(2):
                        r = 2 * r2 + rr
                        cw = plsc.bitcast(plsc.load_gather(coef_v, [zero_idx + (coef_off + r)]), BF16)
                        for i in range(kblock):
                            w = buf[r, pl.ds(col0 + i * lanes, lanes)]
                            p = plsc.bitcast(plsc.bitcast(w, BF16) * cw, jnp.int32)
                            accs[i] = accs[i] + lax.bitcast_convert_type(p << 16, F32)
                            accs[kblock + i] = accs[kblock + i] + lax.bitcast_convert_type(p, F32)
                    return tuple(accs)

                accs = lax.fori_loop(0, gw // 2, rbody, init)
                for i in range(kblock):
                    out_v[t_loc, pl.ds(col0 + i * lanes, lanes)] = accs[i]
                    out_v[t_loc, pl.ds(c + col0 + i * lanes, lanes)] = accs[kblock + i]

        @pl.loop(0, per_worker // ich)
        def _(g):
            cb = base + g * ich
            pltpu.sync_copy(idx_hbm.at[pl.ds(cb, ich)], idx_v)
            pltpu.sync_copy(coef_hbm.at[pl.ds(cb, ich)], coef_v)
            gather(0, rows0, sem0).start()
            gather(1, rows1, sem1).start()

            @pl.loop(0, tok)
            def _(j):
                gather(2 * j, rows0, sem0).wait()
                accumulate_window(rows0, j, j * nsel, True)

                @pl.when(j + 1 < tok)
                def _():
                    gather(2 * j + 2, rows0, sem0).start()

                gather(2 * j + 1, rows1, sem1).wait()
                accumulate_window(rows1, j, j * nsel + gw, False)

                @pl.when(j + 1 < tok)
                def _():
                    gather(2 * j + 3, rows1, sem1).start()

            tok_base = pl.multiple_of(wid * (per_worker // nsel) + g * tok, tok)
            pltpu.sync_copy(out_v, out_hbm.at[pl.ds(tok_base, tok)])

    return mix(table, idx, coef)


def _peer_tail(xm, hn, et, wt, tabs, gfin, y_acc, row_off):
    u_tab, v_tab = tabs
    t, nsel = et.shape
    eidx = et.reshape(t * nsel)
    act = _expert_dots(u_tab, eidx, _pack_pairs(hn), nsel).reshape(t, nsel)
    coef = _expert_coefs(act, wt).reshape(t * nsel)
    mixed = _expert_mix(v_tab, eidx, coef, nsel)
    return _finish(xm, mixed, gfin, y_acc, row_off)


def kernel(x_prompt, x_sample, cache_k_window, cache_v_window, state_gla, meta_tokens, g_norm_mix, w_in,
           w_gate_up, b_gate, attn_sinks, g_gla_norm, w_branch_a, w_branch_b, w_out, g_norm_ffn, w_peer_q,
           peer_sub_keys, peer_u, peer_v, g_norm_final):
    bsz, seq, d = x_prompt.shape
    dbsz, tdec, _ = x_sample.shape
    n_meta = meta_tokens.shape[0]
    depth = w_in.shape[0]
    window = cache_k_window.shape[2]
    kv_heads, head_dim = cache_k_window.shape[3], cache_k_window.shape[4]
    gate_rank = w_gate_up.shape[1]
    bqk = w_gate_up.shape[2]
    n_ph, _, n_keys, p_half = peer_sub_keys.shape[1:]
    assert depth == 1 and d == 1024 and window == ATTN_BLOCK and kv_heads == 2 and head_dim == 64
    assert bqk == 256 and state_gla.shape[2:] == (4, 64, 128) and n_meta <= ATTN_BLOCK
    assert seq % ATTN_BLOCK == 0 and tdec <= SAMPLE_PAD and n_keys == 128 and p_half == 64 and n_ph == 8
    rope_dim = head_dim // 4
    meta_pad = ATTN_BLOCK - n_meta
    lp = ATTN_BLOCK + seq
    nblk = lp // ATTN_BLOCK

    w = w_in[0]
    c_lr = 2304
    c_gate = c_lr + gate_rank
    w1 = w[:, :c_lr].astype(BF16)
    wlr = jnp.pad(w[:, c_lr:c_gate], ((0, 0), (0, LANES - gate_rank))).astype(BF16)
    w2 = w[:, c_gate:].astype(BF16)
    wgu = jnp.pad(w_gate_up[0], ((0, LANES - gate_rank), (0, 0))).astype(BF16)
    bg = b_gate[0][None, :]
    gmix = g_norm_mix[0][None, :]
    wa = w_branch_a[0].astype(BF16)
    wb = w_branch_b[0].astype(BF16)
    wo = w_out[0].astype(BF16)
    gffn = g_norm_ffn[0][None, :]
    wq = w_peer_q[0].astype(BF16)
    keys = peer_sub_keys[0].reshape(n_ph * 2, n_keys, p_half).astype(BF16)
    u_tab = _pack_pairs(peer_u[0])
    v_tab = _pack_pairs(peer_v[0])
    tabs = (u_tab, v_tab)
    gfin = g_norm_final[None, :]
    gn = g_gla_norm[0][None, :]
    sinks = attn_sinks[0]
    qk_scale = float(bqk // 4) ** -0.5

    rows_p = jnp.arange(lp)
    tab_p = _rope_table(rows_p - meta_pad, rows_p >= meta_pad, rope_dim, head_dim)
    proj_rows = max(r for r in range(16, PROJ_ROWS + 1, 16) if lp % r == 0)
    nq = nblk - 1
    nchunks = lp // GLA_CHUNK
    skip = ATTN_BLOCK // GLA_CHUNK
    ncq = nchunks - skip
    per_seq = seq // MERGE_ROWS
    gt_map_p = lambda i: ((i // per_seq) * nblk + 1 + (i % per_seq), 0)

    def prompt_sequences(xg, y_acc, row_off):
        gb = xg.shape[0]
        meta = jnp.broadcast_to(meta_tokens[None].astype(xg.dtype), (gb, n_meta, d))
        xpad = jnp.concatenate([jnp.zeros((gb, meta_pad, d), xg.dtype), meta, xg], axis=1).reshape(gb * lp, d)
        qa, kv, gl, gt = _project(xpad, gmix, tab_p, w1, wlr, wgu, bg, w2, proj_rows, qk_scale)
        ya = _attention(
            sinks, qa, kv, kv, gb, nq, ATTN_BLOCK,
            lambda b, n: (b * nblk + n + 1, 0), lambda b, n: (b * nblk + n, 0), lambda b, n: (b * nblk + n + 1, 0),
            lambda b, n: (b * nq + n, 0), gb * seq, first_valid_key=meta_pad, block_offset=1)
        s0 = jnp.zeros((gb,) + state_gla.shape[2:], F32)
        yb, s_fin = _gla(gl, s0, gn, gb, nchunks, GLA_CHUNK,
                         lambda b, c: (b * nchunks + c, 0),
                         lambda b, c: (b * ncq + jnp.maximum(c - skip, 0), 0), gb * seq)
        xm, hn, et, wt = _merge_route(xg.reshape(gb * seq, d), gt, ya, yb, wa, wb, wo, gffn, wq, keys,
                                      gt_map_p)
        y_acc = _peer_tail(xm, hn, et, wt, tabs, gfin, y_acc, row_off)
        kv_w = kv.reshape(gb, lp, 2, kv_heads, head_dim)[:, lp - window:]
        return y_acc, kv_w, s_fin

    group = PROMPT_GROUP if (bsz % PROMPT_GROUP == 0 and (PROMPT_GROUP * seq) % GATHER_ROW_QUANTUM == 0) else bsz
    y_acc, kv_parts, s_parts = jnp.zeros((bsz * seq, d), F32), [], []
    for b0 in range(0, bsz, group):
        y_acc, kv_w, s_fin = prompt_sequences(x_prompt[b0:b0 + group], y_acc, b0 * seq)
        kv_parts.append(kv_w)
        s_parts.append(s_fin)
    y_prompt = y_acc.reshape(bsz, seq, d)
    kv_p = jnp.concatenate(kv_parts, axis=0)
    s_fin_p = jnp.concatenate(s_parts, axis=0)
    new_k_p = kv_p[:, :, 0][None]
    new_v_p = kv_p[:, :, 1][None]

    sp = SAMPLE_PAD
    xs_pad = jnp.pad(x_sample, ((0, 0), (0, sp - tdec), (0, 0))).reshape(dbsz * sp, d)
    rows_s = jnp.arange(sp)
    reps = 256 // sp
    tab_s = jnp.tile(_rope_table(PAST_LEN + rows_s, rows_s < tdec, rope_dim, head_dim), (reps, 1))
    qa_s, kv_s, gl_s, gt_s = _project(xs_pad, gmix, tab_s, w1, wlr, wgu, bg, w2, 256, qk_scale)

    cache_kv = jnp.concatenate([cache_k_window[0].reshape(dbsz * window, kv_heads * head_dim),
                                cache_v_window[0].reshape(dbsz * window, kv_heads * head_dim)], axis=1)
    seq_map = lambda b, n: (b, 0)
    ya_s = _attention(sinks, qa_s, cache_kv, kv_s, dbsz, 1, sp, seq_map, seq_map, seq_map, seq_map,
                      dbsz * sp, first_valid_key=None, block_offset=0)
    yb_s, s_fin_s = _gla(gl_s, state_gla[0], gn, dbsz, 1, sp, seq_map, seq_map, dbsz * sp)

    def real_rows(a):
        return a.reshape(dbsz, sp, a.shape[-1])[:, :tdec].reshape(dbsz * tdec, a.shape[-1])

    xs_rows = x_sample.reshape(dbsz * tdec, d)
    xm_s, hn_s, et_s, wt_s = _merge_route(xs_rows, real_rows(gt_s), real_rows(ya_s), real_rows(yb_s),
                                          wa, wb, wo, gffn, wq, keys, lambda i: (i, 0))
    y_sample = _peer_tail(xm_s, hn_s, et_s, wt_s, tabs, gfin, jnp.zeros((dbsz * tdec, d), F32), 0).reshape(dbsz, tdec, d)

    kv_new = real_rows(kv_s).reshape(dbsz, tdec, 2, kv_heads, head_dim)
    new_k_s = jnp.concatenate([cache_k_window[0].astype(F32), kv_new[:, :, 0]], axis=1)[:, -window:][None]
    new_v_s = jnp.concatenate([cache_v_window[0].astype(F32), kv_new[:, :, 1]], axis=1)[:, -window:][None]

    return (y_prompt, y_sample, new_k_p, new_v_p, s_fin_p[None], new_k_s, new_v_s, s_fin_s[None])
```

```python
import functools
import math

import jax
import jax.numpy as jnp
from jax import lax
from jax.experimental import pallas as pl
from jax.experimental.pallas import tpu as pltpu
from jax.experimental.pallas import tpu_sc as plsc

F32 = jnp.float32
BF16 = jnp.bfloat16

EPS = 1e-6
NEG_INF = -1e30
PAST_LEN = 16384
ROPE_THETA = 500000.0
GATE_NORMALIZER = 16.0
PEER_TOPK = 16

LANES = 128
SUBLANES = 8
VMEM_LIMIT_BYTES = 56 * 1024 * 1024

ATTN_BLOCK = 128
GLA_CHUNK = 64
SAMPLE_PAD = 16
PROJ_ROWS = 544
MERGE_ROWS = 128
GATHER_WINDOW = 64
GATHER_INDEX_CHUNK = 2048
GATHER_ROW_QUANTUM = 512
PROMPT_GROUP = 1


def _cparams(sem):
    return pltpu.CompilerParams(dimension_semantics=sem, vmem_limit_bytes=VMEM_LIMIT_BYTES)


def _rms(x, g):
    ms = jnp.mean(x * x, axis=-1, keepdims=True)
    return (x * lax.rsqrt(ms + EPS)) * g


def _proj_kernel(x_ref, g_ref, tab_ref, w1_ref, wlr_ref, wgu_ref, bg_ref, w2_ref,
                 qa_ref, kv_ref, gl_ref, gt_ref, *, period, qk_scale):
    i = pl.program_id(0)
    tr = x_ref.shape[0]
    hb = _rms(x_ref[...], g_ref[...]).astype(BF16)
    z1 = jnp.dot(hb, w1_ref[...], preferred_element_type=F32)

    start = pl.multiple_of((i * tr) % period, SUBLANES)
    tab = tab_ref[pl.ds(start, tr), :]
    cosf = tab[:, 0:LANES]
    sin_lo = tab[:, LANES:2 * LANES]
    sin_hi = tab[:, 2 * LANES:3 * LANES]
    valid = tab[:, 3 * LANES:3 * LANES + 1]

    def rope(xg):
        return xg * cosf + pltpu.roll(xg, 8, 1) * sin_lo + pltpu.roll(xg, LANES - 8, 1) * sin_hi

    for gi in range(4):
        sl = slice(gi * LANES, (gi + 1) * LANES)
        qa_ref[:, sl] = rope(z1[:, sl]).astype(BF16)
    kv_ref[:, 0:LANES] = rope(z1[:, 512:640])
    kv_ref[:, LANES:2 * LANES] = z1[:, 640:768]

    lr = jnp.dot(hb, wlr_ref[...], preferred_element_type=F32)
    pre = jnp.dot(lr.astype(BF16), wgu_ref[...], preferred_element_type=F32) + bg_ref[...]
    log_sig = jnp.minimum(pre, 0.0) - jnp.log1p(jnp.exp(-jnp.abs(pre)))
    ld = jnp.where(valid > 0.5, log_sig / GATE_NORMALIZER, 0.0)

    gl_ref[:, 0:256] = z1[:, 768:1024] * qk_scale
    gl_ref[:, 256:512] = z1[:, 1024:1280]
    gl_ref[:, 512:768] = ld
    gl_ref[:, 768:1792] = z1[:, 1280:2304]
    gt_ref[...] = jnp.dot(hb, w2_ref[...], preferred_element_type=F32)


def _project(x, g, tab, w1, wlr, wgu, bg, w2, rows, qk_scale):
    r, d = x.shape
    period = tab.shape[0]
    const = lambda i: (0, 0)
    row = lambda i: (i, 0)
    return pl.pallas_call(
        functools.partial(_proj_kernel, period=period, qk_scale=qk_scale),
        grid=(r // rows,),
        in_specs=[
            pl.BlockSpec((rows, d), row),
            pl.BlockSpec(g.shape, const),
            pl.BlockSpec(tab.shape, const),
            pl.BlockSpec(w1.shape, const),
            pl.BlockSpec(wlr.shape, const),
            pl.BlockSpec(wgu.shape, const),
            pl.BlockSpec(bg.shape, const),
            pl.BlockSpec(w2.shape, const),
        ],
        out_specs=[
            pl.BlockSpec((rows, 512), row),
            pl.BlockSpec((rows, 256), row),
            pl.BlockSpec((rows, 1792), row),
            pl.BlockSpec((rows, 2048), row),
        ],
        out_shape=[
            jax.ShapeDtypeStruct((r, 512), BF16),
            jax.ShapeDtypeStruct((r, 256), F32),
            jax.ShapeDtypeStruct((r, 1792), F32),
            jax.ShapeDtypeStruct((r, 2048), F32),
        ],
        compiler_params=_cparams(("arbitrary",)),
        name="proj",
    )(x, g, tab, w1, wlr, wgu, bg, w2)


def _attn_kernel(sink_ref, q_ref, prev_ref, cur_ref, o_ref, *, first_valid_key, block_offset):
    n = pl.program_id(1)
    qr = q_ref.shape[0]
    kr = cur_ref.shape[0]
    w = prev_ref.shape[0]
    nk = w + kr
    group = 4
    hd = 64

    rows = lax.broadcasted_iota(jnp.int32, (group * qr, nk), 0)
    cols = lax.broadcasted_iota(jnp.int32, (group * qr, nk), 1)
    head_of_row = rows // qr
    diff = (rows - head_of_row * qr) - cols + w
    mask = (diff >= 0) & (diff <= w)
    if first_valid_key is not None:
        blk = n + block_offset
        mask = mask & (cols >= first_valid_key + w - blk * w)

    prev = prev_ref[...]
    cur = cur_ref[...]
    q = q_ref[...]
    row_head = lax.broadcasted_iota(jnp.int32, (group * qr, 1), 0) // qr
    for kh in range(2):
        k = jnp.concatenate([prev[:, kh * hd:(kh + 1) * hd], cur[:, kh * hd:(kh + 1) * hd]], axis=0).astype(BF16)
        v = jnp.concatenate([prev[:, LANES + kh * hd:LANES + (kh + 1) * hd],
                             cur[:, LANES + kh * hd:LANES + (kh + 1) * hd]], axis=0).astype(BF16)
        qs = jnp.concatenate([q[:, (group * kh + g) * hd:(group * kh + g + 1) * hd] for g in range(group)], axis=0)
        s = lax.dot_general(qs, k, (((1,), (1,)), ((), ())), preferred_element_type=F32) * (hd ** -0.5)
        s = jnp.where(mask, s, NEG_INF)
        sink = jnp.zeros((group * qr, 1), F32)
        for g in range(group):
            sink = jnp.where(row_head == g, sink_ref[group * kh + g], sink)
        m = jnp.maximum(jnp.max(s, axis=-1, keepdims=True), sink)
        e = jnp.exp(s - m)
        p = e / (jnp.sum(e, axis=-1, keepdims=True) + jnp.exp(sink - m))
        o = jnp.dot(p.astype(BF16), v, preferred_element_type=F32)
        for g in range(group):
            h = group * kh + g
            o_ref[:, h * hd:(h + 1) * hd] = o[g * qr:(g + 1) * qr].astype(BF16)


def _attention(sinks, q, kv_prev, kv_cur, nb, nblk, qr, q_map, prev_map, cur_map, out_map, out_rows,
               first_valid_key, block_offset):
    w = ATTN_BLOCK
    return pl.pallas_call(
        functools.partial(_attn_kernel, first_valid_key=first_valid_key, block_offset=block_offset),
        grid=(nb, nblk),
        in_specs=[
            pl.BlockSpec(memory_space=pltpu.SMEM),
            pl.BlockSpec((qr, 512), q_map),
            pl.BlockSpec((w, 256), prev_map),
            pl.BlockSpec((qr, 256), cur_map),
        ],
        out_specs=pl.BlockSpec((qr, 512), out_map),
        out_shape=jax.ShapeDtypeStruct((out_rows, 512), BF16),
        compiler_params=_cparams(("arbitrary", "arbitrary")),
        name="swa",
    )(sinks, q, kv_prev, kv_cur)


def _gla_kernel(gl_ref, s0_ref, gn_ref, yb_ref, sfin_ref, st_ref):
    c = pl.program_id(1)
    ch = gl_ref.shape[0]
    nh, dk, dv = 4, 64, 128

    @pl.when(c == 0)
    def _():
        for h in range(nh):
            st_ref[h] = s0_ref[0, h].T

    gl = gl_ref[...]
    q = gl[:, 0:256]
    k = gl[:, 256:512]
    b = gl[:, 512:768]
    row = lax.broadcasted_iota(jnp.int32, (ch, nh * dk), 0)
    sh = 1
    while sh < ch:
        b = b + jnp.where(row >= sh, pltpu.roll(b, sh, 0), 0.0)
        sh *= 2
    b_last = b[ch - 1:ch, :]
    q_t = (q * jnp.exp(b)).astype(BF16)
    k_t = (k * jnp.exp(-b)).astype(BF16)
    k_end = (k * jnp.exp(b_last - b)).astype(BF16)
    decay = jnp.exp(b_last)
    causal = (lax.broadcasted_iota(jnp.int32, (ch, ch), 0) >= lax.broadcasted_iota(jnp.int32, (ch, ch), 1))
    gn = gn_ref[...]
    nt = (((1,), (1,)), ((), ()))
    for h in range(nh):
        ks = slice(h * dk, (h + 1) * dk)
        v = gl[:, 768 + h * dv:768 + (h + 1) * dv]
        vb = v.astype(BF16)
        a = lax.dot_general(q_t[:, ks], k_t[:, ks], nt, preferred_element_type=F32)
        a = jnp.where(causal, a, 0.0)
        s_t = st_ref[h]
        o = jnp.dot(a.astype(BF16), vb, preferred_element_type=F32)
        o = o + lax.dot_general(q_t[:, ks], s_t.astype(BF16), nt, preferred_element_type=F32)
        upd = jnp.dot(v.T.astype(BF16), k_end[:, ks], preferred_element_type=F32)
        st_ref[h] = s_t * decay[:, ks] + upd
        go = gl[:, 1280 + h * dv:1280 + (h + 1) * dv]
        y = _rms(o, gn) * (go * jax.nn.sigmoid(go))
        yb_ref[:, h * dv:(h + 1) * dv] = y.astype(BF16)

    @pl.when(c == pl.num_programs(1) - 1)
    def _():
        for h in range(nh):
            sfin_ref[0, h] = st_ref[h].T


def _gla(gl, s0, gn, nb, nchunks, ch, in_map, out_map, out_rows):
    return pl.pallas_call(
        _gla_kernel,
        grid=(nb, nchunks),
        in_specs=[
            pl.BlockSpec((ch, 1792), in_map),
            pl.BlockSpec((1, 4, 64, 128), lambda b, c: (b, 0, 0, 0)),
            pl.BlockSpec((1, 128), lambda b, c: (0, 0)),
        ],
        out_specs=[
            pl.BlockSpec((ch, 512), out_map),
            pl.BlockSpec((1, 4, 64, 128), lambda b, c: (b, 0, 0, 0)),
        ],
        out_shape=[
            jax.ShapeDtypeStruct((out_rows, 512), BF16),
            jax.ShapeDtypeStruct((nb, 4, 64, 128), F32),
        ],
        scratch_shapes=[pltpu.VMEM((4, 128, 64), F32)],
        compiler_params=_cparams(("arbitrary", "arbitrary")),
        name="gla",
    )(gl, s0, gn)


def _extract_topk(work, nsel, iota0, sentinel):
    vals, idxs = [], []
    for j in range(nsel):
        m = jnp.max(work, axis=0, keepdims=True)
        idx = jnp.min(jnp.where(work == m, iota0, sentinel), axis=0, keepdims=True)
        vals.append(m)
        idxs.append(idx)
        if j + 1 < nsel:
            work = jnp.where(iota0 == idx, -jnp.inf, work)
    return vals, idxs


def _merge_kernel(x_ref, gt_ref, ya_ref, yb_ref, wa_ref, wb_ref, wo_ref, gf_ref, wq_ref, keys_ref,
                  xm_ref, hn_ref, et_ref, wt_ref):
    td = x_ref.shape[0]
    nkeys = keys_ref.shape[1]
    half = keys_ref.shape[2]
    nheads = keys_ref.shape[0] // 2
    topk = PEER_TOPK

    gt = gt_ref[...]
    d = x_ref.shape[1]
    ma = jnp.dot(ya_ref[...], wa_ref[...], preferred_element_type=F32)
    mb = jnp.dot(yb_ref[...], wb_ref[...], preferred_element_type=F32)
    m = jax.nn.sigmoid(gt[:, 0:d]) * ma + jax.nn.sigmoid(gt[:, d:2 * d]) * mb
    xm = x_ref[...] + jnp.dot(m.astype(BF16), wo_ref[...], preferred_element_type=F32)
    xm_ref[...] = xm
    hn = _rms(xm, gf_ref[...])
    hn_ref[...] = hn
    q = jnp.dot(hn.astype(BF16), wq_ref[...], preferred_element_type=F32).astype(BF16)

    nt = (((1,), (1,)), ((), ()))
    iota_k = lax.broadcasted_iota(jnp.int32, (nkeys, td), 0)
    pair_rows = [topk // (a + 1) for a in range(topk)]
    cand_pad = -sum(pair_rows) % SUBLANES
    ncand = sum(pair_rows) + cand_pad
    iota_c = lax.broadcasted_iota(jnp.int32, (ncand, td), 0)
    wts, ids = [], []
    for h in range(nheads):
        sv, si = [], []
        for c in range(2):
            gi = 2 * h + c
            s_t = lax.dot_general(keys_ref[gi], q[:, gi * half:(gi + 1) * half], nt,
                                  preferred_element_type=F32)
            vals, idxs = _extract_topk(s_t, topk, iota_k, nkeys)
            sv.append(vals)
            si.append(idxs)
        sv1 = jnp.concatenate(sv[1], axis=0)
        si1 = jnp.concatenate(si[1], axis=0)
        cand = jnp.concatenate([sv[0][a] + sv1[0:nb] for a, nb in enumerate(pair_rows)]
                               + [jnp.full((cand_pad, td), -jnp.inf, F32)], axis=0)
        cidx = jnp.concatenate([si[0][a] * nkeys + si1[0:nb] for a, nb in enumerate(pair_rows)]
                               + [jnp.full((cand_pad, td), -1, jnp.int32)], axis=0)
        fvals, eids = [], []
        work = cand
        for j in range(topk):
            mx = jnp.max(work, axis=0, keepdims=True)
            pos = jnp.min(jnp.where(work == mx, iota_c, ncand), axis=0, keepdims=True)
            hit = iota_c == pos
            eids.append(jnp.max(jnp.where(hit, cidx, -1), axis=0, keepdims=True))
            fvals.append(mx)
            if j + 1 < topk:
                work = jnp.where(hit, -jnp.inf, work)
        fv = jnp.concatenate(fvals, axis=0)
        e = jnp.exp(fv - fvals[0])
        wts.append(e / jnp.sum(e, axis=0, keepdims=True))
        ids.extend(eids)
    wt_ref[...] = jnp.concatenate(wts, axis=0).T
    et_ref[...] = jnp.concatenate(ids, axis=0).T


def _merge_route(x, gt, ya, yb, wa, wb, wo, gf, wq, keys, gt_map):
    t, d = x.shape
    td = MERGE_ROWS
    nsel = (keys.shape[0] // 2) * PEER_TOPK
    const2 = lambda i: (0, 0)
    row = lambda i: (i, 0)
    return pl.pallas_call(
        _merge_kernel,
        grid=(t // td,),
        in_specs=[
            pl.BlockSpec((td, d), row),
            pl.BlockSpec((td, 2 * d), gt_map),
            pl.BlockSpec((td, ya.shape[1]), row),
            pl.BlockSpec((td, yb.shape[1]), row),
            pl.BlockSpec(wa.shape, const2),
            pl.BlockSpec(wb.shape, const2),
            pl.BlockSpec(wo.shape, const2),
            pl.BlockSpec(gf.shape, const2),
            pl.BlockSpec(wq.shape, const2),
            pl.BlockSpec(keys.shape, lambda i: (0, 0, 0)),
        ],
        out_specs=[
            pl.BlockSpec((td, d), row),
            pl.BlockSpec((td, d), row),
            pl.BlockSpec((td, nsel), row),
            pl.BlockSpec((td, nsel), row),
        ],
        out_shape=[
            jax.ShapeDtypeStruct((t, d), F32),
            jax.ShapeDtypeStruct((t, d), F32),
            jax.ShapeDtypeStruct((t, nsel), jnp.int32),
            jax.ShapeDtypeStruct((t, nsel), F32),
        ],
        compiler_params=_cparams(("arbitrary",)),
        name="merge_route",
    )(x, gt, ya, yb, wa, wb, wo, gf, wq, keys)


def _coef_kernel(act_ref, wt_ref, o_ref):
    act = act_ref[...]
    gelu = 0.5 * act * (1.0 + lax.erf(act * (2.0 ** -0.5)))
    bits = pltpu.bitcast((wt_ref[...] * gelu).astype(BF16).astype(F32), jnp.int32)
    o_ref[...] = bits | lax.shift_right_logical(bits, 16)


def _expert_coefs(act, wt):
    t, nsel = wt.shape
    rows = math.gcd(t, 512)
    row = lambda i: (i, 0)
    return pl.pallas_call(
        _coef_kernel,
        grid=(t // rows,),
        in_specs=[pl.BlockSpec((rows, nsel), row), pl.BlockSpec((rows, nsel), row)],
        out_specs=pl.BlockSpec((rows, nsel), row),
        out_shape=jax.ShapeDtypeStruct((t, nsel), jnp.int32),
        compiler_params=_cparams(("arbitrary",)),
        name="expert_coefs",
    )(act, wt)


def _finish_kernel(xm_ref, o_ref, gfin_ref, yacc_hbm, y_ref):
    del yacc_hbm
    y_ref[...] = _rms(xm_ref[...] + o_ref[...], gfin_ref[...])


def _finish(xm, o, gfin, y_acc, row_off):
    t, d = xm.shape
    rows = math.gcd(t, 512)
    first = row_off // rows
    row = lambda i: (i, 0)
    return pl.pallas_call(
        _finish_kernel,
        grid=(t // rows,),
        in_specs=[pl.BlockSpec((rows, d), row), pl.BlockSpec((rows, d), row),
                  pl.BlockSpec(gfin.shape, lambda i: (0, 0)), pl.BlockSpec(memory_space=pl.ANY)],
        out_specs=pl.BlockSpec((rows, d), lambda i: (i + first, 0)),
        out_shape=jax.ShapeDtypeStruct(y_acc.shape, F32),
        input_output_aliases={3: 0},
        compiler_params=_cparams(("arbitrary",)),
        name="finish",
    )(xm, o, gfin, y_acc)


def _rope_table(pos, valid, rope_dim, head_dim):
    half = rope_dim // 2
    inv = ROPE_THETA ** (-jnp.arange(0, rope_dim, 2, dtype=F32) / rope_dim)
    ang = pos.astype(F32)[:, None] * inv[None, :]
    cos, sin = jnp.cos(ang), jnp.sin(ang)
    n = pos.shape[0]
    ones = jnp.ones((n, head_dim - rope_dim), F32)
    zeros_h = jnp.zeros((n, half), F32)
    zeros_r = jnp.zeros((n, head_dim - rope_dim), F32)
    reps = LANES // head_dim
    cosf = jnp.tile(jnp.concatenate([cos, cos, ones], axis=1), (1, reps))
    sin_lo = jnp.tile(jnp.concatenate([zeros_h, sin, zeros_r], axis=1), (1, reps))
    sin_hi = jnp.tile(jnp.concatenate([-sin, zeros_h, zeros_r], axis=1), (1, reps))
    vcol = jnp.broadcast_to(valid.astype(F32)[:, None], (n, LANES))
    return jnp.concatenate([cosf, sin_lo, sin_hi, vcol], axis=1)


def _pack_pairs(x):
    half = x.shape[1] // 2
    bits = lax.bitcast_convert_type(x.astype(BF16), jnp.uint16).astype(jnp.uint32)
    return lax.bitcast_convert_type(bits[:, :half] | (bits[:, half:] << 16), jnp.int32)


def _expert_dots(table, idx, hn, nsel):
    n = idx.shape[0]
    c = table.shape[1]
    sc = plsc.get_sparse_core_info()
    lanes = sc.num_lanes
    workers = sc.num_cores * sc.num_subcores
    gw = GATHER_WINDOW
    ich = GATHER_INDEX_CHUNK
    tok = ich // nsel
    per_worker = n // workers
    assert n % workers == 0 and per_worker % ich == 0 and nsel == 2 * gw and hn.shape[1] == c and gw % lanes == 0
    mesh = plsc.VectorSubcoreMesh(core_axis_name="c", subcore_axis_name="s")

    @functools.partial(
        pl.kernel, out_type=jax.ShapeDtypeStruct((n,), F32), mesh=mesh, name="expert_dots",
        compiler_params=pltpu.CompilerParams(needs_layout_passes=False),
        scratch_types=[pltpu.VMEM((ich,), jnp.int32),
                       pltpu.VMEM((gw, c), jnp.int32), pltpu.VMEM((gw, c), jnp.int32),
                       pltpu.VMEM((tok, c), jnp.int32), pltpu.VMEM((ich,), F32), pltpu.VMEM((lanes * lanes,), F32),
                       pltpu.SemaphoreType.DMA, pltpu.SemaphoreType.DMA])
    def dots(tab_hbm, idx_hbm, hn_hbm, act_hbm, idx_v, rows0, rows1, h_v, act_v, scr, sem0, sem1):
        wid = lax.axis_index("s") * sc.num_cores + lax.axis_index("c")
        base = wid * per_worker
        lane = lax.iota(jnp.int32, lanes)

        def gather(win, buf, sem):
            return pltpu.make_async_copy(tab_hbm.at[idx_v.at[pl.ds(win * gw, gw)]], buf, sem)

        def reduce_window(buf, t_loc, out_off):
            for rb in range(gw // lanes):
                def kbody(k2, accs):
                    k0 = 2 * k2 * lanes
                    hw0 = plsc.bitcast(h_v[t_loc, pl.ds(k0, lanes)], BF16)
                    hw1 = plsc.bitcast(h_v[t_loc, pl.ds(k0 + lanes, lanes)], BF16)
                    out = []
                    for r in range(lanes):
                        w0 = plsc.bitcast(buf[rb * lanes + r, pl.ds(k0, lanes)], BF16)
                        w1 = plsc.bitcast(buf[rb * lanes + r, pl.ds(k0 + lanes, lanes)], BF16)
                        p = plsc.bitcast(w0 * hw0 + w1 * hw1, jnp.int32)
                        out.append(accs[r] + lax.bitcast_convert_type(p << 16, F32)
                                   + lax.bitcast_convert_type(p, F32))
                    return tuple(out)

                accs = lax.fori_loop(0, c // (2 * lanes), kbody,
                                     tuple(jnp.zeros((lanes,), F32) for _ in range(lanes)))
                for r in range(lanes):
                    scr[pl.ds(r * lanes, lanes)] = accs[r]
                tot = plsc.load_gather(scr, [lane * lanes])
                for l in range(1, lanes):
                    tot = tot + plsc.load_gather(scr, [lane * lanes + l])
                act_v[pl.ds(out_off + rb * lanes, lanes)] = tot

        @pl.loop(0, per_worker // ich)
        def _(g):
            cb = base + g * ich
            pltpu.sync_copy(idx_hbm.at[pl.ds(cb, ich)], idx_v)
            tok_base = pl.multiple_of(wid * (per_worker // nsel) + g * tok, tok)
            pltpu.sync_copy(hn_hbm.at[pl.ds(tok_base, tok)], h_v)
            gather(0, rows0, sem0).start()
            gather(1, rows1, sem1).start()

            @pl.loop(0, tok)
            def _(j):
                gather(2 * j, rows0, sem0).wait()
                reduce_window(rows0, j, j * nsel)

                @pl.when(j + 1 < tok)
                def _():
                    gather(2 * j + 2, rows0, sem0).start()

                gather(2 * j + 1, rows1, sem1).wait()
                reduce_window(rows1, j, j * nsel + gw)

                @pl.when(j + 1 < tok)
                def _():
                    gather(2 * j + 3, rows1, sem1).start()

            pltpu.sync_copy(act_v, act_hbm.at[pl.ds(cb, ich)])

    return dots(table, idx, hn)


def _expert_mix(table, idx, coef, nsel):
    n = idx.shape[0]
    c = table.shape[1]
    d = 2 * c
    sc = plsc.get_sparse_core_info()
    lanes = sc.num_lanes
    workers = sc.num_cores * sc.num_subcores
    gw = GATHER_WINDOW
    ich = GATHER_INDEX_CHUNK
    tok = ich // nsel
    per_worker = n // workers
    kblock = 8
    assert n % workers == 0 and per_worker % ich == 0 and nsel == 2 * gw and c % (kblock * lanes) == 0
    mesh = plsc.VectorSubcoreMesh(core_axis_name="c", subcore_axis_name="s")

    @functools.partial(
        pl.kernel, out_type=jax.ShapeDtypeStruct((n // nsel, d), F32), mesh=mesh, name="expert_mix",
        compiler_params=pltpu.CompilerParams(needs_layout_passes=False),
        scratch_types=[pltpu.VMEM((ich,), jnp.int32), pltpu.VMEM((ich,), jnp.int32),
                       pltpu.VMEM((gw, c), jnp.int32), pltpu.VMEM((gw, c), jnp.int32),
                       pltpu.VMEM((tok, d), F32),
                       pltpu.SemaphoreType.DMA, pltpu.SemaphoreType.DMA])
    def mix(tab_hbm, idx_hbm, coef_hbm, out_hbm, idx_v, coef_v, rows0, rows1, out_v, sem0, sem1):
        wid = lax.axis_index("s") * sc.num_cores + lax.axis_index("c")
        base = wid * per_worker
        zero_idx = jnp.zeros((lanes,), jnp.int32)

        def gather(win, buf, sem):
            return pltpu.make_async_copy(tab_hbm.at[idx_v.at[pl.ds(win * gw, gw)]], buf, sem)

        def accumulate_window(buf, t_loc, coef_off, first):
            for kb in range(c // (kblock * lanes)):
                col0 = kb * kblock * lanes
                if first:
                    init = tuple(jnp.zeros((lanes,), F32) for _ in range(2 * kblock))
                else:
                    init = tuple(out_v[t_loc, pl.ds(col0 + i * lanes, lanes)] for i in range(kblock)) + \
                           tuple(out_v[t_loc, pl.ds(c + col0 + i * lanes, lanes)] for i in range(kblock))

                def rbody(r2, accs):
                    accs = list(accs)
                    r = 2 * r2
                    cw0 = plsc.bitcast(plsc.load_gather(coef_v, [zero_idx + (coef_off + r)]), BF16)
                    cw1 = plsc.bitcast(plsc.load_gather(coef_v, [zero_idx + (coef_off + r + 1)]), BF16)
                    for i in range(kblock):
                        w0 = plsc.bitcast(buf[r, pl.ds(col0 + i * lanes, lanes)], BF16)
                        w1 = plsc.bitcast(buf[r + 1, pl.ds(col0 + i * lanes, lanes)], BF16)
                        p = plsc.bitcast(w0 * cw0 + w1 * cw1, jnp.int32)
                        accs[i] = accs[i] + lax.bitcast_convert_type(p << 16, F32)
                        accs[kblock + i] = accs[kblock + i] + lax.bitcast_convert_type(p, F32)
                    return tuple(accs)

                accs = lax.fori_loop(0, gw // 2, rbody, init)
                for i in range(kblock):
                    out_v[t_loc, pl.ds(col0 + i * lanes, lanes)] = accs[i]
                    out_v[t_loc, pl.ds(c + col0 + i * lanes, lanes)] = accs[kblock + i]

        @pl.loop(0, per_worker // ich)
        def _(g):
            cb = base + g * ich
            pltpu.sync_copy(idx_hbm.at[pl.ds(cb, ich)], idx_v)
            pltpu.sync_copy(coef_hbm.at[pl.ds(cb, ich)], coef_v)
            gather(0, rows0, sem0).start()
            gather(1, rows1, sem1).start()

            @pl.loop(0, tok)
            def _(j):
                gather(2 * j, rows0, sem0).wait()
                accumulate_window(rows0, j, j * nsel, True)

                @pl.when(j + 1 < tok)
                def _():
                    gather(2 * j + 2, rows0, sem0).start()

                gather(2 * j + 1, rows1, sem1).wait()
                accumulate_window(rows1, j, j * nsel + gw, False)

                @pl.when(j + 1 < tok)
                def _():
                    gather(2 * j + 3, rows1, sem1).start()

            tok_base = pl.multiple_of(wid * (per_worker // nsel) + g * tok, tok)
            pltpu.sync_copy(out_v, out_hbm.at[pl.ds(tok_base, tok)])

    return mix(table, idx, coef)


def _peer_tail(xm, hn, et, wt, tabs, gfin, y_acc, row_off):
    u_tab, v_tab = tabs
    t, nsel = et.shape
    eidx = et.reshape(t * nsel)
    act = _expert_dots(u_tab, eidx, _pack_pairs(hn), nsel).reshape(t, nsel)
    coef = _expert_coefs(act, wt).reshape(t * nsel)
    mixed = _expert_mix(v_tab, eidx, coef, nsel)
    return _finish(xm, mixed, gfin, y_acc, row_off)


def kernel(x_prompt, x_sample, cache_k_window, cache_v_window, state_gla, meta_tokens, g_norm_mix, w_in,
           w_gate_up, b_gate, attn_sinks, g_gla_norm, w_branch_a, w_branch_b, w_out, g_norm_ffn, w_peer_q,
           peer_sub_keys, peer_u, peer_v, g_norm_final):
    bsz, seq, d = x_prompt.shape
    dbsz, tdec, _ = x_sample.shape
    n_meta = meta_tokens.shape[0]
    depth = w_in.shape[0]
    window = cache_k_window.shape[2]
    kv_heads, head_dim = cache_k_window.shape[3], cache_k_window.shape[4]
    gate_rank = w_gate_up.shape[1]
    bqk = w_gate_up.shape[2]
    n_ph, _, n_keys, p_half = peer_sub_keys.shape[1:]
    assert depth == 1 and d == 1024 and window == ATTN_BLOCK and kv_heads == 2 and head_dim == 64
    assert bqk == 256 and state_gla.shape[2:] == (4, 64, 128) and n_meta <= ATTN_BLOCK
    assert seq % ATTN_BLOCK == 0 and tdec <= SAMPLE_PAD and n_keys == 128 and p_half == 64 and n_ph == 8
    rope_dim = head_dim // 4
    meta_pad = ATTN_BLOCK - n_meta
    lp = ATTN_BLOCK + seq
    nblk = lp // ATTN_BLOCK

    w = w_in[0]
    c_lr = 2304
    c_gate = c_lr + gate_rank
    w1 = w[:, :c_lr].astype(BF16)
    wlr = jnp.pad(w[:, c_lr:c_gate], ((0, 0), (0, LANES - gate_rank))).astype(BF16)
    w2 = w[:, c_gate:].astype(BF16)
    wgu = jnp.pad(w_gate_up[0], ((0, LANES - gate_rank), (0, 0))).astype(BF16)
    bg = b_gate[0][None, :]
    gmix = g_norm_mix[0][None, :]
    wa = w_branch_a[0].astype(BF16)
    wb = w_branch_b[0].astype(BF16)
    wo = w_out[0].astype(BF16)
    gffn = g_norm_ffn[0][None, :]
    wq = w_peer_q[0].astype(BF16)
    keys = peer_sub_keys[0].reshape(n_ph * 2, n_keys, p_half).astype(BF16)
    u_tab = _pack_pairs(peer_u[0])
    v_tab = _pack_pairs(peer_v[0])
    tabs = (u_tab, v_tab)
    gfin = g_norm_final[None, :]
    gn = g_gla_norm[0][None, :]
    sinks = attn_sinks[0]
    qk_scale = float(bqk // 4) ** -0.5

    rows_p = jnp.arange(lp)
    tab_p = _rope_table(rows_p - meta_pad, rows_p >= meta_pad, rope_dim, head_dim)
    proj_rows = max(r for r in range(16, PROJ_ROWS + 1, 16) if lp % r == 0)
    nq = nblk - 1
    nchunks = lp // GLA_CHUNK
    skip = ATTN_BLOCK // GLA_CHUNK
    ncq = nchunks - skip
    per_seq = seq // MERGE_ROWS
    gt_map_p = lambda i: ((i // per_seq) * nblk + 1 + (i % per_seq), 0)

    def prompt_sequences(xg, y_acc, row_off):
        gb = xg.shape[0]
        meta = jnp.broadcast_to(meta_tokens[None].astype(xg.dtype), (gb, n_meta, d))
        xpad = jnp.concatenate([jnp.zeros((gb, meta_pad, d), xg.dtype), meta, xg], axis=1).reshape(gb * lp, d)
        qa, kv, gl, gt = _project(xpad, gmix, tab_p, w1, wlr, wgu, bg, w2, proj_rows, qk_scale)
        ya = _attention(
            sinks, qa, kv, kv, gb, nq, ATTN_BLOCK,
            lambda b, n: (b * nblk + n + 1, 0), lambda b, n: (b * nblk + n, 0), lambda b, n: (b * nblk + n + 1, 0),
            lambda b, n: (b * nq + n, 0), gb * seq, first_valid_key=meta_pad, block_offset=1)
        s0 = jnp.zeros((gb,) + state_gla.shape[2:], F32)
        yb, s_fin = _gla(gl, s0, gn, gb, nchunks, GLA_CHUNK,
                         lambda b, c: (b * nchunks + c, 0),
                         lambda b, c: (b * ncq + jnp.maximum(c - skip, 0), 0), gb * seq)
        xm, hn, et, wt = _merge_route(xg.reshape(gb * seq, d), gt, ya, yb, wa, wb, wo, gffn, wq, keys,
                                      gt_map_p)
        y_acc = _peer_tail(xm, hn, et, wt, tabs, gfin, y_acc, row_off)
        kv_w = kv.reshape(gb, lp, 2, kv_heads, head_dim)[:, lp - window:]
        return y_acc, kv_w, s_fin

    group = PROMPT_GROUP if (bsz % PROMPT_GROUP == 0 and (PROMPT_GROUP * seq) % GATHER_ROW_QUANTUM == 0) else bsz
    y_acc, kv_parts, s_parts = jnp.zeros((bsz * seq, d), F32), [], []
    for b0 in range(0, bsz, group):
        y_acc, kv_w, s_fin = prompt_sequences(x_prompt[b0:b0 + group], y_acc, b0 * seq)
        kv_parts.append(kv_w)
        s_parts.append(s_fin)
    y_prompt = y_acc.reshape(bsz, seq, d)
    kv_p = jnp.concatenate(kv_parts, axis=0)
    s_fin_p = jnp.concatenate(s_parts, axis=0)
    new_k_p = kv_p[:, :, 0][None]
    new_v_p = kv_p[:, :, 1][None]

    sp = SAMPLE_PAD
    xs_pad = jnp.pad(x_sample, ((0, 0), (0, sp - tdec), (0, 0))).reshape(dbsz * sp, d)
    rows_s = jnp.arange(sp)
    reps = 256 // sp
    tab_s = jnp.tile(_rope_table(PAST_LEN + rows_s, rows_s < tdec, rope_dim, head_dim), (reps, 1))
    qa_s, kv_s, gl_s, gt_s = _project(xs_pad, gmix, tab_s, w1, wlr, wgu, bg, w2, 256, qk_scale)

    cache_kv = jnp.concatenate([cache_k_window[0].reshape(dbsz * window, kv_heads * head_dim),
                                cache_v_window[0].reshape(dbsz * window, kv_heads * head_dim)], axis=1)
    seq_map = lambda b, n: (b, 0)
    ya_s = _attention(sinks, qa_s, cache_kv, kv_s, dbsz, 1, sp, seq_map, seq_map, seq_map, seq_map,
                      dbsz * sp, first_valid_key=None, block_offset=0)
    yb_s, s_fin_s = _gla(gl_s, state_gla[0], gn, dbsz, 1, sp, seq_map, seq_map, dbsz * sp)

    def real_rows(a):
        return a.reshape(dbsz, sp, a.shape[-1])[:, :tdec].reshape(dbsz * tdec, a.shape[-1])

    xs_rows = x_sample.reshape(dbsz * tdec, d)
    xm_s, hn_s, et_s, wt_s = _merge_route(xs_rows, real_rows(gt_s), real_rows(ya_s), real_rows(yb_s),
                                          wa, wb, wo, gffn, wq, keys, lambda i: (i, 0))
    y_sample = _peer_tail(xm_s, hn_s, et_s, wt_s, tabs, gfin, jnp.zeros((dbsz * tdec, d), F32), 0).reshape(dbsz, tdec, d)

    kv_new = real_rows(kv_s).reshape(dbsz, tdec, 2, kv_heads, head_dim)
    new_k_s = jnp.concatenate([cache_k_window[0].astype(F32), kv_new[:, :, 0]], axis=1)[:, -window:][None]
    new_v_s = jnp.concatenate([cache_v_window[0].astype(F32), kv_new[:, :, 1]], axis=1)[:, -window:][None]

    return (y_prompt, y_sample, new_k_p, new_v_p, s_fin_p[None], new_k_s, new_v_s, s_fin_s[None])
```

```python
import functools
import math

import jax
import jax.numpy as jnp
from jax import lax
from jax.experimental import pallas as pl
from jax.experimental.pallas import tpu as pltpu
from jax.experimental.pallas import tpu_sc as plsc

F32 = jnp.float32
BF16 = jnp.bfloat16

EPS = 1e-6
NEG_INF = -1e30
PAST_LEN = 16384
ROPE_THETA = 500000.0
GATE_NORMALIZER = 16.0
PEER_TOPK = 16

LANES = 128
SUBLANES = 8
VMEM_LIMIT_BYTES = 56 * 1024 * 1024

ATTN_BLOCK = 128
GLA_CHUNK = 64
SAMPLE_PAD = 16
PROJ_ROWS = 544
MERGE_ROWS = 128
GATHER_WINDOW = 32
GATHER_BUFFERS = 4
GATHER_INDEX_CHUNK = 2048
GATHER_ROW_QUANTUM = 512
PROMPT_GROUP = 1


def _cparams(sem):
    return pltpu.CompilerParams(dimension_semantics=sem, vmem_limit_bytes=VMEM_LIMIT_BYTES)


def _rms(x, g):
    ms = jnp.mean(x * x, axis=-1, keepdims=True)
    return (x * lax.rsqrt(ms + EPS)) * g


def _proj_kernel(x_ref, g_ref, tab_ref, w1_ref, wlr_ref, wgu_ref, bg_ref, w2_ref,
                 qa_ref, kv_ref, gl_ref, gt_ref, *, period, qk_scale):
    i = pl.program_id(0)
    tr = x_ref.shape[0]
    hb = _rms(x_ref[...], g_ref[...]).astype(BF16)
    z1 = jnp.dot(hb, w1_ref[...], preferred_element_type=F32)

    start = pl.multiple_of((i * tr) % period, SUBLANES)
    tab = tab_ref[pl.ds(start, tr), :]
    cosf = tab[:, 0:LANES]
    sin_lo = tab[:, LANES:2 * LANES]
    sin_hi = tab[:, 2 * LANES:3 * LANES]
    valid = tab[:, 3 * LANES:3 * LANES + 1]

    def rope(xg):
        return xg * cosf + pltpu.roll(xg, 8, 1) * sin_lo + pltpu.roll(xg, LANES - 8, 1) * sin_hi

    for gi in range(4):
        sl = slice(gi * LANES, (gi + 1) * LANES)
        qa_ref[:, sl] = rope(z1[:, sl]).astype(BF16)
    kv_ref[:, 0:LANES] = rope(z1[:, 512:640])
    kv_ref[:, LANES:2 * LANES] = z1[:, 640:768]

    lr = jnp.dot(hb, wlr_ref[...], preferred_element_type=F32)
    pre = jnp.dot(lr.astype(BF16), wgu_ref[...], preferred_element_type=F32) + bg_ref[...]
    log_sig = jnp.minimum(pre, 0.0) - jnp.log1p(jnp.exp(-jnp.abs(pre)))
    ld = jnp.where(valid > 0.5, log_sig / GATE_NORMALIZER, 0.0)

    gl_ref[:, 0:256] = z1[:, 768:1024] * qk_scale
    gl_ref[:, 256:512] = z1[:, 1024:1280]
    gl_ref[:, 512:768] = ld
    gl_ref[:, 768:1792] = z1[:, 1280:2304]
    gt_ref[...] = jnp.dot(hb, w2_ref[...], preferred_element_type=F32)


def _project(x, g, tab, w1, wlr, wgu, bg, w2, rows, qk_scale):
    r, d = x.shape
    period = tab.shape[0]
    const = lambda i: (0, 0)
    row = lambda i: (i, 0)
    return pl.pallas_call(
        functools.partial(_proj_kernel, period=period, qk_scale=qk_scale),
        grid=(r // rows,),
        in_specs=[
            pl.BlockSpec((rows, d), row),
            pl.BlockSpec(g.shape, const),
            pl.BlockSpec(tab.shape, const),
            pl.BlockSpec(w1.shape, const),
            pl.BlockSpec(wlr.shape, const),
            pl.BlockSpec(wgu.shape, const),
            pl.BlockSpec(bg.shape, const),
            pl.BlockSpec(w2.shape, const),
        ],
        out_specs=[
            pl.BlockSpec((rows, 512), row),
            pl.BlockSpec((rows, 256), row),
            pl.BlockSpec((rows, 1792), row),
            pl.BlockSpec((rows, 2048), row),
        ],
        out_shape=[
            jax.ShapeDtypeStruct((r, 512), BF16),
            jax.ShapeDtypeStruct((r, 256), F32),
            jax.ShapeDtypeStruct((r, 1792), F32),
            jax.ShapeDtypeStruct((r, 2048), F32),
        ],
        compiler_params=_cparams(("arbitrary",)),
        name="proj",
    )(x, g, tab, w1, wlr, wgu, bg, w2)


def _attn_kernel(sink_ref, q_ref, prev_ref, cur_ref, o_ref, *, first_valid_key, block_offset):
    n = pl.program_id(1)
    qr = q_ref.shape[0]
    kr = cur_ref.shape[0]
    w = prev_ref.shape[0]
    nk = w + kr
    group = 4
    hd = 64

    rows = lax.broadcasted_iota(jnp.int32, (group * qr, nk), 0)
    cols = lax.broadcasted_iota(jnp.int32, (group * qr, nk), 1)
    head_of_row = rows // qr
    diff = (rows - head_of_row * qr) - cols + w
    mask = (diff >= 0) & (diff <= w)
    if first_valid_key is not None:
        blk = n + block_offset
        mask = mask & (cols >= first_valid_key + w - blk * w)

    prev = prev_ref[...]
    cur = cur_ref[...]
    q = q_ref[...]
    row_head = lax.broadcasted_iota(jnp.int32, (group * qr, 1), 0) // qr
    for kh in range(2):
        k = jnp.concatenate([prev[:, kh * hd:(kh + 1) * hd], cur[:, kh * hd:(kh + 1) * hd]], axis=0).astype(BF16)
        v = jnp.concatenate([prev[:, LANES + kh * hd:LANES + (kh + 1) * hd],
                             cur[:, LANES + kh * hd:LANES + (kh + 1) * hd]], axis=0).astype(BF16)
        qs = jnp.concatenate([q[:, (group * kh + g) * hd:(group * kh + g + 1) * hd] for g in range(group)], axis=0)
        s = lax.dot_general(qs, k, (((1,), (1,)), ((), ())), preferred_element_type=F32) * (hd ** -0.5)
        s = jnp.where(mask, s, NEG_INF)
        sink = jnp.zeros((group * qr, 1), F32)
        for g in range(group):
            sink = jnp.where(row_head == g, sink_ref[group * kh + g], sink)
        m = jnp.maximum(jnp.max(s, axis=-1, keepdims=True), sink)
        e = jnp.exp(s - m)
        p = e / (jnp.sum(e, axis=-1, keepdims=True) + jnp.exp(sink - m))
        o = jnp.dot(p.astype(BF16), v, preferred_element_type=F32)
        for g in range(group):
            h = group * kh + g
            o_ref[:, h * hd:(h + 1) * hd] = o[g * qr:(g + 1) * qr].astype(BF16)


def _attention(sinks, q, kv_prev, kv_cur, nb, nblk, qr, q_map, prev_map, cur_map, out_map, out_rows,
               first_valid_key, block_offset):
    w = ATTN_BLOCK
    return pl.pallas_call(
        functools.partial(_attn_kernel, first_valid_key=first_valid_key, block_offset=block_offset),
        grid=(nb, nblk),
        in_specs=[
            pl.BlockSpec(memory_space=pltpu.SMEM),
            pl.BlockSpec((qr, 512), q_map),
            pl.BlockSpec((w, 256), prev_map),
            pl.BlockSpec((qr, 256), cur_map),
        ],
        out_specs=pl.BlockSpec((qr, 512), out_map),
        out_shape=jax.ShapeDtypeStruct((out_rows, 512), BF16),
        compiler_params=_cparams(("arbitrary", "arbitrary")),
        name="swa",
    )(sinks, q, kv_prev, kv_cur)


def _gla_kernel(gl_ref, s0_ref, gn_ref, yb_ref, sfin_ref, st_ref):
    c = pl.program_id(1)
    ch = gl_ref.shape[0]
    nh, dk, dv = 4, 64, 128

    @pl.when(c == 0)
    def _():
        for h in range(nh):
            st_ref[h] = s0_ref[0, h].T

    gl = gl_ref[...]
    q = gl[:, 0:256]
    k = gl[:, 256:512]
    b = gl[:, 512:768]
    row = lax.broadcasted_iota(jnp.int32, (ch, nh * dk), 0)
    sh = 1
    while sh < ch:
        b = b + jnp.where(row >= sh, pltpu.roll(b, sh, 0), 0.0)
        sh *= 2
    b_last = b[ch - 1:ch, :]
    q_t = (q * jnp.exp(b)).astype(BF16)
    k_t = (k * jnp.exp(-b)).astype(BF16)
    k_end = (k * jnp.exp(b_last - b)).astype(BF16)
    decay = jnp.exp(b_last)
    causal = (lax.broadcasted_iota(jnp.int32, (ch, ch), 0) >= lax.broadcasted_iota(jnp.int32, (ch, ch), 1))
    gn = gn_ref[...]
    nt = (((1,), (1,)), ((), ()))
    for h in range(nh):
        ks = slice(h * dk, (h + 1) * dk)
        v = gl[:, 768 + h * dv:768 + (h + 1) * dv]
        vb = v.astype(BF16)
        a = lax.dot_general(q_t[:, ks], k_t[:, ks], nt, preferred_element_type=F32)
        a = jnp.where(causal, a, 0.0)
        s_t = st_ref[h]
        o = jnp.dot(a.astype(BF16), vb, preferred_element_type=F32)
        o = o + lax.dot_general(q_t[:, ks], s_t.astype(BF16), nt, preferred_element_type=F32)
        upd = jnp.dot(v.T.astype(BF16), k_end[:, ks], preferred_element_type=F32)
        st_ref[h] = s_t * decay[:, ks] + upd
        go = gl[:, 1280 + h * dv:1280 + (h + 1) * dv]
        y = _rms(o, gn) * (go * jax.nn.sigmoid(go))
        yb_ref[:, h * dv:(h + 1) * dv] = y.astype(BF16)

    @pl.when(c == pl.num_programs(1) - 1)
    def _():
        for h in range(nh):
            sfin_ref[0, h] = st_ref[h].T


def _gla(gl, s0, gn, nb, nchunks, ch, in_map, out_map, out_rows):
    return pl.pallas_call(
        _gla_kernel,
        grid=(nb, nchunks),
        in_specs=[
            pl.BlockSpec((ch, 1792), in_map),
            pl.BlockSpec((1, 4, 64, 128), lambda b, c: (b, 0, 0, 0)),
            pl.BlockSpec((1, 128), lambda b, c: (0, 0)),
        ],
        out_specs=[
            pl.BlockSpec((ch, 512), out_map),
            pl.BlockSpec((1, 4, 64, 128), lambda b, c: (b, 0, 0, 0)),
        ],
        out_shape=[
            jax.ShapeDtypeStruct((out_rows, 512), BF16),
            jax.ShapeDtypeStruct((nb, 4, 64, 128), F32),
        ],
        scratch_shapes=[pltpu.VMEM((4, 128, 64), F32)],
        compiler_params=_cparams(("arbitrary", "arbitrary")),
        name="gla",
    )(gl, s0, gn)


def _extract_topk(work, nsel, iota0, sentinel):
    vals, idxs = [], []
    for j in range(nsel):
        m = jnp.max(work, axis=0, keepdims=True)
        idx = jnp.min(jnp.where(work == m, iota0, sentinel), axis=0, keepdims=True)
        vals.append(m)
        idxs.append(idx)
        if j + 1 < nsel:
            work = jnp.where(iota0 == idx, -jnp.inf, work)
    return vals, idxs


def _merge_kernel(x_ref, gt_ref, ya_ref, yb_ref, wa_ref, wb_ref, wo_ref, gf_ref, wq_ref, keys_ref,
                  xm_ref, hn_ref, et_ref, wt_ref):
    td = x_ref.shape[0]
    nkeys = keys_ref.shape[1]
    half = keys_ref.shape[2]
    nheads = keys_ref.shape[0] // 2
    topk = PEER_TOPK

    gt = gt_ref[...]
    d = x_ref.shape[1]
    ma = jnp.dot(ya_ref[...], wa_ref[...], preferred_element_type=F32)
    mb = jnp.dot(yb_ref[...], wb_ref[...], preferred_element_type=F32)
    m = jax.nn.sigmoid(gt[:, 0:d]) * ma + jax.nn.sigmoid(gt[:, d:2 * d]) * mb
    xm = x_ref[...] + jnp.dot(m.astype(BF16), wo_ref[...], preferred_element_type=F32)
    xm_ref[...] = xm
    hn = _rms(xm, gf_ref[...])
    hn_ref[...] = hn
    q = jnp.dot(hn.astype(BF16), wq_ref[...], preferred_element_type=F32).astype(BF16)

    nt = (((1,), (1,)), ((), ()))
    iota_k = lax.broadcasted_iota(jnp.int32, (nkeys, td), 0)
    pair_rows = [topk // (a + 1) for a in range(topk)]
    cand_pad = -sum(pair_rows) % SUBLANES
    ncand = sum(pair_rows) + cand_pad
    iota_c = lax.broadcasted_iota(jnp.int32, (ncand, td), 0)
    wts, ids = [], []
    for h in range(nheads):
        sv, si = [], []
        for c in range(2):
            gi = 2 * h + c
            s_t = lax.dot_general(keys_ref[gi], q[:, gi * half:(gi + 1) * half], nt,
                                  preferred_element_type=F32)
            vals, idxs = _extract_topk(s_t, topk, iota_k, nkeys)
            sv.append(vals)
            si.append(idxs)
        sv1 = jnp.concatenate(sv[1], axis=0)
        si1 = jnp.concatenate(si[1], axis=0)
        cand = jnp.concatenate([sv[0][a] + sv1[0:nb] for a, nb in enumerate(pair_rows)]
                               + [jnp.full((cand_pad, td), -jnp.inf, F32)], axis=0)
        cidx = jnp.concatenate([si[0][a] * nkeys + si1[0:nb] for a, nb in enumerate(pair_rows)]
                               + [jnp.full((cand_pad, td), -1, jnp.int32)], axis=0)
        fvals, eids = [], []
        work = cand
        for j in range(topk):
            mx = jnp.max(work, axis=0, keepdims=True)
            pos = jnp.min(jnp.where(work == mx, iota_c, ncand), axis=0, keepdims=True)
            hit = iota_c == pos
            eids.append(jnp.max(jnp.where(hit, cidx, -1), axis=0, keepdims=True))
            fvals.append(mx)
            if j + 1 < topk:
                work = jnp.where(hit, -jnp.inf, work)
        fv = jnp.concatenate(fvals, axis=0)
        e = jnp.exp(fv - fvals[0])
        wts.append(e / jnp.sum(e, axis=0, keepdims=True))
        ids.extend(eids)
    wt_ref[...] = jnp.concatenate(wts, axis=0).T
    et_ref[...] = jnp.concatenate(ids, axis=0).T


def _merge_route(x, gt, ya, yb, wa, wb, wo, gf, wq, keys, gt_map):
    t, d = x.shape
    td = MERGE_ROWS
    nsel = (keys.shape[0] // 2) * PEER_TOPK
    const2 = lambda i: (0, 0)
    row = lambda i: (i, 0)
    return pl.pallas_call(
        _merge_kernel,
        grid=(t // td,),
        in_specs=[
            pl.BlockSpec((td, d), row),
            pl.BlockSpec((td, 2 * d), gt_map),
            pl.BlockSpec((td, ya.shape[1]), row),
            pl.BlockSpec((td, yb.shape[1]), row),
            pl.BlockSpec(wa.shape, const2),
            pl.BlockSpec(wb.shape, const2),
            pl.BlockSpec(wo.shape, const2),
            pl.BlockSpec(gf.shape, const2),
            pl.BlockSpec(wq.shape, const2),
            pl.BlockSpec(keys.shape, lambda i: (0, 0, 0)),
        ],
        out_specs=[
            pl.BlockSpec((td, d), row),
            pl.BlockSpec((td, d), row),
            pl.BlockSpec((td, nsel), row),
            pl.BlockSpec((td, nsel), row),
        ],
        out_shape=[
            jax.ShapeDtypeStruct((t, d), F32),
            jax.ShapeDtypeStruct((t, d), F32),
            jax.ShapeDtypeStruct((t, nsel), jnp.int32),
            jax.ShapeDtypeStruct((t, nsel), F32),
        ],
        compiler_params=_cparams(("arbitrary",)),
        name="merge_route",
    )(x, gt, ya, yb, wa, wb, wo, gf, wq, keys)


def _coef_kernel(act_ref, wt_ref, o_ref):
    act = act_ref[...]
    gelu = 0.5 * act * (1.0 + lax.erf(act * (2.0 ** -0.5)))
    bits = pltpu.bitcast((wt_ref[...] * gelu).astype(BF16).astype(F32), jnp.int32)
    o_ref[...] = bits | lax.shift_right_logical(bits, 16)


def _expert_coefs(act, wt):
    t, nsel = wt.shape
    rows = math.gcd(t, 512)
    row = lambda i: (i, 0)
    return pl.pallas_call(
        _coef_kernel,
        grid=(t // rows,),
        in_specs=[pl.BlockSpec((rows, nsel), row), pl.BlockSpec((rows, nsel), row)],
        out_specs=pl.BlockSpec((rows, nsel), row),
        out_shape=jax.ShapeDtypeStruct((t, nsel), jnp.int32),
        compiler_params=_cparams(("arbitrary",)),
        name="expert_coefs",
    )(act, wt)


def _finish_kernel(xm_ref, o_ref, gfin_ref, yacc_hbm, y_ref):
    del yacc_hbm
    y_ref[...] = _rms(xm_ref[...] + o_ref[...], gfin_ref[...])


def _finish(xm, o, gfin, y_acc, row_off):
    t, d = xm.shape
    rows = math.gcd(t, 512)
    first = row_off // rows
    row = lambda i: (i, 0)
    return pl.pallas_call(
        _finish_kernel,
        grid=(t // rows,),
        in_specs=[pl.BlockSpec((rows, d), row), pl.BlockSpec((rows, d), row),
                  pl.BlockSpec(gfin.shape, lambda i: (0, 0)), pl.BlockSpec(memory_space=pl.ANY)],
        out_specs=pl.BlockSpec((rows, d), lambda i: (i + first, 0)),
        out_shape=jax.ShapeDtypeStruct(y_acc.shape, F32),
        input_output_aliases={3: 0},
        compiler_params=_cparams(("arbitrary",)),
        name="finish",
    )(xm, o, gfin, y_acc)


def _rope_table(pos, valid, rope_dim, head_dim):
    half = rope_dim // 2
    inv = ROPE_THETA ** (-jnp.arange(0, rope_dim, 2, dtype=F32) / rope_dim)
    ang = pos.astype(F32)[:, None] * inv[None, :]
    cos, sin = jnp.cos(ang), jnp.sin(ang)
    n = pos.shape[0]
    ones = jnp.ones((n, head_dim - rope_dim), F32)
    zeros_h = jnp.zeros((n, half), F32)
    zeros_r = jnp.zeros((n, head_dim - rope_dim), F32)
    reps = LANES // head_dim
    cosf = jnp.tile(jnp.concatenate([cos, cos, ones], axis=1), (1, reps))
    sin_lo = jnp.tile(jnp.concatenate([zeros_h, sin, zeros_r], axis=1), (1, reps))
    sin_hi = jnp.tile(jnp.concatenate([-sin, zeros_h, zeros_r], axis=1), (1, reps))
    vcol = jnp.broadcast_to(valid.astype(F32)[:, None], (n, LANES))
    return jnp.concatenate([cosf, sin_lo, sin_hi, vcol], axis=1)


def _pack_pairs(x):
    half = x.shape[1] // 2
    bits = lax.bitcast_convert_type(x.astype(BF16), jnp.uint16).astype(jnp.uint32)
    return lax.bitcast_convert_type(bits[:, :half] | (bits[:, half:] << 16), jnp.int32)


def _expert_dots(table, idx, hn, nsel):
    n = idx.shape[0]
    c = table.shape[1]
    sc = plsc.get_sparse_core_info()
    lanes = sc.num_lanes
    workers = sc.num_cores * sc.num_subcores
    gw = GATHER_WINDOW
    nbuf = GATHER_BUFFERS
    ich = GATHER_INDEX_CHUNK
    tok = ich // nsel
    per_worker = n // workers
    assert n % workers == 0 and per_worker % ich == 0 and nsel == nbuf * gw and hn.shape[1] == c and gw % lanes == 0
    mesh = plsc.VectorSubcoreMesh(core_axis_name="c", subcore_axis_name="s")

    @functools.partial(
        pl.kernel, out_type=jax.ShapeDtypeStruct((n,), F32), mesh=mesh, name="expert_dots",
        compiler_params=pltpu.CompilerParams(needs_layout_passes=False),
        scratch_types=[pltpu.VMEM((ich,), jnp.int32), pltpu.VMEM((tok, c), jnp.int32), pltpu.VMEM((ich,), F32),
                       pltpu.VMEM((lanes * lanes,), F32)]
                      + [pltpu.VMEM((gw, c), jnp.int32)] * nbuf + [pltpu.SemaphoreType.DMA] * nbuf)
    def dots(tab_hbm, idx_hbm, hn_hbm, act_hbm, idx_v, h_v, act_v, scr, *bufs_sems):
        rows, sems = bufs_sems[:nbuf], bufs_sems[nbuf:]
        wid = lax.axis_index("s") * sc.num_cores + lax.axis_index("c")
        base = wid * per_worker
        lane = lax.iota(jnp.int32, lanes)

        def gather(win, buf, sem):
            return pltpu.make_async_copy(tab_hbm.at[idx_v.at[pl.ds(win * gw, gw)]], buf, sem)

        def reduce_window(buf, t_loc, out_off):
            for rb in range(gw // lanes):
                def kbody(k2, accs):
                    k0 = 2 * k2 * lanes
                    hw0 = plsc.bitcast(h_v[t_loc, pl.ds(k0, lanes)], BF16)
                    hw1 = plsc.bitcast(h_v[t_loc, pl.ds(k0 + lanes, lanes)], BF16)
                    out = []
                    for r in range(lanes):
                        w0 = plsc.bitcast(buf[rb * lanes + r, pl.ds(k0, lanes)], BF16)
                        w1 = plsc.bitcast(buf[rb * lanes + r, pl.ds(k0 + lanes, lanes)], BF16)
                        p = plsc.bitcast(w0 * hw0 + w1 * hw1, jnp.int32)
                        out.append(accs[r] + lax.bitcast_convert_type(p << 16, F32)
                                   + lax.bitcast_convert_type(p, F32))
                    return tuple(out)

                accs = lax.fori_loop(0, c // (2 * lanes), kbody,
                                     tuple(jnp.zeros((lanes,), F32) for _ in range(lanes)))
                for r in range(lanes):
                    scr[pl.ds(r * lanes, lanes)] = accs[r]
                tot = plsc.load_gather(scr, [lane * lanes])
                for l in range(1, lanes):
                    tot = tot + plsc.load_gather(scr, [lane * lanes + l])
                act_v[pl.ds(out_off + rb * lanes, lanes)] = tot

        @pl.loop(0, per_worker // ich)
        def _(g):
            cb = base + g * ich
            pltpu.sync_copy(idx_hbm.at[pl.ds(cb, ich)], idx_v)
            tok_base = pl.multiple_of(wid * (per_worker // nsel) + g * tok, tok)
            pltpu.sync_copy(hn_hbm.at[pl.ds(tok_base, tok)], h_v)
            for q in range(nbuf):
                gather(q, rows[q], sems[q]).start()

            @pl.loop(0, tok)
            def _(j):
                for q in range(nbuf):
                    gather(nbuf * j + q, rows[q], sems[q]).wait()
                    reduce_window(rows[q], j, j * nsel + q * gw)

                    @pl.when(j + 1 < tok)
                    def _():
                        gather(nbuf * (j + 1) + q, rows[q], sems[q]).start()

            pltpu.sync_copy(act_v, act_hbm.at[pl.ds(cb, ich)])

    return dots(table, idx, hn)


def _expert_mix(table, idx, coef, nsel):
    n = idx.shape[0]
    c = table.shape[1]
    d = 2 * c
    sc = plsc.get_sparse_core_info()
    lanes = sc.num_lanes
    workers = sc.num_cores * sc.num_subcores
    gw = GATHER_WINDOW
    nbuf = GATHER_BUFFERS
    ich = GATHER_INDEX_CHUNK
    tok = ich // nsel
    per_worker = n // workers
    kblock = 8
    assert n % workers == 0 and per_worker % ich == 0 and nsel == nbuf * gw and c % (kblock * lanes) == 0
    mesh = plsc.VectorSubcoreMesh(core_axis_name="c", subcore_axis_name="s")

    @functools.partial(
        pl.kernel, out_type=jax.ShapeDtypeStruct((n // nsel, d), F32), mesh=mesh, name="expert_mix",
        compiler_params=pltpu.CompilerParams(needs_layout_passes=False),
        scratch_types=[pltpu.VMEM((ich,), jnp.int32), pltpu.VMEM((ich,), jnp.int32), pltpu.VMEM((tok, d), F32)]
                      + [pltpu.VMEM((gw, c), jnp.int32)] * nbuf + [pltpu.SemaphoreType.DMA] * nbuf)
    def mix(tab_hbm, idx_hbm, coef_hbm, out_hbm, idx_v, coef_v, out_v, *bufs_sems):
        rows, sems = bufs_sems[:nbuf], bufs_sems[nbuf:]
        wid = lax.axis_index("s") * sc.num_cores + lax.axis_index("c")
        base = wid * per_worker
        zero_idx = jnp.zeros((lanes,), jnp.int32)

        def gather(win, buf, sem):
            return pltpu.make_async_copy(tab_hbm.at[idx_v.at[pl.ds(win * gw, gw)]], buf, sem)

        def accumulate_window(buf, t_loc, coef_off, first):
            for kb in range(c // (kblock * lanes)):
                col0 = kb * kblock * lanes
                if first:
                    init = tuple(jnp.zeros((lanes,), F32) for _ in range(2 * kblock))
                else:
                    init = tuple(out_v[t_loc, pl.ds(col0 + i * lanes, lanes)] for i in range(kblock)) + \
                           tuple(out_v[t_loc, pl.ds(c + col0 + i * lanes, lanes)] for i in range(kblock))

                def rbody(r2, accs):
                    accs = list(accs)
                    r = 2 * r2
                    cw0 = plsc.bitcast(plsc.load_gather(coef_v, [zero_idx + (coef_off + r)]), BF16)
                    cw1 = plsc.bitcast(plsc.load_gather(coef_v, [zero_idx + (coef_off + r + 1)]), BF16)
                    for i in range(kblock):
                        w0 = plsc.bitcast(buf[r, pl.ds(col0 + i * lanes, lanes)], BF16)
                        w1 = plsc.bitcast(buf[r + 1, pl.ds(col0 + i * lanes, lanes)], BF16)
                        p = plsc.bitcast(w0 * cw0 + w1 * cw1, jnp.int32)
                        accs[i] = accs[i] + lax.bitcast_convert_type(p << 16, F32)
                        accs[kblock + i] = accs[kblock + i] + lax.bitcast_convert_type(p, F32)
                    return tuple(accs)

                accs = lax.fori_loop(0, gw // 2, rbody, init)
                for i in range(kblock):
                    out_v[t_loc, pl.ds(col0 + i * lanes, lanes)] = accs[i]
                    out_v[t_loc, pl.ds(c + col0 + i * lanes, lanes)] = accs[kblock + i]

        @pl.loop(0, per_worker // ich)
        def _(g):
            cb = base + g * ich
            pltpu.sync_copy(idx_hbm.at[pl.ds(cb, ich)], idx_v)
            pltpu.sync_copy(coef_hbm.at[pl.ds(cb, ich)], coef_v)
            for q in range(nbuf):
                gather(q, rows[q], sems[q]).start()

            @pl.loop(0, tok)
            def _(j):
                for q in range(nbuf):
                    gather(nbuf * j + q, rows[q], sems[q]).wait()
                    accumulate_window(rows[q], j, j * nsel + q * gw, q == 0)

                    @pl.when(j + 1 < tok)
                    def _():
                        gather(nbuf * (j + 1) + q, rows[q], sems[q]).start()

            tok_base = pl.multiple_of(wid * (per_worker // nsel) + g * tok, tok)
            pltpu.sync_copy(out_v, out_hbm.at[pl.ds(tok_base, tok)])

    return mix(table, idx, coef)


def _peer_tail(xm, hn, et, wt, tabs, gfin, y_acc, row_off):
    u_tab, v_tab = tabs
    t, nsel = et.shape
    eidx = et.reshape(t * nsel)
    act = _expert_dots(u_tab, eidx, _pack_pairs(hn), nsel).reshape(t, nsel)
    coef = _expert_coefs(act, wt).reshape(t * nsel)
    mixed = _expert_mix(v_tab, eidx, coef, nsel)
    return _finish(xm, mixed, gfin, y_acc, row_off)


def kernel(x_prompt, x_sample, cache_k_window, cache_v_window, state_gla, meta_tokens, g_norm_mix, w_in,
           w_gate_up, b_gate, attn_sinks, g_gla_norm, w_branch_a, w_branch_b, w_out, g_norm_ffn, w_peer_q,
           peer_sub_keys, peer_u, peer_v, g_norm_final):
    bsz, seq, d = x_prompt.shape
    dbsz, tdec, _ = x_sample.shape
    n_meta = meta_tokens.shape[0]
    depth = w_in.shape[0]
    window = cache_k_window.shape[2]
    kv_heads, head_dim = cache_k_window.shape[3], cache_k_window.shape[4]
    gate_rank = w_gate_up.shape[1]
    bqk = w_gate_up.shape[2]
    n_ph, _, n_keys, p_half = peer_sub_keys.shape[1:]
    assert depth == 1 and d == 1024 and window == ATTN_BLOCK and kv_heads == 2 and head_dim == 64
    assert bqk == 256 and state_gla.shape[2:] == (4, 64, 128) and n_meta <= ATTN_BLOCK
    assert seq % ATTN_BLOCK == 0 and tdec <= SAMPLE_PAD and n_keys == 128 and p_half == 64 and n_ph == 8
    rope_dim = head_dim // 4
    meta_pad = ATTN_BLOCK - n_meta
    lp = ATTN_BLOCK + seq
    nblk = lp // ATTN_BLOCK

    w = w_in[0]
    c_lr = 2304
    c_gate = c_lr + gate_rank
    w1 = w[:, :c_lr].astype(BF16)
    wlr = jnp.pad(w[:, c_lr:c_gate], ((0, 0), (0, LANES - gate_rank))).astype(BF16)
    w2 = w[:, c_gate:].astype(BF16)
    wgu = jnp.pad(w_gate_up[0], ((0, LANES - gate_rank), (0, 0))).astype(BF16)
    bg = b_gate[0][None, :]
    gmix = g_norm_mix[0][None, :]
    wa = w_branch_a[0].astype(BF16)
    wb = w_branch_b[0].astype(BF16)
    wo = w_out[0].astype(BF16)
    gffn = g_norm_ffn[0][None, :]
    wq = w_peer_q[0].astype(BF16)
    keys = peer_sub_keys[0].reshape(n_ph * 2, n_keys, p_half).astype(BF16)
    u_tab = _pack_pairs(peer_u[0])
    v_tab = _pack_pairs(peer_v[0])
    tabs = (u_tab, v_tab)
    gfin = g_norm_final[None, :]
    gn = g_gla_norm[0][None, :]
    sinks = attn_sinks[0]
    qk_scale = float(bqk // 4) ** -0.5

    rows_p = jnp.arange(lp)
    tab_p = _rope_table(rows_p - meta_pad, rows_p >= meta_pad, rope_dim, head_dim)
    proj_rows = max(r for r in range(16, PROJ_ROWS + 1, 16) if lp % r == 0)
    nq = nblk - 1
    nchunks = lp // GLA_CHUNK
    skip = ATTN_BLOCK // GLA_CHUNK
    ncq = nchunks - skip
    per_seq = seq // MERGE_ROWS
    gt_map_p = lambda i: ((i // per_seq) * nblk + 1 + (i % per_seq), 0)

    def prompt_sequences(xg, y_acc, row_off):
        gb = xg.shape[0]
        meta = jnp.broadcast_to(meta_tokens[None].astype(xg.dtype), (gb, n_meta, d))
        xpad = jnp.concatenate([jnp.zeros((gb, meta_pad, d), xg.dtype), meta, xg], axis=1).reshape(gb * lp, d)
        qa, kv, gl, gt = _project(xpad, gmix, tab_p, w1, wlr, wgu, bg, w2, proj_rows, qk_scale)
        ya = _attention(
            sinks, qa, kv, kv, gb, nq, ATTN_BLOCK,
            lambda b, n: (b * nblk + n + 1, 0), lambda b, n: (b * nblk + n, 0), lambda b, n: (b * nblk + n + 1, 0),
            lambda b, n: (b * nq + n, 0), gb * seq, first_valid_key=meta_pad, block_offset=1)
        s0 = jnp.zeros((gb,) + state_gla.shape[2:], F32)
        yb, s_fin = _gla(gl, s0, gn, gb, nchunks, GLA_CHUNK,
                         lambda b, c: (b * nchunks + c, 0),
                         lambda b, c: (b * ncq + jnp.maximum(c - skip, 0), 0), gb * seq)
        xm, hn, et, wt = _merge_route(xg.reshape(gb * seq, d), gt, ya, yb, wa, wb, wo, gffn, wq, keys,
                                      gt_map_p)
        y_acc = _peer_tail(xm, hn, et, wt, tabs, gfin, y_acc, row_off)
        kv_w = kv.reshape(gb, lp, 2, kv_heads, head_dim)[:, lp - window:]
        return y_acc, kv_w, s_fin

    group = PROMPT_GROUP if (bsz % PROMPT_GROUP == 0 and (PROMPT_GROUP * seq) % GATHER_ROW_QUANTUM == 0) else bsz
    y_acc, kv_parts, s_parts = jnp.zeros((bsz * seq, d), F32), [], []
    for b0 in range(0, bsz, group):
        y_acc, kv_w, s_fin = prompt_sequences(x_prompt[b0:b0 + group], y_acc, b0 * seq)
        kv_parts.append(kv_w)
        s_parts.append(s_fin)
    y_prompt = y_acc.reshape(bsz, seq, d)
    kv_p = jnp.concatenate(kv_parts, axis=0)
    s_fin_p = jnp.concatenate(s_parts, axis=0)
    new_k_p = kv_p[:, :, 0][None]
    new_v_p = kv_p[:, :, 1][None]

    sp = SAMPLE_PAD
    xs_pad = jnp.pad(x_sample, ((0, 0), (0, sp - tdec), (0, 0))).reshape(dbsz * sp, d)
    rows_s = jnp.arange(sp)
    reps = 256 // sp
    tab_s = jnp.tile(_rope_table(PAST_LEN + rows_s, rows_s < tdec, rope_dim, head_dim), (reps, 1))
    qa_s, kv_s, gl_s, gt_s = _project(xs_pad, gmix, tab_s, w1, wlr, wgu, bg, w2, 256, qk_scale)

    cache_kv = jnp.concatenate([cache_k_window[0].reshape(dbsz * window, kv_heads * head_dim),
                                cache_v_window[0].reshape(dbsz * window, kv_heads * head_dim)], axis=1)
    seq_map = lambda b, n: (b, 0)
    ya_s = _attention(sinks, qa_s, cache_kv, kv_s, dbsz, 1, sp, seq_map, seq_map, seq_map, seq_map,
                      dbsz * sp, first_valid_key=None, block_offset=0)
    yb_s, s_fin_s = _gla(gl_s, state_gla[0], gn, dbsz, 1, sp, seq_map, seq_map, dbsz * sp)

    def real_rows(a):
        return a.reshape(dbsz, sp, a.shape[-1])[:, :tdec].reshape(dbsz * tdec, a.shape[-1])

    xs_rows = x_sample.reshape(dbsz * tdec, d)
    xm_s, hn_s, et_s, wt_s = _merge_route(xs_rows, real_rows(gt_s), real_rows(ya_s), real_rows(yb_s),
                                          wa, wb, wo, gffn, wq, keys, lambda i: (i, 0))
    y_sample = _peer_tail(xm_s, hn_s, et_s, wt_s, tabs, gfin, jnp.zeros((dbsz * tdec, d), F32), 0).reshape(dbsz, tdec, d)

    kv_new = real_rows(kv_s).reshape(dbsz, tdec, 2, kv_heads, head_dim)
    new_k_s = jnp.concatenate([cache_k_window[0].astype(F32), kv_new[:, :, 0]], axis=1)[:, -window:][None]
    new_v_s = jnp.concatenate([cache_v_window[0].astype(F32), kv_new[:, :, 1]], axis=1)[:, -window:][None]

    return (y_prompt, y_sample, new_k_p, new_v_p, s_fin_p[None], new_k_s, new_v_s, s_fin_s[None])
```

```python
import functools
import math

import jax
import jax.numpy as jnp
from jax import lax
from jax.experimental import pallas as pl
from jax.experimental.pallas import tpu as pltpu
from jax.experimental.pallas import tpu_sc as plsc

F32 = jnp.float32
BF16 = jnp.bfloat16

EPS = 1e-6
NEG_INF = -1e30
PAST_LEN = 16384
ROPE_THETA = 500000.0
GATE_NORMALIZER = 16.0
PEER_TOPK = 16

LANES = 128
SUBLANES = 8
VMEM_LIMIT_BYTES = 56 * 1024 * 1024

ATTN_BLOCK = 128
GLA_CHUNK = 64
SAMPLE_PAD = 16
PROJ_ROWS = 544
MERGE_ROWS = 128
GATHER_WINDOW = 32
GATHER_BUFFERS = 4
GATHER_INDEX_CHUNK = 2048
GATHER_ROW_QUANTUM = 512
PROMPT_GROUP = 1


def _cparams(sem):
    return pltpu.CompilerParams(dimension_semantics=sem, vmem_limit_bytes=VMEM_LIMIT_BYTES)


def _rms(x, g):
    ms = jnp.mean(x * x, axis=-1, keepdims=True)
    return (x * lax.rsqrt(ms + EPS)) * g


def _proj_kernel(x_ref, g_ref, tab_ref, w1_ref, wlr_ref, wgu_ref, bg_ref, w2_ref,
                 qa_ref, kv_ref, gl_ref, gt_ref, *, period, qk_scale):
    i = pl.program_id(0)
    tr = x_ref.shape[0]
    hb = _rms(x_ref[...], g_ref[...]).astype(BF16)
    z1 = jnp.dot(hb, w1_ref[...], preferred_element_type=F32)

    start = pl.multiple_of((i * tr) % period, SUBLANES)
    tab = tab_ref[pl.ds(start, tr), :]
    cosf = tab[:, 0:LANES]
    sin_lo = tab[:, LANES:2 * LANES]
    sin_hi = tab[:, 2 * LANES:3 * LANES]
    valid = tab[:, 3 * LANES:3 * LANES + 1]

    def rope(xg):
        return xg * cosf + pltpu.roll(xg, 8, 1) * sin_lo + pltpu.roll(xg, LANES - 8, 1) * sin_hi

    for gi in range(4):
        sl = slice(gi * LANES, (gi + 1) * LANES)
        qa_ref[:, sl] = rope(z1[:, sl]).astype(BF16)
    kv_ref[:, 0:LANES] = rope(z1[:, 512:640])
    kv_ref[:, LANES:2 * LANES] = z1[:, 640:768]

    lr = jnp.dot(hb, wlr_ref[...], preferred_element_type=F32)
    pre = jnp.dot(lr.astype(BF16), wgu_ref[...], preferred_element_type=F32) + bg_ref[...]
    log_sig = jnp.minimum(pre, 0.0) - jnp.log1p(jnp.exp(-jnp.abs(pre)))
    ld = jnp.where(valid > 0.5, log_sig / GATE_NORMALIZER, 0.0)

    gl_ref[:, 0:256] = z1[:, 768:1024] * qk_scale
    gl_ref[:, 256:512] = z1[:, 1024:1280]
    gl_ref[:, 512:768] = ld
    gl_ref[:, 768:1792] = z1[:, 1280:2304]
    gt_ref[...] = jnp.dot(hb, w2_ref[...], preferred_element_type=F32)


def _project(x, g, tab, w1, wlr, wgu, bg, w2, rows, qk_scale):
    r, d = x.shape
    period = tab.shape[0]
    const = lambda i: (0, 0)
    row = lambda i: (i, 0)
    return pl.pallas_call(
        functools.partial(_proj_kernel, period=period, qk_scale=qk_scale),
        grid=(r // rows,),
        in_specs=[
            pl.BlockSpec((rows, d), row),
            pl.BlockSpec(g.shape, const),
            pl.BlockSpec(tab.shape, const),
            pl.BlockSpec(w1.shape, const),
            pl.BlockSpec(wlr.shape, const),
            pl.BlockSpec(wgu.shape, const),
            pl.BlockSpec(bg.shape, const),
            pl.BlockSpec(w2.shape, const),
        ],
        out_specs=[
            pl.BlockSpec((rows, 512), row),
            pl.BlockSpec((rows, 256), row),
            pl.BlockSpec((rows, 1792), row),
            pl.BlockSpec((rows, 2048), row),
        ],
        out_shape=[
            jax.ShapeDtypeStruct((r, 512), BF16),
            jax.ShapeDtypeStruct((r, 256), F32),
            jax.ShapeDtypeStruct((r, 1792), F32),
            jax.ShapeDtypeStruct((r, 2048), F32),
        ],
        compiler_params=_cparams(("arbitrary",)),
        name="proj",
    )(x, g, tab, w1, wlr, wgu, bg, w2)


def _attn_kernel(sink_ref, q_ref, prev_ref, cur_ref, o_ref, *, first_valid_key, block_offset):
    n = pl.program_id(1)
    qr = q_ref.shape[0]
    kr = cur_ref.shape[0]
    w = prev_ref.shape[0]
    nk = w + kr
    group = 4
    hd = 64

    rows = lax.broadcasted_iota(jnp.int32, (group * qr, nk), 0)
    cols = lax.broadcasted_iota(jnp.int32, (group * qr, nk), 1)
    head_of_row = rows // qr
    diff = (rows - head_of_row * qr) - cols + w
    mask = (diff >= 0) & (diff <= w)
    if first_valid_key is not None:
        blk = n + block_offset
        mask = mask & (cols >= first_valid_key + w - blk * w)

    prev = prev_ref[...]
    cur = cur_ref[...]
    q = q_ref[...]
    row_head = lax.broadcasted_iota(jnp.int32, (group * qr, 1), 0) // qr
    for kh in range(2):
        k = jnp.concatenate([prev[:, kh * hd:(kh + 1) * hd], cur[:, kh * hd:(kh + 1) * hd]], axis=0).astype(BF16)
        v = jnp.concatenate([prev[:, LANES + kh * hd:LANES + (kh + 1) * hd],
                             cur[:, LANES + kh * hd:LANES + (kh + 1) * hd]], axis=0).astype(BF16)
        qs = jnp.concatenate([q[:, (group * kh + g) * hd:(group * kh + g + 1) * hd] for g in range(group)], axis=0)
        s = lax.dot_general(qs, k, (((1,), (1,)), ((), ())), preferred_element_type=F32) * (hd ** -0.5)
        s = jnp.where(mask, s, NEG_INF)
        sink = jnp.zeros((group * qr, 1), F32)
        for g in range(group):
            sink = jnp.where(row_head == g, sink_ref[group * kh + g], sink)
        m = jnp.maximum(jnp.max(s, axis=-1, keepdims=True), sink)
        e = jnp.exp(s - m)
        p = e / (jnp.sum(e, axis=-1, keepdims=True) + jnp.exp(sink - m))
        o = jnp.dot(p.astype(BF16), v, preferred_element_type=F32)
        for g in range(group):
            h = group * kh + g
            o_ref[:, h * hd:(h + 1) * hd] = o[g * qr:(g + 1) * qr].astype(BF16)


def _attention(sinks, q, kv_prev, kv_cur, nb, nblk, qr, q_map, prev_map, cur_map, out_map, out_rows,
               first_valid_key, block_offset):
    w = ATTN_BLOCK
    return pl.pallas_call(
        functools.partial(_attn_kernel, first_valid_key=first_valid_key, block_offset=block_offset),
        grid=(nb, nblk),
        in_specs=[
            pl.BlockSpec(memory_space=pltpu.SMEM),
            pl.BlockSpec((qr, 512), q_map),
            pl.BlockSpec((w, 256), prev_map),
            pl.BlockSpec((qr, 256), cur_map),
        ],
        out_specs=pl.BlockSpec((qr, 512), out_map),
        out_shape=jax.ShapeDtypeStruct((out_rows, 512), BF16),
        compiler_params=_cparams(("arbitrary", "arbitrary")),
        name="swa",
    )(sinks, q, kv_prev, kv_cur)


def _gla_kernel(gl_ref, s0_ref, gn_ref, yb_ref, sfin_ref, st_ref):
    c = pl.program_id(1)
    ch = gl_ref.shape[0]
    nh, dk, dv = 4, 64, 128

    @pl.when(c == 0)
    def _():
        for h in range(nh):
            st_ref[h] = s0_ref[0, h].T

    gl = gl_ref[...]
    q = gl[:, 0:256]
    k = gl[:, 256:512]
    b = gl[:, 512:768]
    row = lax.broadcasted_iota(jnp.int32, (ch, nh * dk), 0)
    sh = 1
    while sh < ch:
        b = b + jnp.where(row >= sh, pltpu.roll(b, sh, 0), 0.0)
        sh *= 2
    b_last = b[ch - 1:ch, :]
    q_t = (q * jnp.exp(b)).astype(BF16)
    k_t = (k * jnp.exp(-b)).astype(BF16)
    k_end = (k * jnp.exp(b_last - b)).astype(BF16)
    decay = jnp.exp(b_last)
    causal = (lax.broadcasted_iota(jnp.int32, (ch, ch), 0) >= lax.broadcasted_iota(jnp.int32, (ch, ch), 1))
    gn = gn_ref[...]
    nt = (((1,), (1,)), ((), ()))
    for h in range(nh):
        ks = slice(h * dk, (h + 1) * dk)
        v = gl[:, 768 + h * dv:768 + (h + 1) * dv]
        vb = v.astype(BF16)
        a = lax.dot_general(q_t[:, ks], k_t[:, ks], nt, preferred_element_type=F32)
        a = jnp.where(causal, a, 0.0)
        s_t = st_ref[h]
        o = jnp.dot(a.astype(BF16), vb, preferred_element_type=F32)
        o = o + lax.dot_general(q_t[:, ks], s_t.astype(BF16), nt, preferred_element_type=F32)
        upd = jnp.dot(v.T.astype(BF16), k_end[:, ks], preferred_element_type=F32)
        st_ref[h] = s_t * decay[:, ks] + upd
        go = gl[:, 1280 + h * dv:1280 + (h + 1) * dv]
        y = _rms(o, gn) * (go * jax.nn.sigmoid(go))
        yb_ref[:, h * dv:(h + 1) * dv] = y.astype(BF16)

    @pl.when(c == pl.num_programs(1) - 1)
    def _():
        for h in range(nh):
            sfin_ref[0, h] = st_ref[h].T


def _gla(gl, s0, gn, nb, nchunks, ch, in_map, out_map, out_rows):
    return pl.pallas_call(
        _gla_kernel,
        grid=(nb, nchunks),
        in_specs=[
            pl.BlockSpec((ch, 1792), in_map),
            pl.BlockSpec((1, 4, 64, 128), lambda b, c: (b, 0, 0, 0)),
            pl.BlockSpec((1, 128), lambda b, c: (0, 0)),
        ],
        out_specs=[
            pl.BlockSpec((ch, 512), out_map),
            pl.BlockSpec((1, 4, 64, 128), lambda b, c: (b, 0, 0, 0)),
        ],
        out_shape=[
            jax.ShapeDtypeStruct((out_rows, 512), BF16),
            jax.ShapeDtypeStruct((nb, 4, 64, 128), F32),
        ],
        scratch_shapes=[pltpu.VMEM((4, 128, 64), F32)],
        compiler_params=_cparams(("arbitrary", "arbitrary")),
        name="gla",
    )(gl, s0, gn)


def _extract_topk(work, nsel, iota0, sentinel):
    slabs = work.shape[0] // SUBLANES
    vals, idxs = [], []
    for j in range(nsel):
        v = [work[i * SUBLANES:(i + 1) * SUBLANES] for i in range(slabs)]
        ix = [iota0[i * SUBLANES:(i + 1) * SUBLANES] for i in range(slabs)]
        while len(v) > 1:
            keep = [v[i] >= v[i + 1] for i in range(0, len(v), 2)]
            ix = [jnp.where(k, ix[2 * i], ix[2 * i + 1]) for i, k in enumerate(keep)]
            v = [jnp.where(k, v[2 * i], v[2 * i + 1]) for i, k in enumerate(keep)]
        m = jnp.max(v[0], axis=0, keepdims=True)
        idx = jnp.min(jnp.where(v[0] == m, ix[0], sentinel), axis=0, keepdims=True)
        vals.append(m)
        idxs.append(idx)
        if j + 1 < nsel:
            work = jnp.where(iota0 == idx, -jnp.inf, work)
    return vals, idxs


def _merge_kernel(x_ref, gt_ref, ya_ref, yb_ref, wa_ref, wb_ref, wo_ref, gf_ref, wq_ref, keys_ref,
                  xm_ref, hn_ref, et_ref, wt_ref):
    td = x_ref.shape[0]
    nkeys = keys_ref.shape[1]
    half = keys_ref.shape[2]
    nheads = keys_ref.shape[0] // 2
    topk = PEER_TOPK

    gt = gt_ref[...]
    d = x_ref.shape[1]
    ma = jnp.dot(ya_ref[...], wa_ref[...], preferred_element_type=F32)
    mb = jnp.dot(yb_ref[...], wb_ref[...], preferred_element_type=F32)
    m = jax.nn.sigmoid(gt[:, 0:d]) * ma + jax.nn.sigmoid(gt[:, d:2 * d]) * mb
    xm = x_ref[...] + jnp.dot(m.astype(BF16), wo_ref[...], preferred_element_type=F32)
    xm_ref[...] = xm
    hn = _rms(xm, gf_ref[...])
    hn_ref[...] = hn
    q = jnp.dot(hn.astype(BF16), wq_ref[...], preferred_element_type=F32).astype(BF16)

    nt = (((1,), (1,)), ((), ()))
    iota_k = lax.broadcasted_iota(jnp.int32, (nkeys, td), 0)
    pair_rows = [topk // (a + 1) for a in range(topk)]
    cand_pad = -sum(pair_rows) % SUBLANES
    ncand = sum(pair_rows) + cand_pad
    iota_c = lax.broadcasted_iota(jnp.int32, (ncand, td), 0)
    wts, ids = [], []
    for h in range(nheads):
        sv, si = [], []
        for c in range(2):
            gi = 2 * h + c
            s_t = lax.dot_general(keys_ref[gi], q[:, gi * half:(gi + 1) * half], nt,
                                  preferred_element_type=F32)
            vals, idxs = _extract_topk(s_t, topk, iota_k, nkeys)
            sv.append(vals)
            si.append(idxs)
        sv1 = jnp.concatenate(sv[1], axis=0)
        si1 = jnp.concatenate(si[1], axis=0)
        cand = jnp.concatenate([sv[0][a] + sv1[0:nb] for a, nb in enumerate(pair_rows)]
                               + [jnp.full((cand_pad, td), -jnp.inf, F32)], axis=0)
        cidx = jnp.concatenate([si[0][a] * nkeys + si1[0:nb] for a, nb in enumerate(pair_rows)]
                               + [jnp.full((cand_pad, td), -1, jnp.int32)], axis=0)
        fvals, eids = [], []
        work = cand
        for j in range(topk):
            mx = jnp.max(work, axis=0, keepdims=True)
            pos = jnp.min(jnp.where(work == mx, iota_c, ncand), axis=0, keepdims=True)
            hit = iota_c == pos
            eids.append(jnp.max(jnp.where(hit, cidx, -1), axis=0, keepdims=True))
            fvals.append(mx)
            if j + 1 < topk:
                work = jnp.where(hit, -jnp.inf, work)
        fv = jnp.concatenate(fvals, axis=0)
        e = jnp.exp(fv - fvals[0])
        wts.append(e / jnp.sum(e, axis=0, keepdims=True))
        ids.extend(eids)
    wt_ref[...] = jnp.concatenate(wts, axis=0).T
    et_ref[...] = jnp.concatenate(ids, axis=0).T


def _merge_route(x, gt, ya, yb, wa, wb, wo, gf, wq, keys, gt_map):
    t, d = x.shape
    td = MERGE_ROWS
    nsel = (keys.shape[0] // 2) * PEER_TOPK
    const2 = lambda i: (0, 0)
    row = lambda i: (i, 0)
    return pl.pallas_call(
        _merge_kernel,
        grid=(t // td,),
        in_specs=[
            pl.BlockSpec((td, d), row),
            pl.BlockSpec((td, 2 * d), gt_map),
            pl.BlockSpec((td, ya.shape[1]), row),
            pl.BlockSpec((td, yb.shape[1]), row),
            pl.BlockSpec(wa.shape, const2),
            pl.BlockSpec(wb.shape, const2),
            pl.BlockSpec(wo.shape, const2),
            pl.BlockSpec(gf.shape, const2),
            pl.BlockSpec(wq.shape, const2),
            pl.BlockSpec(keys.shape, lambda i: (0, 0, 0)),
        ],
        out_specs=[
            pl.BlockSpec((td, d), row),
            pl.BlockSpec((td, d), row),
            pl.BlockSpec((td, nsel), row),
            pl.BlockSpec((td, nsel), row),
        ],
        out_shape=[
            jax.ShapeDtypeStruct((t, d), F32),
            jax.ShapeDtypeStruct((t, d), F32),
            jax.ShapeDtypeStruct((t, nsel), jnp.int32),
            jax.ShapeDtypeStruct((t, nsel), F32),
        ],
        compiler_params=_cparams(("arbitrary",)),
        name="merge_route",
    )(x, gt, ya, yb, wa, wb, wo, gf, wq, keys)


def _coef_kernel(act_ref, wt_ref, o_ref):
    act = act_ref[...]
    gelu = 0.5 * act * (1.0 + lax.erf(act * (2.0 ** -0.5)))
    bits = pltpu.bitcast((wt_ref[...] * gelu).astype(BF16).astype(F32), jnp.int32)
    o_ref[...] = bits | lax.shift_right_logical(bits, 16)


def _expert_coefs(act, wt):
    t, nsel = wt.shape
    rows = math.gcd(t, 512)
    row = lambda i: (i, 0)
    return pl.pallas_call(
        _coef_kernel,
        grid=(t // rows,),
        in_specs=[pl.BlockSpec((rows, nsel), row), pl.BlockSpec((rows, nsel), row)],
        out_specs=pl.BlockSpec((rows, nsel), row),
        out_shape=jax.ShapeDtypeStruct((t, nsel), jnp.int32),
        compiler_params=_cparams(("arbitrary",)),
        name="expert_coefs",
    )(act, wt)


def _finish_kernel(xm_ref, o_ref, gfin_ref, yacc_hbm, y_ref):
    del yacc_hbm
    y_ref[...] = _rms(xm_ref[...] + o_ref[...], gfin_ref[...])


def _finish(xm, o, gfin, y_acc, row_off):
    t, d = xm.shape
    rows = math.gcd(t, 512)
    first = row_off // rows
    row = lambda i: (i, 0)
    return pl.pallas_call(
        _finish_kernel,
        grid=(t // rows,),
        in_specs=[pl.BlockSpec((rows, d), row), pl.BlockSpec((rows, d), row),
                  pl.BlockSpec(gfin.shape, lambda i: (0, 0)), pl.BlockSpec(memory_space=pl.ANY)],
        out_specs=pl.BlockSpec((rows, d), lambda i: (i + first, 0)),
        out_shape=jax.ShapeDtypeStruct(y_acc.shape, F32),
        input_output_aliases={3: 0},
        compiler_params=_cparams(("arbitrary",)),
        name="finish",
    )(xm, o, gfin, y_acc)


def _rope_table(pos, valid, rope_dim, head_dim):
    half = rope_dim // 2
    inv = ROPE_THETA ** (-jnp.arange(0, rope_dim, 2, dtype=F32) / rope_dim)
    ang = pos.astype(F32)[:, None] * inv[None, :]
    cos, sin = jnp.cos(ang), jnp.sin(ang)
    n = pos.shape[0]
    ones = jnp.ones((n, head_dim - rope_dim), F32)
    zeros_h = jnp.zeros((n, half), F32)
    zeros_r = jnp.zeros((n, head_dim - rope_dim), F32)
    reps = LANES // head_dim
    cosf = jnp.tile(jnp.concatenate([cos, cos, ones], axis=1), (1, reps))
    sin_lo = jnp.tile(jnp.concatenate([zeros_h, sin, zeros_r], axis=1), (1, reps))
    sin_hi = jnp.tile(jnp.concatenate([-sin, zeros_h, zeros_r], axis=1), (1, reps))
    vcol = jnp.broadcast_to(valid.astype(F32)[:, None], (n, LANES))
    return jnp.concatenate([cosf, sin_lo, sin_hi, vcol], axis=1)


def _pack_pairs(x):
    half = x.shape[1] // 2
    bits = lax.bitcast_convert_type(x.astype(BF16), jnp.uint16).astype(jnp.uint32)
    return lax.bitcast_convert_type(bits[:, :half] | (bits[:, half:] << 16), jnp.int32)


def _expert_dots(table, idx, hn, nsel):
    n = idx.shape[0]
    c = table.shape[1]
    sc = plsc.get_sparse_core_info()
    lanes = sc.num_lanes
    workers = sc.num_cores * sc.num_subcores
    gw = GATHER_WINDOW
    nbuf = GATHER_BUFFERS
    ich = GATHER_INDEX_CHUNK
    tok = ich // nsel
    per_worker = n // workers
    assert n % workers == 0 and per_worker % ich == 0 and nsel == nbuf * gw and hn.shape[1] == c and gw % lanes == 0
    mesh = plsc.VectorSubcoreMesh(core_axis_name="c", subcore_axis_name="s")

    @functools.partial(
        pl.kernel, out_type=jax.ShapeDtypeStruct((n,), F32), mesh=mesh, name="expert_dots",
        compiler_params=pltpu.CompilerParams(needs_layout_passes=False),
        scratch_types=[pltpu.VMEM((ich,), jnp.int32), pltpu.VMEM((tok, c), jnp.int32), pltpu.VMEM((ich,), F32),
                       pltpu.VMEM((lanes * lanes,), F32)]
                      + [pltpu.VMEM((gw, c), jnp.int32)] * nbuf + [pltpu.SemaphoreType.DMA] * nbuf)
    def dots(tab_hbm, idx_hbm, hn_hbm, act_hbm, idx_v, h_v, act_v, scr, *bufs_sems):
        rows, sems = bufs_sems[:nbuf], bufs_sems[nbuf:]
        wid = lax.axis_index("s") * sc.num_cores + lax.axis_index("c")
        base = wid * per_worker
        lane = lax.iota(jnp.int32, lanes)

        def gather(win, buf, sem):
            return pltpu.make_async_copy(tab_hbm.at[idx_v.at[pl.ds(win * gw, gw)]], buf, sem)

        def reduce_window(buf, t_loc, out_off):
            for rb in range(gw // lanes):
                def kbody(k2, accs):
                    k0 = 2 * k2 * lanes
                    hw0 = plsc.bitcast(h_v[t_loc, pl.ds(k0, lanes)], BF16)
                    hw1 = plsc.bitcast(h_v[t_loc, pl.ds(k0 + lanes, lanes)], BF16)
                    out = []
                    for r in range(lanes):
                        w0 = plsc.bitcast(buf[rb * lanes + r, pl.ds(k0, lanes)], BF16)
                        w1 = plsc.bitcast(buf[rb * lanes + r, pl.ds(k0 + lanes, lanes)], BF16)
                        p = plsc.bitcast(w0 * hw0 + w1 * hw1, jnp.int32)
                        out.append(accs[r] + lax.bitcast_convert_type(p << 16, F32)
                                   + lax.bitcast_convert_type(p, F32))
                    return tuple(out)

                accs = lax.fori_loop(0, c // (2 * lanes), kbody,
                                     tuple(jnp.zeros((lanes,), F32) for _ in range(lanes)))
                for r in range(lanes):
                    scr[pl.ds(r * lanes, lanes)] = accs[r]
                cols = [plsc.load_gather(scr, [lane * lanes + l]) for l in range(lanes)]
                while len(cols) > 1:
                    cols = [cols[i] + cols[i + 1] for i in range(0, len(cols), 2)]
                act_v[pl.ds(out_off + rb * lanes, lanes)] = cols[0]

        @pl.loop(0, per_worker // ich)
        def _(g):
            cb = base + g * ich
            pltpu.sync_copy(idx_hbm.at[pl.ds(cb, ich)], idx_v)
            tok_base = pl.multiple_of(wid * (per_worker // nsel) + g * tok, tok)
            pltpu.sync_copy(hn_hbm.at[pl.ds(tok_base, tok)], h_v)
            for q in range(nbuf):
                gather(q, rows[q], sems[q]).start()

            @pl.loop(0, tok)
            def _(j):
                for q in range(nbuf):
                    gather(nbuf * j + q, rows[q], sems[q]).wait()
                    reduce_window(rows[q], j, j * nsel + q * gw)

                    @pl.when(j + 1 < tok)
                    def _():
                        gather(nbuf * (j + 1) + q, rows[q], sems[q]).start()

            pltpu.sync_copy(act_v, act_hbm.at[pl.ds(cb, ich)])

    return dots(table, idx, hn)


def _expert_mix(table, idx, coef, nsel):
    n = idx.shape[0]
    c = table.shape[1]
    d = 2 * c
    sc = plsc.get_sparse_core_info()
    lanes = sc.num_lanes
    workers = sc.num_cores * sc.num_subcores
    gw = GATHER_WINDOW
    nbuf = GATHER_BUFFERS
    ich = GATHER_INDEX_CHUNK
    tok = ich // nsel
    per_worker = n // workers
    kblock = 8
    assert n % workers == 0 and per_worker % ich == 0 and nsel == nbuf * gw and c % (kblock * lanes) == 0
    mesh = plsc.VectorSubcoreMesh(core_axis_name="c", subcore_axis_name="s")

    @functools.partial(
        pl.kernel, out_type=jax.ShapeDtypeStruct((n // nsel, d), F32), mesh=mesh, name="expert_mix",
        compiler_params=pltpu.CompilerParams(needs_layout_passes=False),
        scratch_types=[pltpu.VMEM((ich,), jnp.int32), pltpu.VMEM((ich,), jnp.int32), pltpu.VMEM((tok, d), F32)]
                      + [pltpu.VMEM((gw, c), jnp.int32)] * nbuf + [pltpu.SemaphoreType.DMA] * nbuf)
    def mix(tab_hbm, idx_hbm, coef_hbm, out_hbm, idx_v, coef_v, out_v, *bufs_sems):
        rows, sems = bufs_sems[:nbuf], bufs_sems[nbuf:]
        wid = lax.axis_index("s") * sc.num_cores + lax.axis_index("c")
        base = wid * per_worker
        zero_idx = jnp.zeros((lanes,), jnp.int32)

        def gather(win, buf, sem):
            return pltpu.make_async_copy(tab_hbm.at[idx_v.at[pl.ds(win * gw, gw)]], buf, sem)

        def accumulate_window(buf, t_loc, coef_off, first):
            for kb in range(c // (kblock * lanes)):
                col0 = kb * kblock * lanes
                if first:
                    init = tuple(jnp.zeros((lanes,), F32) for _ in range(2 * kblock))
                else:
                    init = tuple(out_v[t_loc, pl.ds(col0 + i * lanes, lanes)] for i in range(kblock)) + \
                           tuple(out_v[t_loc, pl.ds(c + col0 + i * lanes, lanes)] for i in range(kblock))

                def rbody(r2, accs):
                    accs = list(accs)
                    r = 2 * r2
                    cw0 = plsc.bitcast(plsc.load_gather(coef_v, [zero_idx + (coef_off + r)]), BF16)
                    cw1 = plsc.bitcast(plsc.load_gather(coef_v, [zero_idx + (coef_off + r + 1)]), BF16)
                    for i in range(kblock):
                        w0 = plsc.bitcast(buf[r, pl.ds(col0 + i * lanes, lanes)], BF16)
                        w1 = plsc.bitcast(buf[r + 1, pl.ds(col0 + i * lanes, lanes)], BF16)
                        p = plsc.bitcast(w0 * cw0 + w1 * cw1, jnp.int32)
                        accs[i] = accs[i] + lax.bitcast_convert_type(p << 16, F32)
                        accs[kblock + i] = accs[kblock + i] + lax.bitcast_convert_type(p, F32)
                    return tuple(accs)

                accs = lax.fori_loop(0, gw // 2, rbody, init)
                for i in range(kblock):
                    out_v[t_loc, pl.ds(col0 + i * lanes, lanes)] = accs[i]
                    out_v[t_loc, pl.ds(c + col0 + i * lanes, lanes)] = accs[kblock + i]

        @pl.loop(0, per_worker // ich)
        def _(g):
            cb = base + g * ich
            pltpu.sync_copy(idx_hbm.at[pl.ds(cb, ich)], idx_v)
            pltpu.sync_copy(coef_hbm.at[pl.ds(cb, ich)], coef_v)
            for q in range(nbuf):
                gather(q, rows[q], sems[q]).start()

            @pl.loop(0, tok)
            def _(j):
                for q in range(nbuf):
                    gather(nbuf * j + q, rows[q], sems[q]).wait()
                    accumulate_window(rows[q], j, j * nsel + q * gw, q == 0)

                    @pl.when(j + 1 < tok)
                    def _():
                        gather(nbuf * (j + 1) + q, rows[q], sems[q]).start()

            tok_base = pl.multiple_of(wid * (per_worker // nsel) + g * tok, tok)
            pltpu.sync_copy(out_v, out_hbm.at[pl.ds(tok_base, tok)])

    return mix(table, idx, coef)


def _peer_tail(xm, hn, et, wt, tabs, gfin, y_acc, row_off):
    u_tab, v_tab = tabs
    t, nsel = et.shape
    eidx = et.reshape(t * nsel)
    act = _expert_dots(u_tab, eidx, _pack_pairs(hn), nsel).reshape(t, nsel)
    coef = _expert_coefs(act, wt).reshape(t * nsel)
    mixed = _expert_mix(v_tab, eidx, coef, nsel)
    return _finish(xm, mixed, gfin, y_acc, row_off)


def kernel(x_prompt, x_sample, cache_k_window, cache_v_window, state_gla, meta_tokens, g_norm_mix, w_in,
           w_gate_up, b_gate, attn_sinks, g_gla_norm, w_branch_a, w_branch_b, w_out, g_norm_ffn, w_peer_q,
           peer_sub_keys, peer_u, peer_v, g_norm_final):
    bsz, seq, d = x_prompt.shape
    dbsz, tdec, _ = x_sample.shape
    n_meta = meta_tokens.shape[0]
    depth = w_in.shape[0]
    window = cache_k_window.shape[2]
    kv_heads, head_dim = cache_k_window.shape[3], cache_k_window.shape[4]
    gate_rank = w_gate_up.shape[1]
    bqk = w_gate_up.shape[2]
    n_ph, _, n_keys, p_half = peer_sub_keys.shape[1:]
    assert depth == 1 and d == 1024 and window == ATTN_BLOCK and kv_heads == 2 and head_dim == 64
    assert bqk == 256 and state_gla.shape[2:] == (4, 64, 128) and n_meta <= ATTN_BLOCK
    assert seq % ATTN_BLOCK == 0 and tdec <= SAMPLE_PAD and n_keys == 128 and p_half == 64 and n_ph == 8
    rope_dim = head_dim // 4
    meta_pad = ATTN_BLOCK - n_meta
    lp = ATTN_BLOCK + seq
    nblk = lp // ATTN_BLOCK

    w = w_in[0]
    c_lr = 2304
    c_gate = c_lr + gate_rank
    w1 = w[:, :c_lr].astype(BF16)
    wlr = jnp.pad(w[:, c_lr:c_gate], ((0, 0), (0, LANES - gate_rank))).astype(BF16)
    w2 = w[:, c_gate:].astype(BF16)
    wgu = jnp.pad(w_gate_up[0], ((0, LANES - gate_rank), (0, 0))).astype(BF16)
    bg = b_gate[0][None, :]
    gmix = g_norm_mix[0][None, :]
    wa = w_branch_a[0].astype(BF16)
    wb = w_branch_b[0].astype(BF16)
    wo = w_out[0].astype(BF16)
    gffn = g_norm_ffn[0][None, :]
    wq = w_peer_q[0].astype(BF16)
    keys = peer_sub_keys[0].reshape(n_ph * 2, n_keys, p_half).astype(BF16)
    u_tab = _pack_pairs(peer_u[0])
    v_tab = _pack_pairs(peer_v[0])
    tabs = (u_tab, v_tab)
    gfin = g_norm_final[None, :]
    gn = g_gla_norm[0][None, :]
    sinks = attn_sinks[0]
    qk_scale = float(bqk // 4) ** -0.5

    rows_p = jnp.arange(lp)
    tab_p = _rope_table(rows_p - meta_pad, rows_p >= meta_pad, rope_dim, head_dim)
    proj_rows = max(r for r in range(16, PROJ_ROWS + 1, 16) if lp % r == 0)
    nq = nblk - 1
    nchunks = lp // GLA_CHUNK
    skip = ATTN_BLOCK // GLA_CHUNK
    ncq = nchunks - skip
    per_seq = seq // MERGE_ROWS
    gt_map_p = lambda i: ((i // per_seq) * nblk + 1 + (i % per_seq), 0)

    def prompt_sequences(xg, y_acc, row_off):
        gb = xg.shape[0]
        meta = jnp.broadcast_to(meta_tokens[None].astype(xg.dtype), (gb, n_meta, d))
        xpad = jnp.concatenate([jnp.zeros((gb, meta_pad, d), xg.dtype), meta, xg], axis=1).reshape(gb * lp, d)
        qa, kv, gl, gt = _project(xpad, gmix, tab_p, w1, wlr, wgu, bg, w2, proj_rows, qk_scale)
        ya = _attention(
            sinks, qa, kv, kv, gb, nq, ATTN_BLOCK,
            lambda b, n: (b * nblk + n + 1, 0), lambda b, n: (b * nblk + n, 0), lambda b, n: (b * nblk + n + 1, 0),
            lambda b, n: (b * nq + n, 0), gb * seq, first_valid_key=meta_pad, block_offset=1)
        s0 = jnp.zeros((gb,) + state_gla.shape[2:], F32)
        yb, s_fin = _gla(gl, s0, gn, gb, nchunks, GLA_CHUNK,
                         lambda b, c: (b * nchunks + c, 0),
                         lambda b, c: (b * ncq + jnp.maximum(c - skip, 0), 0), gb * seq)
        xm, hn, et, wt = _merge_route(xg.reshape(gb * seq, d), gt, ya, yb, wa, wb, wo, gffn, wq, keys,
                                      gt_map_p)
        y_acc = _peer_tail(xm, hn, et, wt, tabs, gfin, y_acc, row_off)
        kv_w = kv.reshape(gb, lp, 2, kv_heads, head_dim)[:, lp - window:]
        return y_acc, kv_w, s_fin

    group = PROMPT_GROUP if (bsz % PROMPT_GROUP == 0 and (PROMPT_GROUP * seq) % GATHER_ROW_QUANTUM == 0) else bsz
    y_acc, kv_parts, s_parts = jnp.zeros((bsz * seq, d), F32), [], []
    for b0 in range(0, bsz, group):
        y_acc, kv_w, s_fin = prompt_sequences(x_prompt[b0:b0 + group], y_acc, b0 * seq)
        kv_parts.append(kv_w)
        s_parts.append(s_fin)
    y_prompt = y_acc.reshape(bsz, seq, d)
    kv_p = jnp.concatenate(kv_parts, axis=0)
    s_fin_p = jnp.concatenate(s_parts, axis=0)
    new_k_p = kv_p[:, :, 0][None]
    new_v_p = kv_p[:, :, 1][None]

    sp = SAMPLE_PAD
    xs_pad = jnp.pad(x_sample, ((0, 0), (0, sp - tdec), (0, 0))).reshape(dbsz * sp, d)
    rows_s = jnp.arange(sp)
    reps = 256 // sp
    tab_s = jnp.tile(_rope_table(PAST_LEN + rows_s, rows_s < tdec, rope_dim, head_dim), (reps, 1))
    qa_s, kv_s, gl_s, gt_s = _project(xs_pad, gmix, tab_s, w1, wlr, wgu, bg, w2, 256, qk_scale)

    cache_kv = jnp.concatenate([cache_k_window[0].reshape(dbsz * window, kv_heads * head_dim),
                                cache_v_window[0].reshape(dbsz * window, kv_heads * head_dim)], axis=1)
    seq_map = lambda b, n: (b, 0)
    ya_s = _attention(sinks, qa_s, cache_kv, kv_s, dbsz, 1, sp, seq_map, seq_map, seq_map, seq_map,
                      dbsz * sp, first_valid_key=None, block_offset=0)
    yb_s, s_fin_s = _gla(gl_s, state_gla[0], gn, dbsz, 1, sp, seq_map, seq_map, dbsz * sp)

    def real_rows(a):
        return a.reshape(dbsz, sp, a.shape[-1])[:, :tdec].reshape(dbsz * tdec, a.shape[-1])

    xs_rows = x_sample.reshape(dbsz * tdec, d)
    xm_s, hn_s, et_s, wt_s = _merge_route(xs_rows, real_rows(gt_s), real_rows(ya_s), real_rows(yb_s),
                                          wa, wb, wo, gffn, wq, keys, lambda i: (i, 0))
    y_sample = _peer_tail(xm_s, hn_s, et_s, wt_s, tabs, gfin, jnp.zeros((dbsz * tdec, d), F32), 0).reshape(dbsz, tdec, d)

    kv_new = real_rows(kv_s).reshape(dbsz, tdec, 2, kv_heads, head_dim)
    new_k_s = jnp.concatenate([cache_k_window[0].astype(F32), kv_new[:, :, 0]], axis=1)[:, -window:][None]
    new_v_s = jnp.concatenate([cache_v_window[0].astype(F32), kv_new[:, :, 1]], axis=1)[:, -window:][None]

    return (y_prompt, y_sample, new_k_p, new_v_p, s_fin_p[None], new_k_s, new_v_s, s_fin_s[None])
```

```python
import functools
import math

import jax
import jax.numpy as jnp
from jax import lax
from jax.experimental import pallas as pl
from jax.experimental.pallas import tpu as pltpu
from jax.experimental.pallas import tpu_sc as plsc

F32 = jnp.float32
BF16 = jnp.bfloat16

EPS = 1e-6
NEG_INF = -1e30
PAST_LEN = 16384
ROPE_THETA = 500000.0
GATE_NORMALIZER = 16.0
PEER_TOPK = 16

LANES = 128
SUBLANES = 8
VMEM_LIMIT_BYTES = 56 * 1024 * 1024

ATTN_BLOCK = 128
GLA_CHUNK = 64
SAMPLE_PAD = 16
PROJ_ROWS = 544
MERGE_ROWS = 128
GATHER_WINDOW = 32
GATHER_BUFFERS = 4
GATHER_INDEX_CHUNK = 2048
GATHER_ROW_QUANTUM = 512
PROMPT_GROUP = 1


def _cparams(sem):
    return pltpu.CompilerParams(dimension_semantics=sem, vmem_limit_bytes=VMEM_LIMIT_BYTES)


def _rms(x, g):
    ms = jnp.mean(x * x, axis=-1, keepdims=True)
    return (x * lax.rsqrt(ms + EPS)) * g


def _proj_kernel(x_ref, g_ref, tab_ref, w1_ref, wlr_ref, wgu_ref, bg_ref, w2_ref,
                 qa_ref, kv_ref, gl_ref, gt_ref, *, period, qk_scale):
    i = pl.program_id(0)
    tr = x_ref.shape[0]
    hb = _rms(x_ref[...], g_ref[...]).astype(BF16)
    z1 = jnp.dot(hb, w1_ref[...], preferred_element_type=F32)

    start = pl.multiple_of((i * tr) % period, SUBLANES)
    tab = tab_ref[pl.ds(start, tr), :]
    cosf = tab[:, 0:LANES]
    sin_lo = tab[:, LANES:2 * LANES]
    sin_hi = tab[:, 2 * LANES:3 * LANES]
    valid = tab[:, 3 * LANES:3 * LANES + 1]

    def rope(xg):
        return xg * cosf + pltpu.roll(xg, 8, 1) * sin_lo + pltpu.roll(xg, LANES - 8, 1) * sin_hi

    for gi in range(4):
        sl = slice(gi * LANES, (gi + 1) * LANES)
        qa_ref[:, sl] = rope(z1[:, sl]).astype(BF16)
    kv_ref[:, 0:LANES] = rope(z1[:, 512:640])
    kv_ref[:, LANES:2 * LANES] = z1[:, 640:768]

    lr = jnp.dot(hb, wlr_ref[...], preferred_element_type=F32)
    pre = jnp.dot(lr.astype(BF16), wgu_ref[...], preferred_element_type=F32) + bg_ref[...]
    log_sig = jnp.minimum(pre, 0.0) - jnp.log1p(jnp.exp(-jnp.abs(pre)))
    ld = jnp.where(valid > 0.5, log_sig / GATE_NORMALIZER, 0.0)

    gl_ref[:, 0:256] = z1[:, 768:1024] * qk_scale
    gl_ref[:, 256:512] = z1[:, 1024:1280]
    gl_ref[:, 512:768] = ld
    gl_ref[:, 768:1792] = z1[:, 1280:2304]
    gt_ref[...] = jnp.dot(hb, w2_ref[...], preferred_element_type=F32)


def _project(x, g, tab, w1, wlr, wgu, bg, w2, rows, qk_scale):
    r, d = x.shape
    period = tab.shape[0]
    const = lambda i: (0, 0)
    row = lambda i: (i, 0)
    return pl.pallas_call(
        functools.partial(_proj_kernel, period=period, qk_scale=qk_scale),
        grid=(r // rows,),
        in_specs=[
            pl.BlockSpec((rows, d), row),
            pl.BlockSpec(g.shape, const),
            pl.BlockSpec(tab.shape, const),
            pl.BlockSpec(w1.shape, const),
            pl.BlockSpec(wlr.shape, const),
            pl.BlockSpec(wgu.shape, const),
            pl.BlockSpec(bg.shape, const),
            pl.BlockSpec(w2.shape, const),
        ],
        out_specs=[
            pl.BlockSpec((rows, 512), row),
            pl.BlockSpec((rows, 256), row),
            pl.BlockSpec((rows, 1792), row),
            pl.BlockSpec((rows, 2048), row),
        ],
        out_shape=[
            jax.ShapeDtypeStruct((r, 512), BF16),
            jax.ShapeDtypeStruct((r, 256), F32),
            jax.ShapeDtypeStruct((r, 1792), F32),
            jax.ShapeDtypeStruct((r, 2048), F32),
        ],
        compiler_params=_cparams(("arbitrary",)),
        name="proj",
    )(x, g, tab, w1, wlr, wgu, bg, w2)


def _attn_kernel(sink_ref, q_ref, prev_ref, cur_ref, o_ref, *, first_valid_key, block_offset):
    n = pl.program_id(1)
    qr = q_ref.shape[0]
    kr = cur_ref.shape[0]
    w = prev_ref.shape[0]
    nk = w + kr
    group = 4
    hd = 64

    rows = lax.broadcasted_iota(jnp.int32, (group * qr, nk), 0)
    cols = lax.broadcasted_iota(jnp.int32, (group * qr, nk), 1)
    head_of_row = rows // qr
    diff = (rows - head_of_row * qr) - cols + w
    mask = (diff >= 0) & (diff <= w)
    if first_valid_key is not None:
        blk = n + block_offset
        mask = mask & (cols >= first_valid_key + w - blk * w)

    prev = prev_ref[...]
    cur = cur_ref[...]
    q = q_ref[...]
    row_head = lax.broadcasted_iota(jnp.int32, (group * qr, 1), 0) // qr
    for kh in range(2):
        k = jnp.concatenate([prev[:, kh * hd:(kh + 1) * hd], cur[:, kh * hd:(kh + 1) * hd]], axis=0).astype(BF16)
        v = jnp.concatenate([prev[:, LANES + kh * hd:LANES + (kh + 1) * hd],
                             cur[:, LANES + kh * hd:LANES + (kh + 1) * hd]], axis=0).astype(BF16)
        qs = jnp.concatenate([q[:, (group * kh + g) * hd:(group * kh + g + 1) * hd] for g in range(group)], axis=0)
        s = lax.dot_general(qs, k, (((1,), (1,)), ((), ())), preferred_element_type=F32) * (hd ** -0.5)
        s = jnp.where(mask, s, NEG_INF)
        sink = jnp.zeros((group * qr, 1), F32)
        for g in range(group):
            sink = jnp.where(row_head == g, sink_ref[group * kh + g], sink)
        m = jnp.maximum(jnp.max(s, axis=-1, keepdims=True), sink)
        e = jnp.exp(s - m)
        p = e / (jnp.sum(e, axis=-1, keepdims=True) + jnp.exp(sink - m))
        o = jnp.dot(p.astype(BF16), v, preferred_element_type=F32)
        for g in range(group):
            h = group * kh + g
            o_ref[:, h * hd:(h + 1) * hd] = o[g * qr:(g + 1) * qr].astype(BF16)


def _attention(sinks, q, kv_prev, kv_cur, nb, nblk, qr, q_map, prev_map, cur_map, out_map, out_rows,
               first_valid_key, block_offset):
    w = ATTN_BLOCK
    return pl.pallas_call(
        functools.partial(_attn_kernel, first_valid_key=first_valid_key, block_offset=block_offset),
        grid=(nb, nblk),
        in_specs=[
            pl.BlockSpec(memory_space=pltpu.SMEM),
            pl.BlockSpec((qr, 512), q_map),
            pl.BlockSpec((w, 256), prev_map),
            pl.BlockSpec((qr, 256), cur_map),
        ],
        out_specs=pl.BlockSpec((qr, 512), out_map),
        out_shape=jax.ShapeDtypeStruct((out_rows, 512), BF16),
        compiler_params=_cparams(("arbitrary", "arbitrary")),
        name="swa",
    )(sinks, q, kv_prev, kv_cur)


def _gla_kernel(gl_ref, s0_ref, gn_ref, yb_ref, sfin_ref, st_ref):
    c = pl.program_id(1)
    ch = gl_ref.shape[0]
    nh, dk, dv = 4, 64, 128

    @pl.when(c == 0)
    def _():
        for h in range(nh):
            st_ref[h] = s0_ref[0, h].T

    gl = gl_ref[...]
    q = gl[:, 0:256]
    k = gl[:, 256:512]
    b = gl[:, 512:768]
    row = lax.broadcasted_iota(jnp.int32, (ch, nh * dk), 0)
    sh = 1
    while sh < ch:
        b = b + jnp.where(row >= sh, pltpu.roll(b, sh, 0), 0.0)
        sh *= 2
    b_last = b[ch - 1:ch, :]
    q_t = (q * jnp.exp(b)).astype(BF16)
    k_t = (k * jnp.exp(-b)).astype(BF16)
    k_end = (k * jnp.exp(b_last - b)).astype(BF16)
    decay = jnp.exp(b_last)
    causal = (lax.broadcasted_iota(jnp.int32, (ch, ch), 0) >= lax.broadcasted_iota(jnp.int32, (ch, ch), 1))
    gn = gn_ref[...]
    nt = (((1,), (1,)), ((), ()))
    for h in range(nh):
        ks = slice(h * dk, (h + 1) * dk)
        v = gl[:, 768 + h * dv:768 + (h + 1) * dv]
        vb = v.astype(BF16)
        a = lax.dot_general(q_t[:, ks], k_t[:, ks], nt, preferred_element_type=F32)
        a = jnp.where(causal, a, 0.0)
        s_t = st_ref[h]
        o = jnp.dot(a.astype(BF16), vb, preferred_element_type=F32)
        o = o + lax.dot_general(q_t[:, ks], s_t.astype(BF16), nt, preferred_element_type=F32)
        upd = jnp.dot(v.T.astype(BF16), k_end[:, ks], preferred_element_type=F32)
        st_ref[h] = s_t * decay[:, ks] + upd
        go = gl[:, 1280 + h * dv:1280 + (h + 1) * dv]
        y = _rms(o, gn) * (go * jax.nn.sigmoid(go))
        yb_ref[:, h * dv:(h + 1) * dv] = y.astype(BF16)

    @pl.when(c == pl.num_programs(1) - 1)
    def _():
        for h in range(nh):
            sfin_ref[0, h] = st_ref[h].T


def _gla(gl, s0, gn, nb, nchunks, ch, in_map, out_map, out_rows):
    return pl.pallas_call(
        _gla_kernel,
        grid=(nb, nchunks),
        in_specs=[
            pl.BlockSpec((ch, 1792), in_map),
            pl.BlockSpec((1, 4, 64, 128), lambda b, c: (b, 0, 0, 0)),
            pl.BlockSpec((1, 128), lambda b, c: (0, 0)),
        ],
        out_specs=[
            pl.BlockSpec((ch, 512), out_map),
            pl.BlockSpec((1, 4, 64, 128), lambda b, c: (b, 0, 0, 0)),
        ],
        out_shape=[
            jax.ShapeDtypeStruct((out_rows, 512), BF16),
            jax.ShapeDtypeStruct((nb, 4, 64, 128), F32),
        ],
        scratch_shapes=[pltpu.VMEM((4, 128, 64), F32)],
        compiler_params=_cparams(("arbitrary", "arbitrary")),
        name="gla",
    )(gl, s0, gn)


def _extract_topk(work, nsel, iota0, sentinel):
    slabs = work.shape[0] // SUBLANES
    vals, idxs = [], []
    for j in range(nsel):
        v = [work[i * SUBLANES:(i + 1) * SUBLANES] for i in range(slabs)]
        ix = [iota0[i * SUBLANES:(i + 1) * SUBLANES] for i in range(slabs)]
        while len(v) > 1:
            keep = [v[i] >= v[i + 1] for i in range(0, len(v), 2)]
            ix = [jnp.where(k, ix[2 * i], ix[2 * i + 1]) for i, k in enumerate(keep)]
            v = [jnp.where(k, v[2 * i], v[2 * i + 1]) for i, k in enumerate(keep)]
        m = jnp.max(v[0], axis=0, keepdims=True)
        idx = jnp.min(jnp.where(v[0] == m, ix[0], sentinel), axis=0, keepdims=True)
        vals.append(m)
        idxs.append(idx)
        if j + 1 < nsel:
            work = jnp.where(iota0 == idx, -jnp.inf, work)
    return vals, idxs


def _merge_kernel(x_ref, gt_ref, ya_ref, yb_ref, wa_ref, wb_ref, wo_ref, gf_ref, wq_ref, keys_ref,
                  xm_ref, hn_ref, et_ref, wt_ref):
    td = x_ref.shape[0]
    nkeys = keys_ref.shape[1]
    half = keys_ref.shape[2]
    nheads = keys_ref.shape[0] // 2
    topk = PEER_TOPK

    gt = gt_ref[...]
    d = x_ref.shape[1]
    ma = jnp.dot(ya_ref[...], wa_ref[...], preferred_element_type=F32)
    mb = jnp.dot(yb_ref[...], wb_ref[...], preferred_element_type=F32)
    m = jax.nn.sigmoid(gt[:, 0:d]) * ma + jax.nn.sigmoid(gt[:, d:2 * d]) * mb
    xm = x_ref[...] + jnp.dot(m.astype(BF16), wo_ref[...], preferred_element_type=F32)
    xm_ref[...] = xm
    hn = _rms(xm, gf_ref[...])
    hn_ref[...] = hn
    q = jnp.dot(hn.astype(BF16), wq_ref[...], preferred_element_type=F32).astype(BF16)

    nt = (((1,), (1,)), ((), ()))
    iota_k = lax.broadcasted_iota(jnp.int32, (nkeys, td), 0)
    pair_rows = [topk // (a + 1) for a in range(topk)]
    cand_pad = -sum(pair_rows) % SUBLANES
    ncand = sum(pair_rows) + cand_pad
    iota_c = lax.broadcasted_iota(jnp.int32, (ncand, td), 0)
    wts, ids = [], []
    for h in range(nheads):
        sv, si = [], []
        for c in range(2):
            gi = 2 * h + c
            s_t = lax.dot_general(keys_ref[gi], q[:, gi * half:(gi + 1) * half], nt,
                                  preferred_element_type=F32)
            vals, idxs = _extract_topk(s_t, topk, iota_k, nkeys)
            sv.append(vals)
            si.append(idxs)
        sv1 = jnp.concatenate(sv[1], axis=0)
        si1 = jnp.concatenate(si[1], axis=0)
        cand = jnp.concatenate([sv[0][a] + sv1[0:nb] for a, nb in enumerate(pair_rows)]
                               + [jnp.full((cand_pad, td), -jnp.inf, F32)], axis=0)
        cidx = jnp.concatenate([si[0][a] * nkeys + si1[0:nb] for a, nb in enumerate(pair_rows)]
                               + [jnp.full((cand_pad, td), -1, jnp.int32)], axis=0)
        fvals, eids = [], []
        work = cand
        for j in range(topk):
            mx = jnp.max(work, axis=0, keepdims=True)
            pos = jnp.min(jnp.where(work == mx, iota_c, ncand), axis=0, keepdims=True)
            hit = iota_c == pos
            eids.append(jnp.max(jnp.where(hit, cidx, -1), axis=0, keepdims=True))
            fvals.append(mx)
            if j + 1 < topk:
                work = jnp.where(hit, -jnp.inf, work)
        fv = jnp.concatenate(fvals, axis=0)
        e = jnp.exp(fv - fvals[0])
        wts.append(e / jnp.sum(e, axis=0, keepdims=True))
        ids.extend(eids)
    wt_ref[...] = jnp.concatenate(wts, axis=0).T
    et_ref[...] = jnp.concatenate(ids, axis=0).T


def _merge_route(x, gt, ya, yb, wa, wb, wo, gf, wq, keys, gt_map):
    t, d = x.shape
    td = MERGE_ROWS
    nsel = (keys.shape[0] // 2) * PEER_TOPK
    const2 = lambda i: (0, 0)
    row = lambda i: (i, 0)
    return pl.pallas_call(
        _merge_kernel,
        grid=(t // td,),
        in_specs=[
            pl.BlockSpec((td, d), row),
            pl.BlockSpec((td, 2 * d), gt_map),
            pl.BlockSpec((td, ya.shape[1]), row),
            pl.BlockSpec((td, yb.shape[1]), row),
            pl.BlockSpec(wa.shape, const2),
            pl.BlockSpec(wb.shape, const2),
            pl.BlockSpec(wo.shape, const2),
            pl.BlockSpec(gf.shape, const2),
            pl.BlockSpec(wq.shape, const2),
            pl.BlockSpec(keys.shape, lambda i: (0, 0, 0)),
        ],
        out_specs=[
            pl.BlockSpec((td, d), row),
            pl.BlockSpec((td, d), row),
            pl.BlockSpec((td, nsel), row),
            pl.BlockSpec((td, nsel), row),
        ],
        out_shape=[
            jax.ShapeDtypeStruct((t, d), F32),
            jax.ShapeDtypeStruct((t, d), F32),
            jax.ShapeDtypeStruct((t, nsel), jnp.int32),
            jax.ShapeDtypeStruct((t, nsel), F32),
        ],
        compiler_params=_cparams(("arbitrary",)),
        name="merge_route",
    )(x, gt, ya, yb, wa, wb, wo, gf, wq, keys)


def _coef_kernel(act_ref, wt_ref, o_ref):
    act = act_ref[...]
    gelu = 0.5 * act * (1.0 + lax.erf(act * (2.0 ** -0.5)))
    bits = pltpu.bitcast((wt_ref[...] * gelu).astype(BF16).astype(F32), jnp.int32)
    o_ref[...] = bits | lax.shift_right_logical(bits, 16)


def _expert_coefs(act, wt):
    t, nsel = wt.shape
    rows = math.gcd(t, 512)
    row = lambda i: (i, 0)
    return pl.pallas_call(
        _coef_kernel,
        grid=(t // rows,),
        in_specs=[pl.BlockSpec((rows, nsel), row), pl.BlockSpec((rows, nsel), row)],
        out_specs=pl.BlockSpec((rows, nsel), row),
        out_shape=jax.ShapeDtypeStruct((t, nsel), jnp.int32),
        compiler_params=_cparams(("arbitrary",)),
        name="expert_coefs",
    )(act, wt)


def _finish_kernel(xm_ref, o_ref, gfin_ref, yacc_hbm, y_ref):
    del yacc_hbm
    y_ref[...] = _rms(xm_ref[...] + o_ref[...], gfin_ref[...])


def _finish(xm, o, gfin, y_acc, row_off):
    t, d = xm.shape
    rows = math.gcd(t, 512)
    first = row_off // rows
    row = lambda i: (i, 0)
    return pl.pallas_call(
        _finish_kernel,
        grid=(t // rows,),
        in_specs=[pl.BlockSpec((rows, d), row), pl.BlockSpec((rows, d), row),
                  pl.BlockSpec(gfin.shape, lambda i: (0, 0)), pl.BlockSpec(memory_space=pl.ANY)],
        out_specs=pl.BlockSpec((rows, d), lambda i: (i + first, 0)),
        out_shape=jax.ShapeDtypeStruct(y_acc.shape, F32),
        input_output_aliases={3: 0},
        compiler_params=_cparams(("arbitrary",)),
        name="finish",
    )(xm, o, gfin, y_acc)


def _rope_table(pos, valid, rope_dim, head_dim):
    half = rope_dim // 2
    inv = ROPE_THETA ** (-jnp.arange(0, rope_dim, 2, dtype=F32) / rope_dim)
    ang = pos.astype(F32)[:, None] * inv[None, :]
    cos, sin = jnp.cos(ang), jnp.sin(ang)
    n = pos.shape[0]
    ones = jnp.ones((n, head_dim - rope_dim), F32)
    zeros_h = jnp.zeros((n, half), F32)
    zeros_r = jnp.zeros((n, head_dim - rope_dim), F32)
    reps = LANES // head_dim
    cosf = jnp.tile(jnp.concatenate([cos, cos, ones], axis=1), (1, reps))
    sin_lo = jnp.tile(jnp.concatenate([zeros_h, sin, zeros_r], axis=1), (1, reps))
    sin_hi = jnp.tile(jnp.concatenate([-sin, zeros_h, zeros_r], axis=1), (1, reps))
    vcol = jnp.broadcast_to(valid.astype(F32)[:, None], (n, LANES))
    return jnp.concatenate([cosf, sin_lo, sin_hi, vcol], axis=1)


def _pack_pairs(x):
    half = x.shape[1] // 2
    bits = lax.bitcast_convert_type(x.astype(BF16), jnp.uint16).astype(jnp.uint32)
    return lax.bitcast_convert_type(bits[:, :half] | (bits[:, half:] << 16), jnp.int32)


def _expert_dots(table, idx, hn, nsel):
    n = idx.shape[0]
    c = table.shape[1]
    sc = plsc.get_sparse_core_info()
    lanes = sc.num_lanes
    workers = sc.num_cores * sc.num_subcores
    gw = GATHER_WINDOW
    nbuf = GATHER_BUFFERS
    ich = GATHER_INDEX_CHUNK
    tok = ich // nsel
    per_worker = n // workers
    assert n % workers == 0 and per_worker % ich == 0 and nsel == nbuf * gw and hn.shape[1] == c and gw % lanes == 0
    mesh = plsc.VectorSubcoreMesh(core_axis_name="c", subcore_axis_name="s")

    @functools.partial(
        pl.kernel, out_type=jax.ShapeDtypeStruct((n,), F32), mesh=mesh, name="expert_dots",
        compiler_params=pltpu.CompilerParams(needs_layout_passes=False),
        scratch_types=[pltpu.VMEM((ich,), jnp.int32), pltpu.VMEM((tok, c), jnp.int32), pltpu.VMEM((ich,), F32),
                       pltpu.VMEM((lanes * lanes,), F32)]
                      + [pltpu.VMEM((gw, c), jnp.int32)] * nbuf + [pltpu.SemaphoreType.DMA] * nbuf)
    def dots(tab_hbm, idx_hbm, hn_hbm, act_hbm, idx_v, h_v, act_v, scr, *bufs_sems):
        rows, sems = bufs_sems[:nbuf], bufs_sems[nbuf:]
        wid = lax.axis_index("s") * sc.num_cores + lax.axis_index("c")
        base = wid * per_worker
        lane = lax.iota(jnp.int32, lanes)

        def gather(win, buf, sem):
            return pltpu.make_async_copy(tab_hbm.at[idx_v.at[pl.ds(win * gw, gw)]], buf, sem)

        def reduce_window(buf, t_loc, out_off):
            for rb in range(gw // lanes):
                def kbody(k2, accs):
                    k0 = 2 * k2 * lanes
                    hw0 = plsc.bitcast(h_v[t_loc, pl.ds(k0, lanes)], BF16)
                    hw1 = plsc.bitcast(h_v[t_loc, pl.ds(k0 + lanes, lanes)], BF16)
                    out = []
                    for r in range(lanes):
                        w0 = plsc.bitcast(buf[rb * lanes + r, pl.ds(k0, lanes)], BF16)
                        w1 = plsc.bitcast(buf[rb * lanes + r, pl.ds(k0 + lanes, lanes)], BF16)
                        p = plsc.bitcast(w0 * hw0 + w1 * hw1, jnp.int32)
                        out.append(accs[r] + lax.bitcast_convert_type(p << 16, F32)
                                   + lax.bitcast_convert_type(p, F32))
                    return tuple(out)

                accs = lax.fori_loop(0, c // (2 * lanes), kbody,
                                     tuple(jnp.zeros((lanes,), F32) for _ in range(lanes)))
                for r in range(lanes):
                    scr[pl.ds(r * lanes, lanes)] = accs[r]
                cols = [plsc.load_gather(scr, [lane * lanes + l]) for l in range(lanes)]
                while len(cols) > 1:
                    cols = [cols[i] + cols[i + 1] for i in range(0, len(cols), 2)]
                act_v[pl.ds(out_off + rb * lanes, lanes)] = cols[0]

        @pl.loop(0, per_worker // ich)
        def _(g):
            cb = base + g * ich
            pltpu.sync_copy(idx_hbm.at[pl.ds(cb, ich)], idx_v)
            tok_base = pl.multiple_of(wid * (per_worker // nsel) + g * tok, tok)
            pltpu.sync_copy(hn_hbm.at[pl.ds(tok_base, tok)], h_v)
            for q in range(nbuf):
                gather(q, rows[q], sems[q]).start()

            @pl.loop(0, tok)
            def _(j):
                for q in range(nbuf):
                    gather(nbuf * j + q, rows[q], sems[q]).wait()
                    reduce_window(rows[q], j, j * nsel + q * gw)

                    @pl.when(j + 1 < tok)
                    def _():
                        gather(nbuf * (j + 1) + q, rows[q], sems[q]).start()

            pltpu.sync_copy(act_v, act_hbm.at[pl.ds(cb, ich)])

    return dots(table, idx, hn)


def _expert_mix(table, idx, coef, nsel):
    n = idx.shape[0]
    c = table.shape[1]
    d = 2 * c
    sc = plsc.get_sparse_core_info()
    lanes = sc.num_lanes
    workers = sc.num_cores * sc.num_subcores
    gw = GATHER_WINDOW
    nbuf = GATHER_BUFFERS
    ich = GATHER_INDEX_CHUNK
    tok = ich // nsel
    per_worker = n // workers
    kblock = 16
    assert n % workers == 0 and per_worker % ich == 0 and nsel == nbuf * gw and c % (kblock * lanes) == 0
    mesh = plsc.VectorSubcoreMesh(core_axis_name="c", subcore_axis_name="s")

    @functools.partial(
        pl.kernel, out_type=jax.ShapeDtypeStruct((n // nsel, d), F32), mesh=mesh, name="expert_mix",
        compiler_params=pltpu.CompilerParams(needs_layout_passes=False),
        scratch_types=[pltpu.VMEM((ich,), jnp.int32), pltpu.VMEM((ich,), jnp.int32), pltpu.VMEM((tok, d), F32)]
                      + [pltpu.VMEM((gw, c), jnp.int32)] * nbuf + [pltpu.SemaphoreType.DMA] * nbuf)
    def mix(tab_hbm, idx_hbm, coef_hbm, out_hbm, idx_v, coef_v, out_v, *bufs_sems):
        rows, sems = bufs_sems[:nbuf], bufs_sems[nbuf:]
        wid = lax.axis_index("s") * sc.num_cores + lax.axis_index("c")
        base = wid * per_worker
        zero_idx = jnp.zeros((lanes,), jnp.int32)

        def gather(win, buf, sem):
            return pltpu.make_async_copy(tab_hbm.at[idx_v.at[pl.ds(win * gw, gw)]], buf, sem)

        def accumulate_window(buf, t_loc, coef_off, first):
            for kb in range(c // (kblock * lanes)):
                col0 = kb * kblock * lanes
                if first:
                    init = tuple(jnp.zeros((lanes,), F32) for _ in range(2 * kblock))
                else:
                    init = tuple(out_v[t_loc, pl.ds(col0 + i * lanes, lanes)] for i in range(kblock)) + \
                           tuple(out_v[t_loc, pl.ds(c + col0 + i * lanes, lanes)] for i in range(kblock))

                def rbody(r2, accs):
                    accs = list(accs)
                    r = 2 * r2
                    cw0 = plsc.bitcast(plsc.load_gather(coef_v, [zero_idx + (coef_off + r)]), BF16)
                    cw1 = plsc.bitcast(plsc.load_gather(coef_v, [zero_idx + (coef_off + r + 1)]), BF16)
                    for i in range(kblock):
                        w0 = plsc.bitcast(buf[r, pl.ds(col0 + i * lanes, lanes)], BF16)
                        w1 = plsc.bitcast(buf[r + 1, pl.ds(col0 + i * lanes, lanes)], BF16)
                        p = plsc.bitcast(w0 * cw0 + w1 * cw1, jnp.int32)
                        accs[i] = accs[i] + lax.bitcast_convert_type(p << 16, F32)
                        accs[kblock + i] = accs[kblock + i] + lax.bitcast_convert_type(p, F32)
                    return tuple(accs)

                accs = lax.fori_loop(0, gw // 2, rbody, init)
                for i in range(kblock):
                    out_v[t_loc, pl.ds(col0 + i * lanes, lanes)] = accs[i]
                    out_v[t_loc, pl.ds(c + col0 + i * lanes, lanes)] = accs[kblock + i]

        @pl.loop(0, per_worker // ich)
        def _(g):
            cb = base + g * ich
            pltpu.sync_copy(idx_hbm.at[pl.ds(cb, ich)], idx_v)
            pltpu.sync_copy(coef_hbm.at[pl.ds(cb, ich)], coef_v)
            for q in range(nbuf):
                gather(q, rows[q], sems[q]).start()

            @pl.loop(0, tok)
            def _(j):
                for q in range(nbuf):
                    gather(nbuf * j + q, rows[q], sems[q]).wait()
                    accumulate_window(rows[q], j, j * nsel + q * gw, q == 0)

                    @pl.when(j + 1 < tok)
                    def _():
                        gather(nbuf * (j + 1) + q, rows[q], sems[q]).start()

            tok_base = pl.multiple_of(wid * (per_worker // nsel) + g * tok, tok)
            pltpu.sync_copy(out_v, out_hbm.at[pl.ds(tok_base, tok)])

    return mix(table, idx, coef)


def _peer_tail(xm, hn, et, wt, tabs, gfin, y_acc, row_off):
    u_tab, v_tab = tabs
    t, nsel = et.shape
    eidx = et.reshape(t * nsel)
    act = _expert_dots(u_tab, eidx, _pack_pairs(hn), nsel).reshape(t, nsel)
    coef = _expert_coefs(act, wt).reshape(t * nsel)
    mixed = _expert_mix(v_tab, eidx, coef, nsel)
    return _finish(xm, mixed, gfin, y_acc, row_off)


def kernel(x_prompt, x_sample, cache_k_window, cache_v_window, state_gla, meta_tokens, g_norm_mix, w_in,
           w_gate_up, b_gate, attn_sinks, g_gla_norm, w_branch_a, w_branch_b, w_out, g_norm_ffn, w_peer_q,
           peer_sub_keys, peer_u, peer_v, g_norm_final):
    bsz, seq, d = x_prompt.shape
    dbsz, tdec, _ = x_sample.shape
    n_meta = meta_tokens.shape[0]
    depth = w_in.shape[0]
    window = cache_k_window.shape[2]
    kv_heads, head_dim = cache_k_window.shape[3], cache_k_window.shape[4]
    gate_rank = w_gate_up.shape[1]
    bqk = w_gate_up.shape[2]
    n_ph, _, n_keys, p_half = peer_sub_keys.shape[1:]
    assert depth == 1 and d == 1024 and window == ATTN_BLOCK and kv_heads == 2 and head_dim == 64
    assert bqk == 256 and state_gla.shape[2:] == (4, 64, 128) and n_meta <= ATTN_BLOCK
    assert seq % ATTN_BLOCK == 0 and tdec <= SAMPLE_PAD and n_keys == 128 and p_half == 64 and n_ph == 8
    rope_dim = head_dim // 4
    meta_pad = ATTN_BLOCK - n_meta
    lp = ATTN_BLOCK + seq
    nblk = lp // ATTN_BLOCK

    w = w_in[0]
    c_lr = 2304
    c_gate = c_lr + gate_rank
    w1 = w[:, :c_lr].astype(BF16)
    wlr = jnp.pad(w[:, c_lr:c_gate], ((0, 0), (0, LANES - gate_rank))).astype(BF16)
    w2 = w[:, c_gate:].astype(BF16)
    wgu = jnp.pad(w_gate_up[0], ((0, LANES - gate_rank), (0, 0))).astype(BF16)
    bg = b_gate[0][None, :]
    gmix = g_norm_mix[0][None, :]
    wa = w_branch_a[0].astype(BF16)
    wb = w_branch_b[0].astype(BF16)
    wo = w_out[0].astype(BF16)
    gffn = g_norm_ffn[0][None, :]
    wq = w_peer_q[0].astype(BF16)
    keys = peer_sub_keys[0].reshape(n_ph * 2, n_keys, p_half).astype(BF16)
    u_tab = _pack_pairs(peer_u[0])
    gfin = g_norm_final[None, :]
    gn = g_gla_norm[0][None, :]
    sinks = attn_sinks[0]
    qk_scale = float(bqk // 4) ** -0.5

    rows_p = jnp.arange(lp)
    tab_p = _rope_table(rows_p - meta_pad, rows_p >= meta_pad, rope_dim, head_dim)
    proj_rows = max(r for r in range(16, PROJ_ROWS + 1, 16) if lp % r == 0)
    nq = nblk - 1
    nchunks = lp // GLA_CHUNK
    skip = ATTN_BLOCK // GLA_CHUNK
    ncq = nchunks - skip
    per_seq = seq // MERGE_ROWS
    gt_map_p = lambda i: ((i // per_seq) * nblk + 1 + (i % per_seq), 0)

    def mix_and_route(xg):
        gb = xg.shape[0]
        meta = jnp.broadcast_to(meta_tokens[None].astype(xg.dtype), (gb, n_meta, d))
        xpad = jnp.concatenate([jnp.zeros((gb, meta_pad, d), xg.dtype), meta, xg], axis=1).reshape(gb * lp, d)
        qa, kv, gl, gt = _project(xpad, gmix, tab_p, w1, wlr, wgu, bg, w2, proj_rows, qk_scale)
        ya = _attention(
            sinks, qa, kv, kv, gb, nq, ATTN_BLOCK,
            lambda b, n: (b * nblk + n + 1, 0), lambda b, n: (b * nblk + n, 0), lambda b, n: (b * nblk + n + 1, 0),
            lambda b, n: (b * nq + n, 0), gb * seq, first_valid_key=meta_pad, block_offset=1)
        s0 = jnp.zeros((gb,) + state_gla.shape[2:], F32)
        yb, s_fin = _gla(gl, s0, gn, gb, nchunks, GLA_CHUNK,
                         lambda b, c: (b * nchunks + c, 0),
                         lambda b, c: (b * ncq + jnp.maximum(c - skip, 0), 0), gb * seq)
        xm, hn, et, wt = _merge_route(xg.reshape(gb * seq, d), gt, ya, yb, wa, wb, wo, gffn, wq, keys,
                                      gt_map_p)
        kv_w = kv.reshape(gb, lp, 2, kv_heads, head_dim)[:, lp - window:]
        return (xm, hn, et, wt), kv_w, s_fin

    group = PROMPT_GROUP if (bsz % PROMPT_GROUP == 0 and (PROMPT_GROUP * seq) % GATHER_ROW_QUANTUM == 0) else bsz
    y_acc, tabs, kv_parts, s_parts = None, None, [], []
    for b0 in range(0, bsz, group):
        routed, kv_w, s_fin = mix_and_route(x_prompt[b0:b0 + group])
        if tabs is None:
            anchor = (routed[2][0, 0] * 0).astype(F32)
            tabs = (u_tab, _pack_pairs(peer_v[0] + anchor))
            y_acc = jnp.zeros((bsz * seq, d), F32) + anchor
        y_acc = _peer_tail(*routed, tabs, gfin, y_acc, b0 * seq)
        kv_parts.append(kv_w)
        s_parts.append(s_fin)
    y_prompt = y_acc.reshape(bsz, seq, d)
    kv_p = jnp.concatenate(kv_parts, axis=0)
    s_fin_p = jnp.concatenate(s_parts, axis=0)
    new_k_p = kv_p[:, :, 0][None]
    new_v_p = kv_p[:, :, 1][None]

    sp = SAMPLE_PAD
    xs_pad = jnp.pad(x_sample, ((0, 0), (0, sp - tdec), (0, 0))).reshape(dbsz * sp, d)
    rows_s = jnp.arange(sp)
    reps = 256 // sp
    tab_s = jnp.tile(_rope_table(PAST_LEN + rows_s, rows_s < tdec, rope_dim, head_dim), (reps, 1))
    qa_s, kv_s, gl_s, gt_s = _project(xs_pad, gmix, tab_s, w1, wlr, wgu, bg, w2, 256, qk_scale)

    cache_kv = jnp.concatenate([cache_k_window[0].reshape(dbsz * window, kv_heads * head_dim),
                                cache_v_window[0].reshape(dbsz * window, kv_heads * head_dim)], axis=1)
    seq_map = lambda b, n: (b, 0)
    ya_s = _attention(sinks, qa_s, cache_kv, kv_s, dbsz, 1, sp, seq_map, seq_map, seq_map, seq_map,
                      dbsz * sp, first_valid_key=None, block_offset=0)
    yb_s, s_fin_s = _gla(gl_s, state_gla[0], gn, dbsz, 1, sp, seq_map, seq_map, dbsz * sp)

    def real_rows(a):
        return a.reshape(dbsz, sp, a.shape[-1])[:, :tdec].reshape(dbsz * tdec, a.shape[-1])

    xs_rows = x_sample.reshape(dbsz * tdec, d)
    xm_s, hn_s, et_s, wt_s = _merge_route(xs_rows, real_rows(gt_s), real_rows(ya_s), real_rows(yb_s),
                                          wa, wb, wo, gffn, wq, keys, lambda i: (i, 0))
    y_sample = _peer_tail(xm_s, hn_s, et_s, wt_s, tabs, gfin, jnp.zeros((dbsz * tdec, d), F32), 0).reshape(dbsz, tdec, d)

    kv_new = real_rows(kv_s).reshape(dbsz, tdec, 2, kv_heads, head_dim)
    new_k_s = jnp.concatenate([cache_k_window[0].astype(F32), kv_new[:, :, 0]], axis=1)[:, -window:][None]
    new_v_s = jnp.concatenate([cache_v_window[0].astype(F32), kv_new[:, :, 1]], axis=1)[:, -window:][None]

    return (y_prompt, y_sample, new_k_p, new_v_p, s_fin_p[None], new_k_s, new_v_s, s_fin_s[None])
```

```python
import functools
import math

import jax
import jax.numpy as jnp
from jax import lax
from jax.experimental import pallas as pl
from jax.experimental.pallas import tpu as pltpu
from jax.experimental.pallas import tpu_sc as plsc

F32 = jnp.float32
BF16 = jnp.bfloat16

EPS = 1e-6
NEG_INF = -1e30
PAST_LEN = 16384
ROPE_THETA = 500000.0
GATE_NORMALIZER = 16.0
PEER_TOPK = 16

LANES = 128
SUBLANES = 8
VMEM_LIMIT_BYTES = 56 * 1024 * 1024

ATTN_BLOCK = 128
GLA_CHUNK = 64
SAMPLE_PAD = 16
PROJ_ROWS = 544
MERGE_ROWS = 128
GATHER_WINDOW = 32
GATHER_BUFFERS = 4
GATHER_INDEX_CHUNK = 2048
GATHER_ROW_QUANTUM = 512
PROMPT_GROUP = 1


def _cparams(sem):
    return pltpu.CompilerParams(dimension_semantics=sem, vmem_limit_bytes=VMEM_LIMIT_BYTES)


def _rms(x, g):
    ms = jnp.mean(x * x, axis=-1, keepdims=True)
    return (x * lax.rsqrt(ms + EPS)) * g


def _proj_kernel(x_ref, g_ref, tab_ref, w1_ref, wlr_ref, wgu_ref, bg_ref, w2_ref,
                 qa_ref, kv_ref, gl_ref, gt_ref, *, period, qk_scale):
    i = pl.program_id(0)
    tr = x_ref.shape[0]
    hb = _rms(x_ref[...], g_ref[...]).astype(BF16)
    z1 = jnp.dot(hb, w1_ref[...], preferred_element_type=F32)

    start = pl.multiple_of((i * tr) % period, SUBLANES)
    tab = tab_ref[pl.ds(start, tr), :]
    cosf = tab[:, 0:LANES]
    sin_lo = tab[:, LANES:2 * LANES]
    sin_hi = tab[:, 2 * LANES:3 * LANES]
    valid = tab[:, 3 * LANES:3 * LANES + 1]

    def rope(xg):
        return xg * cosf + pltpu.roll(xg, 8, 1) * sin_lo + pltpu.roll(xg, LANES - 8, 1) * sin_hi

    for gi in range(4):
        sl = slice(gi * LANES, (gi + 1) * LANES)
        qa_ref[:, sl] = rope(z1[:, sl]).astype(BF16)
    kv_ref[:, 0:LANES] = rope(z1[:, 512:640])
    kv_ref[:, LANES:2 * LANES] = z1[:, 640:768]

    lr = jnp.dot(hb, wlr_ref[...], preferred_element_type=F32)
    pre = jnp.dot(lr.astype(BF16), wgu_ref[...], preferred_element_type=F32) + bg_ref[...]
    log_sig = jnp.minimum(pre, 0.0) - jnp.log1p(jnp.exp(-jnp.abs(pre)))
    ld = jnp.where(valid > 0.5, log_sig / GATE_NORMALIZER, 0.0)

    gl_ref[:, 0:256] = z1[:, 768:1024] * qk_scale
    gl_ref[:, 256:512] = z1[:, 1024:1280]
    gl_ref[:, 512:768] = ld
    gl_ref[:, 768:1792] = z1[:, 1280:2304]
    gt_ref[...] = jnp.dot(hb, w2_ref[...], preferred_element_type=F32)


def _project(x, g, tab, w1, wlr, wgu, bg, w2, rows, qk_scale, first_row, r):
    d = x.shape[1]
    period = tab.shape[0]
    const = lambda i: (0, 0)
    row = lambda i: (i, 0)
    first = first_row // rows
    return pl.pallas_call(
        functools.partial(_proj_kernel, period=period, qk_scale=qk_scale),
        grid=(r // rows,),
        in_specs=[
            pl.BlockSpec((rows, d), lambda i: (i + first, 0)),
            pl.BlockSpec(g.shape, const),
            pl.BlockSpec(tab.shape, const),
            pl.BlockSpec(w1.shape, const),
            pl.BlockSpec(wlr.shape, const),
            pl.BlockSpec(wgu.shape, const),
            pl.BlockSpec(bg.shape, const),
            pl.BlockSpec(w2.shape, const),
        ],
        out_specs=[
            pl.BlockSpec((rows, 512), row),
            pl.BlockSpec((rows, 256), row),
            pl.BlockSpec((rows, 1792), row),
            pl.BlockSpec((rows, 2048), row),
        ],
        out_shape=[
            jax.ShapeDtypeStruct((r, 512), BF16),
            jax.ShapeDtypeStruct((r, 256), F32),
            jax.ShapeDtypeStruct((r, 1792), F32),
            jax.ShapeDtypeStruct((r, 2048), F32),
        ],
        compiler_params=_cparams(("arbitrary",)),
        name="proj",
    )(x, g, tab, w1, wlr, wgu, bg, w2)


def _attn_kernel(sink_ref, q_ref, prev_ref, cur_ref, o_ref, *, first_valid_key, block_offset):
    n = pl.program_id(1)
    qr = q_ref.shape[0]
    kr = cur_ref.shape[0]
    w = prev_ref.shape[0]
    nk = w + kr
    group = 4
    hd = 64

    rows = lax.broadcasted_iota(jnp.int32, (group * qr, nk), 0)
    cols = lax.broadcasted_iota(jnp.int32, (group * qr, nk), 1)
    head_of_row = rows // qr
    diff = (rows - head_of_row * qr) - cols + w
    mask = (diff >= 0) & (diff <= w)
    if first_valid_key is not None:
        blk = n + block_offset
        mask = mask & (cols >= first_valid_key + w - blk * w)

    prev = prev_ref[...]
    cur = cur_ref[...]
    q = q_ref[...]
    row_head = lax.broadcasted_iota(jnp.int32, (group * qr, 1), 0) // qr
    for kh in range(2):
        k = jnp.concatenate([prev[:, kh * hd:(kh + 1) * hd], cur[:, kh * hd:(kh + 1) * hd]], axis=0).astype(BF16)
        v = jnp.concatenate([prev[:, LANES + kh * hd:LANES + (kh + 1) * hd],
                             cur[:, LANES + kh * hd:LANES + (kh + 1) * hd]], axis=0).astype(BF16)
        qs = jnp.concatenate([q[:, (group * kh + g) * hd:(group * kh + g + 1) * hd] for g in range(group)], axis=0)
        s = lax.dot_general(qs, k, (((1,), (1,)), ((), ())), preferred_element_type=F32) * (hd ** -0.5)
        s = jnp.where(mask, s, NEG_INF)
        sink = jnp.zeros((group * qr, 1), F32)
        for g in range(group):
            sink = jnp.where(row_head == g, sink_ref[group * kh + g], sink)
        m = jnp.maximum(jnp.max(s, axis=-1, keepdims=True), sink)
        e = jnp.exp(s - m)
        p = e / (jnp.sum(e, axis=-1, keepdims=True) + jnp.exp(sink - m))
        o = jnp.dot(p.astype(BF16), v, preferred_element_type=F32)
        for g in range(group):
            h = group * kh + g
            o_ref[:, h * hd:(h + 1) * hd] = o[g * qr:(g + 1) * qr].astype(BF16)


def _attention(sinks, q, kv_prev, kv_cur, nb, nblk, qr, q_map, prev_map, cur_map, out_map, out_rows,
               first_valid_key, block_offset):
    w = ATTN_BLOCK
    return pl.pallas_call(
        functools.partial(_attn_kernel, first_valid_key=first_valid_key, block_offset=block_offset),
        grid=(nb, nblk),
        in_specs=[
            pl.BlockSpec(memory_space=pltpu.SMEM),
            pl.BlockSpec((qr, 512), q_map),
            pl.BlockSpec((w, 256), prev_map),
            pl.BlockSpec((qr, 256), cur_map),
        ],
        out_specs=pl.BlockSpec((qr, 512), out_map),
        out_shape=jax.ShapeDtypeStruct((out_rows, 512), BF16),
        compiler_params=_cparams(("arbitrary", "arbitrary")),
        name="swa",
    )(sinks, q, kv_prev, kv_cur)


def _gla_kernel(gl_ref, s0_ref, gn_ref, yb_ref, sfin_ref, st_ref):
    c = pl.program_id(1)
    ch = gl_ref.shape[0]
    nh, dk, dv = 4, 64, 128

    @pl.when(c == 0)
    def _():
        for h in range(nh):
            st_ref[h] = s0_ref[0, h].T

    gl = gl_ref[...]
    q = gl[:, 0:256]
    k = gl[:, 256:512]
    b = gl[:, 512:768]
    row = lax.broadcasted_iota(jnp.int32, (ch, nh * dk), 0)
    sh = 1
    while sh < ch:
        b = b + jnp.where(row >= sh, pltpu.roll(b, sh, 0), 0.0)
        sh *= 2
    b_last = b[ch - 1:ch, :]
    q_t = (q * jnp.exp(b)).astype(BF16)
    k_t = (k * jnp.exp(-b)).astype(BF16)
    k_end = (k * jnp.exp(b_last - b)).astype(BF16)
    decay = jnp.exp(b_last)
    causal = (lax.broadcasted_iota(jnp.int32, (ch, ch), 0) >= lax.broadcasted_iota(jnp.int32, (ch, ch), 1))
    gn = gn_ref[...]
    nt = (((1,), (1,)), ((), ()))
    for h in range(nh):
        ks = slice(h * dk, (h + 1) * dk)
        v = gl[:, 768 + h * dv:768 + (h + 1) * dv]
        vb = v.astype(BF16)
        a = lax.dot_general(q_t[:, ks], k_t[:, ks], nt, preferred_element_type=F32)
        a = jnp.where(causal, a, 0.0)
        s_t = st_ref[h]
        o = jnp.dot(a.astype(BF16), vb, preferred_element_type=F32)
        o = o + lax.dot_general(q_t[:, ks], s_t.astype(BF16), nt, preferred_element_type=F32)
        upd = jnp.dot(v.T.astype(BF16), k_end[:, ks], preferred_element_type=F32)
        st_ref[h] = s_t * decay[:, ks] + upd
        go = gl[:, 1280 + h * dv:1280 + (h + 1) * dv]
        y = _rms(o, gn) * (go * jax.nn.sigmoid(go))
        yb_ref[:, h * dv:(h + 1) * dv] = y.astype(BF16)

    @pl.when(c == pl.num_programs(1) - 1)
    def _():
        for h in range(nh):
            sfin_ref[0, h] = st_ref[h].T


def _gla(gl, s0, gn, nb, nchunks, ch, in_map, out_map, out_rows):
    return pl.pallas_call(
        _gla_kernel,
        grid=(nb, nchunks),
        in_specs=[
            pl.BlockSpec((ch, 1792), in_map),
            pl.BlockSpec((1, 4, 64, 128), lambda b, c: (b, 0, 0, 0)),
            pl.BlockSpec((1, 128), lambda b, c: (0, 0)),
        ],
        out_specs=[
            pl.BlockSpec((ch, 512), out_map),
            pl.BlockSpec((1, 4, 64, 128), lambda b, c: (b, 0, 0, 0)),
        ],
        out_shape=[
            jax.ShapeDtypeStruct((out_rows, 512), BF16),
            jax.ShapeDtypeStruct((nb, 4, 64, 128), F32),
        ],
        scratch_shapes=[pltpu.VMEM((4, 128, 64), F32)],
        compiler_params=_cparams(("arbitrary", "arbitrary")),
        name="gla",
    )(gl, s0, gn)


def _extract_topk(work, nsel, iota0, sentinel):
    slabs = work.shape[0] // SUBLANES
    vals, idxs = [], []
    for j in range(nsel):
        v = [work[i * SUBLANES:(i + 1) * SUBLANES] for i in range(slabs)]
        ix = [iota0[i * SUBLANES:(i + 1) * SUBLANES] for i in range(slabs)]
        while len(v) > 1:
            keep = [v[i] >= v[i + 1] for i in range(0, len(v), 2)]
            ix = [jnp.where(k, ix[2 * i], ix[2 * i + 1]) for i, k in enumerate(keep)]
            v = [jnp.where(k, v[2 * i], v[2 * i + 1]) for i, k in enumerate(keep)]
        m = jnp.max(v[0], axis=0, keepdims=True)
        idx = jnp.min(jnp.where(v[0] == m, ix[0], sentinel), axis=0, keepdims=True)
        vals.append(m)
        idxs.append(idx)
        if j + 1 < nsel:
            work = jnp.where(iota0 == idx, -jnp.inf, work)
    return vals, idxs


def _merge_kernel(x_ref, gt_ref, ya_ref, yb_ref, wa_ref, wb_ref, wo_ref, gf_ref, wq_ref, keys_ref,
                  xm_ref, hn_ref, et_ref, wt_ref):
    td = x_ref.shape[0]
    nkeys = keys_ref.shape[1]
    half = keys_ref.shape[2]
    nheads = keys_ref.shape[0] // 2
    topk = PEER_TOPK

    gt = gt_ref[...]
    d = x_ref.shape[1]
    ma = jnp.dot(ya_ref[...], wa_ref[...], preferred_element_type=F32)
    mb = jnp.dot(yb_ref[...], wb_ref[...], preferred_element_type=F32)
    m = jax.nn.sigmoid(gt[:, 0:d]) * ma + jax.nn.sigmoid(gt[:, d:2 * d]) * mb
    xm = x_ref[...] + jnp.dot(m.astype(BF16), wo_ref[...], preferred_element_type=F32)
    xm_ref[...] = xm
    hn = _rms(xm, gf_ref[...])
    hn_ref[...] = hn
    q = jnp.dot(hn.astype(BF16), wq_ref[...], preferred_element_type=F32).astype(BF16)

    nt = (((1,), (1,)), ((), ()))
    iota_k = lax.broadcasted_iota(jnp.int32, (nkeys, td), 0)
    pair_rows = [topk // (a + 1) for a in range(topk)]
    cand_pad = -sum(pair_rows) % SUBLANES
    ncand = sum(pair_rows) + cand_pad
    iota_c = lax.broadcasted_iota(jnp.int32, (ncand, td), 0)
    wts, ids = [], []
    for h in range(nheads):
        sv, si = [], []
        for c in range(2):
            gi = 2 * h + c
            s_t = lax.dot_general(keys_ref[gi], q[:, gi * half:(gi + 1) * half], nt,
                                  preferred_element_type=F32)
            vals, idxs = _extract_topk(s_t, topk, iota_k, nkeys)
            sv.append(vals)
            si.append(idxs)
        sv1 = jnp.concatenate(sv[1], axis=0)
        si1 = jnp.concatenate(si[1], axis=0)
        cand = jnp.concatenate([sv[0][a] + sv1[0:nb] for a, nb in enumerate(pair_rows)]
                               + [jnp.full((cand_pad, td), -jnp.inf, F32)], axis=0)
        cidx = jnp.concatenate([si[0][a] * nkeys + si1[0:nb] for a, nb in enumerate(pair_rows)]
                               + [jnp.full((cand_pad, td), -1, jnp.int32)], axis=0)
        fvals, eids = [], []
        work = cand
        for j in range(topk):
            mx = jnp.max(work, axis=0, keepdims=True)
            pos = jnp.min(jnp.where(work == mx, iota_c, ncand), axis=0, keepdims=True)
            hit = iota_c == pos
            eids.append(jnp.max(jnp.where(hit, cidx, -1), axis=0, keepdims=True))
            fvals.append(mx)
            if j + 1 < topk:
                work = jnp.where(hit, -jnp.inf, work)
        fv = jnp.concatenate(fvals, axis=0)
        e = jnp.exp(fv - fvals[0])
        wts.append(e / jnp.sum(e, axis=0, keepdims=True))
        ids.extend(eids)
    wt_ref[...] = jnp.concatenate(wts, axis=0).T
    et_ref[...] = jnp.concatenate(ids, axis=0).T


def _merge_route(x, gt, ya, yb, wa, wb, wo, gf, wq, keys, gt_map, x_first_row):
    t = ya.shape[0]
    d = x.shape[1]
    td = MERGE_ROWS
    nsel = (keys.shape[0] // 2) * PEER_TOPK
    const2 = lambda i: (0, 0)
    row = lambda i: (i, 0)
    x_first = x_first_row // td
    return pl.pallas_call(
        _merge_kernel,
        grid=(t // td,),
        in_specs=[
            pl.BlockSpec((td, d), lambda i: (i + x_first, 0)),
            pl.BlockSpec((td, 2 * d), gt_map),
            pl.BlockSpec((td, ya.shape[1]), row),
            pl.BlockSpec((td, yb.shape[1]), row),
            pl.BlockSpec(wa.shape, const2),
            pl.BlockSpec(wb.shape, const2),
            pl.BlockSpec(wo.shape, const2),
            pl.BlockSpec(gf.shape, const2),
            pl.BlockSpec(wq.shape, const2),
            pl.BlockSpec(keys.shape, lambda i: (0, 0, 0)),
        ],
        out_specs=[
            pl.BlockSpec((td, d), row),
            pl.BlockSpec((td, d), row),
            pl.BlockSpec((td, nsel), row),
            pl.BlockSpec((td, nsel), row),
        ],
        out_shape=[
            jax.ShapeDtypeStruct((t, d), F32),
            jax.ShapeDtypeStruct((t, d), F32),
            jax.ShapeDtypeStruct((t, nsel), jnp.int32),
            jax.ShapeDtypeStruct((t, nsel), F32),
        ],
        compiler_params=_cparams(("arbitrary",)),
        name="merge_route",
    )(x, gt, ya, yb, wa, wb, wo, gf, wq, keys)


def _coef_kernel(act_ref, wt_ref, o_ref):
    act = act_ref[...]
    gelu = 0.5 * act * (1.0 + lax.erf(act * (2.0 ** -0.5)))
    bits = pltpu.bitcast((wt_ref[...] * gelu).astype(BF16).astype(F32), jnp.int32)
    o_ref[...] = bits | lax.shift_right_logical(bits, 16)


def _expert_coefs(act, wt):
    t, nsel = wt.shape
    rows = math.gcd(t, 512)
    row = lambda i: (i, 0)
    return pl.pallas_call(
        _coef_kernel,
        grid=(t // rows,),
        in_specs=[pl.BlockSpec((rows, nsel), row), pl.BlockSpec((rows, nsel), row)],
        out_specs=pl.BlockSpec((rows, nsel), row),
        out_shape=jax.ShapeDtypeStruct((t, nsel), jnp.int32),
        compiler_params=_cparams(("arbitrary",)),
        name="expert_coefs",
    )(act, wt)


def _finish_kernel(xm_ref, o_ref, gfin_ref, yacc_hbm, y_ref):
    del yacc_hbm
    y_ref[...] = _rms(xm_ref[...] + o_ref[...], gfin_ref[...])


def _finish(xm, o, gfin, y_acc, row_off):
    t, d = xm.shape
    rows = math.gcd(t, 512)
    first = row_off // rows
    row = lambda i: (i, 0)
    return pl.pallas_call(
        _finish_kernel,
        grid=(t // rows,),
        in_specs=[pl.BlockSpec((rows, d), row), pl.BlockSpec((rows, d), row),
                  pl.BlockSpec(gfin.shape, lambda i: (0, 0)), pl.BlockSpec(memory_space=pl.ANY)],
        out_specs=pl.BlockSpec((rows, d), lambda i: (i + first, 0)),
        out_shape=jax.ShapeDtypeStruct(y_acc.shape, F32),
        input_output_aliases={3: 0},
        compiler_params=_cparams(("arbitrary",)),
        name="finish",
    )(xm, o, gfin, y_acc)


def _rope_table(pos, valid, rope_dim, head_dim):
    half = rope_dim // 2
    inv = ROPE_THETA ** (-jnp.arange(0, rope_dim, 2, dtype=F32) / rope_dim)
    ang = pos.astype(F32)[:, None] * inv[None, :]
    cos, sin = jnp.cos(ang), jnp.sin(ang)
    n = pos.shape[0]
    ones = jnp.ones((n, head_dim - rope_dim), F32)
    zeros_h = jnp.zeros((n, half), F32)
    zeros_r = jnp.zeros((n, head_dim - rope_dim), F32)
    reps = LANES // head_dim
    cosf = jnp.tile(jnp.concatenate([cos, cos, ones], axis=1), (1, reps))
    sin_lo = jnp.tile(jnp.concatenate([zeros_h, sin, zeros_r], axis=1), (1, reps))
    sin_hi = jnp.tile(jnp.concatenate([-sin, zeros_h, zeros_r], axis=1), (1, reps))
    vcol = jnp.broadcast_to(valid.astype(F32)[:, None], (n, LANES))
    return jnp.concatenate([cosf, sin_lo, sin_hi, vcol], axis=1)


def _pack_pairs(x):
    half = x.shape[1] // 2
    lo = lax.bitcast_convert_type(x[:, :half].astype(BF16), jnp.uint16).astype(jnp.uint32)
    hi = lax.bitcast_convert_type(x[:, half:].astype(BF16), jnp.uint16).astype(jnp.uint32)
    return lax.bitcast_convert_type(lo | (hi << 16), jnp.int32)


def _expert_dots(table, idx, hn, nsel):
    n = idx.shape[0]
    c = table.shape[1]
    sc = plsc.get_sparse_core_info()
    lanes = sc.num_lanes
    workers = sc.num_cores * sc.num_subcores
    gw = GATHER_WINDOW
    nbuf = GATHER_BUFFERS
    ich = GATHER_INDEX_CHUNK
    tok = ich // nsel
    per_worker = n // workers
    assert n % workers == 0 and per_worker % ich == 0 and nsel == nbuf * gw and hn.shape[1] == c and gw % lanes == 0
    mesh = plsc.VectorSubcoreMesh(core_axis_name="c", subcore_axis_name="s")

    @functools.partial(
        pl.kernel, out_type=jax.ShapeDtypeStruct((n,), F32), mesh=mesh, name="expert_dots",
        compiler_params=pltpu.CompilerParams(needs_layout_passes=False),
        scratch_types=[pltpu.VMEM((ich,), jnp.int32), pltpu.VMEM((tok, c), jnp.int32), pltpu.VMEM((ich,), F32),
                       pltpu.VMEM((lanes * lanes,), F32)]
                      + [pltpu.VMEM((gw, c), jnp.int32)] * nbuf + [pltpu.SemaphoreType.DMA] * nbuf)
    def dots(tab_hbm, idx_hbm, hn_hbm, act_hbm, idx_v, h_v, act_v, scr, *bufs_sems):
        rows, sems = bufs_sems[:nbuf], bufs_sems[nbuf:]
        wid = lax.axis_index("s") * sc.num_cores + lax.axis_index("c")
        base = wid * per_worker
        lane = lax.iota(jnp.int32, lanes)

        def gather(win, buf, sem):
            return pltpu.make_async_copy(tab_hbm.at[idx_v.at[pl.ds(win * gw, gw)]], buf, sem)

        def reduce_window(buf, t_loc, out_off):
            for rb in range(gw // lanes):
                def kbody(k2, accs):
                    k0 = 2 * k2 * lanes
                    hw0 = plsc.bitcast(h_v[t_loc, pl.ds(k0, lanes)], BF16)
                    hw1 = plsc.bitcast(h_v[t_loc, pl.ds(k0 + lanes, lanes)], BF16)
                    out = []
                    for r in range(lanes):
                        w0 = plsc.bitcast(buf[rb * lanes + r, pl.ds(k0, lanes)], BF16)
                        w1 = plsc.bitcast(buf[rb * lanes + r, pl.ds(k0 + lanes, lanes)], BF16)
                        p = plsc.bitcast(w0 * hw0 + w1 * hw1, jnp.int32)
                        out.append(accs[r] + lax.bitcast_convert_type(p << 16, F32)
                                   + lax.bitcast_convert_type(p, F32))
                    return tuple(out)

                accs = lax.fori_loop(0, c // (2 * lanes), kbody,
                                     tuple(jnp.zeros((lanes,), F32) for _ in range(lanes)))
                for r in range(lanes):
                    scr[pl.ds(r * lanes, lanes)] = accs[r]
                cols = [plsc.load_gather(scr, [lane * lanes + l]) for l in range(lanes)]
                while len(cols) > 1:
                    cols = [cols[i] + cols[i + 1] for i in range(0, len(cols), 2)]
                act_v[pl.ds(out_off + rb * lanes, lanes)] = cols[0]

        @pl.loop(0, per_worker // ich)
        def _(g):
            cb = base + g * ich
            pltpu.sync_copy(idx_hbm.at[pl.ds(cb, ich)], idx_v)
            tok_base = pl.multiple_of(wid * (per_worker // nsel) + g * tok, tok)
            pltpu.sync_copy(hn_hbm.at[pl.ds(tok_base, tok)], h_v)
            for q in range(nbuf):
                gather(q, rows[q], sems[q]).start()

            @pl.loop(0, tok)
            def _(j):
                for q in range(nbuf):
                    gather(nbuf * j + q, rows[q], sems[q]).wait()
                    reduce_window(rows[q], j, j * nsel + q * gw)

                    @pl.when(j + 1 < tok)
                    def _():
                        gather(nbuf * (j + 1) + q, rows[q], sems[q]).start()

            pltpu.sync_copy(act_v, act_hbm.at[pl.ds(cb, ich)])

    return dots(table, idx, hn)


def _expert_mix(table, idx, coef, nsel):
    n = idx.shape[0]
    c = table.shape[1]
    d = 2 * c
    sc = plsc.get_sparse_core_info()
    lanes = sc.num_lanes
    workers = sc.num_cores * sc.num_subcores
    gw = GATHER_WINDOW
    nbuf = GATHER_BUFFERS
    ich = GATHER_INDEX_CHUNK
    tok = ich // nsel
    per_worker = n // workers
    kblock = 16
    assert n % workers == 0 and per_worker % ich == 0 and nsel == nbuf * gw and c % (kblock * lanes) == 0
    mesh = plsc.VectorSubcoreMesh(core_axis_name="c", subcore_axis_name="s")

    @functools.partial(
        pl.kernel, out_type=jax.ShapeDtypeStruct((n // nsel, d), F32), mesh=mesh, name="expert_mix",
        compiler_params=pltpu.CompilerParams(needs_layout_passes=False),
        scratch_types=[pltpu.VMEM((ich,), jnp.int32), pltpu.VMEM((ich,), jnp.int32), pltpu.VMEM((tok, d), F32)]
                      + [pltpu.VMEM((gw, c), jnp.int32)] * nbuf + [pltpu.SemaphoreType.DMA] * nbuf)
    def mix(tab_hbm, idx_hbm, coef_hbm, out_hbm, idx_v, coef_v, out_v, *bufs_sems):
        rows, sems = bufs_sems[:nbuf], bufs_sems[nbuf:]
        wid = lax.axis_index("s") * sc.num_cores + lax.axis_index("c")
        base = wid * per_worker
        zero_idx = jnp.zeros((lanes,), jnp.int32)

        def gather(win, buf, sem):
            return pltpu.make_async_copy(tab_hbm.at[idx_v.at[pl.ds(win * gw, gw)]], buf, sem)

        def accumulate_window(buf, t_loc, coef_off, first):
            for kb in range(c // (kblock * lanes)):
                col0 = kb * kblock * lanes
                if first:
                    init = tuple(jnp.zeros((lanes,), F32) for _ in range(2 * kblock))
                else:
                    init = tuple(out_v[t_loc, pl.ds(col0 + i * lanes, lanes)] for i in range(kblock)) + \
                           tuple(out_v[t_loc, pl.ds(c + col0 + i * lanes, lanes)] for i in range(kblock))

                def rbody(r2, accs):
                    accs = list(accs)
                    r = 2 * r2
                    cw0 = plsc.bitcast(plsc.load_gather(coef_v, [zero_idx + (coef_off + r)]), BF16)
                    cw1 = plsc.bitcast(plsc.load_gather(coef_v, [zero_idx + (coef_off + r + 1)]), BF16)
                    for i in range(kblock):
                        w0 = plsc.bitcast(buf[r, pl.ds(col0 + i * lanes, lanes)], BF16)
                        w1 = plsc.bitcast(buf[r + 1, pl.ds(col0 + i * lanes, lanes)], BF16)
                        p = plsc.bitcast(w0 * cw0 + w1 * cw1, jnp.int32)
                        accs[i] = accs[i] + lax.bitcast_convert_type(p << 16, F32)
                        accs[kblock + i] = accs[kblock + i] + lax.bitcast_convert_type(p, F32)
                    return tuple(accs)

                accs = lax.fori_loop(0, gw // 2, rbody, init)
                for i in range(kblock):
                    out_v[t_loc, pl.ds(col0 + i * lanes, lanes)] = accs[i]
                    out_v[t_loc, pl.ds(c + col0 + i * lanes, lanes)] = accs[kblock + i]

        @pl.loop(0, per_worker // ich)
        def _(g):
            cb = base + g * ich
            pltpu.sync_copy(idx_hbm.at[pl.ds(cb, ich)], idx_v)
            pltpu.sync_copy(coef_hbm.at[pl.ds(cb, ich)], coef_v)
            for q in range(nbuf):
                gather(q, rows[q], sems[q]).start()

            @pl.loop(0, tok)
            def _(j):
                for q in range(nbuf):
                    gather(nbuf * j + q, rows[q], sems[q]).wait()
                    accumulate_window(rows[q], j, j * nsel + q * gw, q == 0)

                    @pl.when(j + 1 < tok)
                    def _():
                        gather(nbuf * (j + 1) + q, rows[q], sems[q]).start()

            tok_base = pl.multiple_of(wid * (per_worker // nsel) + g * tok, tok)
            pltpu.sync_copy(out_v, out_hbm.at[pl.ds(tok_base, tok)])

    return mix(table, idx, coef)


def _peer_tail(xm, hn, et, wt, tabs, gfin, y_acc, row_off):
    u_tab, v_tab = tabs
    t, nsel = et.shape
    eidx = et.reshape(t * nsel)
    act = _expert_dots(u_tab, eidx, _pack_pairs(hn), nsel).reshape(t, nsel)
    coef = _expert_coefs(act, wt).reshape(t * nsel)
    mixed = _expert_mix(v_tab, eidx, coef, nsel)
    return _finish(xm, mixed, gfin, y_acc, row_off)


def kernel(x_prompt, x_sample, cache_k_window, cache_v_window, state_gla, meta_tokens, g_norm_mix, w_in,
           w_gate_up, b_gate, attn_sinks, g_gla_norm, w_branch_a, w_branch_b, w_out, g_norm_ffn, w_peer_q,
           peer_sub_keys, peer_u, peer_v, g_norm_final):
    bsz, seq, d = x_prompt.shape
    dbsz, tdec, _ = x_sample.shape
    n_meta = meta_tokens.shape[0]
    depth = w_in.shape[0]
    window = cache_k_window.shape[2]
    kv_heads, head_dim = cache_k_window.shape[3], cache_k_window.shape[4]
    gate_rank = w_gate_up.shape[1]
    bqk = w_gate_up.shape[2]
    n_ph, _, n_keys, p_half = peer_sub_keys.shape[1:]
    assert depth == 1 and d == 1024 and window == ATTN_BLOCK and kv_heads == 2 and head_dim == 64
    assert bqk == 256 and state_gla.shape[2:] == (4, 64, 128) and n_meta <= ATTN_BLOCK
    assert seq % ATTN_BLOCK == 0 and tdec <= SAMPLE_PAD and n_keys == 128 and p_half == 64 and n_ph == 8
    rope_dim = head_dim // 4
    meta_pad = ATTN_BLOCK - n_meta
    lp = ATTN_BLOCK + seq
    nblk = lp // ATTN_BLOCK

    w = w_in[0]
    c_lr = 2304
    c_gate = c_lr + gate_rank
    w1 = w[:, :c_lr].astype(BF16)
    wlr = jnp.pad(w[:, c_lr:c_gate], ((0, 0), (0, LANES - gate_rank))).astype(BF16)
    w2 = w[:, c_gate:].astype(BF16)
    wgu = jnp.pad(w_gate_up[0], ((0, LANES - gate_rank), (0, 0))).astype(BF16)
    bg = b_gate[0][None, :]
    gmix = g_norm_mix[0][None, :]
    wa = w_branch_a[0].astype(BF16)
    wb = w_branch_b[0].astype(BF16)
    wo = w_out[0].astype(BF16)
    gffn = g_norm_ffn[0][None, :]
    wq = w_peer_q[0].astype(BF16)
    keys = peer_sub_keys[0].reshape(n_ph * 2, n_keys, p_half).astype(BF16)
    u_tab = _pack_pairs(peer_u[0])
    gfin = g_norm_final[None, :]
    gn = g_gla_norm[0][None, :]
    sinks = attn_sinks[0]
    qk_scale = float(bqk // 4) ** -0.5

    rows_p = jnp.arange(lp)
    tab_p = _rope_table(rows_p - meta_pad, rows_p >= meta_pad, rope_dim, head_dim)
    proj_rows = max(r for r in range(16, PROJ_ROWS + 1, 16) if lp % r == 0)
    nq = nblk - 1
    nchunks = lp // GLA_CHUNK
    skip = ATTN_BLOCK // GLA_CHUNK
    ncq = nchunks - skip
    per_seq = seq // MERGE_ROWS
    gt_map_p = lambda i: ((i // per_seq) * nblk + 1 + (i % per_seq), 0)

    meta = jnp.broadcast_to(meta_tokens[None].astype(x_prompt.dtype), (bsz, n_meta, d))
    xpad = jnp.concatenate([jnp.zeros((bsz, meta_pad, d), x_prompt.dtype), meta, x_prompt], axis=1)
    xpad = xpad.reshape(bsz * lp, d)
    x_rows = x_prompt.reshape(bsz * seq, d)

    def mix_and_route(b0, gb):
        qa, kv, gl, gt = _project(xpad, gmix, tab_p, w1, wlr, wgu, bg, w2, proj_rows, qk_scale, b0 * lp, gb * lp)
        ya = _attention(
            sinks, qa, kv, kv, gb, nq, ATTN_BLOCK,
            lambda b, n: (b * nblk + n + 1, 0), lambda b, n: (b * nblk + n, 0), lambda b, n: (b * nblk + n + 1, 0),
            lambda b, n: (b * nq + n, 0), gb * seq, first_valid_key=meta_pad, block_offset=1)
        s0 = jnp.zeros((gb,) + state_gla.shape[2:], F32)
        yb, s_fin = _gla(gl, s0, gn, gb, nchunks, GLA_CHUNK,
                         lambda b, c: (b * nchunks + c, 0),
                         lambda b, c: (b * ncq + jnp.maximum(c - skip, 0), 0), gb * seq)
        xm, hn, et, wt = _merge_route(x_rows, gt, ya, yb, wa, wb, wo, gffn, wq, keys, gt_map_p, b0 * seq)
        kv_w = kv.reshape(gb, lp, 2, kv_heads, head_dim)[:, lp - window:]
        return (xm, hn, et, wt), kv_w, s_fin

    group = PROMPT_GROUP if (bsz % PROMPT_GROUP == 0 and (PROMPT_GROUP * seq) % GATHER_ROW_QUANTUM == 0) else bsz
    y_acc, tabs, kv_parts, s_parts = None, None, [], []
    for b0 in range(0, bsz, group):
        routed, kv_w, s_fin = mix_and_route(b0, group)
        if tabs is None:
            anchor = (routed[2][0, 0] * 0).astype(F32)
            tabs = (u_tab, _pack_pairs(peer_v[0] + anchor))
            y_acc = jnp.zeros((bsz * seq, d), F32) + anchor
        y_acc = _peer_tail(*routed, tabs, gfin, y_acc, b0 * seq)
        kv_parts.append(kv_w)
        s_parts.append(s_fin)
    y_prompt = y_acc.reshape(bsz, seq, d)
    kv_p = jnp.concatenate(kv_parts, axis=0)
    s_fin_p = jnp.concatenate(s_parts, axis=0)
    new_k_p = kv_p[:, :, 0][None]
    new_v_p = kv_p[:, :, 1][None]

    sp = SAMPLE_PAD
    xs_pad = jnp.pad(x_sample, ((0, 0), (0, sp - tdec), (0, 0))).reshape(dbsz * sp, d)
    rows_s = jnp.arange(sp)
    reps = 256 // sp
    tab_s = jnp.tile(_rope_table(PAST_LEN + rows_s, rows_s < tdec, rope_dim, head_dim), (reps, 1))
    qa_s, kv_s, gl_s, gt_s = _project(xs_pad, gmix, tab_s, w1, wlr, wgu, bg, w2, 256, qk_scale, 0, dbsz * sp)

    cache_kv = jnp.concatenate([cache_k_window[0].reshape(dbsz * window, kv_heads * head_dim),
                                cache_v_window[0].reshape(dbsz * window, kv_heads * head_dim)], axis=1)
    seq_map = lambda b, n: (b, 0)
    ya_s = _attention(sinks, qa_s, cache_kv, kv_s, dbsz, 1, sp, seq_map, seq_map, seq_map, seq_map,
                      dbsz * sp, first_valid_key=None, block_offset=0)
    yb_s, s_fin_s = _gla(gl_s, state_gla[0], gn, dbsz, 1, sp, seq_map, seq_map, dbsz * sp)

    def real_rows(a):
        return a.reshape(dbsz, sp, a.shape[-1])[:, :tdec].reshape(dbsz * tdec, a.shape[-1])

    xs_rows = x_sample.reshape(dbsz * tdec, d)
    xm_s, hn_s, et_s, wt_s = _merge_route(xs_rows, real_rows(gt_s), real_rows(ya_s), real_rows(yb_s),
                                          wa, wb, wo, gffn, wq, keys, lambda i: (i, 0), 0)
    y_sample = _peer_tail(xm_s, hn_s, et_s, wt_s, tabs, gfin, jnp.zeros((dbsz * tdec, d), F32), 0).reshape(dbsz, tdec, d)

    kv_new = real_rows(kv_s).reshape(dbsz, tdec, 2, kv_heads, head_dim)
    new_k_s = jnp.concatenate([cache_k_window[0].astype(F32), kv_new[:, :, 0]], axis=1)[:, -window:][None]
    new_v_s = jnp.concatenate([cache_v_window[0].astype(F32), kv_new[:, :, 1]], axis=1)[:, -window:][None]

    return (y_prompt, y_sample, new_k_p, new_v_p, s_fin_p[None], new_k_s, new_v_s, s_fin_s[None])
```

```python
import functools
import math

import jax
import jax.numpy as jnp
from jax import lax
from jax.experimental import pallas as pl
from jax.experimental.pallas import tpu as pltpu
from jax.experimental.pallas import tpu_sc as plsc

F32 = jnp.float32
BF16 = jnp.bfloat16

EPS = 1e-6
NEG_INF = -1e30
PAST_LEN = 16384
ROPE_THETA = 500000.0
GATE_NORMALIZER = 16.0
PEER_TOPK = 16

LANES = 128
SUBLANES = 8
VMEM_LIMIT_BYTES = 56 * 1024 * 1024

ATTN_BLOCK = 128
GLA_CHUNK = 64
SAMPLE_PAD = 16
PROJ_ROWS = 544
MERGE_ROWS = 128
GATHER_WINDOW = 32
GATHER_BUFFERS = 4
DOTS_INDEX_CHUNK = 8192
MIX_INDEX_CHUNK = 4096
GATHER_ROW_QUANTUM = 512
PROMPT_GROUP = 1


def _cparams(sem):
    return pltpu.CompilerParams(dimension_semantics=sem, vmem_limit_bytes=VMEM_LIMIT_BYTES)


def _rms(x, g):
    ms = jnp.mean(x * x, axis=-1, keepdims=True)
    return (x * lax.rsqrt(ms + EPS)) * g


def _proj_kernel(x_ref, g_ref, tab_ref, w1_ref, wlr_ref, wgu_ref, bg_ref, w2_ref,
                 qa_ref, kv_ref, gl_ref, gt_ref, *, period, qk_scale):
    i = pl.program_id(0)
    tr = x_ref.shape[0]
    hb = _rms(x_ref[...], g_ref[...]).astype(BF16)
    z1 = jnp.dot(hb, w1_ref[...], preferred_element_type=F32)

    start = pl.multiple_of((i * tr) % period, SUBLANES)
    tab = tab_ref[pl.ds(start, tr), :]
    cosf = tab[:, 0:LANES]
    sin_lo = tab[:, LANES:2 * LANES]
    sin_hi = tab[:, 2 * LANES:3 * LANES]
    valid = tab[:, 3 * LANES:3 * LANES + 1]

    def rope(xg):
        return xg * cosf + pltpu.roll(xg, 8, 1) * sin_lo + pltpu.roll(xg, LANES - 8, 1) * sin_hi

    for gi in range(4):
        sl = slice(gi * LANES, (gi + 1) * LANES)
        qa_ref[:, sl] = rope(z1[:, sl]).astype(BF16)
    kv_ref[:, 0:LANES] = rope(z1[:, 512:640])
    kv_ref[:, LANES:2 * LANES] = z1[:, 640:768]

    lr = jnp.dot(hb, wlr_ref[...], preferred_element_type=F32)
    pre = jnp.dot(lr.astype(BF16), wgu_ref[...], preferred_element_type=F32) + bg_ref[...]
    log_sig = jnp.minimum(pre, 0.0) - jnp.log1p(jnp.exp(-jnp.abs(pre)))
    ld = jnp.where(valid > 0.5, log_sig / GATE_NORMALIZER, 0.0)

    gl_ref[:, 0:256] = z1[:, 768:1024] * qk_scale
    gl_ref[:, 256:512] = z1[:, 1024:1280]
    gl_ref[:, 512:768] = ld
    gl_ref[:, 768:1792] = z1[:, 1280:2304]
    gt_ref[...] = jnp.dot(hb, w2_ref[...], preferred_element_type=F32)


def _project(x, g, tab, w1, wlr, wgu, bg, w2, rows, qk_scale, first_row, r):
    d = x.shape[1]
    period = tab.shape[0]
    const = lambda i: (0, 0)
    row = lambda i: (i, 0)
    first = first_row // rows
    return pl.pallas_call(
        functools.partial(_proj_kernel, period=period, qk_scale=qk_scale),
        grid=(r // rows,),
        in_specs=[
            pl.BlockSpec((rows, d), lambda i: (i + first, 0)),
            pl.BlockSpec(g.shape, const),
            pl.BlockSpec(tab.shape, const),
            pl.BlockSpec(w1.shape, const),
            pl.BlockSpec(wlr.shape, const),
            pl.BlockSpec(wgu.shape, const),
            pl.BlockSpec(bg.shape, const),
            pl.BlockSpec(w2.shape, const),
        ],
        out_specs=[
            pl.BlockSpec((rows, 512), row),
            pl.BlockSpec((rows, 256), row),
            pl.BlockSpec((rows, 1792), row),
            pl.BlockSpec((rows, 2048), row),
        ],
        out_shape=[
            jax.ShapeDtypeStruct((r, 512), BF16),
            jax.ShapeDtypeStruct((r, 256), F32),
            jax.ShapeDtypeStruct((r, 1792), F32),
            jax.ShapeDtypeStruct((r, 2048), F32),
        ],
        compiler_params=_cparams(("arbitrary",)),
        name="proj",
    )(x, g, tab, w1, wlr, wgu, bg, w2)


def _attn_kernel(sink_ref, q_ref, prev_ref, cur_ref, o_ref, *, first_valid_key, block_offset):
    n = pl.program_id(1)
    qr = q_ref.shape[0]
    kr = cur_ref.shape[0]
    w = prev_ref.shape[0]
    nk = w + kr
    group = 4
    hd = 64

    rows = lax.broadcasted_iota(jnp.int32, (group * qr, nk), 0)
    cols = lax.broadcasted_iota(jnp.int32, (group * qr, nk), 1)
    head_of_row = rows // qr
    diff = (rows - head_of_row * qr) - cols + w
    mask = (diff >= 0) & (diff <= w)
    if first_valid_key is not None:
        blk = n + block_offset
        mask = mask & (cols >= first_valid_key + w - blk * w)

    prev = prev_ref[...]
    cur = cur_ref[...]
    q = q_ref[...]
    row_head = lax.broadcasted_iota(jnp.int32, (group * qr, 1), 0) // qr
    for kh in range(2):
        k = jnp.concatenate([prev[:, kh * hd:(kh + 1) * hd], cur[:, kh * hd:(kh + 1) * hd]], axis=0).astype(BF16)
        v = jnp.concatenate([prev[:, LANES + kh * hd:LANES + (kh + 1) * hd],
                             cur[:, LANES + kh * hd:LANES + (kh + 1) * hd]], axis=0).astype(BF16)
        qs = jnp.concatenate([q[:, (group * kh + g) * hd:(group * kh + g + 1) * hd] for g in range(group)], axis=0)
        s = lax.dot_general(qs, k, (((1,), (1,)), ((), ())), preferred_element_type=F32) * (hd ** -0.5)
        s = jnp.where(mask, s, NEG_INF)
        sink = jnp.zeros((group * qr, 1), F32)
        for g in range(group):
            sink = jnp.where(row_head == g, sink_ref[group * kh + g], sink)
        m = jnp.maximum(jnp.max(s, axis=-1, keepdims=True), sink)
        e = jnp.exp(s - m)
        p = e / (jnp.sum(e, axis=-1, keepdims=True) + jnp.exp(sink - m))
        o = jnp.dot(p.astype(BF16), v, preferred_element_type=F32)
        for g in range(group):
            h = group * kh + g
            o_ref[:, h * hd:(h + 1) * hd] = o[g * qr:(g + 1) * qr].astype(BF16)


def _attention(sinks, q, kv_prev, kv_cur, nb, nblk, qr, q_map, prev_map, cur_map, out_map, out_rows,
               first_valid_key, block_offset):
    w = ATTN_BLOCK
    return pl.pallas_call(
        functools.partial(_attn_kernel, first_valid_key=first_valid_key, block_offset=block_offset),
        grid=(nb, nblk),
        in_specs=[
            pl.BlockSpec(memory_space=pltpu.SMEM),
            pl.BlockSpec((qr, 512), q_map),
            pl.BlockSpec((w, 256), prev_map),
            pl.BlockSpec((qr, 256), cur_map),
        ],
        out_specs=pl.BlockSpec((qr, 512), out_map),
        out_shape=jax.ShapeDtypeStruct((out_rows, 512), BF16),
        compiler_params=_cparams(("arbitrary", "arbitrary")),
        name="swa",
    )(sinks, q, kv_prev, kv_cur)


def _gla_kernel(gl_ref, s0_ref, gn_ref, yb_ref, sfin_ref, st_ref):
    c = pl.program_id(1)
    ch = gl_ref.shape[0]
    nh, dk, dv = 4, 64, 128

    @pl.when(c == 0)
    def _():
        for h in range(nh):
            st_ref[h] = s0_ref[0, h].T

    gl = gl_ref[...]
    q = gl[:, 0:256]
    k = gl[:, 256:512]
    b = gl[:, 512:768]
    row = lax.broadcasted_iota(jnp.int32, (ch, nh * dk), 0)
    sh = 1
    while sh < ch:
        b = b + jnp.where(row >= sh, pltpu.roll(b, sh, 0), 0.0)
        sh *= 2
    b_last = b[ch - 1:ch, :]
    q_t = (q * jnp.exp(b)).astype(BF16)
    k_t = (k * jnp.exp(-b)).astype(BF16)
    k_end = (k * jnp.exp(b_last - b)).astype(BF16)
    decay = jnp.exp(b_last)
    causal = (lax.broadcasted_iota(jnp.int32, (ch, ch), 0) >= lax.broadcasted_iota(jnp.int32, (ch, ch), 1))
    gn = gn_ref[...]
    nt = (((1,), (1,)), ((), ()))
    for h in range(nh):
        ks = slice(h * dk, (h + 1) * dk)
        v = gl[:, 768 + h * dv:768 + (h + 1) * dv]
        vb = v.astype(BF16)
        a = lax.dot_general(q_t[:, ks], k_t[:, ks], nt, preferred_element_type=F32)
        a = jnp.where(causal, a, 0.0)
        s_t = st_ref[h]
        o = jnp.dot(a.astype(BF16), vb, preferred_element_type=F32)
        o = o + lax.dot_general(q_t[:, ks], s_t.astype(BF16), nt, preferred_element_type=F32)
        upd = jnp.dot(v.T.astype(BF16), k_end[:, ks], preferred_element_type=F32)
        st_ref[h] = s_t * decay[:, ks] + upd
        go = gl[:, 1280 + h * dv:1280 + (h + 1) * dv]
        y = _rms(o, gn) * (go * jax.nn.sigmoid(go))
        yb_ref[:, h * dv:(h + 1) * dv] = y.astype(BF16)

    @pl.when(c == pl.num_programs(1) - 1)
    def _():
        for h in range(nh):
            sfin_ref[0, h] = st_ref[h].T


def _gla(gl, s0, gn, nb, nchunks, ch, in_map, out_map, out_rows):
    return pl.pallas_call(
        _gla_kernel,
        grid=(nb, nchunks),
        in_specs=[
            pl.BlockSpec((ch, 1792), in_map),
            pl.BlockSpec((1, 4, 64, 128), lambda b, c: (b, 0, 0, 0)),
            pl.BlockSpec((1, 128), lambda b, c: (0, 0)),
        ],
        out_specs=[
            pl.BlockSpec((ch, 512), out_map),
            pl.BlockSpec((1, 4, 64, 128), lambda b, c: (b, 0, 0, 0)),
        ],
        out_shape=[
            jax.ShapeDtypeStruct((out_rows, 512), BF16),
            jax.ShapeDtypeStruct((nb, 4, 64, 128), F32),
        ],
        scratch_shapes=[pltpu.VMEM((4, 128, 64), F32)],
        compiler_params=_cparams(("arbitrary", "arbitrary")),
        name="gla",
    )(gl, s0, gn)


def _extract_topk(work, nsel, iota0, sentinel):
    slabs = work.shape[0] // SUBLANES
    vals, idxs = [], []
    for j in range(nsel):
        v = [work[i * SUBLANES:(i + 1) * SUBLANES] for i in range(slabs)]
        ix = [iota0[i * SUBLANES:(i + 1) * SUBLANES] for i in range(slabs)]
        while len(v) > 1:
            keep = [v[i] >= v[i + 1] for i in range(0, len(v), 2)]
            ix = [jnp.where(k, ix[2 * i], ix[2 * i + 1]) for i, k in enumerate(keep)]
            v = [jnp.where(k, v[2 * i], v[2 * i + 1]) for i, k in enumerate(keep)]
        m = jnp.max(v[0], axis=0, keepdims=True)
        idx = jnp.min(jnp.where(v[0] == m, ix[0], sentinel), axis=0, keepdims=True)
        vals.append(m)
        idxs.append(idx)
        if j + 1 < nsel:
            work = jnp.where(iota0 == idx, -jnp.inf, work)
    return vals, idxs


def _merge_kernel(x_ref, gt_ref, ya_ref, yb_ref, wa_ref, wb_ref, wo_ref, gf_ref, wq_ref, keys_ref,
                  xm_ref, hn_ref, et_ref, wt_ref):
    td = x_ref.shape[0]
    nkeys = keys_ref.shape[1]
    half = keys_ref.shape[2]
    nheads = keys_ref.shape[0] // 2
    topk = PEER_TOPK

    gt = gt_ref[...]
    d = x_ref.shape[1]
    ma = jnp.dot(ya_ref[...], wa_ref[...], preferred_element_type=F32)
    mb = jnp.dot(yb_ref[...], wb_ref[...], preferred_element_type=F32)
    m = jax.nn.sigmoid(gt[:, 0:d]) * ma + jax.nn.sigmoid(gt[:, d:2 * d]) * mb
    xm = x_ref[...] + jnp.dot(m.astype(BF16), wo_ref[...], preferred_element_type=F32)
    xm_ref[...] = xm
    hn = _rms(xm, gf_ref[...])
    hn_ref[...] = hn
    q = jnp.dot(hn.astype(BF16), wq_ref[...], preferred_element_type=F32).astype(BF16)

    nt = (((1,), (1,)), ((), ()))
    iota_k = lax.broadcasted_iota(jnp.int32, (nkeys, td), 0)
    pair_rows = [topk // (a + 1) for a in range(topk)]
    cand_pad = -sum(pair_rows) % SUBLANES
    ncand = sum(pair_rows) + cand_pad
    iota_c = lax.broadcasted_iota(jnp.int32, (ncand, td), 0)
    wts, ids = [], []
    for h in range(nheads):
        sv, si = [], []
        for c in range(2):
            gi = 2 * h + c
            s_t = lax.dot_general(keys_ref[gi], q[:, gi * half:(gi + 1) * half], nt,
                                  preferred_element_type=F32)
            vals, idxs = _extract_topk(s_t, topk, iota_k, nkeys)
            sv.append(vals)
            si.append(idxs)
        sv1 = jnp.concatenate(sv[1], axis=0)
        si1 = jnp.concatenate(si[1], axis=0)
        cand = jnp.concatenate([sv[0][a] + sv1[0:nb] for a, nb in enumerate(pair_rows)]
                               + [jnp.full((cand_pad, td), -jnp.inf, F32)], axis=0)
        cidx = jnp.concatenate([si[0][a] * nkeys + si1[0:nb] for a, nb in enumerate(pair_rows)]
                               + [jnp.full((cand_pad, td), -1, jnp.int32)], axis=0)
        fvals, eids = [], []
        work = cand
        for j in range(topk):
            mx = jnp.max(work, axis=0, keepdims=True)
            pos = jnp.min(jnp.where(work == mx, iota_c, ncand), axis=0, keepdims=True)
            hit = iota_c == pos
            eids.append(jnp.max(jnp.where(hit, cidx, -1), axis=0, keepdims=True))
            fvals.append(mx)
            if j + 1 < topk:
                work = jnp.where(hit, -jnp.inf, work)
        fv = jnp.concatenate(fvals, axis=0)
        e = jnp.exp(fv - fvals[0])
        wts.append(e / jnp.sum(e, axis=0, keepdims=True))
        ids.extend(eids)
    wt_ref[...] = jnp.concatenate(wts, axis=0).T
    et_ref[...] = jnp.concatenate(ids, axis=0).T


def _merge_route(x, gt, ya, yb, wa, wb, wo, gf, wq, keys, gt_map, x_first_row):
    t = ya.shape[0]
    d = x.shape[1]
    td = MERGE_ROWS
    nsel = (keys.shape[0] // 2) * PEER_TOPK
    const2 = lambda i: (0, 0)
    row = lambda i: (i, 0)
    x_first = x_first_row // td
    return pl.pallas_call(
        _merge_kernel,
        grid=(t // td,),
        in_specs=[
            pl.BlockSpec((td, d), lambda i: (i + x_first, 0)),
            pl.BlockSpec((td, 2 * d), gt_map),
            pl.BlockSpec((td, ya.shape[1]), row),
            pl.BlockSpec((td, yb.shape[1]), row),
            pl.BlockSpec(wa.shape, const2),
            pl.BlockSpec(wb.shape, const2),
            pl.BlockSpec(wo.shape, const2),
            pl.BlockSpec(gf.shape, const2),
            pl.BlockSpec(wq.shape, const2),
            pl.BlockSpec(keys.shape, lambda i: (0, 0, 0)),
        ],
        out_specs=[
            pl.BlockSpec((td, d), row),
            pl.BlockSpec((td, d), row),
            pl.BlockSpec((td, nsel), row),
            pl.BlockSpec((td, nsel), row),
        ],
        out_shape=[
            jax.ShapeDtypeStruct((t, d), F32),
            jax.ShapeDtypeStruct((t, d), F32),
            jax.ShapeDtypeStruct((t, nsel), jnp.int32),
            jax.ShapeDtypeStruct((t, nsel), F32),
        ],
        compiler_params=_cparams(("arbitrary",)),
        name="merge_route",
    )(x, gt, ya, yb, wa, wb, wo, gf, wq, keys)


def _coef_kernel(act_ref, wt_ref, o_ref):
    act = act_ref[...]
    gelu = 0.5 * act * (1.0 + lax.erf(act * (2.0 ** -0.5)))
    bits = pltpu.bitcast((wt_ref[...] * gelu).astype(BF16).astype(F32), jnp.int32)
    o_ref[...] = bits | lax.shift_right_logical(bits, 16)


def _expert_coefs(act, wt):
    t, nsel = wt.shape
    rows = math.gcd(t, 512)
    row = lambda i: (i, 0)
    return pl.pallas_call(
        _coef_kernel,
        grid=(t // rows,),
        in_specs=[pl.BlockSpec((rows, nsel), row), pl.BlockSpec((rows, nsel), row)],
        out_specs=pl.BlockSpec((rows, nsel), row),
        out_shape=jax.ShapeDtypeStruct((t, nsel), jnp.int32),
        compiler_params=_cparams(("arbitrary",)),
        name="expert_coefs",
    )(act, wt)


def _finish_kernel(xm_ref, o_ref, gfin_ref, yacc_hbm, y_ref):
    del yacc_hbm
    y_ref[...] = _rms(xm_ref[...] + o_ref[...], gfin_ref[...])


def _finish(xm, o, gfin, y_acc, row_off):
    t, d = xm.shape
    rows = math.gcd(t, 512)
    first = row_off // rows
    row = lambda i: (i, 0)
    return pl.pallas_call(
        _finish_kernel,
        grid=(t // rows,),
        in_specs=[pl.BlockSpec((rows, d), row), pl.BlockSpec((rows, d), row),
                  pl.BlockSpec(gfin.shape, lambda i: (0, 0)), pl.BlockSpec(memory_space=pl.ANY)],
        out_specs=pl.BlockSpec((rows, d), lambda i: (i + first, 0)),
        out_shape=jax.ShapeDtypeStruct(y_acc.shape, F32),
        input_output_aliases={3: 0},
        compiler_params=_cparams(("arbitrary",)),
        name="finish",
    )(xm, o, gfin, y_acc)


def _rope_table(pos, valid, rope_dim, head_dim):
    half = rope_dim // 2
    inv = ROPE_THETA ** (-jnp.arange(0, rope_dim, 2, dtype=F32) / rope_dim)
    ang = pos.astype(F32)[:, None] * inv[None, :]
    cos, sin = jnp.cos(ang), jnp.sin(ang)
    n = pos.shape[0]
    ones = jnp.ones((n, head_dim - rope_dim), F32)
    zeros_h = jnp.zeros((n, half), F32)
    zeros_r = jnp.zeros((n, head_dim - rope_dim), F32)
    reps = LANES // head_dim
    cosf = jnp.tile(jnp.concatenate([cos, cos, ones], axis=1), (1, reps))
    sin_lo = jnp.tile(jnp.concatenate([zeros_h, sin, zeros_r], axis=1), (1, reps))
    sin_hi = jnp.tile(jnp.concatenate([-sin, zeros_h, zeros_r], axis=1), (1, reps))
    vcol = jnp.broadcast_to(valid.astype(F32)[:, None], (n, LANES))
    return jnp.concatenate([cosf, sin_lo, sin_hi, vcol], axis=1)


def _pack_pairs(x):
    half = x.shape[1] // 2
    lo = lax.bitcast_convert_type(x[:, :half].astype(BF16), jnp.uint16).astype(jnp.uint32)
    hi = lax.bitcast_convert_type(x[:, half:].astype(BF16), jnp.uint16).astype(jnp.uint32)
    return lax.bitcast_convert_type(lo | (hi << 16), jnp.int32)


def _expert_dots(table, idx, hn, nsel):
    n = idx.shape[0]
    c = table.shape[1]
    sc = plsc.get_sparse_core_info()
    lanes = sc.num_lanes
    workers = sc.num_cores * sc.num_subcores
    gw = GATHER_WINDOW
    nbuf = GATHER_BUFFERS
    per_worker = n // workers
    ich = min(DOTS_INDEX_CHUNK, per_worker)
    tok = ich // nsel
    assert n % workers == 0 and per_worker % ich == 0 and nsel == nbuf * gw and hn.shape[1] == c and gw % lanes == 0
    mesh = plsc.VectorSubcoreMesh(core_axis_name="c", subcore_axis_name="s")

    @functools.partial(
        pl.kernel, out_type=jax.ShapeDtypeStruct((n,), F32), mesh=mesh, name="expert_dots",
        compiler_params=pltpu.CompilerParams(needs_layout_passes=False),
        scratch_types=[pltpu.VMEM((ich,), jnp.int32), pltpu.VMEM((tok, c), jnp.int32), pltpu.VMEM((ich,), F32),
                       pltpu.VMEM((lanes * lanes,), F32)]
                      + [pltpu.VMEM((gw, c), jnp.int32)] * nbuf + [pltpu.SemaphoreType.DMA] * nbuf)
    def dots(tab_hbm, idx_hbm, hn_hbm, act_hbm, idx_v, h_v, act_v, scr, *bufs_sems):
        rows, sems = bufs_sems[:nbuf], bufs_sems[nbuf:]
        wid = lax.axis_index("s") * sc.num_cores + lax.axis_index("c")
        base = wid * per_worker
        lane = lax.iota(jnp.int32, lanes)

        def gather(win, buf, sem):
            return pltpu.make_async_copy(tab_hbm.at[idx_v.at[pl.ds(win * gw, gw)]], buf, sem)

        def reduce_window(buf, t_loc, out_off):
            for rb in range(gw // lanes):
                def kbody(k2, accs):
                    k0 = 2 * k2 * lanes
                    hw0 = plsc.bitcast(h_v[t_loc, pl.ds(k0, lanes)], BF16)
                    hw1 = plsc.bitcast(h_v[t_loc, pl.ds(k0 + lanes, lanes)], BF16)
                    out = []
                    for r in range(lanes):
                        w0 = plsc.bitcast(buf[rb * lanes + r, pl.ds(k0, lanes)], BF16)
                        w1 = plsc.bitcast(buf[rb * lanes + r, pl.ds(k0 + lanes, lanes)], BF16)
                        p = plsc.bitcast(w0 * hw0 + w1 * hw1, jnp.int32)
                        out.append(accs[r] + lax.bitcast_convert_type(p << 16, F32)
                                   + lax.bitcast_convert_type(p, F32))
                    return tuple(out)

                accs = lax.fori_loop(0, c // (2 * lanes), kbody,
                                     tuple(jnp.zeros((lanes,), F32) for _ in range(lanes)))
                for r in range(lanes):
                    scr[pl.ds(r * lanes, lanes)] = accs[r]
                cols = [plsc.load_gather(scr, [lane * lanes + l]) for l in range(lanes)]
                while len(cols) > 1:
                    cols = [cols[i] + cols[i + 1] for i in range(0, len(cols), 2)]
                act_v[pl.ds(out_off + rb * lanes, lanes)] = cols[0]

        @pl.loop(0, per_worker // ich)
        def _(g):
            cb = base + g * ich
            pltpu.sync_copy(idx_hbm.at[pl.ds(cb, ich)], idx_v)
            tok_base = pl.multiple_of(wid * (per_worker // nsel) + g * tok, tok)
            pltpu.sync_copy(hn_hbm.at[pl.ds(tok_base, tok)], h_v)
            for q in range(nbuf):
                gather(q, rows[q], sems[q]).start()

            @pl.loop(0, tok)
            def _(j):
                for q in range(nbuf):
                    gather(nbuf * j + q, rows[q], sems[q]).wait()
                    reduce_window(rows[q], j, j * nsel + q * gw)

                    @pl.when(j + 1 < tok)
                    def _():
                        gather(nbuf * (j + 1) + q, rows[q], sems[q]).start()

            pltpu.sync_copy(act_v, act_hbm.at[pl.ds(cb, ich)])

    return dots(table, idx, hn)


def _expert_mix(table, idx, coef, nsel):
    n = idx.shape[0]
    c = table.shape[1]
    d = 2 * c
    sc = plsc.get_sparse_core_info()
    lanes = sc.num_lanes
    workers = sc.num_cores * sc.num_subcores
    gw = GATHER_WINDOW
    nbuf = GATHER_BUFFERS
    per_worker = n // workers
    ich = min(MIX_INDEX_CHUNK, per_worker)
    tok = ich // nsel
    kblock = 16
    assert n % workers == 0 and per_worker % ich == 0 and nsel == nbuf * gw and c % (kblock * lanes) == 0
    mesh = plsc.VectorSubcoreMesh(core_axis_name="c", subcore_axis_name="s")

    @functools.partial(
        pl.kernel, out_type=jax.ShapeDtypeStruct((n // nsel, d), F32), mesh=mesh, name="expert_mix",
        compiler_params=pltpu.CompilerParams(needs_layout_passes=False),
        scratch_types=[pltpu.VMEM((ich,), jnp.int32), pltpu.VMEM((ich,), jnp.int32), pltpu.VMEM((tok, d), F32)]
                      + [pltpu.VMEM((gw, c), jnp.int32)] * nbuf + [pltpu.SemaphoreType.DMA] * nbuf)
    def mix(tab_hbm, idx_hbm, coef_hbm, out_hbm, idx_v, coef_v, out_v, *bufs_sems):
        rows, sems = bufs_sems[:nbuf], bufs_sems[nbuf:]
        wid = lax.axis_index("s") * sc.num_cores + lax.axis_index("c")
        base = wid * per_worker
        zero_idx = jnp.zeros((lanes,), jnp.int32)

        def gather(win, buf, sem):
            return pltpu.make_async_copy(tab_hbm.at[idx_v.at[pl.ds(win * gw, gw)]], buf, sem)

        def accumulate_window(buf, t_loc, coef_off, first):
            for kb in range(c // (kblock * lanes)):
                col0 = kb * kblock * lanes
                if first:
                    init = tuple(jnp.zeros((lanes,), F32) for _ in range(2 * kblock))
                else:
                    init = tuple(out_v[t_loc, pl.ds(col0 + i * lanes, lanes)] for i in range(kblock)) + \
                           tuple(out_v[t_loc, pl.ds(c + col0 + i * lanes, lanes)] for i in range(kblock))

                def rbody(r2, accs):
                    accs = list(accs)
                    r = 2 * r2
                    cw0 = plsc.bitcast(plsc.load_gather(coef_v, [zero_idx + (coef_off + r)]), BF16)
                    cw1 = plsc.bitcast(plsc.load_gather(coef_v, [zero_idx + (coef_off + r + 1)]), BF16)
                    for i in range(kblock):
                        w0 = plsc.bitcast(buf[r, pl.ds(col0 + i * lanes, lanes)], BF16)
                        w1 = plsc.bitcast(buf[r + 1, pl.ds(col0 + i * lanes, lanes)], BF16)
                        p = plsc.bitcast(w0 * cw0 + w1 * cw1, jnp.int32)
                        accs[i] = accs[i] + lax.bitcast_convert_type(p << 16, F32)
                        accs[kblock + i] = accs[kblock + i] + lax.bitcast_convert_type(p, F32)
                    return tuple(accs)

                accs = lax.fori_loop(0, gw // 2, rbody, init)
                for i in range(kblock):
                    out_v[t_loc, pl.ds(col0 + i * lanes, lanes)] = accs[i]
                    out_v[t_loc, pl.ds(c + col0 + i * lanes, lanes)] = accs[kblock + i]

        @pl.loop(0, per_worker // ich)
        def _(g):
            cb = base + g * ich
            pltpu.sync_copy(idx_hbm.at[pl.ds(cb, ich)], idx_v)
            pltpu.sync_copy(coef_hbm.at[pl.ds(cb, ich)], coef_v)
            for q in range(nbuf):
                gather(q, rows[q], sems[q]).start()

            @pl.loop(0, tok)
            def _(j):
                for q in range(nbuf):
                    gather(nbuf * j + q, rows[q], sems[q]).wait()
                    accumulate_window(rows[q], j, j * nsel + q * gw, q == 0)

                    @pl.when(j + 1 < tok)
                    def _():
                        gather(nbuf * (j + 1) + q, rows[q], sems[q]).start()

            tok_base = pl.multiple_of(wid * (per_worker // nsel) + g * tok, tok)
            pltpu.sync_copy(out_v, out_hbm.at[pl.ds(tok_base, tok)])

    return mix(table, idx, coef)


def _peer_tail(xm, hn, et, wt, tabs, gfin, y_acc, row_off):
    u_tab, v_tab = tabs
    t, nsel = et.shape
    eidx = et.reshape(t * nsel)
    act = _expert_dots(u_tab, eidx, _pack_pairs(hn), nsel).reshape(t, nsel)
    coef = _expert_coefs(act, wt).reshape(t * nsel)
    mixed = _expert_mix(v_tab, eidx, coef, nsel)
    return _finish(xm, mixed, gfin, y_acc, row_off)


def kernel(x_prompt, x_sample, cache_k_window, cache_v_window, state_gla, meta_tokens, g_norm_mix, w_in,
           w_gate_up, b_gate, attn_sinks, g_gla_norm, w_branch_a, w_branch_b, w_out, g_norm_ffn, w_peer_q,
           peer_sub_keys, peer_u, peer_v, g_norm_final):
    bsz, seq, d = x_prompt.shape
    dbsz, tdec, _ = x_sample.shape
    n_meta = meta_tokens.shape[0]
    depth = w_in.shape[0]
    window = cache_k_window.shape[2]
    kv_heads, head_dim = cache_k_window.shape[3], cache_k_window.shape[4]
    gate_rank = w_gate_up.shape[1]
    bqk = w_gate_up.shape[2]
    n_ph, _, n_keys, p_half = peer_sub_keys.shape[1:]
    assert depth == 1 and d == 1024 and window == ATTN_BLOCK and kv_heads == 2 and head_dim == 64
    assert bqk == 256 and state_gla.shape[2:] == (4, 64, 128) and n_meta <= ATTN_BLOCK
    assert seq % ATTN_BLOCK == 0 and tdec <= SAMPLE_PAD and n_keys == 128 and p_half == 64 and n_ph == 8
    rope_dim = head_dim // 4
    meta_pad = ATTN_BLOCK - n_meta
    lp = ATTN_BLOCK + seq
    nblk = lp // ATTN_BLOCK

    w = w_in[0]
    c_lr = 2304
    c_gate = c_lr + gate_rank
    w1 = w[:, :c_lr].astype(BF16)
    wlr = jnp.pad(w[:, c_lr:c_gate], ((0, 0), (0, LANES - gate_rank))).astype(BF16)
    w2 = w[:, c_gate:].astype(BF16)
    wgu = jnp.pad(w_gate_up[0], ((0, LANES - gate_rank), (0, 0))).astype(BF16)
    bg = b_gate[0][None, :]
    gmix = g_norm_mix[0][None, :]
    wa = w_branch_a[0].astype(BF16)
    wb = w_branch_b[0].astype(BF16)
    wo = w_out[0].astype(BF16)
    gffn = g_norm_ffn[0][None, :]
    wq = w_peer_q[0].astype(BF16)
    keys = peer_sub_keys[0].reshape(n_ph * 2, n_keys, p_half).astype(BF16)
    u_tab = _pack_pairs(peer_u[0])
    gfin = g_norm_final[None, :]
    gn = g_gla_norm[0][None, :]
    sinks = attn_sinks[0]
    qk_scale = float(bqk // 4) ** -0.5

    rows_p = jnp.arange(lp)
    tab_p = _rope_table(rows_p - meta_pad, rows_p >= meta_pad, rope_dim, head_dim)
    proj_rows = max(r for r in range(16, PROJ_ROWS + 1, 16) if lp % r == 0)
    nq = nblk - 1
    nchunks = lp // GLA_CHUNK
    skip = ATTN_BLOCK // GLA_CHUNK
    ncq = nchunks - skip
    per_seq = seq // MERGE_ROWS
    gt_map_p = lambda i: ((i // per_seq) * nblk + 1 + (i % per_seq), 0)

    meta = jnp.broadcast_to(meta_tokens[None].astype(x_prompt.dtype), (bsz, n_meta, d))
    xpad = jnp.concatenate([jnp.zeros((bsz, meta_pad, d), x_prompt.dtype), meta, x_prompt], axis=1)
    xpad = xpad.reshape(bsz * lp, d)
    x_rows = x_prompt.reshape(bsz * seq, d)

    def mix_and_route(b0, gb):
        qa, kv, gl, gt = _project(xpad, gmix, tab_p, w1, wlr, wgu, bg, w2, proj_rows, qk_scale, b0 * lp, gb * lp)
        ya = _attention(
            sinks, qa, kv, kv, gb, nq, ATTN_BLOCK,
            lambda b, n: (b * nblk + n + 1, 0), lambda b, n: (b * nblk + n, 0), lambda b, n: (b * nblk + n + 1, 0),
            lambda b, n: (b * nq + n, 0), gb * seq, first_valid_key=meta_pad, block_offset=1)
        s0 = jnp.zeros((gb,) + state_gla.shape[2:], F32)
        yb, s_fin = _gla(gl, s0, gn, gb, nchunks, GLA_CHUNK,
                         lambda b, c: (b * nchunks + c, 0),
                         lambda b, c: (b * ncq + jnp.maximum(c - skip, 0), 0), gb * seq)
        xm, hn, et, wt = _merge_route(x_rows, gt, ya, yb, wa, wb, wo, gffn, wq, keys, gt_map_p, b0 * seq)
        kv_w = kv.reshape(gb, lp, 2, kv_heads, head_dim)[:, lp - window:]
        return (xm, hn, et, wt), kv_w, s_fin

    group = PROMPT_GROUP if (bsz % PROMPT_GROUP == 0 and (PROMPT_GROUP * seq) % GATHER_ROW_QUANTUM == 0) else bsz
    y_acc, tabs, kv_parts, s_parts = None, None, [], []
    for b0 in range(0, bsz, group):
        routed, kv_w, s_fin = mix_and_route(b0, group)
        if tabs is None:
            anchor = (routed[2][0, 0] * 0).astype(F32)
            tabs = (u_tab, _pack_pairs(peer_v[0] + anchor))
            y_acc = jnp.zeros((bsz * seq, d), F32) + anchor
        y_acc = _peer_tail(*routed, tabs, gfin, y_acc, b0 * seq)
        kv_parts.append(kv_w)
        s_parts.append(s_fin)
    y_prompt = y_acc.reshape(bsz, seq, d)
    kv_p = jnp.concatenate(kv_parts, axis=0)
    s_fin_p = jnp.concatenate(s_parts, axis=0)
    new_k_p = kv_p[:, :, 0][None]
    new_v_p = kv_p[:, :, 1][None]

    sp = SAMPLE_PAD
    xs_pad = jnp.pad(x_sample, ((0, 0), (0, sp - tdec), (0, 0))).reshape(dbsz * sp, d)
    rows_s = jnp.arange(sp)
    reps = 256 // sp
    tab_s = jnp.tile(_rope_table(PAST_LEN + rows_s, rows_s < tdec, rope_dim, head_dim), (reps, 1))
    qa_s, kv_s, gl_s, gt_s = _project(xs_pad, gmix, tab_s, w1, wlr, wgu, bg, w2, 256, qk_scale, 0, dbsz * sp)

    cache_kv = jnp.concatenate([cache_k_window[0].reshape(dbsz * window, kv_heads * head_dim),
                                cache_v_window[0].reshape(dbsz * window, kv_heads * head_dim)], axis=1)
    seq_map = lambda b, n: (b, 0)
    ya_s = _attention(sinks, qa_s, cache_kv, kv_s, dbsz, 1, sp, seq_map, seq_map, seq_map, seq_map,
                      dbsz * sp, first_valid_key=None, block_offset=0)
    yb_s, s_fin_s = _gla(gl_s, state_gla[0], gn, dbsz, 1, sp, seq_map, seq_map, dbsz * sp)

    def real_rows(a):
        return a.reshape(dbsz, sp, a.shape[-1])[:, :tdec].reshape(dbsz * tdec, a.shape[-1])

    xs_rows = x_sample.reshape(dbsz * tdec, d)
    xm_s, hn_s, et_s, wt_s = _merge_route(xs_rows, real_rows(gt_s), real_rows(ya_s), real_rows(yb_s),
                                          wa, wb, wo, gffn, wq, keys, lambda i: (i, 0), 0)
    y_sample = _peer_tail(xm_s, hn_s, et_s, wt_s, tabs, gfin, jnp.zeros((dbsz * tdec, d), F32), 0).reshape(dbsz, tdec, d)

    kv_new = real_rows(kv_s).reshape(dbsz, tdec, 2, kv_heads, head_dim)
    new_k_s = jnp.concatenate([cache_k_window[0].astype(F32), kv_new[:, :, 0]], axis=1)[:, -window:][None]
    new_v_s = jnp.concatenate([cache_v_window[0].astype(F32), kv_new[:, :, 1]], axis=1)[:, -window:][None]

    return (y_prompt, y_sample, new_k_p, new_v_p, s_fin_p[None], new_k_s, new_v_s, s_fin_s[None])
```

```python
import functools
import math

import jax
import jax.numpy as jnp
from jax import lax
from jax.experimental import pallas as pl
from jax.experimental.pallas import tpu as pltpu
from jax.experimental.pallas import tpu_sc as plsc

F32 = jnp.float32
BF16 = jnp.bfloat16

EPS = 1e-6
NEG_INF = -1e30
PAST_LEN = 16384
ROPE_THETA = 500000.0
GATE_NORMALIZER = 16.0
PEER_TOPK = 16

LANES = 128
SUBLANES = 8
VMEM_LIMIT_BYTES = 56 * 1024 * 1024

ATTN_BLOCK = 128
GLA_CHUNK = 64
SAMPLE_PAD = 16
PROJ_ROWS = 544
MERGE_ROWS = 128
GATHER_WINDOW = 32
GATHER_BUFFERS = 4
DOTS_INDEX_CHUNK = 8192
MIX_INDEX_CHUNK = 4096
GATHER_ROW_QUANTUM = 512
PROMPT_GROUP = 1


def _cparams(sem):
    return pltpu.CompilerParams(dimension_semantics=sem, vmem_limit_bytes=VMEM_LIMIT_BYTES)


def _rms(x, g):
    ms = jnp.mean(x * x, axis=-1, keepdims=True)
    return (x * lax.rsqrt(ms + EPS)) * g


def _proj_kernel(x_ref, g_ref, tab_ref, w1_ref, wlr_ref, wgu_ref, bg_ref, w2_ref,
                 qa_ref, kv_ref, gl_ref, gt_ref, *, period, qk_scale):
    i = pl.program_id(0)
    tr = x_ref.shape[0]
    hb = _rms(x_ref[...], g_ref[...]).astype(BF16)
    z1 = jnp.dot(hb, w1_ref[...], preferred_element_type=F32)

    start = pl.multiple_of((i * tr) % period, SUBLANES)
    tab = tab_ref[pl.ds(start, tr), :]
    cosf = tab[:, 0:LANES]
    sin_lo = tab[:, LANES:2 * LANES]
    sin_hi = tab[:, 2 * LANES:3 * LANES]
    valid = tab[:, 3 * LANES:3 * LANES + 1]

    def rope(xg):
        return xg * cosf + pltpu.roll(xg, 8, 1) * sin_lo + pltpu.roll(xg, LANES - 8, 1) * sin_hi

    for gi in range(4):
        sl = slice(gi * LANES, (gi + 1) * LANES)
        qa_ref[:, sl] = rope(z1[:, sl]).astype(BF16)
    kv_ref[:, 0:LANES] = rope(z1[:, 512:640])
    kv_ref[:, LANES:2 * LANES] = z1[:, 640:768]

    lr = jnp.dot(hb, wlr_ref[...], preferred_element_type=F32)
    pre = jnp.dot(lr.astype(BF16), wgu_ref[...], preferred_element_type=F32) + bg_ref[...]
    log_sig = jnp.minimum(pre, 0.0) - jnp.log1p(jnp.exp(-jnp.abs(pre)))
    ld = jnp.where(valid > 0.5, log_sig / GATE_NORMALIZER, 0.0)

    gl_ref[:, 0:256] = z1[:, 768:1024] * qk_scale
    gl_ref[:, 256:512] = z1[:, 1024:1280]
    gl_ref[:, 512:768] = ld
    gl_ref[:, 768:1792] = z1[:, 1280:2304]
    gt_ref[...] = jnp.dot(hb, w2_ref[...], preferred_element_type=F32)


def _project(x, g, tab, w1, wlr, wgu, bg, w2, rows, qk_scale, first_row, r):
    d = x.shape[1]
    period = tab.shape[0]
    const = lambda i: (0, 0)
    row = lambda i: (i, 0)
    first = first_row // rows
    return pl.pallas_call(
        functools.partial(_proj_kernel, period=period, qk_scale=qk_scale),
        grid=(r // rows,),
        in_specs=[
            pl.BlockSpec((rows, d), lambda i: (i + first, 0)),
            pl.BlockSpec(g.shape, const),
            pl.BlockSpec(tab.shape, const),
            pl.BlockSpec(w1.shape, const),
            pl.BlockSpec(wlr.shape, const),
            pl.BlockSpec(wgu.shape, const),
            pl.BlockSpec(bg.shape, const),
            pl.BlockSpec(w2.shape, const),
        ],
        out_specs=[
            pl.BlockSpec((rows, 512), row),
            pl.BlockSpec((rows, 256), row),
            pl.BlockSpec((rows, 1792), row),
            pl.BlockSpec((rows, 2048), row),
        ],
        out_shape=[
            jax.ShapeDtypeStruct((r, 512), BF16),
            jax.ShapeDtypeStruct((r, 256), F32),
            jax.ShapeDtypeStruct((r, 1792), F32),
            jax.ShapeDtypeStruct((r, 2048), F32),
        ],
        compiler_params=_cparams(("arbitrary",)),
        name="proj",
    )(x, g, tab, w1, wlr, wgu, bg, w2)


def _attn_kernel(sink_ref, q_ref, prev_ref, cur_ref, o_ref, *, first_valid_key, block_offset):
    n = pl.program_id(1)
    qr = q_ref.shape[0]
    kr = cur_ref.shape[0]
    w = prev_ref.shape[0]
    nk = w + kr
    group = 4
    hd = 64

    rows = lax.broadcasted_iota(jnp.int32, (group * qr, nk), 0)
    cols = lax.broadcasted_iota(jnp.int32, (group * qr, nk), 1)
    head_of_row = rows // qr
    diff = (rows - head_of_row * qr) - cols + w
    mask = (diff >= 0) & (diff <= w)
    if first_valid_key is not None:
        blk = n + block_offset
        mask = mask & (cols >= first_valid_key + w - blk * w)

    prev = prev_ref[...]
    cur = cur_ref[...]
    q = q_ref[...]
    row_head = lax.broadcasted_iota(jnp.int32, (group * qr, 1), 0) // qr
    for kh in range(2):
        k = jnp.concatenate([prev[:, kh * hd:(kh + 1) * hd], cur[:, kh * hd:(kh + 1) * hd]], axis=0).astype(BF16)
        v = jnp.concatenate([prev[:, LANES + kh * hd:LANES + (kh + 1) * hd],
                             cur[:, LANES + kh * hd:LANES + (kh + 1) * hd]], axis=0).astype(BF16)
        qs = jnp.concatenate([q[:, (group * kh + g) * hd:(group * kh + g + 1) * hd] for g in range(group)], axis=0)
        s = lax.dot_general(qs, k, (((1,), (1,)), ((), ())), preferred_element_type=F32) * (hd ** -0.5)
        s = jnp.where(mask, s, NEG_INF)
        sink = jnp.zeros((group * qr, 1), F32)
        for g in range(group):
            sink = jnp.where(row_head == g, sink_ref[group * kh + g], sink)
        m = jnp.maximum(jnp.max(s, axis=-1, keepdims=True), sink)
        e = jnp.exp(s - m)
        p = e / (jnp.sum(e, axis=-1, keepdims=True) + jnp.exp(sink - m))
        o = jnp.dot(p.astype(BF16), v, preferred_element_type=F32)
        for g in range(group):
            h = group * kh + g
            o_ref[:, h * hd:(h + 1) * hd] = o[g * qr:(g + 1) * qr].astype(BF16)


def _attention(sinks, q, kv_prev, kv_cur, nb, nblk, qr, q_map, prev_map, cur_map, out_map, out_rows,
               first_valid_key, block_offset):
    w = ATTN_BLOCK
    return pl.pallas_call(
        functools.partial(_attn_kernel, first_valid_key=first_valid_key, block_offset=block_offset),
        grid=(nb, nblk),
        in_specs=[
            pl.BlockSpec(memory_space=pltpu.SMEM),
            pl.BlockSpec((qr, 512), q_map),
            pl.BlockSpec((w, 256), prev_map),
            pl.BlockSpec((qr, 256), cur_map),
        ],
        out_specs=pl.BlockSpec((qr, 512), out_map),
        out_shape=jax.ShapeDtypeStruct((out_rows, 512), BF16),
        compiler_params=_cparams(("arbitrary", "arbitrary")),
        name="swa",
    )(sinks, q, kv_prev, kv_cur)


def _gla_kernel(gl_ref, s0_ref, gn_ref, yb_ref, sfin_ref, st_ref):
    c = pl.program_id(1)
    ch = gl_ref.shape[0]
    nh, dk, dv = 4, 64, 128

    @pl.when(c == 0)
    def _():
        for h in range(nh):
            st_ref[h] = s0_ref[0, h].T

    gl = gl_ref[...]
    q = gl[:, 0:256]
    k = gl[:, 256:512]
    b = gl[:, 512:768]
    row = lax.broadcasted_iota(jnp.int32, (ch, nh * dk), 0)
    sh = 1
    while sh < ch:
        b = b + jnp.where(row >= sh, pltpu.roll(b, sh, 0), 0.0)
        sh *= 2
    b_last = b[ch - 1:ch, :]
    q_t = (q * jnp.exp(b)).astype(BF16)
    k_t = (k * jnp.exp(-b)).astype(BF16)
    k_end = (k * jnp.exp(b_last - b)).astype(BF16)
    decay = jnp.exp(b_last)
    causal = (lax.broadcasted_iota(jnp.int32, (ch, ch), 0) >= lax.broadcasted_iota(jnp.int32, (ch, ch), 1))
    gn = gn_ref[...]
    nt = (((1,), (1,)), ((), ()))
    for h in range(nh):
        ks = slice(h * dk, (h + 1) * dk)
        v = gl[:, 768 + h * dv:768 + (h + 1) * dv]
        vb = v.astype(BF16)
        a = lax.dot_general(q_t[:, ks], k_t[:, ks], nt, preferred_element_type=F32)
        a = jnp.where(causal, a, 0.0)
        s_t = st_ref[h]
        o = jnp.dot(a.astype(BF16), vb, preferred_element_type=F32)
        o = o + lax.dot_general(q_t[:, ks], s_t.astype(BF16), nt, preferred_element_type=F32)
        upd = jnp.dot(v.T.astype(BF16), k_end[:, ks], preferred_element_type=F32)
        st_ref[h] = s_t * decay[:, ks] + upd
        go = gl[:, 1280 + h * dv:1280 + (h + 1) * dv]
        y = _rms(o, gn) * (go * jax.nn.sigmoid(go))
        yb_ref[:, h * dv:(h + 1) * dv] = y.astype(BF16)

    @pl.when(c == pl.num_programs(1) - 1)
    def _():
        for h in range(nh):
            sfin_ref[0, h] = st_ref[h].T


def _gla(gl, s0, gn, nb, nchunks, ch, in_map, out_map, out_rows):
    return pl.pallas_call(
        _gla_kernel,
        grid=(nb, nchunks),
        in_specs=[
            pl.BlockSpec((ch, 1792), in_map),
            pl.BlockSpec((1, 4, 64, 128), lambda b, c: (b, 0, 0, 0)),
            pl.BlockSpec((1, 128), lambda b, c: (0, 0)),
        ],
        out_specs=[
            pl.BlockSpec((ch, 512), out_map),
            pl.BlockSpec((1, 4, 64, 128), lambda b, c: (b, 0, 0, 0)),
        ],
        out_shape=[
            jax.ShapeDtypeStruct((out_rows, 512), BF16),
            jax.ShapeDtypeStruct((nb, 4, 64, 128), F32),
        ],
        scratch_shapes=[pltpu.VMEM((4, 128, 64), F32)],
        compiler_params=_cparams(("arbitrary", "arbitrary")),
        name="gla",
    )(gl, s0, gn)


def _extract_topk(work, nsel, iota0, sentinel):
    slabs = work.shape[0] // SUBLANES
    vals, idxs = [], []
    for j in range(nsel):
        v = [work[i * SUBLANES:(i + 1) * SUBLANES] for i in range(slabs)]
        ix = [iota0[i * SUBLANES:(i + 1) * SUBLANES] for i in range(slabs)]
        while len(v) > 1:
            keep = [v[i] >= v[i + 1] for i in range(0, len(v), 2)]
            ix = [jnp.where(k, ix[2 * i], ix[2 * i + 1]) for i, k in enumerate(keep)]
            v = [jnp.where(k, v[2 * i], v[2 * i + 1]) for i, k in enumerate(keep)]
        m = jnp.max(v[0], axis=0, keepdims=True)
        idx = jnp.min(jnp.where(v[0] == m, ix[0], sentinel), axis=0, keepdims=True)
        vals.append(m)
        idxs.append(idx)
        if j + 1 < nsel:
            work = jnp.where(iota0 == idx, -jnp.inf, work)
    return vals, idxs


def _merge_kernel(x_ref, gt_ref, ya_ref, yb_ref, wa_ref, wb_ref, wo_ref, gf_ref, wq_ref, keys_ref,
                  xm_ref, hn_ref, et_ref, wt_ref):
    td = x_ref.shape[0]
    nkeys = keys_ref.shape[1]
    half = keys_ref.shape[2]
    nheads = keys_ref.shape[0] // 2
    topk = PEER_TOPK

    gt = gt_ref[...]
    d = x_ref.shape[1]
    ma = jnp.dot(ya_ref[...], wa_ref[...], preferred_element_type=F32)
    mb = jnp.dot(yb_ref[...], wb_ref[...], preferred_element_type=F32)
    m = jax.nn.sigmoid(gt[:, 0:d]) * ma + jax.nn.sigmoid(gt[:, d:2 * d]) * mb
    xm = x_ref[...] + jnp.dot(m.astype(BF16), wo_ref[...], preferred_element_type=F32)
    xm_ref[...] = xm
    hn = _rms(xm, gf_ref[...])
    hn_ref[...] = hn
    q = jnp.dot(hn.astype(BF16), wq_ref[...], preferred_element_type=F32).astype(BF16)

    nt = (((1,), (1,)), ((), ()))
    iota_k = lax.broadcasted_iota(jnp.int32, (nkeys, td), 0)
    pair_rows = [topk // (a + 1) for a in range(topk)]
    cand_pad = -sum(pair_rows) % SUBLANES
    ncand = sum(pair_rows) + cand_pad
    iota_c = lax.broadcasted_iota(jnp.int32, (ncand, td), 0)
    wts, ids = [], []
    for h in range(nheads):
        sv, si = [], []
        for c in range(2):
            gi = 2 * h + c
            s_t = lax.dot_general(keys_ref[gi], q[:, gi * half:(gi + 1) * half], nt,
                                  preferred_element_type=F32)
            vals, idxs = _extract_topk(s_t, topk, iota_k, nkeys)
            sv.append(vals)
            si.append(idxs)
        sv1 = jnp.concatenate(sv[1], axis=0)
        si1 = jnp.concatenate(si[1], axis=0)
        cand = jnp.concatenate([sv[0][a] + sv1[0:nb] for a, nb in enumerate(pair_rows)]
                               + [jnp.full((cand_pad, td), -jnp.inf, F32)], axis=0)
        cidx = jnp.concatenate([si[0][a] * nkeys + si1[0:nb] for a, nb in enumerate(pair_rows)]
                               + [jnp.full((cand_pad, td), -1, jnp.int32)], axis=0)
        fvals, eids = [], []
        work = cand
        for j in range(topk):
            mx = jnp.max(work, axis=0, keepdims=True)
            pos = jnp.min(jnp.where(work == mx, iota_c, ncand), axis=0, keepdims=True)
            hit = iota_c == pos
            eids.append(jnp.max(jnp.where(hit, cidx, -1), axis=0, keepdims=True))
            fvals.append(mx)
            if j + 1 < topk:
                work = jnp.where(hit, -jnp.inf, work)
        fv = jnp.concatenate(fvals, axis=0)
        e = jnp.exp(fv - fvals[0])
        wts.append(e / jnp.sum(e, axis=0, keepdims=True))
        ids.extend(eids)
    wt_ref[...] = jnp.concatenate(wts, axis=0).T
    et_ref[...] = jnp.concatenate(ids, axis=0).T


def _merge_route(x, gt, ya, yb, wa, wb, wo, gf, wq, keys, gt_map, x_first_row):
    t = ya.shape[0]
    d = x.shape[1]
    td = MERGE_ROWS
    nsel = (keys.shape[0] // 2) * PEER_TOPK
    const2 = lambda i: (0, 0)
    row = lambda i: (i, 0)
    x_first = x_first_row // td
    return pl.pallas_call(
        _merge_kernel,
        grid=(t // td,),
        in_specs=[
            pl.BlockSpec((td, d), lambda i: (i + x_first, 0)),
            pl.BlockSpec((td, 2 * d), gt_map),
            pl.BlockSpec((td, ya.shape[1]), row),
            pl.BlockSpec((td, yb.shape[1]), row),
            pl.BlockSpec(wa.shape, const2),
            pl.BlockSpec(wb.shape, const2),
            pl.BlockSpec(wo.shape, const2),
            pl.BlockSpec(gf.shape, const2),
            pl.BlockSpec(wq.shape, const2),
            pl.BlockSpec(keys.shape, lambda i: (0, 0, 0)),
        ],
        out_specs=[
            pl.BlockSpec((td, d), row),
            pl.BlockSpec((td, d), row),
            pl.BlockSpec((td, nsel), row),
            pl.BlockSpec((td, nsel), row),
        ],
        out_shape=[
            jax.ShapeDtypeStruct((t, d), F32),
            jax.ShapeDtypeStruct((t, d), F32),
            jax.ShapeDtypeStruct((t, nsel), jnp.int32),
            jax.ShapeDtypeStruct((t, nsel), F32),
        ],
        compiler_params=_cparams(("arbitrary",)),
        name="merge_route",
    )(x, gt, ya, yb, wa, wb, wo, gf, wq, keys)


def _coef_kernel(act_ref, wt_ref, o_ref):
    act = act_ref[...]
    gelu = 0.5 * act * (1.0 + lax.erf(act * (2.0 ** -0.5)))
    bits = pltpu.bitcast((wt_ref[...] * gelu).astype(BF16).astype(F32), jnp.int32)
    o_ref[...] = bits | lax.shift_right_logical(bits, 16)


def _expert_coefs(act, wt):
    t, nsel = wt.shape
    rows = math.gcd(t, 512)
    row = lambda i: (i, 0)
    return pl.pallas_call(
        _coef_kernel,
        grid=(t // rows,),
        in_specs=[pl.BlockSpec((rows, nsel), row), pl.BlockSpec((rows, nsel), row)],
        out_specs=pl.BlockSpec((rows, nsel), row),
        out_shape=jax.ShapeDtypeStruct((t, nsel), jnp.int32),
        compiler_params=_cparams(("arbitrary",)),
        name="expert_coefs",
    )(act, wt)


def _finish_kernel(xm_ref, o_ref, gfin_ref, yacc_hbm, y_ref):
    del yacc_hbm
    y_ref[...] = _rms(xm_ref[...] + o_ref[...], gfin_ref[...])


def _finish(xm, o, gfin, y_acc, row_off):
    t, d = xm.shape
    rows = math.gcd(t, 512)
    first = row_off // rows
    row = lambda i: (i, 0)
    return pl.pallas_call(
        _finish_kernel,
        grid=(t // rows,),
        in_specs=[pl.BlockSpec((rows, d), row), pl.BlockSpec((rows, d), row),
                  pl.BlockSpec(gfin.shape, lambda i: (0, 0)), pl.BlockSpec(memory_space=pl.ANY)],
        out_specs=pl.BlockSpec((rows, d), lambda i: (i + first, 0)),
        out_shape=jax.ShapeDtypeStruct(y_acc.shape, F32),
        input_output_aliases={3: 0},
        compiler_params=_cparams(("arbitrary",)),
        name="finish",
    )(xm, o, gfin, y_acc)


def _rope_table(pos, valid, rope_dim, head_dim):
    half = rope_dim // 2
    inv = ROPE_THETA ** (-jnp.arange(0, rope_dim, 2, dtype=F32) / rope_dim)
    ang = pos.astype(F32)[:, None] * inv[None, :]
    cos, sin = jnp.cos(ang), jnp.sin(ang)
    n = pos.shape[0]
    ones = jnp.ones((n, head_dim - rope_dim), F32)
    zeros_h = jnp.zeros((n, half), F32)
    zeros_r = jnp.zeros((n, head_dim - rope_dim), F32)
    reps = LANES // head_dim
    cosf = jnp.tile(jnp.concatenate([cos, cos, ones], axis=1), (1, reps))
    sin_lo = jnp.tile(jnp.concatenate([zeros_h, sin, zeros_r], axis=1), (1, reps))
    sin_hi = jnp.tile(jnp.concatenate([-sin, zeros_h, zeros_r], axis=1), (1, reps))
    vcol = jnp.broadcast_to(valid.astype(F32)[:, None], (n, LANES))
    return jnp.concatenate([cosf, sin_lo, sin_hi, vcol], axis=1)


def _pack_pairs(x):
    half = x.shape[1] // 2
    lo = lax.bitcast_convert_type(x[:, :half].astype(BF16), jnp.uint16).astype(jnp.uint32)
    hi = lax.bitcast_convert_type(x[:, half:].astype(BF16), jnp.uint16).astype(jnp.uint32)
    return lax.bitcast_convert_type(lo | (hi << 16), jnp.int32)


def _expert_dots(table, idx, hn, nsel):
    n = idx.shape[0]
    c = table.shape[1]
    sc = plsc.get_sparse_core_info()
    lanes = sc.num_lanes
    workers = sc.num_cores * sc.num_subcores
    gw = GATHER_WINDOW
    nbuf = GATHER_BUFFERS
    per_worker = n // workers
    ich = min(DOTS_INDEX_CHUNK, per_worker)
    tok = ich // nsel
    assert n % workers == 0 and per_worker % ich == 0 and nsel == nbuf * gw and hn.shape[1] == c and gw % lanes == 0
    mesh = plsc.VectorSubcoreMesh(core_axis_name="c", subcore_axis_name="s")

    @functools.partial(
        pl.kernel, out_type=jax.ShapeDtypeStruct((n,), F32), mesh=mesh, name="expert_dots",
        compiler_params=pltpu.CompilerParams(needs_layout_passes=False),
        scratch_types=[pltpu.VMEM((ich,), jnp.int32), pltpu.VMEM((tok, c), jnp.int32), pltpu.VMEM((ich,), F32),
                       pltpu.VMEM((lanes * lanes,), F32)]
                      + [pltpu.VMEM((gw, c), jnp.int32)] * nbuf + [pltpu.SemaphoreType.DMA] * nbuf)
    def dots(tab_hbm, idx_hbm, hn_hbm, act_hbm, idx_v, h_v, act_v, scr, *bufs_sems):
        rows, sems = bufs_sems[:nbuf], bufs_sems[nbuf:]
        wid = lax.axis_index("s") * sc.num_cores + lax.axis_index("c")
        base = wid * per_worker
        lane = lax.iota(jnp.int32, lanes)

        def gather(win, buf, sem):
            return pltpu.make_async_copy(tab_hbm.at[idx_v.at[pl.ds(win * gw, gw)]], buf, sem)

        def reduce_window(buf, t_loc, out_off):
            for rb in range(gw // lanes):
                def kbody(k2, accs):
                    k0 = 2 * k2 * lanes
                    hw0 = plsc.bitcast(h_v[t_loc, pl.ds(k0, lanes)], BF16)
                    hw1 = plsc.bitcast(h_v[t_loc, pl.ds(k0 + lanes, lanes)], BF16)
                    out = []
                    for r in range(lanes):
                        w0 = plsc.bitcast(buf[rb * lanes + r, pl.ds(k0, lanes)], BF16)
                        w1 = plsc.bitcast(buf[rb * lanes + r, pl.ds(k0 + lanes, lanes)], BF16)
                        p = plsc.bitcast(w0 * hw0 + w1 * hw1, jnp.int32)
                        out.append(accs[r] + lax.bitcast_convert_type(p << 16, F32)
                                   + lax.bitcast_convert_type(p, F32))
                    return tuple(out)

                accs = lax.fori_loop(0, c // (2 * lanes), kbody,
                                     tuple(jnp.zeros((lanes,), F32) for _ in range(lanes)))
                for r in range(lanes):
                    scr[pl.ds(r * lanes, lanes)] = accs[r]
                cols = [plsc.load_gather(scr, [lane * lanes + l]) for l in range(lanes)]
                while len(cols) > 1:
                    cols = [cols[i] + cols[i + 1] for i in range(0, len(cols), 2)]
                act_v[pl.ds(out_off + rb * lanes, lanes)] = cols[0]

        @pl.loop(0, per_worker // ich)
        def _(g):
            cb = base + g * ich
            pltpu.sync_copy(idx_hbm.at[pl.ds(cb, ich)], idx_v)
            tok_base = pl.multiple_of(wid * (per_worker // nsel) + g * tok, tok)
            pltpu.sync_copy(hn_hbm.at[pl.ds(tok_base, tok)], h_v)
            for q in range(nbuf):
                gather(q, rows[q], sems[q]).start()

            @pl.loop(0, tok)
            def _(j):
                for q in range(nbuf):
                    gather(nbuf * j + q, rows[q], sems[q]).wait()
                    reduce_window(rows[q], j, j * nsel + q * gw)

                    @pl.when(j + 1 < tok)
                    def _():
                        gather(nbuf * (j + 1) + q, rows[q], sems[q]).start()

            pltpu.sync_copy(act_v, act_hbm.at[pl.ds(cb, ich)])

    return dots(table, idx, hn)


def _expert_mix(table, idx, coef, nsel):
    n = idx.shape[0]
    c = table.shape[1]
    d = 2 * c
    sc = plsc.get_sparse_core_info()
    lanes = sc.num_lanes
    workers = sc.num_cores * sc.num_subcores
    gw = GATHER_WINDOW
    nbuf = GATHER_BUFFERS
    per_worker = n // workers
    ich = min(MIX_INDEX_CHUNK, per_worker)
    tok = ich // nsel
    kblock = 16
    assert n % workers == 0 and per_worker % ich == 0 and nsel == nbuf * gw and c % (kblock * lanes) == 0
    mesh = plsc.VectorSubcoreMesh(core_axis_name="c", subcore_axis_name="s")

    @functools.partial(
        pl.kernel, out_type=jax.ShapeDtypeStruct((n // nsel, d), F32), mesh=mesh, name="expert_mix",
        compiler_params=pltpu.CompilerParams(needs_layout_passes=False),
        scratch_types=[pltpu.VMEM((ich,), jnp.int32), pltpu.VMEM((ich,), jnp.int32), pltpu.VMEM((tok, d), F32)]
                      + [pltpu.VMEM((gw, c), jnp.int32)] * nbuf + [pltpu.SemaphoreType.DMA] * nbuf)
    def mix(tab_hbm, idx_hbm, coef_hbm, out_hbm, idx_v, coef_v, out_v, *bufs_sems):
        rows, sems = bufs_sems[:nbuf], bufs_sems[nbuf:]
        wid = lax.axis_index("s") * sc.num_cores + lax.axis_index("c")
        base = wid * per_worker
        zero_idx = jnp.zeros((lanes,), jnp.int32)

        def gather(win, buf, sem):
            return pltpu.make_async_copy(tab_hbm.at[idx_v.at[pl.ds(win * gw, gw)]], buf, sem)

        def accumulate_window(buf, t_loc, coef_off, first):
            for kb in range(c // (kblock * lanes)):
                col0 = kb * kblock * lanes
                if first:
                    init = tuple(jnp.zeros((lanes,), F32) for _ in range(2 * kblock))
                else:
                    init = tuple(out_v[t_loc, pl.ds(col0 + i * lanes, lanes)] for i in range(kblock)) + \
                           tuple(out_v[t_loc, pl.ds(c + col0 + i * lanes, lanes)] for i in range(kblock))

                def rbody(r2, accs):
                    accs = list(accs)
                    r = 2 * r2
                    cw0 = plsc.bitcast(plsc.load_gather(coef_v, [zero_idx + (coef_off + r)]), BF16)
                    cw1 = plsc.bitcast(plsc.load_gather(coef_v, [zero_idx + (coef_off + r + 1)]), BF16)
                    for i in range(kblock):
                        w0 = plsc.bitcast(buf[r, pl.ds(col0 + i * lanes, lanes)], BF16)
                        w1 = plsc.bitcast(buf[r + 1, pl.ds(col0 + i * lanes, lanes)], BF16)
                        p = plsc.bitcast(w0 * cw0 + w1 * cw1, jnp.int32)
                        accs[i] = accs[i] + lax.bitcast_convert_type(p << 16, F32)
                        accs[kblock + i] = accs[kblock + i] + lax.bitcast_convert_type(p, F32)
                    return tuple(accs)

                accs = lax.fori_loop(0, gw // 2, rbody, init)
                for i in range(kblock):
                    out_v[t_loc, pl.ds(col0 + i * lanes, lanes)] = accs[i]
                    out_v[t_loc, pl.ds(c + col0 + i * lanes, lanes)] = accs[kblock + i]

        @pl.loop(0, per_worker // ich)
        def _(g):
            cb = base + g * ich
            pltpu.sync_copy(idx_hbm.at[pl.ds(cb, ich)], idx_v)
            pltpu.sync_copy(coef_hbm.at[pl.ds(cb, ich)], coef_v)
            for q in range(nbuf):
                gather(q, rows[q], sems[q]).start()

            @pl.loop(0, tok)
            def _(j):
                for q in range(nbuf):
                    gather(nbuf * j + q, rows[q], sems[q]).wait()
                    accumulate_window(rows[q], j, j * nsel + q * gw, q == 0)

                    @pl.when(j + 1 < tok)
                    def _():
                        gather(nbuf * (j + 1) + q, rows[q], sems[q]).start()

            tok_base = pl.multiple_of(wid * (per_worker // nsel) + g * tok, tok)
            pltpu.sync_copy(out_v, out_hbm.at[pl.ds(tok_base, tok)])

    return mix(table, idx, coef)


def _peer_tail(xm, hn, et, wt, tabs, gfin, y_acc, row_off):
    u_tab, v_tab = tabs
    t, nsel = et.shape
    eidx = et.reshape(t * nsel)
    act = _expert_dots(u_tab, eidx, _pack_pairs(hn), nsel).reshape(t, nsel)
    coef = _expert_coefs(act, wt).reshape(t * nsel)
    mixed = _expert_mix(v_tab, eidx, coef, nsel)
    return _finish(xm, mixed, gfin, y_acc, row_off)


def kernel(x_prompt, x_sample, cache_k_window, cache_v_window, state_gla, meta_tokens, g_norm_mix, w_in,
           w_gate_up, b_gate, attn_sinks, g_gla_norm, w_branch_a, w_branch_b, w_out, g_norm_ffn, w_peer_q,
           peer_sub_keys, peer_u, peer_v, g_norm_final):
    bsz, seq, d = x_prompt.shape
    dbsz, tdec, _ = x_sample.shape
    n_meta = meta_tokens.shape[0]
    depth = w_in.shape[0]
    window = cache_k_window.shape[2]
    kv_heads, head_dim = cache_k_window.shape[3], cache_k_window.shape[4]
    gate_rank = w_gate_up.shape[1]
    bqk = w_gate_up.shape[2]
    n_ph, _, n_keys, p_half = peer_sub_keys.shape[1:]
    assert depth == 1 and d == 1024 and window == ATTN_BLOCK and kv_heads == 2 and head_dim == 64
    assert bqk == 256 and state_gla.shape[2:] == (4, 64, 128) and n_meta <= ATTN_BLOCK
    assert seq % ATTN_BLOCK == 0 and tdec <= SAMPLE_PAD and n_keys == 128 and p_half == 64 and n_ph == 8
    rope_dim = head_dim // 4
    meta_pad = ATTN_BLOCK - n_meta
    lp = ATTN_BLOCK + seq
    nblk = lp // ATTN_BLOCK

    w = w_in[0]
    c_lr = 2304
    c_gate = c_lr + gate_rank
    w1 = w[:, :c_lr].astype(BF16)
    wlr = jnp.pad(w[:, c_lr:c_gate], ((0, 0), (0, LANES - gate_rank))).astype(BF16)
    w2 = w[:, c_gate:].astype(BF16)
    wgu = jnp.pad(w_gate_up[0], ((0, LANES - gate_rank), (0, 0))).astype(BF16)
    bg = b_gate[0][None, :]
    gmix = g_norm_mix[0][None, :]
    wa = w_branch_a[0].astype(BF16)
    wb = w_branch_b[0].astype(BF16)
    wo = w_out[0].astype(BF16)
    gffn = g_norm_ffn[0][None, :]
    wq = w_peer_q[0].astype(BF16)
    keys = peer_sub_keys[0].reshape(n_ph * 2, n_keys, p_half).astype(BF16)
    u_tab = _pack_pairs(peer_u[0])
    gfin = g_norm_final[None, :]
    gn = g_gla_norm[0][None, :]
    sinks = attn_sinks[0]
    qk_scale = float(bqk // 4) ** -0.5

    rows_p = jnp.arange(lp)
    tab_p = _rope_table(rows_p - meta_pad, rows_p >= meta_pad, rope_dim, head_dim)
    proj_rows = max(r for r in range(16, PROJ_ROWS + 1, 16) if lp % r == 0)
    nq = nblk - 1
    nchunks = lp // GLA_CHUNK
    skip = ATTN_BLOCK // GLA_CHUNK
    ncq = nchunks - skip
    per_seq = seq // MERGE_ROWS
    gt_map_p = lambda i: ((i // per_seq) * nblk + 1 + (i % per_seq), 0)

    meta = jnp.broadcast_to(meta_tokens[None].astype(x_prompt.dtype), (bsz, n_meta, d))
    xpad = jnp.concatenate([jnp.zeros((bsz, meta_pad, d), x_prompt.dtype), meta, x_prompt], axis=1)
    xpad = xpad.reshape(bsz * lp, d)
    x_rows = x_prompt.reshape(bsz * seq, d)

    def mix_and_route(b0, gb):
        qa, kv, gl, gt = _project(xpad, gmix, tab_p, w1, wlr, wgu, bg, w2, proj_rows, qk_scale, b0 * lp, gb * lp)
        ya = _attention(
            sinks, qa, kv, kv, gb, nq, ATTN_BLOCK,
            lambda b, n: (b * nblk + n + 1, 0), lambda b, n: (b * nblk + n, 0), lambda b, n: (b * nblk + n + 1, 0),
            lambda b, n: (b * nq + n, 0), gb * seq, first_valid_key=meta_pad, block_offset=1)
        s0 = jnp.zeros((gb,) + state_gla.shape[2:], F32)
        yb, s_fin = _gla(gl, s0, gn, gb, nchunks, GLA_CHUNK,
                         lambda b, c: (b * nchunks + c, 0),
                         lambda b, c: (b * ncq + jnp.maximum(c - skip, 0), 0), gb * seq)
        xm, hn, et, wt = _merge_route(x_rows, gt, ya, yb, wa, wb, wo, gffn, wq, keys, gt_map_p, b0 * seq)
        kv_w = kv.reshape(gb, lp, 2, kv_heads, head_dim)[:, lp - window:]
        return (xm, hn, et, wt), kv_w, s_fin

    group = PROMPT_GROUP if (bsz % PROMPT_GROUP == 0 and (PROMPT_GROUP * seq) % GATHER_ROW_QUANTUM == 0) else bsz
    y_acc, tabs, kv_parts, s_parts = None, None, [], []
    for b0 in range(0, bsz, group):
        routed, kv_w, s_fin = mix_and_route(b0, group)
        if tabs is None:
            anchor = (routed[2][0, 0] * 0).astype(F32)
            tabs = (u_tab, _pack_pairs(peer_v[0] + anchor))
            y_acc = jnp.zeros((bsz * seq, d), F32) + anchor
        y_acc = _peer_tail(*routed, tabs, gfin, y_acc, b0 * seq)
        kv_parts.append(kv_w)
        s_parts.append(s_fin)
    y_prompt = y_acc.reshape(bsz, seq, d)
    kv_p = jnp.concatenate(kv_parts, axis=0)
    s_fin_p = jnp.concatenate(s_parts, axis=0)
    new_k_p = kv_p[:, :, 0][None]
    new_v_p = kv_p[:, :, 1][None]

    sp = SAMPLE_PAD
    xs_pad = jnp.pad(x_sample, ((0, 0), (0, sp - tdec), (0, 0))).reshape(dbsz * sp, d)
    rows_s = jnp.arange(sp)
    reps = 256 // sp
    tab_s = jnp.tile(_rope_table(PAST_LEN + rows_s, rows_s < tdec, rope_dim, head_dim), (reps, 1))
    qa_s, kv_s, gl_s, gt_s = _project(xs_pad, gmix, tab_s, w1, wlr, wgu, bg, w2, 256, qk_scale, 0, dbsz * sp)

    cache_kv = jnp.concatenate([cache_k_window[0].reshape(dbsz * window, kv_heads * head_dim),
                                cache_v_window[0].reshape(dbsz * window, kv_heads * head_dim)], axis=1)
    seq_map = lambda b, n: (b, 0)
    ya_s = _attention(sinks, qa_s, cache_kv, kv_s, dbsz, 1, sp, seq_map, seq_map, seq_map, seq_map,
                      dbsz * sp, first_valid_key=None, block_offset=0)
    yb_s, s_fin_s = _gla(gl_s, state_gla[0], gn, dbsz, 1, sp, seq_map, seq_map, dbsz * sp)

    def real_rows(a):
        return a.reshape(dbsz, sp, a.shape[-1])[:, :tdec].reshape(dbsz * tdec, a.shape[-1])

    xs_rows = x_sample.reshape(dbsz * tdec, d)
    xm_s, hn_s, et_s, wt_s = _merge_route(xs_rows, real_rows(gt_s), real_rows(ya_s), real_rows(yb_s),
                                          wa, wb, wo, gffn, wq, keys, lambda i: (i, 0), 0)
    after_prompt = (lax.bitcast_convert_type(y_acc[0, 0], jnp.int32) * 0).astype(F32)
    y_sample = _peer_tail(xm_s, hn_s + after_prompt, et_s, wt_s, tabs, gfin,
                          jnp.zeros((dbsz * tdec, d), F32), 0).reshape(dbsz, tdec, d)

    kv_new = real_rows(kv_s).reshape(dbsz, tdec, 2, kv_heads, head_dim)
    new_k_s = jnp.concatenate([cache_k_window[0].astype(F32), kv_new[:, :, 0]], axis=1)[:, -window:][None]
    new_v_s = jnp.concatenate([cache_v_window[0].astype(F32), kv_new[:, :, 1]], axis=1)[:, -window:][None]

    return (y_prompt, y_sample, new_k_p, new_v_p, s_fin_p[None], new_k_s, new_v_s, s_fin_s[None])
```

```python
import functools
import math

import jax
import jax.numpy as jnp
from jax import lax
from jax.experimental import pallas as pl
from jax.experimental.pallas import tpu as pltpu
from jax.experimental.pallas import tpu_sc as plsc

F32 = jnp.float32
BF16 = jnp.bfloat16

EPS = 1e-6
NEG_INF = -1e30
PAST_LEN = 16384
ROPE_THETA = 500000.0
GATE_NORMALIZER = 16.0
PEER_TOPK = 16

LANES = 128
SUBLANES = 8
VMEM_LIMIT_BYTES = 56 * 1024 * 1024

ATTN_BLOCK = 128
GLA_CHUNK = 64
SAMPLE_PAD = 16
PROJ_ROWS = 544
MERGE_ROWS = 128
GATHER_WINDOW = 32
GATHER_BUFFERS = 4
DOTS_INDEX_CHUNK = 8192
MIX_INDEX_CHUNK = 4096
GATHER_ROW_QUANTUM = 512
PROMPT_GROUP = 1


def _cparams(sem):
    return pltpu.CompilerParams(dimension_semantics=sem, vmem_limit_bytes=VMEM_LIMIT_BYTES)


def _rms(x, g):
    ms = jnp.mean(x * x, axis=-1, keepdims=True)
    return (x * lax.rsqrt(ms + EPS)) * g


def _proj_kernel(x_ref, g_ref, tab_ref, w1_ref, wlr_ref, wgu_ref, bg_ref, w2_ref,
                 qa_ref, kv_ref, gl_ref, gt_ref, *, period, qk_scale):
    i = pl.program_id(0)
    tr = x_ref.shape[0]
    hb = _rms(x_ref[...], g_ref[...]).astype(BF16)
    z1 = jnp.dot(hb, w1_ref[...], preferred_element_type=F32)

    start = pl.multiple_of((i * tr) % period, SUBLANES)
    tab = tab_ref[pl.ds(start, tr), :]
    cosf = tab[:, 0:LANES]
    sin_lo = tab[:, LANES:2 * LANES]
    sin_hi = tab[:, 2 * LANES:3 * LANES]
    valid = tab[:, 3 * LANES:3 * LANES + 1]

    def rope(xg):
        return xg * cosf + pltpu.roll(xg, 8, 1) * sin_lo + pltpu.roll(xg, LANES - 8, 1) * sin_hi

    for gi in range(4):
        sl = slice(gi * LANES, (gi + 1) * LANES)
        qa_ref[:, sl] = rope(z1[:, sl]).astype(BF16)
    kv_ref[:, 0:LANES] = rope(z1[:, 512:640])
    kv_ref[:, LANES:2 * LANES] = z1[:, 640:768]

    lr = jnp.dot(hb, wlr_ref[...], preferred_element_type=F32)
    pre = jnp.dot(lr.astype(BF16), wgu_ref[...], preferred_element_type=F32) + bg_ref[...]
    log_sig = jnp.minimum(pre, 0.0) - jnp.log1p(jnp.exp(-jnp.abs(pre)))
    ld = jnp.where(valid > 0.5, log_sig / GATE_NORMALIZER, 0.0)

    gl_ref[:, 0:256] = z1[:, 768:1024] * qk_scale
    gl_ref[:, 256:512] = z1[:, 1024:1280]
    gl_ref[:, 512:768] = ld
    gl_ref[:, 768:1792] = z1[:, 1280:2304]
    gt_ref[...] = jnp.dot(hb, w2_ref[...], preferred_element_type=F32)


def _project(x, g, tab, w1, wlr, wgu, bg, w2, rows, qk_scale, first_row, r):
    d = x.shape[1]
    period = tab.shape[0]
    const = lambda i: (0, 0)
    row = lambda i: (i, 0)
    first = first_row // rows
    return pl.pallas_call(
        functools.partial(_proj_kernel, period=period, qk_scale=qk_scale),
        grid=(r // rows,),
        in_specs=[
            pl.BlockSpec((rows, d), lambda i: (i + first, 0)),
            pl.BlockSpec(g.shape, const),
            pl.BlockSpec(tab.shape, const),
            pl.BlockSpec(w1.shape, const),
            pl.BlockSpec(wlr.shape, const),
            pl.BlockSpec(wgu.shape, const),
            pl.BlockSpec(bg.shape, const),
            pl.BlockSpec(w2.shape, const),
        ],
        out_specs=[
            pl.BlockSpec((rows, 512), row),
            pl.BlockSpec((rows, 256), row),
            pl.BlockSpec((rows, 1792), row),
            pl.BlockSpec((rows, 2048), row),
        ],
        out_shape=[
            jax.ShapeDtypeStruct((r, 512), BF16),
            jax.ShapeDtypeStruct((r, 256), F32),
            jax.ShapeDtypeStruct((r, 1792), F32),
            jax.ShapeDtypeStruct((r, 2048), F32),
        ],
        compiler_params=_cparams(("arbitrary",)),
        name="proj",
    )(x, g, tab, w1, wlr, wgu, bg, w2)


def _attn_kernel(sink_ref, q_ref, prev_ref, cur_ref, o_ref, *, first_valid_key, block_offset):
    n = pl.program_id(1)
    qr = q_ref.shape[0]
    kr = cur_ref.shape[0]
    w = prev_ref.shape[0]
    nk = w + kr
    group = 4
    hd = 64

    rows = lax.broadcasted_iota(jnp.int32, (group * qr, nk), 0)
    cols = lax.broadcasted_iota(jnp.int32, (group * qr, nk), 1)
    head_of_row = rows // qr
    diff = (rows - head_of_row * qr) - cols + w
    mask = (diff >= 0) & (diff <= w)
    if first_valid_key is not None:
        blk = n + block_offset
        mask = mask & (cols >= first_valid_key + w - blk * w)

    prev = prev_ref[...]
    cur = cur_ref[...]
    q = q_ref[...]
    row_head = lax.broadcasted_iota(jnp.int32, (group * qr, 1), 0) // qr
    for kh in range(2):
        k = jnp.concatenate([prev[:, kh * hd:(kh + 1) * hd], cur[:, kh * hd:(kh + 1) * hd]], axis=0).astype(BF16)
        v = jnp.concatenate([prev[:, LANES + kh * hd:LANES + (kh + 1) * hd],
                             cur[:, LANES + kh * hd:LANES + (kh + 1) * hd]], axis=0).astype(BF16)
        qs = jnp.concatenate([q[:, (group * kh + g) * hd:(group * kh + g + 1) * hd] for g in range(group)], axis=0)
        s = lax.dot_general(qs, k, (((1,), (1,)), ((), ())), preferred_element_type=F32) * (hd ** -0.5)
        s = jnp.where(mask, s, NEG_INF)
        sink = jnp.zeros((group * qr, 1), F32)
        for g in range(group):
            sink = jnp.where(row_head == g, sink_ref[group * kh + g], sink)
        m = jnp.maximum(jnp.max(s, axis=-1, keepdims=True), sink)
        e = jnp.exp(s - m)
        p = e / (jnp.sum(e, axis=-1, keepdims=True) + jnp.exp(sink - m))
        o = jnp.dot(p.astype(BF16), v, preferred_element_type=F32)
        for g in range(group):
            h = group * kh + g
            o_ref[:, h * hd:(h + 1) * hd] = o[g * qr:(g + 1) * qr].astype(BF16)


def _attention(sinks, q, kv_prev, kv_cur, nb, nblk, qr, q_map, prev_map, cur_map, out_map, out_rows,
               first_valid_key, block_offset):
    w = ATTN_BLOCK
    return pl.pallas_call(
        functools.partial(_attn_kernel, first_valid_key=first_valid_key, block_offset=block_offset),
        grid=(nb, nblk),
        in_specs=[
            pl.BlockSpec(memory_space=pltpu.SMEM),
            pl.BlockSpec((qr, 512), q_map),
            pl.BlockSpec((w, 256), prev_map),
            pl.BlockSpec((qr, 256), cur_map),
        ],
        out_specs=pl.BlockSpec((qr, 512), out_map),
        out_shape=jax.ShapeDtypeStruct((out_rows, 512), BF16),
        compiler_params=_cparams(("arbitrary", "arbitrary")),
        name="swa",
    )(sinks, q, kv_prev, kv_cur)


def _gla_kernel(gl_ref, s0_ref, gn_ref, yb_ref, sfin_ref, st_ref):
    c = pl.program_id(1)
    ch = gl_ref.shape[0]
    nh, dk, dv = 4, 64, 128

    @pl.when(c == 0)
    def _():
        for h in range(nh):
            st_ref[h] = s0_ref[0, h].T

    gl = gl_ref[...]
    q = gl[:, 0:256]
    k = gl[:, 256:512]
    b = gl[:, 512:768]
    row = lax.broadcasted_iota(jnp.int32, (ch, nh * dk), 0)
    sh = 1
    while sh < ch:
        b = b + jnp.where(row >= sh, pltpu.roll(b, sh, 0), 0.0)
        sh *= 2
    b_last = b[ch - 1:ch, :]
    q_t = (q * jnp.exp(b)).astype(BF16)
    k_t = (k * jnp.exp(-b)).astype(BF16)
    k_end = (k * jnp.exp(b_last - b)).astype(BF16)
    decay = jnp.exp(b_last)
    causal = (lax.broadcasted_iota(jnp.int32, (ch, ch), 0) >= lax.broadcasted_iota(jnp.int32, (ch, ch), 1))
    gn = gn_ref[...]
    nt = (((1,), (1,)), ((), ()))
    for h in range(nh):
        ks = slice(h * dk, (h + 1) * dk)
        v = gl[:, 768 + h * dv:768 + (h + 1) * dv]
        vb = v.astype(BF16)
        a = lax.dot_general(q_t[:, ks], k_t[:, ks], nt, preferred_element_type=F32)
        a = jnp.where(causal, a, 0.0)
        s_t = st_ref[h]
        o = jnp.dot(a.astype(BF16), vb, preferred_element_type=F32)
        o = o + lax.dot_general(q_t[:, ks], s_t.astype(BF16), nt, preferred_element_type=F32)
        upd = jnp.dot(v.T.astype(BF16), k_end[:, ks], preferred_element_type=F32)
        st_ref[h] = s_t * decay[:, ks] + upd
        go = gl[:, 1280 + h * dv:1280 + (h + 1) * dv]
        y = _rms(o, gn) * (go * jax.nn.sigmoid(go))
        yb_ref[:, h * dv:(h + 1) * dv] = y.astype(BF16)

    @pl.when(c == pl.num_programs(1) - 1)
    def _():
        for h in range(nh):
            sfin_ref[0, h] = st_ref[h].T


def _gla(gl, s0, gn, nb, nchunks, ch, in_map, out_map, out_rows):
    return pl.pallas_call(
        _gla_kernel,
        grid=(nb, nchunks),
        in_specs=[
            pl.BlockSpec((ch, 1792), in_map),
            pl.BlockSpec((1, 4, 64, 128), lambda b, c: (b, 0, 0, 0)),
            pl.BlockSpec((1, 128), lambda b, c: (0, 0)),
        ],
        out_specs=[
            pl.BlockSpec((ch, 512), out_map),
            pl.BlockSpec((1, 4, 64, 128), lambda b, c: (b, 0, 0, 0)),
        ],
        out_shape=[
            jax.ShapeDtypeStruct((out_rows, 512), BF16),
            jax.ShapeDtypeStruct((nb, 4, 64, 128), F32),
        ],
        scratch_shapes=[pltpu.VMEM((4, 128, 64), F32)],
        compiler_params=_cparams(("arbitrary", "arbitrary")),
        name="gla",
    )(gl, s0, gn)


def _extract_topk(work, nsel, iota0, sentinel):
    slabs = work.shape[0] // SUBLANES
    vals, idxs = [], []
    for j in range(nsel):
        v = [work[i * SUBLANES:(i + 1) * SUBLANES] for i in range(slabs)]
        ix = [iota0[i * SUBLANES:(i + 1) * SUBLANES] for i in range(slabs)]
        while len(v) > 1:
            keep = [v[i] >= v[i + 1] for i in range(0, len(v), 2)]
            ix = [jnp.where(k, ix[2 * i], ix[2 * i + 1]) for i, k in enumerate(keep)]
            v = [jnp.where(k, v[2 * i], v[2 * i + 1]) for i, k in enumerate(keep)]
        m = jnp.max(v[0], axis=0, keepdims=True)
        idx = jnp.min(jnp.where(v[0] == m, ix[0], sentinel), axis=0, keepdims=True)
        vals.append(m)
        idxs.append(idx)
        if j + 1 < nsel:
            work = jnp.where(iota0 == idx, -jnp.inf, work)
    return vals, idxs


def _merge_kernel(x_ref, gt_ref, ya_ref, yb_ref, wa_ref, wb_ref, wo_ref, gf_ref, wq_ref, keys_ref,
                  xm_ref, hn_ref, et_ref, wt_ref):
    td = x_ref.shape[0]
    nkeys = keys_ref.shape[1]
    half = keys_ref.shape[2]
    nheads = keys_ref.shape[0] // 2
    topk = PEER_TOPK

    gt = gt_ref[...]
    d = x_ref.shape[1]
    ma = jnp.dot(ya_ref[...], wa_ref[...], preferred_element_type=F32)
    mb = jnp.dot(yb_ref[...], wb_ref[...], preferred_element_type=F32)
    m = jax.nn.sigmoid(gt[:, 0:d]) * ma + jax.nn.sigmoid(gt[:, d:2 * d]) * mb
    xm = x_ref[...] + jnp.dot(m.astype(BF16), wo_ref[...], preferred_element_type=F32)
    xm_ref[...] = xm
    hn = _rms(xm, gf_ref[...])
    hn_ref[...] = hn
    q = jnp.dot(hn.astype(BF16), wq_ref[...], preferred_element_type=F32).astype(BF16)

    nt = (((1,), (1,)), ((), ()))
    iota_k = lax.broadcasted_iota(jnp.int32, (nkeys, td), 0)
    pair_rows = [topk // (a + 1) for a in range(topk)]
    cand_pad = -sum(pair_rows) % SUBLANES
    ncand = sum(pair_rows) + cand_pad
    iota_c = lax.broadcasted_iota(jnp.int32, (ncand, td), 0)
    wts, ids = [], []
    for h in range(nheads):
        sv, si = [], []
        for c in range(2):
            gi = 2 * h + c
            s_t = lax.dot_general(keys_ref[gi], q[:, gi * half:(gi + 1) * half], nt,
                                  preferred_element_type=F32)
            vals, idxs = _extract_topk(s_t, topk, iota_k, nkeys)
            sv.append(vals)
            si.append(idxs)
        sv1 = jnp.concatenate(sv[1], axis=0)
        si1 = jnp.concatenate(si[1], axis=0)
        cand = jnp.concatenate([sv[0][a] + sv1[0:nb] for a, nb in enumerate(pair_rows)]
                               + [jnp.full((cand_pad, td), -jnp.inf, F32)], axis=0)
        cidx = jnp.concatenate([si[0][a] * nkeys + si1[0:nb] for a, nb in enumerate(pair_rows)]
                               + [jnp.full((cand_pad, td), -1, jnp.int32)], axis=0)
        fvals, eids = [], []
        work = cand
        for j in range(topk):
            mx = jnp.max(work, axis=0, keepdims=True)
            pos = jnp.min(jnp.where(work == mx, iota_c, ncand), axis=0, keepdims=True)
            hit = iota_c == pos
            eids.append(jnp.max(jnp.where(hit, cidx, -1), axis=0, keepdims=True))
            fvals.append(mx)
            if j + 1 < topk:
                work = jnp.where(hit, -jnp.inf, work)
        fv = jnp.concatenate(fvals, axis=0)
        e = jnp.exp(fv - fvals[0])
        wts.append(e / jnp.sum(e, axis=0, keepdims=True))
        ids.extend(eids)
    wt_ref[...] = jnp.concatenate(wts, axis=0).T
    et_ref[...] = jnp.concatenate(ids, axis=0).T


def _merge_route(x, gt, ya, yb, wa, wb, wo, gf, wq, keys, gt_map, x_first_row):
    t = ya.shape[0]
    d = x.shape[1]
    td = MERGE_ROWS
    nsel = (keys.shape[0] // 2) * PEER_TOPK
    const2 = lambda i: (0, 0)
    row = lambda i: (i, 0)
    x_first = x_first_row // td
    return pl.pallas_call(
        _merge_kernel,
        grid=(t // td,),
        in_specs=[
            pl.BlockSpec((td, d), lambda i: (i + x_first, 0)),
            pl.BlockSpec((td, 2 * d), gt_map),
            pl.BlockSpec((td, ya.shape[1]), row),
            pl.BlockSpec((td, yb.shape[1]), row),
            pl.BlockSpec(wa.shape, const2),
            pl.BlockSpec(wb.shape, const2),
            pl.BlockSpec(wo.shape, const2),
            pl.BlockSpec(gf.shape, const2),
            pl.BlockSpec(wq.shape, const2),
            pl.BlockSpec(keys.shape, lambda i: (0, 0, 0)),
        ],
        out_specs=[
            pl.BlockSpec((td, d), row),
            pl.BlockSpec((td, d), row),
            pl.BlockSpec((td, nsel), row),
            pl.BlockSpec((td, nsel), row),
        ],
        out_shape=[
            jax.ShapeDtypeStruct((t, d), F32),
            jax.ShapeDtypeStruct((t, d), F32),
            jax.ShapeDtypeStruct((t, nsel), jnp.int32),
            jax.ShapeDtypeStruct((t, nsel), F32),
        ],
        compiler_params=_cparams(("arbitrary",)),
        name="merge_route",
    )(x, gt, ya, yb, wa, wb, wo, gf, wq, keys)


def _coef_kernel(act_ref, wt_ref, o_ref):
    act = act_ref[...]
    gelu = 0.5 * act * (1.0 + lax.erf(act * (2.0 ** -0.5)))
    bits = pltpu.bitcast((wt_ref[...] * gelu).astype(BF16).astype(F32), jnp.int32)
    o_ref[...] = bits | lax.shift_right_logical(bits, 16)


def _expert_coefs(act, wt):
    t, nsel = wt.shape
    rows = math.gcd(t, 512)
    row = lambda i: (i, 0)
    return pl.pallas_call(
        _coef_kernel,
        grid=(t // rows,),
        in_specs=[pl.BlockSpec((rows, nsel), row), pl.BlockSpec((rows, nsel), row)],
        out_specs=pl.BlockSpec((rows, nsel), row),
        out_shape=jax.ShapeDtypeStruct((t, nsel), jnp.int32),
        compiler_params=_cparams(("arbitrary",)),
        name="expert_coefs",
    )(act, wt)


def _finish_kernel(xm_ref, o_ref, gfin_ref, yacc_hbm, y_ref):
    del yacc_hbm
    y_ref[...] = _rms(xm_ref[...] + o_ref[...], gfin_ref[...])


def _finish(xm, o, gfin, y_acc, row_off):
    t, d = xm.shape
    rows = math.gcd(t, 512)
    first = row_off // rows
    row = lambda i: (i, 0)
    return pl.pallas_call(
        _finish_kernel,
        grid=(t // rows,),
        in_specs=[pl.BlockSpec((rows, d), row), pl.BlockSpec((rows, d), row),
                  pl.BlockSpec(gfin.shape, lambda i: (0, 0)), pl.BlockSpec(memory_space=pl.ANY)],
        out_specs=pl.BlockSpec((rows, d), lambda i: (i + first, 0)),
        out_shape=jax.ShapeDtypeStruct(y_acc.shape, F32),
        input_output_aliases={3: 0},
        compiler_params=_cparams(("arbitrary",)),
        name="finish",
    )(xm, o, gfin, y_acc)


def _rope_table(pos, valid, rope_dim, head_dim):
    half = rope_dim // 2
    inv = ROPE_THETA ** (-jnp.arange(0, rope_dim, 2, dtype=F32) / rope_dim)
    ang = pos.astype(F32)[:, None] * inv[None, :]
    cos, sin = jnp.cos(ang), jnp.sin(ang)
    n = pos.shape[0]
    ones = jnp.ones((n, head_dim - rope_dim), F32)
    zeros_h = jnp.zeros((n, half), F32)
    zeros_r = jnp.zeros((n, head_dim - rope_dim), F32)
    reps = LANES // head_dim
    cosf = jnp.tile(jnp.concatenate([cos, cos, ones], axis=1), (1, reps))
    sin_lo = jnp.tile(jnp.concatenate([zeros_h, sin, zeros_r], axis=1), (1, reps))
    sin_hi = jnp.tile(jnp.concatenate([-sin, zeros_h, zeros_r], axis=1), (1, reps))
    vcol = jnp.broadcast_to(valid.astype(F32)[:, None], (n, LANES))
    return jnp.concatenate([cosf, sin_lo, sin_hi, vcol], axis=1)


def _pack_pairs(x):
    half = x.shape[1] // 2
    lo = lax.bitcast_convert_type(x[:, :half].astype(BF16), jnp.uint16).astype(jnp.uint32)
    hi = lax.bitcast_convert_type(x[:, half:].astype(BF16), jnp.uint16).astype(jnp.uint32)
    return lax.bitcast_convert_type(lo | (hi << 16), jnp.int32)


def _expert_dots(table, idx, hn, nsel):
    n = idx.shape[0]
    c = table.shape[1]
    sc = plsc.get_sparse_core_info()
    lanes = sc.num_lanes
    workers = sc.num_cores * sc.num_subcores
    gw = GATHER_WINDOW
    nbuf = GATHER_BUFFERS
    per_worker = n // workers
    ich = min(DOTS_INDEX_CHUNK, per_worker)
    tok = ich // nsel
    assert n % workers == 0 and per_worker % ich == 0 and nsel == nbuf * gw and hn.shape[1] == c and gw % lanes == 0
    mesh = plsc.VectorSubcoreMesh(core_axis_name="c", subcore_axis_name="s")

    @functools.partial(
        pl.kernel, out_type=jax.ShapeDtypeStruct((n,), F32), mesh=mesh, name="expert_dots",
        compiler_params=pltpu.CompilerParams(needs_layout_passes=False),
        scratch_types=[pltpu.VMEM((ich,), jnp.int32), pltpu.VMEM((tok, c), jnp.int32), pltpu.VMEM((ich,), F32),
                       pltpu.VMEM((lanes * lanes,), F32)]
                      + [pltpu.VMEM((gw, c), jnp.int32)] * nbuf + [pltpu.SemaphoreType.DMA] * nbuf)
    def dots(tab_hbm, idx_hbm, hn_hbm, act_hbm, idx_v, h_v, act_v, scr, *bufs_sems):
        rows, sems = bufs_sems[:nbuf], bufs_sems[nbuf:]
        wid = lax.axis_index("s") * sc.num_cores + lax.axis_index("c")
        base = wid * per_worker
        lane = lax.iota(jnp.int32, lanes)

        def gather(win, buf, sem):
            return pltpu.make_async_copy(tab_hbm.at[idx_v.at[pl.ds(win * gw, gw)]], buf, sem)

        def reduce_window(buf, t_loc, out_off):
            for rb in range(gw // lanes):
                def kbody(k2, accs):
                    k0 = 2 * k2 * lanes
                    hw0 = plsc.bitcast(h_v[t_loc, pl.ds(k0, lanes)], BF16)
                    hw1 = plsc.bitcast(h_v[t_loc, pl.ds(k0 + lanes, lanes)], BF16)
                    out = []
                    for r in range(lanes):
                        w0 = plsc.bitcast(buf[rb * lanes + r, pl.ds(k0, lanes)], BF16)
                        w1 = plsc.bitcast(buf[rb * lanes + r, pl.ds(k0 + lanes, lanes)], BF16)
                        p = plsc.bitcast(w0 * hw0 + w1 * hw1, jnp.int32)
                        out.append(accs[r] + lax.bitcast_convert_type(p << 16, F32)
                                   + lax.bitcast_convert_type(p, F32))
                    return tuple(out)

                accs = lax.fori_loop(0, c // (2 * lanes), kbody,
                                     tuple(jnp.zeros((lanes,), F32) for _ in range(lanes)))
                for r in range(lanes):
                    scr[pl.ds(r * lanes, lanes)] = accs[r]
                cols = [plsc.load_gather(scr, [lane * lanes + l]) for l in range(lanes)]
                while len(cols) > 1:
                    cols = [cols[i] + cols[i + 1] for i in range(0, len(cols), 2)]
                act_v[pl.ds(out_off + rb * lanes, lanes)] = cols[0]

        @pl.loop(0, per_worker // ich)
        def _(g):
            cb = base + g * ich
            pltpu.sync_copy(idx_hbm.at[pl.ds(cb, ich)], idx_v)
            tok_base = pl.multiple_of(wid * (per_worker // nsel) + g * tok, tok)
            pltpu.sync_copy(hn_hbm.at[pl.ds(tok_base, tok)], h_v)
            for q in range(nbuf):
                gather(q, rows[q], sems[q]).start()

            @pl.loop(0, tok)
            def _(j):
                for q in range(nbuf):
                    gather(nbuf * j + q, rows[q], sems[q]).wait()
                    reduce_window(rows[q], j, j * nsel + q * gw)

                    @pl.when(j + 1 < tok)
                    def _():
                        gather(nbuf * (j + 1) + q, rows[q], sems[q]).start()

            pltpu.sync_copy(act_v, act_hbm.at[pl.ds(cb, ich)])

    return dots(table, idx, hn)


def _expert_mix(table, idx, coef, nsel):
    n = idx.shape[0]
    c = table.shape[1]
    d = 2 * c
    sc = plsc.get_sparse_core_info()
    lanes = sc.num_lanes
    workers = sc.num_cores * sc.num_subcores
    gw = GATHER_WINDOW
    nbuf = GATHER_BUFFERS
    per_worker = n // workers
    ich = min(MIX_INDEX_CHUNK, per_worker)
    tok = ich // nsel
    kblock = 16
    assert n % workers == 0 and per_worker % ich == 0 and nsel == nbuf * gw and c % (kblock * lanes) == 0
    mesh = plsc.VectorSubcoreMesh(core_axis_name="c", subcore_axis_name="s")

    @functools.partial(
        pl.kernel, out_type=jax.ShapeDtypeStruct((n // nsel, d), F32), mesh=mesh, name="expert_mix",
        compiler_params=pltpu.CompilerParams(needs_layout_passes=False),
        scratch_types=[pltpu.VMEM((ich,), jnp.int32), pltpu.VMEM((ich,), jnp.int32), pltpu.VMEM((tok, d), F32)]
                      + [pltpu.VMEM((gw, c), jnp.int32)] * nbuf + [pltpu.SemaphoreType.DMA] * nbuf)
    def mix(tab_hbm, idx_hbm, coef_hbm, out_hbm, idx_v, coef_v, out_v, *bufs_sems):
        rows, sems = bufs_sems[:nbuf], bufs_sems[nbuf:]
        wid = lax.axis_index("s") * sc.num_cores + lax.axis_index("c")
        base = wid * per_worker
        zero_idx = jnp.zeros((lanes,), jnp.int32)

        def gather(win, buf, sem):
            return pltpu.make_async_copy(tab_hbm.at[idx_v.at[pl.ds(win * gw, gw)]], buf, sem)

        def accumulate_window(buf, t_loc, coef_off, first):
            for kb in range(c // (kblock * lanes)):
                col0 = kb * kblock * lanes
                if first:
                    init = tuple(jnp.zeros((lanes,), F32) for _ in range(2 * kblock))
                else:
                    init = tuple(out_v[t_loc, pl.ds(col0 + i * lanes, lanes)] for i in range(kblock)) + \
                           tuple(out_v[t_loc, pl.ds(c + col0 + i * lanes, lanes)] for i in range(kblock))

                def rbody(r2, accs):
                    accs = list(accs)
                    r = 2 * r2
                    cw0 = plsc.bitcast(plsc.load_gather(coef_v, [zero_idx + (coef_off + r)]), BF16)
                    cw1 = plsc.bitcast(plsc.load_gather(coef_v, [zero_idx + (coef_off + r + 1)]), BF16)
                    for i in range(kblock):
                        w0 = plsc.bitcast(buf[r, pl.ds(col0 + i * lanes, lanes)], BF16)
                        w1 = plsc.bitcast(buf[r + 1, pl.ds(col0 + i * lanes, lanes)], BF16)
                        p = plsc.bitcast(w0 * cw0 + w1 * cw1, jnp.int32)
                        accs[i] = accs[i] + lax.bitcast_convert_type(p << 16, F32)
                        accs[kblock + i] = accs[kblock + i] + lax.bitcast_convert_type(p, F32)
                    return tuple(accs)

                accs = lax.fori_loop(0, gw // 2, rbody, init)
                for i in range(kblock):
                    out_v[t_loc, pl.ds(col0 + i * lanes, lanes)] = accs[i]
                    out_v[t_loc, pl.ds(c + col0 + i * lanes, lanes)] = accs[kblock + i]

        @pl.loop(0, per_worker // ich)
        def _(g):
            cb = base + g * ich
            pltpu.sync_copy(idx_hbm.at[pl.ds(cb, ich)], idx_v)
            pltpu.sync_copy(coef_hbm.at[pl.ds(cb, ich)], coef_v)
            for q in range(nbuf):
                gather(q, rows[q], sems[q]).start()

            @pl.loop(0, tok)
            def _(j):
                for q in range(nbuf):
                    gather(nbuf * j + q, rows[q], sems[q]).wait()
                    accumulate_window(rows[q], j, j * nsel + q * gw, q == 0)

                    @pl.when(j + 1 < tok)
                    def _():
                        gather(nbuf * (j + 1) + q, rows[q], sems[q]).start()

            tok_base = pl.multiple_of(wid * (per_worker // nsel) + g * tok, tok)
            pltpu.sync_copy(out_v, out_hbm.at[pl.ds(tok_base, tok)])

    return mix(table, idx, coef)


def _peer_tail(xm, hn, et, wt, tabs, gfin, y_acc, row_off):
    u_tab, v_tab = tabs
    t, nsel = et.shape
    eidx = et.reshape(t * nsel)
    act = _expert_dots(u_tab, eidx, _pack_pairs(hn), nsel).reshape(t, nsel)
    coef = _expert_coefs(act, wt).reshape(t * nsel)
    mixed = _expert_mix(v_tab, eidx, coef, nsel)
    return _finish(xm, mixed, gfin, y_acc, row_off)


def kernel(x_prompt, x_sample, cache_k_window, cache_v_window, state_gla, meta_tokens, g_norm_mix, w_in,
           w_gate_up, b_gate, attn_sinks, g_gla_norm, w_branch_a, w_branch_b, w_out, g_norm_ffn, w_peer_q,
           peer_sub_keys, peer_u, peer_v, g_norm_final):
    bsz, seq, d = x_prompt.shape
    dbsz, tdec, _ = x_sample.shape
    n_meta = meta_tokens.shape[0]
    depth = w_in.shape[0]
    window = cache_k_window.shape[2]
    kv_heads, head_dim = cache_k_window.shape[3], cache_k_window.shape[4]
    gate_rank = w_gate_up.shape[1]
    bqk = w_gate_up.shape[2]
    n_ph, _, n_keys, p_half = peer_sub_keys.shape[1:]
    assert depth == 1 and d == 1024 and window == ATTN_BLOCK and kv_heads == 2 and head_dim == 64
    assert bqk == 256 and state_gla.shape[2:] == (4, 64, 128) and n_meta <= ATTN_BLOCK
    assert seq % ATTN_BLOCK == 0 and tdec <= SAMPLE_PAD and n_keys == 128 and p_half == 64 and n_ph == 8
    rope_dim = head_dim // 4
    meta_pad = ATTN_BLOCK - n_meta
    lp = ATTN_BLOCK + seq
    nblk = lp // ATTN_BLOCK

    w = w_in[0]
    c_lr = 2304
    c_gate = c_lr + gate_rank
    w1 = w[:, :c_lr].astype(BF16)
    wlr = jnp.pad(w[:, c_lr:c_gate], ((0, 0), (0, LANES - gate_rank))).astype(BF16)
    w2 = w[:, c_gate:].astype(BF16)
    wgu = jnp.pad(w_gate_up[0], ((0, LANES - gate_rank), (0, 0))).astype(BF16)
    bg = b_gate[0][None, :]
    gmix = g_norm_mix[0][None, :]
    wa = w_branch_a[0].astype(BF16)
    wb = w_branch_b[0].astype(BF16)
    wo = w_out[0].astype(BF16)
    gffn = g_norm_ffn[0][None, :]
    wq = w_peer_q[0].astype(BF16)
    keys = peer_sub_keys[0].reshape(n_ph * 2, n_keys, p_half).astype(BF16)
    u_tab = _pack_pairs(peer_u[0])
    gfin = g_norm_final[None, :]
    gn = g_gla_norm[0][None, :]
    sinks = attn_sinks[0]
    qk_scale = float(bqk // 4) ** -0.5

    rows_p = jnp.arange(lp)
    tab_p = _rope_table(rows_p - meta_pad, rows_p >= meta_pad, rope_dim, head_dim)
    proj_rows = max(r for r in range(16, PROJ_ROWS + 1, 16) if lp % r == 0)
    nq = nblk - 1
    nchunks = lp // GLA_CHUNK
    skip = ATTN_BLOCK // GLA_CHUNK
    ncq = nchunks - skip
    per_seq = seq // MERGE_ROWS
    gt_map_p = lambda i: ((i // per_seq) * nblk + 1 + (i % per_seq), 0)

    meta = jnp.broadcast_to(meta_tokens[None].astype(x_prompt.dtype), (bsz, n_meta, d))
    xpad = jnp.concatenate([jnp.zeros((bsz, meta_pad, d), x_prompt.dtype), meta, x_prompt], axis=1)
    xpad = xpad.reshape(bsz * lp, d)
    x_rows = x_prompt.reshape(bsz * seq, d)

    def mix_and_route(b0, gb):
        qa, kv, gl, gt = _project(xpad, gmix, tab_p, w1, wlr, wgu, bg, w2, proj_rows, qk_scale, b0 * lp, gb * lp)
        ya = _attention(
            sinks, qa, kv, kv, gb, nq, ATTN_BLOCK,
            lambda b, n: (b * nblk + n + 1, 0), lambda b, n: (b * nblk + n, 0), lambda b, n: (b * nblk + n + 1, 0),
            lambda b, n: (b * nq + n, 0), gb * seq, first_valid_key=meta_pad, block_offset=1)
        s0 = jnp.zeros((gb,) + state_gla.shape[2:], F32)
        yb, s_fin = _gla(gl, s0, gn, gb, nchunks, GLA_CHUNK,
                         lambda b, c: (b * nchunks + c, 0),
                         lambda b, c: (b * ncq + jnp.maximum(c - skip, 0), 0), gb * seq)
        xm, hn, et, wt = _merge_route(x_rows, gt, ya, yb, wa, wb, wo, gffn, wq, keys, gt_map_p, b0 * seq)
        kv_w = kv.reshape(gb, lp, 2, kv_heads, head_dim)[:, lp - window:]
        return (xm, hn, et, wt), kv_w, s_fin

    group = PROMPT_GROUP if (bsz % PROMPT_GROUP == 0 and (PROMPT_GROUP * seq) % GATHER_ROW_QUANTUM == 0) else bsz
    tabs = (u_tab, _pack_pairs(peer_v[0]))
    y_acc, kv_parts, s_parts = jnp.zeros((bsz * seq, d), F32), [], []
    for b0 in range(0, bsz, group):
        routed, kv_w, s_fin = mix_and_route(b0, group)
        y_acc = _peer_tail(*routed, tabs, gfin, y_acc, b0 * seq)
        kv_parts.append(kv_w)
        s_parts.append(s_fin)
    y_prompt = y_acc.reshape(bsz, seq, d)
    kv_p = jnp.concatenate(kv_parts, axis=0)
    s_fin_p = jnp.concatenate(s_parts, axis=0)
    new_k_p = kv_p[:, :, 0][None]
    new_v_p = kv_p[:, :, 1][None]

    sp = SAMPLE_PAD
    xs_pad = jnp.pad(x_sample, ((0, 0), (0, sp - tdec), (0, 0))).reshape(dbsz * sp, d)
    rows_s = jnp.arange(sp)
    reps = 256 // sp
    tab_s = jnp.tile(_rope_table(PAST_LEN + rows_s, rows_s < tdec, rope_dim, head_dim), (reps, 1))
    qa_s, kv_s, gl_s, gt_s = _project(xs_pad, gmix, tab_s, w1, wlr, wgu, bg, w2, 256, qk_scale, 0, dbsz * sp)

    cache_kv = jnp.concatenate([cache_k_window[0].reshape(dbsz * window, kv_heads * head_dim),
                                cache_v_window[0].reshape(dbsz * window, kv_heads * head_dim)], axis=1)
    seq_map = lambda b, n: (b, 0)
    ya_s = _attention(sinks, qa_s, cache_kv, kv_s, dbsz, 1, sp, seq_map, seq_map, seq_map, seq_map,
                      dbsz * sp, first_valid_key=None, block_offset=0)
    yb_s, s_fin_s = _gla(gl_s, state_gla[0], gn, dbsz, 1, sp, seq_map, seq_map, dbsz * sp)

    def real_rows(a):
        return a.reshape(dbsz, sp, a.shape[-1])[:, :tdec].reshape(dbsz * tdec, a.shape[-1])

    xs_rows = x_sample.reshape(dbsz * tdec, d)
    xm_s, hn_s, et_s, wt_s = _merge_route(xs_rows, real_rows(gt_s), real_rows(ya_s), real_rows(yb_s),
                                          wa, wb, wo, gffn, wq, keys, lambda i: (i, 0), 0)
    hn_s, _ = lax.optimization_barrier((hn_s, y_acc))
    y_sample = _peer_tail(xm_s, hn_s, et_s, wt_s, tabs, gfin,
                          jnp.zeros((dbsz * tdec, d), F32), 0).reshape(dbsz, tdec, d)

    kv_new = real_rows(kv_s).reshape(dbsz, tdec, 2, kv_heads, head_dim)
    new_k_s = jnp.concatenate([cache_k_window[0].astype(F32), kv_new[:, :, 0]], axis=1)[:, -window:][None]
    new_v_s = jnp.concatenate([cache_v_window[0].astype(F32), kv_new[:, :, 1]], axis=1)[:, -window:][None]

    return (y_prompt, y_sample, new_k_p, new_v_p, s_fin_p[None], new_k_s, new_v_s, s_fin_s[None])
```

```python
import functools
import math

import jax
import jax.numpy as jnp
from jax import lax
from jax.experimental import pallas as pl
from jax.experimental.pallas import tpu as pltpu
from jax.experimental.pallas import tpu_sc as plsc

F32 = jnp.float32
BF16 = jnp.bfloat16

EPS = 1e-6
NEG_INF = -1e30
PAST_LEN = 16384
ROPE_THETA = 500000.0
GATE_NORMALIZER = 16.0
PEER_TOPK = 16

LANES = 128
SUBLANES = 8
VMEM_LIMIT_BYTES = 56 * 1024 * 1024

ATTN_BLOCK = 128
GLA_CHUNK = 64
SAMPLE_PAD = 16
PROJ_ROWS = 544
MERGE_ROWS = 128
GATHER_WINDOW = 32
GATHER_BUFFERS = 4
DOTS_INDEX_CHUNK = 8192
MIX_INDEX_CHUNK = 4096
GATHER_ROW_QUANTUM = 512
PROMPT_GROUP = 1


def _cparams(sem):
    return pltpu.CompilerParams(dimension_semantics=sem, vmem_limit_bytes=VMEM_LIMIT_BYTES)


def _rms(x, g):
    ms = jnp.mean(x * x, axis=-1, keepdims=True)
    return (x * lax.rsqrt(ms + EPS)) * g


def _proj_kernel(x_ref, g_ref, tab_ref, w1_ref, wlr_ref, wgu_ref, bg_ref, w2_ref,
                 qa_ref, kv_ref, gl_ref, gt_ref, *, period, qk_scale):
    i = pl.program_id(0)
    tr = x_ref.shape[0]
    hb = _rms(x_ref[...], g_ref[...]).astype(BF16)
    z1 = jnp.dot(hb, w1_ref[...], preferred_element_type=F32)

    start = pl.multiple_of((i * tr) % period, SUBLANES)
    tab = tab_ref[pl.ds(start, tr), :]
    cosf = tab[:, 0:LANES]
    sin_lo = tab[:, LANES:2 * LANES]
    sin_hi = tab[:, 2 * LANES:3 * LANES]
    valid = tab[:, 3 * LANES:3 * LANES + 1]

    def rope(xg):
        return xg * cosf + pltpu.roll(xg, 8, 1) * sin_lo + pltpu.roll(xg, LANES - 8, 1) * sin_hi

    for gi in range(4):
        sl = slice(gi * LANES, (gi + 1) * LANES)
        qa_ref[:, sl] = rope(z1[:, sl]).astype(BF16)
    kv_ref[:, 0:LANES] = rope(z1[:, 512:640])
    kv_ref[:, LANES:2 * LANES] = z1[:, 640:768]

    lr = jnp.dot(hb, wlr_ref[...], preferred_element_type=F32)
    pre = jnp.dot(lr.astype(BF16), wgu_ref[...], preferred_element_type=F32) + bg_ref[...]
    log_sig = jnp.minimum(pre, 0.0) - jnp.log1p(jnp.exp(-jnp.abs(pre)))
    ld = jnp.where(valid > 0.5, log_sig / GATE_NORMALIZER, 0.0)

    gl_ref[:, 0:256] = z1[:, 768:1024] * qk_scale
    gl_ref[:, 256:512] = z1[:, 1024:1280]
    gl_ref[:, 512:768] = ld
    gl_ref[:, 768:1792] = z1[:, 1280:2304]
    gt_ref[...] = jnp.dot(hb, w2_ref[...], preferred_element_type=F32)


def _project(x, g, tab, w1, wlr, wgu, bg, w2, rows, qk_scale, first_row, r):
    d = x.shape[1]
    period = tab.shape[0]
    const = lambda i: (0, 0)
    row = lambda i: (i, 0)
    first = first_row // rows
    return pl.pallas_call(
        functools.partial(_proj_kernel, period=period, qk_scale=qk_scale),
        grid=(r // rows,),
        in_specs=[
            pl.BlockSpec((rows, d), lambda i: (i + first, 0)),
            pl.BlockSpec(g.shape, const),
            pl.BlockSpec(tab.shape, const),
            pl.BlockSpec(w1.shape, const),
            pl.BlockSpec(wlr.shape, const),
            pl.BlockSpec(wgu.shape, const),
            pl.BlockSpec(bg.shape, const),
            pl.BlockSpec(w2.shape, const),
        ],
        out_specs=[
            pl.BlockSpec((rows, 512), row),
            pl.BlockSpec((rows, 256), row),
            pl.BlockSpec((rows, 1792), row),
            pl.BlockSpec((rows, 2048), row),
        ],
        out_shape=[
            jax.ShapeDtypeStruct((r, 512), BF16),
            jax.ShapeDtypeStruct((r, 256), F32),
            jax.ShapeDtypeStruct((r, 1792), F32),
            jax.ShapeDtypeStruct((r, 2048), F32),
        ],
        compiler_params=_cparams(("arbitrary",)),
        name="proj",
    )(x, g, tab, w1, wlr, wgu, bg, w2)


def _attn_kernel(sink_ref, q_ref, prev_ref, cur_ref, o_ref, *, first_valid_key, block_offset):
    n = pl.program_id(1)
    qr = q_ref.shape[0]
    kr = cur_ref.shape[0]
    w = prev_ref.shape[0]
    nk = w + kr
    group = 4
    hd = 64

    rows = lax.broadcasted_iota(jnp.int32, (group * qr, nk), 0)
    cols = lax.broadcasted_iota(jnp.int32, (group * qr, nk), 1)
    head_of_row = rows // qr
    diff = (rows - head_of_row * qr) - cols + w
    mask = (diff >= 0) & (diff <= w)
    if first_valid_key is not None:
        blk = n + block_offset
        mask = mask & (cols >= first_valid_key + w - blk * w)

    prev = prev_ref[...]
    cur = cur_ref[...]
    q = q_ref[...]
    row_head = lax.broadcasted_iota(jnp.int32, (group * qr, 1), 0) // qr
    for kh in range(2):
        k = jnp.concatenate([prev[:, kh * hd:(kh + 1) * hd], cur[:, kh * hd:(kh + 1) * hd]], axis=0).astype(BF16)
        v = jnp.concatenate([prev[:, LANES + kh * hd:LANES + (kh + 1) * hd],
                             cur[:, LANES + kh * hd:LANES + (kh + 1) * hd]], axis=0).astype(BF16)
        qs = jnp.concatenate([q[:, (group * kh + g) * hd:(group * kh + g + 1) * hd] for g in range(group)], axis=0)
        s = lax.dot_general(qs, k, (((1,), (1,)), ((), ())), preferred_element_type=F32) * (hd ** -0.5)
        s = jnp.where(mask, s, NEG_INF)
        sink = jnp.zeros((group * qr, 1), F32)
        for g in range(group):
            sink = jnp.where(row_head == g, sink_ref[group * kh + g], sink)
        m = jnp.maximum(jnp.max(s, axis=-1, keepdims=True), sink)
        e = jnp.exp(s - m)
        p = e / (jnp.sum(e, axis=-1, keepdims=True) + jnp.exp(sink - m))
        o = jnp.dot(p.astype(BF16), v, preferred_element_type=F32)
        for g in range(group):
            h = group * kh + g
            o_ref[:, h * hd:(h + 1) * hd] = o[g * qr:(g + 1) * qr].astype(BF16)


def _attention(sinks, q, kv_prev, kv_cur, nb, nblk, qr, q_map, prev_map, cur_map, out_map, out_rows,
               first_valid_key, block_offset):
    w = ATTN_BLOCK
    return pl.pallas_call(
        functools.partial(_attn_kernel, first_valid_key=first_valid_key, block_offset=block_offset),
        grid=(nb, nblk),
        in_specs=[
            pl.BlockSpec(memory_space=pltpu.SMEM),
            pl.BlockSpec((qr, 512), q_map),
            pl.BlockSpec((w, 256), prev_map),
            pl.BlockSpec((qr, 256), cur_map),
        ],
        out_specs=pl.BlockSpec((qr, 512), out_map),
        out_shape=jax.ShapeDtypeStruct((out_rows, 512), BF16),
        compiler_params=_cparams(("arbitrary", "arbitrary")),
        name="swa",
    )(sinks, q, kv_prev, kv_cur)


def _gla_kernel(gl_ref, s0_ref, gn_ref, yb_ref, sfin_ref, st_ref):
    c = pl.program_id(1)
    ch = gl_ref.shape[0]
    nh, dk, dv = 4, 64, 128

    @pl.when(c == 0)
    def _():
        for h in range(nh):
            st_ref[h] = s0_ref[0, h].T

    gl = gl_ref[...]
    q = gl[:, 0:256]
    k = gl[:, 256:512]
    b = gl[:, 512:768]
    row = lax.broadcasted_iota(jnp.int32, (ch, nh * dk), 0)
    sh = 1
    while sh < ch:
        b = b + jnp.where(row >= sh, pltpu.roll(b, sh, 0), 0.0)
        sh *= 2
    b_last = b[ch - 1:ch, :]
    q_t = (q * jnp.exp(b)).astype(BF16)
    k_t = (k * jnp.exp(-b)).astype(BF16)
    k_end = (k * jnp.exp(b_last - b)).astype(BF16)
    decay = jnp.exp(b_last)
    causal = (lax.broadcasted_iota(jnp.int32, (ch, ch), 0) >= lax.broadcasted_iota(jnp.int32, (ch, ch), 1))
    gn = gn_ref[...]
    nt = (((1,), (1,)), ((), ()))
    for h in range(nh):
        ks = slice(h * dk, (h + 1) * dk)
        v = gl[:, 768 + h * dv:768 + (h + 1) * dv]
        vb = v.astype(BF16)
        a = lax.dot_general(q_t[:, ks], k_t[:, ks], nt, preferred_element_type=F32)
        a = jnp.where(causal, a, 0.0)
        s_t = st_ref[h]
        o = jnp.dot(a.astype(BF16), vb, preferred_element_type=F32)
        o = o + lax.dot_general(q_t[:, ks], s_t.astype(BF16), nt, preferred_element_type=F32)
        upd = jnp.dot(v.T.astype(BF16), k_end[:, ks], preferred_element_type=F32)
        st_ref[h] = s_t * decay[:, ks] + upd
        go = gl[:, 1280 + h * dv:1280 + (h + 1) * dv]
        y = _rms(o, gn) * (go * jax.nn.sigmoid(go))
        yb_ref[:, h * dv:(h + 1) * dv] = y.astype(BF16)

    @pl.when(c == pl.num_programs(1) - 1)
    def _():
        for h in range(nh):
            sfin_ref[0, h] = st_ref[h].T


def _gla(gl, s0, gn, nb, nchunks, ch, in_map, out_map, out_rows):
    return pl.pallas_call(
        _gla_kernel,
        grid=(nb, nchunks),
        in_specs=[
            pl.BlockSpec((ch, 1792), in_map),
            pl.BlockSpec((1, 4, 64, 128), lambda b, c: (b, 0, 0, 0)),
            pl.BlockSpec((1, 128), lambda b, c: (0, 0)),
        ],
        out_specs=[
            pl.BlockSpec((ch, 512), out_map),
            pl.BlockSpec((1, 4, 64, 128), lambda b, c: (b, 0, 0, 0)),
        ],
        out_shape=[
            jax.ShapeDtypeStruct((out_rows, 512), BF16),
            jax.ShapeDtypeStruct((nb, 4, 64, 128), F32),
        ],
        scratch_shapes=[pltpu.VMEM((4, 128, 64), F32)],
        compiler_params=_cparams(("arbitrary", "arbitrary")),
        name="gla",
    )(gl, s0, gn)


def _extract_topk(work, nsel, iota0, sentinel):
    slabs = work.shape[0] // SUBLANES
    vals, idxs = [], []
    for j in range(nsel):
        v = [work[i * SUBLANES:(i + 1) * SUBLANES] for i in range(slabs)]
        ix = [iota0[i * SUBLANES:(i + 1) * SUBLANES] for i in range(slabs)]
        while len(v) > 1:
            keep = [v[i] >= v[i + 1] for i in range(0, len(v), 2)]
            ix = [jnp.where(k, ix[2 * i], ix[2 * i + 1]) for i, k in enumerate(keep)]
            v = [jnp.where(k, v[2 * i], v[2 * i + 1]) for i, k in enumerate(keep)]
        m = jnp.max(v[0], axis=0, keepdims=True)
        idx = jnp.min(jnp.where(v[0] == m, ix[0], sentinel), axis=0, keepdims=True)
        vals.append(m)
        idxs.append(idx)
        if j + 1 < nsel:
            work = jnp.where(iota0 == idx, -jnp.inf, work)
    return vals, idxs


def _merge_kernel(x_ref, gt_ref, ya_ref, yb_ref, wa_ref, wb_ref, wo_ref, gf_ref, wq_ref, keys_ref,
                  xm_ref, hn_ref, et_ref, wt_ref):
    td = x_ref.shape[0]
    nkeys = keys_ref.shape[1]
    half = keys_ref.shape[2]
    nheads = keys_ref.shape[0] // 2
    topk = PEER_TOPK

    gt = gt_ref[...]
    d = x_ref.shape[1]
    ma = jnp.dot(ya_ref[...], wa_ref[...], preferred_element_type=F32)
    mb = jnp.dot(yb_ref[...], wb_ref[...], preferred_element_type=F32)
    m = jax.nn.sigmoid(gt[:, 0:d]) * ma + jax.nn.sigmoid(gt[:, d:2 * d]) * mb
    xm = x_ref[...] + jnp.dot(m.astype(BF16), wo_ref[...], preferred_element_type=F32)
    xm_ref[...] = xm
    hn = _rms(xm, gf_ref[...])
    hn_ref[...] = hn
    q = jnp.dot(hn.astype(BF16), wq_ref[...], preferred_element_type=F32).astype(BF16)

    nt = (((1,), (1,)), ((), ()))
    iota_k = lax.broadcasted_iota(jnp.int32, (nkeys, td), 0)
    pair_rows = [topk // (a + 1) for a in range(topk)]
    cand_pad = -sum(pair_rows) % SUBLANES
    ncand = sum(pair_rows) + cand_pad
    iota_c = lax.broadcasted_iota(jnp.int32, (ncand, td), 0)
    wts, ids = [], []
    for h in range(nheads):
        sv, si = [], []
        for c in range(2):
            gi = 2 * h + c
            s_t = lax.dot_general(keys_ref[gi], q[:, gi * half:(gi + 1) * half], nt,
                                  preferred_element_type=F32)
            vals, idxs = _extract_topk(s_t, topk, iota_k, nkeys)
            sv.append(vals)
            si.append(idxs)
        sv1 = jnp.concatenate(sv[1], axis=0)
        si1 = jnp.concatenate(si[1], axis=0)
        cand = jnp.concatenate([sv[0][a] + sv1[0:nb] for a, nb in enumerate(pair_rows)]
                               + [jnp.full((cand_pad, td), -jnp.inf, F32)], axis=0)
        cidx = jnp.concatenate([si[0][a] * nkeys + si1[0:nb] for a, nb in enumerate(pair_rows)]
                               + [jnp.full((cand_pad, td), -1, jnp.int32)], axis=0)
        fvals, eids = [], []
        work = cand
        for j in range(topk):
            mx = jnp.max(work, axis=0, keepdims=True)
            pos = jnp.min(jnp.where(work == mx, iota_c, ncand), axis=0, keepdims=True)
            hit = iota_c == pos
            eids.append(jnp.max(jnp.where(hit, cidx, -1), axis=0, keepdims=True))
            fvals.append(mx)
            if j + 1 < topk:
                work = jnp.where(hit, -jnp.inf, work)
        fv = jnp.concatenate(fvals, axis=0)
        e = jnp.exp(fv - fvals[0])
        wts.append(e / jnp.sum(e, axis=0, keepdims=True))
        ids.extend(eids)
    wt_ref[...] = jnp.concatenate(wts, axis=0).T
    et_ref[...] = jnp.concatenate(ids, axis=0).T


def _merge_route(x, gt, ya, yb, wa, wb, wo, gf, wq, keys, gt_map, x_first_row):
    t = ya.shape[0]
    d = x.shape[1]
    td = MERGE_ROWS
    nsel = (keys.shape[0] // 2) * PEER_TOPK
    const2 = lambda i: (0, 0)
    row = lambda i: (i, 0)
    x_first = x_first_row // td
    return pl.pallas_call(
        _merge_kernel,
        grid=(t // td,),
        in_specs=[
            pl.BlockSpec((td, d), lambda i: (i + x_first, 0)),
            pl.BlockSpec((td, 2 * d), gt_map),
            pl.BlockSpec((td, ya.shape[1]), row),
            pl.BlockSpec((td, yb.shape[1]), row),
            pl.BlockSpec(wa.shape, const2),
            pl.BlockSpec(wb.shape, const2),
            pl.BlockSpec(wo.shape, const2),
            pl.BlockSpec(gf.shape, const2),
            pl.BlockSpec(wq.shape, const2),
            pl.BlockSpec(keys.shape, lambda i: (0, 0, 0)),
        ],
        out_specs=[
            pl.BlockSpec((td, d), row),
            pl.BlockSpec((td, d), row),
            pl.BlockSpec((td, nsel), row),
            pl.BlockSpec((td, nsel), row),
        ],
        out_shape=[
            jax.ShapeDtypeStruct((t, d), F32),
            jax.ShapeDtypeStruct((t, d), F32),
            jax.ShapeDtypeStruct((t, nsel), jnp.int32),
            jax.ShapeDtypeStruct((t, nsel), F32),
        ],
        compiler_params=_cparams(("arbitrary",)),
        name="merge_route",
    )(x, gt, ya, yb, wa, wb, wo, gf, wq, keys)


def _coef_kernel(act_ref, wt_ref, o_ref):
    act = act_ref[...]
    gelu = 0.5 * act * (1.0 + lax.erf(act * (2.0 ** -0.5)))
    bits = pltpu.bitcast((wt_ref[...] * gelu).astype(BF16).astype(F32), jnp.int32)
    o_ref[...] = bits | lax.shift_right_logical(bits, 16)


def _expert_coefs(act, wt):
    t, nsel = wt.shape
    rows = math.gcd(t, 512)
    row = lambda i: (i, 0)
    return pl.pallas_call(
        _coef_kernel,
        grid=(t // rows,),
        in_specs=[pl.BlockSpec((rows, nsel), row), pl.BlockSpec((rows, nsel), row)],
        out_specs=pl.BlockSpec((rows, nsel), row),
        out_shape=jax.ShapeDtypeStruct((t, nsel), jnp.int32),
        compiler_params=_cparams(("arbitrary",)),
        name="expert_coefs",
    )(act, wt)


def _finish_kernel(xm_ref, o_ref, gfin_ref, yacc_hbm, y_ref):
    del yacc_hbm
    y_ref[...] = _rms(xm_ref[...] + o_ref[...], gfin_ref[...])


def _finish(xm, o, gfin, y_acc, row_off):
    t, d = xm.shape
    rows = math.gcd(t, 512)
    first = row_off // rows
    row = lambda i: (i, 0)
    return pl.pallas_call(
        _finish_kernel,
        grid=(t // rows,),
        in_specs=[pl.BlockSpec((rows, d), row), pl.BlockSpec((rows, d), row),
                  pl.BlockSpec(gfin.shape, lambda i: (0, 0)), pl.BlockSpec(memory_space=pl.ANY)],
        out_specs=pl.BlockSpec((rows, d), lambda i: (i + first, 0)),
        out_shape=jax.ShapeDtypeStruct(y_acc.shape, F32),
        input_output_aliases={3: 0},
        compiler_params=_cparams(("arbitrary",)),
        name="finish",
    )(xm, o, gfin, y_acc)


def _rope_table(pos, valid, rope_dim, head_dim):
    half = rope_dim // 2
    inv = ROPE_THETA ** (-jnp.arange(0, rope_dim, 2, dtype=F32) / rope_dim)
    ang = pos.astype(F32)[:, None] * inv[None, :]
    cos, sin = jnp.cos(ang), jnp.sin(ang)
    n = pos.shape[0]
    ones = jnp.ones((n, head_dim - rope_dim), F32)
    zeros_h = jnp.zeros((n, half), F32)
    zeros_r = jnp.zeros((n, head_dim - rope_dim), F32)
    reps = LANES // head_dim
    cosf = jnp.tile(jnp.concatenate([cos, cos, ones], axis=1), (1, reps))
    sin_lo = jnp.tile(jnp.concatenate([zeros_h, sin, zeros_r], axis=1), (1, reps))
    sin_hi = jnp.tile(jnp.concatenate([-sin, zeros_h, zeros_r], axis=1), (1, reps))
    vcol = jnp.broadcast_to(valid.astype(F32)[:, None], (n, LANES))
    return jnp.concatenate([cosf, sin_lo, sin_hi, vcol], axis=1)


def _pack_pairs(x):
    half = x.shape[1] // 2
    lo = lax.bitcast_convert_type(x[:, :half].astype(BF16), jnp.uint16).astype(jnp.uint32)
    hi = lax.bitcast_convert_type(x[:, half:].astype(BF16), jnp.uint16).astype(jnp.uint32)
    return lax.bitcast_convert_type(lo | (hi << 16), jnp.int32)


def _expert_dots(table, idx, hn, nsel):
    n = idx.shape[0]
    c = table.shape[1]
    sc = plsc.get_sparse_core_info()
    lanes = sc.num_lanes
    workers = sc.num_cores * sc.num_subcores
    gw = GATHER_WINDOW
    nbuf = GATHER_BUFFERS
    per_worker = n // workers
    ich = min(DOTS_INDEX_CHUNK, per_worker)
    tok = ich // nsel
    assert n % workers == 0 and per_worker % ich == 0 and nsel == nbuf * gw and hn.shape[1] == c and gw % lanes == 0
    mesh = plsc.VectorSubcoreMesh(core_axis_name="c", subcore_axis_name="s")

    @functools.partial(
        pl.kernel, out_type=jax.ShapeDtypeStruct((n,), F32), mesh=mesh, name="expert_dots",
        compiler_params=pltpu.CompilerParams(needs_layout_passes=False),
        scratch_types=[pltpu.VMEM((ich,), jnp.int32), pltpu.VMEM((tok, c), jnp.int32), pltpu.VMEM((ich,), F32),
                       pltpu.VMEM((lanes * lanes,), F32)]
                      + [pltpu.VMEM((gw, c), jnp.int32)] * nbuf + [pltpu.SemaphoreType.DMA] * nbuf)
    def dots(tab_hbm, idx_hbm, hn_hbm, act_hbm, idx_v, h_v, act_v, scr, *bufs_sems):
        rows, sems = bufs_sems[:nbuf], bufs_sems[nbuf:]
        wid = lax.axis_index("s") * sc.num_cores + lax.axis_index("c")
        base = wid * per_worker
        lane = lax.iota(jnp.int32, lanes)

        def gather(win, buf, sem):
            return pltpu.make_async_copy(tab_hbm.at[idx_v.at[pl.ds(win * gw, gw)]], buf, sem)

        def reduce_window(buf, t_loc, out_off):
            for rb in range(gw // lanes):
                def kbody(k2, accs):
                    k0 = 2 * k2 * lanes
                    hw0 = plsc.bitcast(h_v[t_loc, pl.ds(k0, lanes)], BF16)
                    hw1 = plsc.bitcast(h_v[t_loc, pl.ds(k0 + lanes, lanes)], BF16)
                    out = []
                    for r in range(lanes):
                        w0 = plsc.bitcast(buf[rb * lanes + r, pl.ds(k0, lanes)], BF16)
                        w1 = plsc.bitcast(buf[rb * lanes + r, pl.ds(k0 + lanes, lanes)], BF16)
                        p = plsc.bitcast(w0 * hw0 + w1 * hw1, jnp.int32)
                        out.append(accs[r] + lax.bitcast_convert_type(p << 16, F32)
                                   + lax.bitcast_convert_type(p, F32))
                    return tuple(out)

                accs = lax.fori_loop(0, c // (2 * lanes), kbody,
                                     tuple(jnp.zeros((lanes,), F32) for _ in range(lanes)))
                for r in range(lanes):
                    scr[pl.ds(r * lanes, lanes)] = accs[r]
                cols = [plsc.load_gather(scr, [lane * lanes + l]) for l in range(lanes)]
                while len(cols) > 1:
                    cols = [cols[i] + cols[i + 1] for i in range(0, len(cols), 2)]
                act_v[pl.ds(out_off + rb * lanes, lanes)] = cols[0]

        @pl.loop(0, per_worker // ich)
        def _(g):
            cb = base + g * ich
            pltpu.sync_copy(idx_hbm.at[pl.ds(cb, ich)], idx_v)
            tok_base = pl.multiple_of(wid * (per_worker // nsel) + g * tok, tok)
            pltpu.sync_copy(hn_hbm.at[pl.ds(tok_base, tok)], h_v)
            for q in range(nbuf):
                gather(q, rows[q], sems[q]).start()

            @pl.loop(0, tok)
            def _(j):
                for q in range(nbuf):
                    gather(nbuf * j + q, rows[q], sems[q]).wait()
                    reduce_window(rows[q], j, j * nsel + q * gw)

                    @pl.when(j + 1 < tok)
                    def _():
                        gather(nbuf * (j + 1) + q, rows[q], sems[q]).start()

            pltpu.sync_copy(act_v, act_hbm.at[pl.ds(cb, ich)])

    return dots(table, idx, hn)


def _expert_mix(table, idx, coef, nsel):
    n = idx.shape[0]
    c = table.shape[1]
    d = 2 * c
    sc = plsc.get_sparse_core_info()
    lanes = sc.num_lanes
    workers = sc.num_cores * sc.num_subcores
    gw = GATHER_WINDOW
    nbuf = GATHER_BUFFERS
    per_worker = n // workers
    ich = min(MIX_INDEX_CHUNK, per_worker)
    tok = ich // nsel
    kblock = 16
    assert n % workers == 0 and per_worker % ich == 0 and nsel == nbuf * gw and c % (kblock * lanes) == 0
    mesh = plsc.VectorSubcoreMesh(core_axis_name="c", subcore_axis_name="s")

    @functools.partial(
        pl.kernel, out_type=jax.ShapeDtypeStruct((n // nsel, d), F32), mesh=mesh, name="expert_mix",
        compiler_params=pltpu.CompilerParams(needs_layout_passes=False),
        scratch_types=[pltpu.VMEM((ich,), jnp.int32), pltpu.VMEM((ich,), jnp.int32), pltpu.VMEM((tok, d), F32)]
                      + [pltpu.VMEM((gw, c), jnp.int32)] * nbuf + [pltpu.SemaphoreType.DMA] * nbuf)
    def mix(tab_hbm, idx_hbm, coef_hbm, out_hbm, idx_v, coef_v, out_v, *bufs_sems):
        rows, sems = bufs_sems[:nbuf], bufs_sems[nbuf:]
        wid = lax.axis_index("s") * sc.num_cores + lax.axis_index("c")
        base = wid * per_worker
        zero_idx = jnp.zeros((lanes,), jnp.int32)

        def gather(win, buf, sem):
            return pltpu.make_async_copy(tab_hbm.at[idx_v.at[pl.ds(win * gw, gw)]], buf, sem)

        def accumulate_window(buf, t_loc, coef_off, first):
            for kb in range(c // (kblock * lanes)):
                col0 = kb * kblock * lanes
                if first:
                    init = tuple(jnp.zeros((lanes,), F32) for _ in range(2 * kblock))
                else:
                    init = tuple(out_v[t_loc, pl.ds(col0 + i * lanes, lanes)] for i in range(kblock)) + \
                           tuple(out_v[t_loc, pl.ds(c + col0 + i * lanes, lanes)] for i in range(kblock))

                def rbody(r2, accs):
                    accs = list(accs)
                    r = 2 * r2
                    cw0 = plsc.bitcast(plsc.load_gather(coef_v, [zero_idx + (coef_off + r)]), BF16)
                    cw1 = plsc.bitcast(plsc.load_gather(coef_v, [zero_idx + (coef_off + r + 1)]), BF16)
                    for i in range(kblock):
                        w0 = plsc.bitcast(buf[r, pl.ds(col0 + i * lanes, lanes)], BF16)
                        w1 = plsc.bitcast(buf[r + 1, pl.ds(col0 + i * lanes, lanes)], BF16)
                        p = plsc.bitcast(w0 * cw0 + w1 * cw1, jnp.int32)
                        accs[i] = accs[i] + lax.bitcast_convert_type(p << 16, F32)
                        accs[kblock + i] = accs[kblock + i] + lax.bitcast_convert_type(p, F32)
                    return tuple(accs)

                accs = lax.fori_loop(0, gw // 2, rbody, init)
                for i in range(kblock):
                    out_v[t_loc, pl.ds(col0 + i * lanes, lanes)] = accs[i]
                    out_v[t_loc, pl.ds(c + col0 + i * lanes, lanes)] = accs[kblock + i]

        @pl.loop(0, per_worker // ich)
        def _(g):
            cb = base + g * ich
            pltpu.sync_copy(idx_hbm.at[pl.ds(cb, ich)], idx_v)
            pltpu.sync_copy(coef_hbm.at[pl.ds(cb, ich)], coef_v)
            for q in range(nbuf):
                gather(q, rows[q], sems[q]).start()

            @pl.loop(0, tok)
            def _(j):
                for q in range(nbuf):
                    gather(nbuf * j + q, rows[q], sems[q]).wait()
                    accumulate_window(rows[q], j, j * nsel + q * gw, q == 0)

                    @pl.when(j + 1 < tok)
                    def _():
                        gather(nbuf * (j + 1) + q, rows[q], sems[q]).start()

            tok_base = pl.multiple_of(wid * (per_worker // nsel) + g * tok, tok)
            pltpu.sync_copy(out_v, out_hbm.at[pl.ds(tok_base, tok)])

    return mix(table, idx, coef)


def _peer_tail(xm, hn, et, wt, tabs, gfin, y_acc, row_off, after):
    u_tab, v_tab = tabs
    hn, _ = lax.optimization_barrier((hn, after))
    t, nsel = et.shape
    eidx = et.reshape(t * nsel)
    act = _expert_dots(u_tab, eidx, _pack_pairs(hn), nsel).reshape(t, nsel)
    coef = _expert_coefs(act, wt).reshape(t * nsel)
    mixed = _expert_mix(v_tab, eidx, coef, nsel)
    return _finish(xm, mixed, gfin, y_acc, row_off), mixed


def kernel(x_prompt, x_sample, cache_k_window, cache_v_window, state_gla, meta_tokens, g_norm_mix, w_in,
           w_gate_up, b_gate, attn_sinks, g_gla_norm, w_branch_a, w_branch_b, w_out, g_norm_ffn, w_peer_q,
           peer_sub_keys, peer_u, peer_v, g_norm_final):
    bsz, seq, d = x_prompt.shape
    dbsz, tdec, _ = x_sample.shape
    n_meta = meta_tokens.shape[0]
    depth = w_in.shape[0]
    window = cache_k_window.shape[2]
    kv_heads, head_dim = cache_k_window.shape[3], cache_k_window.shape[4]
    gate_rank = w_gate_up.shape[1]
    bqk = w_gate_up.shape[2]
    n_ph, _, n_keys, p_half = peer_sub_keys.shape[1:]
    assert depth == 1 and d == 1024 and window == ATTN_BLOCK and kv_heads == 2 and head_dim == 64
    assert bqk == 256 and state_gla.shape[2:] == (4, 64, 128) and n_meta <= ATTN_BLOCK
    assert seq % ATTN_BLOCK == 0 and tdec <= SAMPLE_PAD and n_keys == 128 and p_half == 64 and n_ph == 8
    rope_dim = head_dim // 4
    meta_pad = ATTN_BLOCK - n_meta
    lp = ATTN_BLOCK + seq
    nblk = lp // ATTN_BLOCK

    w = w_in[0]
    c_lr = 2304
    c_gate = c_lr + gate_rank
    w1 = w[:, :c_lr].astype(BF16)
    wlr = jnp.pad(w[:, c_lr:c_gate], ((0, 0), (0, LANES - gate_rank))).astype(BF16)
    w2 = w[:, c_gate:].astype(BF16)
    wgu = jnp.pad(w_gate_up[0], ((0, LANES - gate_rank), (0, 0))).astype(BF16)
    bg = b_gate[0][None, :]
    gmix = g_norm_mix[0][None, :]
    wa = w_branch_a[0].astype(BF16)
    wb = w_branch_b[0].astype(BF16)
    wo = w_out[0].astype(BF16)
    gffn = g_norm_ffn[0][None, :]
    wq = w_peer_q[0].astype(BF16)
    keys = peer_sub_keys[0].reshape(n_ph * 2, n_keys, p_half).astype(BF16)
    u_tab = _pack_pairs(peer_u[0])
    gfin = g_norm_final[None, :]
    gn = g_gla_norm[0][None, :]
    sinks = attn_sinks[0]
    qk_scale = float(bqk // 4) ** -0.5

    rows_p = jnp.arange(lp)
    tab_p = _rope_table(rows_p - meta_pad, rows_p >= meta_pad, rope_dim, head_dim)
    proj_rows = max(r for r in range(16, PROJ_ROWS + 1, 16) if lp % r == 0)
    nq = nblk - 1
    nchunks = lp // GLA_CHUNK
    skip = ATTN_BLOCK // GLA_CHUNK
    ncq = nchunks - skip
    per_seq = seq // MERGE_ROWS
    gt_map_p = lambda i: ((i // per_seq) * nblk + 1 + (i % per_seq), 0)

    meta = jnp.broadcast_to(meta_tokens[None].astype(x_prompt.dtype), (bsz, n_meta, d))
    xpad = jnp.concatenate([jnp.zeros((bsz, meta_pad, d), x_prompt.dtype), meta, x_prompt], axis=1)
    xpad = xpad.reshape(bsz * lp, d)
    x_rows = x_prompt.reshape(bsz * seq, d)

    def mix_and_route(b0, gb):
        qa, kv, gl, gt = _project(xpad, gmix, tab_p, w1, wlr, wgu, bg, w2, proj_rows, qk_scale, b0 * lp, gb * lp)
        ya = _attention(
            sinks, qa, kv, kv, gb, nq, ATTN_BLOCK,
            lambda b, n: (b * nblk + n + 1, 0), lambda b, n: (b * nblk + n, 0), lambda b, n: (b * nblk + n + 1, 0),
            lambda b, n: (b * nq + n, 0), gb * seq, first_valid_key=meta_pad, block_offset=1)
        s0 = jnp.zeros((gb,) + state_gla.shape[2:], F32)
        yb, s_fin = _gla(gl, s0, gn, gb, nchunks, GLA_CHUNK,
                         lambda b, c: (b * nchunks + c, 0),
                         lambda b, c: (b * ncq + jnp.maximum(c - skip, 0), 0), gb * seq)
        xm, hn, et, wt = _merge_route(x_rows, gt, ya, yb, wa, wb, wo, gffn, wq, keys, gt_map_p, b0 * seq)
        kv_w = kv.reshape(gb, lp, 2, kv_heads, head_dim)[:, lp - window:]
        return (xm, hn, et, wt), kv_w, s_fin

    group = PROMPT_GROUP if (bsz % PROMPT_GROUP == 0 and (PROMPT_GROUP * seq) % GATHER_ROW_QUANTUM == 0) else bsz
    tabs = (u_tab, _pack_pairs(peer_v[0]))
    y_acc, kv_parts, s_parts = jnp.zeros((bsz * seq, d), F32), [], []
    mixes = [u_tab, u_tab]
    for b0 in range(0, bsz, group):
        routed, kv_w, s_fin = mix_and_route(b0, group)
        y_acc, mixed = _peer_tail(*routed, tabs, gfin, y_acc, b0 * seq, mixes[-2])
        mixes.append(mixed)
        kv_parts.append(kv_w)
        s_parts.append(s_fin)
    y_prompt = y_acc.reshape(bsz, seq, d)
    kv_p = jnp.concatenate(kv_parts, axis=0)
    s_fin_p = jnp.concatenate(s_parts, axis=0)
    new_k_p = kv_p[:, :, 0][None]
    new_v_p = kv_p[:, :, 1][None]

    sp = SAMPLE_PAD
    xs_pad = jnp.pad(x_sample, ((0, 0), (0, sp - tdec), (0, 0))).reshape(dbsz * sp, d)
    rows_s = jnp.arange(sp)
    reps = 256 // sp
    tab_s = jnp.tile(_rope_table(PAST_LEN + rows_s, rows_s < tdec, rope_dim, head_dim), (reps, 1))
    qa_s, kv_s, gl_s, gt_s = _project(xs_pad, gmix, tab_s, w1, wlr, wgu, bg, w2, 256, qk_scale, 0, dbsz * sp)

    cache_kv = jnp.concatenate([cache_k_window[0].reshape(dbsz * window, kv_heads * head_dim),
                                cache_v_window[0].reshape(dbsz * window, kv_heads * head_dim)], axis=1)
    seq_map = lambda b, n: (b, 0)
    ya_s = _attention(sinks, qa_s, cache_kv, kv_s, dbsz, 1, sp, seq_map, seq_map, seq_map, seq_map,
                      dbsz * sp, first_valid_key=None, block_offset=0)
    yb_s, s_fin_s = _gla(gl_s, state_gla[0], gn, dbsz, 1, sp, seq_map, seq_map, dbsz * sp)

    def real_rows(a):
        return a.reshape(dbsz, sp, a.shape[-1])[:, :tdec].reshape(dbsz * tdec, a.shape[-1])

    xs_rows = x_sample.reshape(dbsz * tdec, d)
    xm_s, hn_s, et_s, wt_s = _merge_route(xs_rows, real_rows(gt_s), real_rows(ya_s), real_rows(yb_s),
                                          wa, wb, wo, gffn, wq, keys, lambda i: (i, 0), 0)
    y_sample, _ = _peer_tail(xm_s, hn_s, et_s, wt_s, tabs, gfin, jnp.zeros((dbsz * tdec, d), F32), 0, y_acc)
    y_sample = y_sample.reshape(dbsz, tdec, d)

    kv_new = real_rows(kv_s).reshape(dbsz, tdec, 2, kv_heads, head_dim)
    new_k_s = jnp.concatenate([cache_k_window[0].astype(F32), kv_new[:, :, 0]], axis=1)[:, -window:][None]
    new_v_s = jnp.concatenate([cache_v_window[0].astype(F32), kv_new[:, :, 1]], axis=1)[:, -window:][None]

    return (y_prompt, y_sample, new_k_p, new_v_p, s_fin_p[None], new_k_s, new_v_s, s_fin_s[None])
```

```python
import functools
import math

import jax
import jax.numpy as jnp
from jax import lax
from jax.experimental import pallas as pl
from jax.experimental.pallas import tpu as pltpu
from jax.experimental.pallas import tpu_sc as plsc

F32 = jnp.float32
BF16 = jnp.bfloat16

EPS = 1e-6
NEG_INF = -1e30
PAST_LEN = 16384
ROPE_THETA = 500000.0
GATE_NORMALIZER = 16.0
PEER_TOPK = 16

LANES = 128
SUBLANES = 8
VMEM_LIMIT_BYTES = 56 * 1024 * 1024

ATTN_BLOCK = 128
GLA_CHUNK = 64
SAMPLE_PAD = 16
PROJ_ROWS = 544
MERGE_ROWS = 128
GATHER_WINDOW = 16
GATHER_BUFFERS = 8
DOTS_INDEX_CHUNK = 8192
MIX_INDEX_CHUNK = 4096
GATHER_ROW_QUANTUM = 512
PROMPT_GROUP = 1


def _cparams(sem):
    return pltpu.CompilerParams(dimension_semantics=sem, vmem_limit_bytes=VMEM_LIMIT_BYTES)


def _rms(x, g):
    ms = jnp.mean(x * x, axis=-1, keepdims=True)
    return (x * lax.rsqrt(ms + EPS)) * g


def _proj_kernel(x_ref, g_ref, tab_ref, w1_ref, wlr_ref, wgu_ref, bg_ref, w2_ref,
                 qa_ref, kv_ref, gl_ref, gt_ref, *, period, qk_scale):
    i = pl.program_id(0)
    tr = x_ref.shape[0]
    hb = _rms(x_ref[...], g_ref[...]).astype(BF16)
    z1 = jnp.dot(hb, w1_ref[...], preferred_element_type=F32)

    start = pl.multiple_of((i * tr) % period, SUBLANES)
    tab = tab_ref[pl.ds(start, tr), :]
    cosf = tab[:, 0:LANES]
    sin_lo = tab[:, LANES:2 * LANES]
    sin_hi = tab[:, 2 * LANES:3 * LANES]
    valid = tab[:, 3 * LANES:3 * LANES + 1]

    def rope(xg):
        return xg * cosf + pltpu.roll(xg, 8, 1) * sin_lo + pltpu.roll(xg, LANES - 8, 1) * sin_hi

    for gi in range(4):
        sl = slice(gi * LANES, (gi + 1) * LANES)
        qa_ref[:, sl] = rope(z1[:, sl]).astype(BF16)
    kv_ref[:, 0:LANES] = rope(z1[:, 512:640])
    kv_ref[:, LANES:2 * LANES] = z1[:, 640:768]

    lr = jnp.dot(hb, wlr_ref[...], preferred_element_type=F32)
    pre = jnp.dot(lr.astype(BF16), wgu_ref[...], preferred_element_type=F32) + bg_ref[...]
    log_sig = jnp.minimum(pre, 0.0) - jnp.log1p(jnp.exp(-jnp.abs(pre)))
    ld = jnp.where(valid > 0.5, log_sig / GATE_NORMALIZER, 0.0)

    gl_ref[:, 0:256] = z1[:, 768:1024] * qk_scale
    gl_ref[:, 256:512] = z1[:, 1024:1280]
    gl_ref[:, 512:768] = ld
    gl_ref[:, 768:1792] = z1[:, 1280:2304]
    gt_ref[...] = jnp.dot(hb, w2_ref[...], preferred_element_type=F32)


def _project(x, g, tab, w1, wlr, wgu, bg, w2, rows, qk_scale, first_row, r):
    d = x.shape[1]
    period = tab.shape[0]
    const = lambda i: (0, 0)
    row = lambda i: (i, 0)
    first = first_row // rows
    return pl.pallas_call(
        functools.partial(_proj_kernel, period=period, qk_scale=qk_scale),
        grid=(r // rows,),
        in_specs=[
            pl.BlockSpec((rows, d), lambda i: (i + first, 0)),
            pl.BlockSpec(g.shape, const),
            pl.BlockSpec(tab.shape, const),
            pl.BlockSpec(w1.shape, const),
            pl.BlockSpec(wlr.shape, const),
            pl.BlockSpec(wgu.shape, const),
            pl.BlockSpec(bg.shape, const),
            pl.BlockSpec(w2.shape, const),
        ],
        out_specs=[
            pl.BlockSpec((rows, 512), row),
            pl.BlockSpec((rows, 256), row),
            pl.BlockSpec((rows, 1792), row),
            pl.BlockSpec((rows, 2048), row),
        ],
        out_shape=[
            jax.ShapeDtypeStruct((r, 512), BF16),
            jax.ShapeDtypeStruct((r, 256), F32),
            jax.ShapeDtypeStruct((r, 1792), F32),
            jax.ShapeDtypeStruct((r, 2048), F32),
        ],
        compiler_params=_cparams(("arbitrary",)),
        name="proj",
    )(x, g, tab, w1, wlr, wgu, bg, w2)


def _attn_kernel(sink_ref, q_ref, prev_ref, cur_ref, o_ref, *, first_valid_key, block_offset):
    n = pl.program_id(1)
    qr = q_ref.shape[0]
    kr = cur_ref.shape[0]
    w = prev_ref.shape[0]
    nk = w + kr
    group = 4
    hd = 64

    rows = lax.broadcasted_iota(jnp.int32, (group * qr, nk), 0)
    cols = lax.broadcasted_iota(jnp.int32, (group * qr, nk), 1)
    head_of_row = rows // qr
    diff = (rows - head_of_row * qr) - cols + w
    mask = (diff >= 0) & (diff <= w)
    if first_valid_key is not None:
        blk = n + block_offset
        mask = mask & (cols >= first_valid_key + w - blk * w)

    prev = prev_ref[...]
    cur = cur_ref[...]
    q = q_ref[...]
    row_head = lax.broadcasted_iota(jnp.int32, (group * qr, 1), 0) // qr
    for kh in range(2):
        k = jnp.concatenate([prev[:, kh * hd:(kh + 1) * hd], cur[:, kh * hd:(kh + 1) * hd]], axis=0).astype(BF16)
        v = jnp.concatenate([prev[:, LANES + kh * hd:LANES + (kh + 1) * hd],
                             cur[:, LANES + kh * hd:LANES + (kh + 1) * hd]], axis=0).astype(BF16)
        qs = jnp.concatenate([q[:, (group * kh + g) * hd:(group * kh + g + 1) * hd] for g in range(group)], axis=0)
        s = lax.dot_general(qs, k, (((1,), (1,)), ((), ())), preferred_element_type=F32) * (hd ** -0.5)
        s = jnp.where(mask, s, NEG_INF)
        sink = jnp.zeros((group * qr, 1), F32)
        for g in range(group):
            sink = jnp.where(row_head == g, sink_ref[group * kh + g], sink)
        m = jnp.maximum(jnp.max(s, axis=-1, keepdims=True), sink)
        e = jnp.exp(s - m)
        p = e / (jnp.sum(e, axis=-1, keepdims=True) + jnp.exp(sink - m))
        o = jnp.dot(p.astype(BF16), v, preferred_element_type=F32)
        for g in range(group):
            h = group * kh + g
            o_ref[:, h * hd:(h + 1) * hd] = o[g * qr:(g + 1) * qr].astype(BF16)


def _attention(sinks, q, kv_prev, kv_cur, nb, nblk, qr, q_map, prev_map, cur_map, out_map, out_rows,
               first_valid_key, block_offset):
    w = ATTN_BLOCK
    return pl.pallas_call(
        functools.partial(_attn_kernel, first_valid_key=first_valid_key, block_offset=block_offset),
        grid=(nb, nblk),
        in_specs=[
            pl.BlockSpec(memory_space=pltpu.SMEM),
            pl.BlockSpec((qr, 512), q_map),
            pl.BlockSpec((w, 256), prev_map),
            pl.BlockSpec((qr, 256), cur_map),
        ],
        out_specs=pl.BlockSpec((qr, 512), out_map),
        out_shape=jax.ShapeDtypeStruct((out_rows, 512), BF16),
        compiler_params=_cparams(("arbitrary", "arbitrary")),
        name="swa",
    )(sinks, q, kv_prev, kv_cur)


def _gla_kernel(gl_ref, s0_ref, gn_ref, yb_ref, sfin_ref, st_ref):
    c = pl.program_id(1)
    ch = gl_ref.shape[0]
    nh, dk, dv = 4, 64, 128

    @pl.when(c == 0)
    def _():
        for h in range(nh):
            st_ref[h] = s0_ref[0, h].T

    gl = gl_ref[...]
    q = gl[:, 0:256]
    k = gl[:, 256:512]
    b = gl[:, 512:768]
    row = lax.broadcasted_iota(jnp.int32, (ch, nh * dk), 0)
    sh = 1
    while sh < ch:
        b = b + jnp.where(row >= sh, pltpu.roll(b, sh, 0), 0.0)
        sh *= 2
    b_last = b[ch - 1:ch, :]
    q_t = (q * jnp.exp(b)).astype(BF16)
    k_t = (k * jnp.exp(-b)).astype(BF16)
    k_end = (k * jnp.exp(b_last - b)).astype(BF16)
    decay = jnp.exp(b_last)
    causal = (lax.broadcasted_iota(jnp.int32, (ch, ch), 0) >= lax.broadcasted_iota(jnp.int32, (ch, ch), 1))
    gn = gn_ref[...]
    nt = (((1,), (1,)), ((), ()))
    for h in range(nh):
        ks = slice(h * dk, (h + 1) * dk)
        v = gl[:, 768 + h * dv:768 + (h + 1) * dv]
        vb = v.astype(BF16)
        a = lax.dot_general(q_t[:, ks], k_t[:, ks], nt, preferred_element_type=F32)
        a = jnp.where(causal, a, 0.0)
        s_t = st_ref[h]
        o = jnp.dot(a.astype(BF16), vb, preferred_element_type=F32)
        o = o + lax.dot_general(q_t[:, ks], s_t.astype(BF16), nt, preferred_element_type=F32)
        upd = jnp.dot(v.T.astype(BF16), k_end[:, ks], preferred_element_type=F32)
        st_ref[h] = s_t * decay[:, ks] + upd
        go = gl[:, 1280 + h * dv:1280 + (h + 1) * dv]
        y = _rms(o, gn) * (go * jax.nn.sigmoid(go))
        yb_ref[:, h * dv:(h + 1) * dv] = y.astype(BF16)

    @pl.when(c == pl.num_programs(1) - 1)
    def _():
        for h in range(nh):
            sfin_ref[0, h] = st_ref[h].T


def _gla(gl, s0, gn, nb, nchunks, ch, in_map, out_map, out_rows):
    return pl.pallas_call(
        _gla_kernel,
        grid=(nb, nchunks),
        in_specs=[
            pl.BlockSpec((ch, 1792), in_map),
            pl.BlockSpec((1, 4, 64, 128), lambda b, c: (b, 0, 0, 0)),
            pl.BlockSpec((1, 128), lambda b, c: (0, 0)),
        ],
        out_specs=[
            pl.BlockSpec((ch, 512), out_map),
            pl.BlockSpec((1, 4, 64, 128), lambda b, c: (b, 0, 0, 0)),
        ],
        out_shape=[
            jax.ShapeDtypeStruct((out_rows, 512), BF16),
            jax.ShapeDtypeStruct((nb, 4, 64, 128), F32),
        ],
        scratch_shapes=[pltpu.VMEM((4, 128, 64), F32)],
        compiler_params=_cparams(("arbitrary", "arbitrary")),
        name="gla",
    )(gl, s0, gn)


def _extract_topk(work, nsel, iota0, sentinel):
    slabs = work.shape[0] // SUBLANES
    vals, idxs = [], []
    for j in range(nsel):
        v = [work[i * SUBLANES:(i + 1) * SUBLANES] for i in range(slabs)]
        ix = [iota0[i * SUBLANES:(i + 1) * SUBLANES] for i in range(slabs)]
        while len(v) > 1:
            keep = [v[i] >= v[i + 1] for i in range(0, len(v), 2)]
            ix = [jnp.where(k, ix[2 * i], ix[2 * i + 1]) for i, k in enumerate(keep)]
            v = [jnp.where(k, v[2 * i], v[2 * i + 1]) for i, k in enumerate(keep)]
        m = jnp.max(v[0], axis=0, keepdims=True)
        idx = jnp.min(jnp.where(v[0] == m, ix[0], sentinel), axis=0, keepdims=True)
        vals.append(m)
        idxs.append(idx)
        if j + 1 < nsel:
            work = jnp.where(iota0 == idx, -jnp.inf, work)
    return vals, idxs


def _merge_kernel(x_ref, gt_ref, ya_ref, yb_ref, wa_ref, wb_ref, wo_ref, gf_ref, wq_ref, keys_ref,
                  xm_ref, hn_ref, et_ref, wt_ref):
    td = x_ref.shape[0]
    nkeys = keys_ref.shape[1]
    half = keys_ref.shape[2]
    nheads = keys_ref.shape[0] // 2
    topk = PEER_TOPK

    gt = gt_ref[...]
    d = x_ref.shape[1]
    ma = jnp.dot(ya_ref[...], wa_ref[...], preferred_element_type=F32)
    mb = jnp.dot(yb_ref[...], wb_ref[...], preferred_element_type=F32)
    m = jax.nn.sigmoid(gt[:, 0:d]) * ma + jax.nn.sigmoid(gt[:, d:2 * d]) * mb
    xm = x_ref[...] + jnp.dot(m.astype(BF16), wo_ref[...], preferred_element_type=F32)
    xm_ref[...] = xm
    hn = _rms(xm, gf_ref[...])
    hn_ref[...] = hn
    q = jnp.dot(hn.astype(BF16), wq_ref[...], preferred_element_type=F32).astype(BF16)

    nt = (((1,), (1,)), ((), ()))
    iota_k = lax.broadcasted_iota(jnp.int32, (nkeys, td), 0)
    pair_rows = [topk // (a + 1) for a in range(topk)]
    cand_pad = -sum(pair_rows) % SUBLANES
    ncand = sum(pair_rows) + cand_pad
    iota_c = lax.broadcasted_iota(jnp.int32, (ncand, td), 0)
    wts, ids = [], []
    for h in range(nheads):
        sv, si = [], []
        for c in range(2):
            gi = 2 * h + c
            s_t = lax.dot_general(keys_ref[gi], q[:, gi * half:(gi + 1) * half], nt,
                                  preferred_element_type=F32)
            vals, idxs = _extract_topk(s_t, topk, iota_k, nkeys)
            sv.append(vals)
            si.append(idxs)
        sv1 = jnp.concatenate(sv[1], axis=0)
        si1 = jnp.concatenate(si[1], axis=0)
        cand = jnp.concatenate([sv[0][a] + sv1[0:nb] for a, nb in enumerate(pair_rows)]
                               + [jnp.full((cand_pad, td), -jnp.inf, F32)], axis=0)
        cidx = jnp.concatenate([si[0][a] * nkeys + si1[0:nb] for a, nb in enumerate(pair_rows)]
                               + [jnp.full((cand_pad, td), -1, jnp.int32)], axis=0)
        fvals, eids = [], []
        work = cand
        for j in range(topk):
            mx = jnp.max(work, axis=0, keepdims=True)
            pos = jnp.min(jnp.where(work == mx, iota_c, ncand), axis=0, keepdims=True)
            hit = iota_c == pos
            eids.append(jnp.max(jnp.where(hit, cidx, -1), axis=0, keepdims=True))
            fvals.append(mx)
            if j + 1 < topk:
                work = jnp.where(hit, -jnp.inf, work)
        fv = jnp.concatenate(fvals, axis=0)
        e = jnp.exp(fv - fvals[0])
        wts.append(e / jnp.sum(e, axis=0, keepdims=True))
        ids.extend(eids)
    wt_ref[...] = jnp.concatenate(wts, axis=0).T
    et_ref[...] = jnp.concatenate(ids, axis=0).T


def _merge_route(x, gt, ya, yb, wa, wb, wo, gf, wq, keys, gt_map, x_first_row):
    t = ya.shape[0]
    d = x.shape[1]
    td = MERGE_ROWS
    nsel = (keys.shape[0] // 2) * PEER_TOPK
    const2 = lambda i: (0, 0)
    row = lambda i: (i, 0)
    x_first = x_first_row // td
    return pl.pallas_call(
        _merge_kernel,
        grid=(t // td,),
        in_specs=[
            pl.BlockSpec((td, d), lambda i: (i + x_first, 0)),
            pl.BlockSpec((td, 2 * d), gt_map),
            pl.BlockSpec((td, ya.shape[1]), row),
            pl.BlockSpec((td, yb.shape[1]), row),
            pl.BlockSpec(wa.shape, const2),
            pl.BlockSpec(wb.shape, const2),
            pl.BlockSpec(wo.shape, const2),
            pl.BlockSpec(gf.shape, const2),
            pl.BlockSpec(wq.shape, const2),
            pl.BlockSpec(keys.shape, lambda i: (0, 0, 0)),
        ],
        out_specs=[
            pl.BlockSpec((td, d), row),
            pl.BlockSpec((td, d), row),
            pl.BlockSpec((td, nsel), row),
            pl.BlockSpec((td, nsel), row),
        ],
        out_shape=[
            jax.ShapeDtypeStruct((t, d), F32),
            jax.ShapeDtypeStruct((t, d), F32),
            jax.ShapeDtypeStruct((t, nsel), jnp.int32),
            jax.ShapeDtypeStruct((t, nsel), F32),
        ],
        compiler_params=_cparams(("arbitrary",)),
        name="merge_route",
    )(x, gt, ya, yb, wa, wb, wo, gf, wq, keys)


def _coef_kernel(act_ref, wt_ref, o_ref):
    act = act_ref[...]
    gelu = 0.5 * act * (1.0 + lax.erf(act * (2.0 ** -0.5)))
    bits = pltpu.bitcast((wt_ref[...] * gelu).astype(BF16).astype(F32), jnp.int32)
    o_ref[...] = bits | lax.shift_right_logical(bits, 16)


def _expert_coefs(act, wt):
    t, nsel = wt.shape
    rows = math.gcd(t, 512)
    row = lambda i: (i, 0)
    return pl.pallas_call(
        _coef_kernel,
        grid=(t // rows,),
        in_specs=[pl.BlockSpec((rows, nsel), row), pl.BlockSpec((rows, nsel), row)],
        out_specs=pl.BlockSpec((rows, nsel), row),
        out_shape=jax.ShapeDtypeStruct((t, nsel), jnp.int32),
        compiler_params=_cparams(("arbitrary",)),
        name="expert_coefs",
    )(act, wt)


def _finish_kernel(xm_ref, o_ref, gfin_ref, yacc_hbm, y_ref):
    del yacc_hbm
    y_ref[...] = _rms(xm_ref[...] + o_ref[...], gfin_ref[...])


def _finish(xm, o, gfin, y_acc, row_off):
    t, d = xm.shape
    rows = math.gcd(t, 512)
    first = row_off // rows
    row = lambda i: (i, 0)
    return pl.pallas_call(
        _finish_kernel,
        grid=(t // rows,),
        in_specs=[pl.BlockSpec((rows, d), row), pl.BlockSpec((rows, d), row),
                  pl.BlockSpec(gfin.shape, lambda i: (0, 0)), pl.BlockSpec(memory_space=pl.ANY)],
        out_specs=pl.BlockSpec((rows, d), lambda i: (i + first, 0)),
        out_shape=jax.ShapeDtypeStruct(y_acc.shape, F32),
        input_output_aliases={3: 0},
        compiler_params=_cparams(("arbitrary",)),
        name="finish",
    )(xm, o, gfin, y_acc)


def _rope_table(pos, valid, rope_dim, head_dim):
    half = rope_dim // 2
    inv = ROPE_THETA ** (-jnp.arange(0, rope_dim, 2, dtype=F32) / rope_dim)
    ang = pos.astype(F32)[:, None] * inv[None, :]
    cos, sin = jnp.cos(ang), jnp.sin(ang)
    n = pos.shape[0]
    ones = jnp.ones((n, head_dim - rope_dim), F32)
    zeros_h = jnp.zeros((n, half), F32)
    zeros_r = jnp.zeros((n, head_dim - rope_dim), F32)
    reps = LANES // head_dim
    cosf = jnp.tile(jnp.concatenate([cos, cos, ones], axis=1), (1, reps))
    sin_lo = jnp.tile(jnp.concatenate([zeros_h, sin, zeros_r], axis=1), (1, reps))
    sin_hi = jnp.tile(jnp.concatenate([-sin, zeros_h, zeros_r], axis=1), (1, reps))
    vcol = jnp.broadcast_to(valid.astype(F32)[:, None], (n, LANES))
    return jnp.concatenate([cosf, sin_lo, sin_hi, vcol], axis=1)


def _pack_pairs(x):
    half = x.shape[1] // 2
    lo = lax.bitcast_convert_type(x[:, :half].astype(BF16), jnp.uint16).astype(jnp.uint32)
    hi = lax.bitcast_convert_type(x[:, half:].astype(BF16), jnp.uint16).astype(jnp.uint32)
    return lax.bitcast_convert_type(lo | (hi << 16), jnp.int32)


def _expert_dots(table, idx, hn, nsel):
    n = idx.shape[0]
    c = table.shape[1]
    sc = plsc.get_sparse_core_info()
    lanes = sc.num_lanes
    workers = sc.num_cores * sc.num_subcores
    gw = GATHER_WINDOW
    nbuf = GATHER_BUFFERS
    per_worker = n // workers
    ich = min(DOTS_INDEX_CHUNK, per_worker)
    tok = ich // nsel
    assert n % workers == 0 and per_worker % ich == 0 and nsel == nbuf * gw and hn.shape[1] == c and gw % lanes == 0
    mesh = plsc.VectorSubcoreMesh(core_axis_name="c", subcore_axis_name="s")

    @functools.partial(
        pl.kernel, out_type=jax.ShapeDtypeStruct((n,), F32), mesh=mesh, name="expert_dots",
        compiler_params=pltpu.CompilerParams(needs_layout_passes=False),
        scratch_types=[pltpu.VMEM((ich,), jnp.int32), pltpu.VMEM((tok, c), jnp.int32), pltpu.VMEM((ich,), F32),
                       pltpu.VMEM((lanes * lanes,), F32)]
                      + [pltpu.VMEM((gw, c), jnp.int32)] * nbuf + [pltpu.SemaphoreType.DMA] * nbuf)
    def dots(tab_hbm, idx_hbm, hn_hbm, act_hbm, idx_v, h_v, act_v, scr, *bufs_sems):
        rows, sems = bufs_sems[:nbuf], bufs_sems[nbuf:]
        wid = lax.axis_index("s") * sc.num_cores + lax.axis_index("c")
        base = wid * per_worker
        lane = lax.iota(jnp.int32, lanes)

        def gather(win, buf, sem):
            return pltpu.make_async_copy(tab_hbm.at[idx_v.at[pl.ds(win * gw, gw)]], buf, sem)

        def reduce_window(buf, t_loc, out_off):
            for rb in range(gw // lanes):
                def kbody(k2, accs):
                    k0 = 2 * k2 * lanes
                    hw0 = plsc.bitcast(h_v[t_loc, pl.ds(k0, lanes)], BF16)
                    hw1 = plsc.bitcast(h_v[t_loc, pl.ds(k0 + lanes, lanes)], BF16)
                    out = []
                    for r in range(lanes):
                        w0 = plsc.bitcast(buf[rb * lanes + r, pl.ds(k0, lanes)], BF16)
                        w1 = plsc.bitcast(buf[rb * lanes + r, pl.ds(k0 + lanes, lanes)], BF16)
                        p = plsc.bitcast(w0 * hw0 + w1 * hw1, jnp.int32)
                        out.append(accs[r] + lax.bitcast_convert_type(p << 16, F32)
                                   + lax.bitcast_convert_type(p, F32))
                    return tuple(out)

                accs = lax.fori_loop(0, c // (2 * lanes), kbody,
                                     tuple(jnp.zeros((lanes,), F32) for _ in range(lanes)))
                for r in range(lanes):
                    scr[pl.ds(r * lanes, lanes)] = accs[r]
                cols = [plsc.load_gather(scr, [lane * lanes + l]) for l in range(lanes)]
                while len(cols) > 1:
                    cols = [cols[i] + cols[i + 1] for i in range(0, len(cols), 2)]
                act_v[pl.ds(out_off + rb * lanes, lanes)] = cols[0]

        @pl.loop(0, per_worker // ich)
        def _(g):
            cb = base + g * ich
            pltpu.sync_copy(idx_hbm.at[pl.ds(cb, ich)], idx_v)
            tok_base = pl.multiple_of(wid * (per_worker // nsel) + g * tok, tok)
            pltpu.sync_copy(hn_hbm.at[pl.ds(tok_base, tok)], h_v)
            for q in range(nbuf):
                gather(q, rows[q], sems[q]).start()

            @pl.loop(0, tok)
            def _(j):
                for q in range(nbuf):
                    gather(nbuf * j + q, rows[q], sems[q]).wait()
                    reduce_window(rows[q], j, j * nsel + q * gw)

                    @pl.when(j + 1 < tok)
                    def _():
                        gather(nbuf * (j + 1) + q, rows[q], sems[q]).start()

            pltpu.sync_copy(act_v, act_hbm.at[pl.ds(cb, ich)])

    return dots(table, idx, hn)


def _expert_mix(table, idx, coef, nsel):
    n = idx.shape[0]
    c = table.shape[1]
    d = 2 * c
    sc = plsc.get_sparse_core_info()
    lanes = sc.num_lanes
    workers = sc.num_cores * sc.num_subcores
    gw = GATHER_WINDOW
    nbuf = GATHER_BUFFERS
    per_worker = n // workers
    ich = min(MIX_INDEX_CHUNK, per_worker)
    tok = ich // nsel
    kblock = 16
    assert n % workers == 0 and per_worker % ich == 0 and nsel == nbuf * gw and c % (kblock * lanes) == 0
    mesh = plsc.VectorSubcoreMesh(core_axis_name="c", subcore_axis_name="s")

    @functools.partial(
        pl.kernel, out_type=jax.ShapeDtypeStruct((n // nsel, d), F32), mesh=mesh, name="expert_mix",
        compiler_params=pltpu.CompilerParams(needs_layout_passes=False),
        scratch_types=[pltpu.VMEM((ich,), jnp.int32), pltpu.VMEM((ich,), jnp.int32), pltpu.VMEM((tok, d), F32)]
                      + [pltpu.VMEM((gw, c), jnp.int32)] * nbuf + [pltpu.SemaphoreType.DMA] * nbuf)
    def mix(tab_hbm, idx_hbm, coef_hbm, out_hbm, idx_v, coef_v, out_v, *bufs_sems):
        rows, sems = bufs_sems[:nbuf], bufs_sems[nbuf:]
        wid = lax.axis_index("s") * sc.num_cores + lax.axis_index("c")
        base = wid * per_worker
        zero_idx = jnp.zeros((lanes,), jnp.int32)

        def gather(win, buf, sem):
            return pltpu.make_async_copy(tab_hbm.at[idx_v.at[pl.ds(win * gw, gw)]], buf, sem)

        def accumulate_window(buf, t_loc, coef_off, first):
            for kb in range(c // (kblock * lanes)):
                col0 = kb * kblock * lanes
                if first:
                    init = tuple(jnp.zeros((lanes,), F32) for _ in range(2 * kblock))
                else:
                    init = tuple(out_v[t_loc, pl.ds(col0 + i * lanes, lanes)] for i in range(kblock)) + \
                           tuple(out_v[t_loc, pl.ds(c + col0 + i * lanes, lanes)] for i in range(kblock))

                def rbody(r2, accs):
                    accs = list(accs)
                    r = 2 * r2
                    cw0 = plsc.bitcast(plsc.load_gather(coef_v, [zero_idx + (coef_off + r)]), BF16)
                    cw1 = plsc.bitcast(plsc.load_gather(coef_v, [zero_idx + (coef_off + r + 1)]), BF16)
                    for i in range(kblock):
                        w0 = plsc.bitcast(buf[r, pl.ds(col0 + i * lanes, lanes)], BF16)
                        w1 = plsc.bitcast(buf[r + 1, pl.ds(col0 + i * lanes, lanes)], BF16)
                        p = plsc.bitcast(w0 * cw0 + w1 * cw1, jnp.int32)
                        accs[i] = accs[i] + lax.bitcast_convert_type(p << 16, F32)
                        accs[kblock + i] = accs[kblock + i] + lax.bitcast_convert_type(p, F32)
                    return tuple(accs)

                accs = lax.fori_loop(0, gw // 2, rbody, init)
                for i in range(kblock):
                    out_v[t_loc, pl.ds(col0 + i * lanes, lanes)] = accs[i]
                    out_v[t_loc, pl.ds(c + col0 + i * lanes, lanes)] = accs[kblock + i]

        @pl.loop(0, per_worker // ich)
        def _(g):
            cb = base + g * ich
            pltpu.sync_copy(idx_hbm.at[pl.ds(cb, ich)], idx_v)
            pltpu.sync_copy(coef_hbm.at[pl.ds(cb, ich)], coef_v)
            for q in range(nbuf):
                gather(q, rows[q], sems[q]).start()

            @pl.loop(0, tok)
            def _(j):
                for q in range(nbuf):
                    gather(nbuf * j + q, rows[q], sems[q]).wait()
                    accumulate_window(rows[q], j, j * nsel + q * gw, q == 0)

                    @pl.when(j + 1 < tok)
                    def _():
                        gather(nbuf * (j + 1) + q, rows[q], sems[q]).start()

            tok_base = pl.multiple_of(wid * (per_worker // nsel) + g * tok, tok)
            pltpu.sync_copy(out_v, out_hbm.at[pl.ds(tok_base, tok)])

    return mix(table, idx, coef)


def _peer_tail(xm, hn, et, wt, tabs, gfin, y_acc, row_off):
    u_tab, v_tab = tabs
    t, nsel = et.shape
    eidx = et.reshape(t * nsel)
    act = _expert_dots(u_tab, eidx, _pack_pairs(hn), nsel).reshape(t, nsel)
    coef = _expert_coefs(act, wt).reshape(t * nsel)
    mixed = _expert_mix(v_tab, eidx, coef, nsel)
    return _finish(xm, mixed, gfin, y_acc, row_off)


def kernel(x_prompt, x_sample, cache_k_window, cache_v_window, state_gla, meta_tokens, g_norm_mix, w_in,
           w_gate_up, b_gate, attn_sinks, g_gla_norm, w_branch_a, w_branch_b, w_out, g_norm_ffn, w_peer_q,
           peer_sub_keys, peer_u, peer_v, g_norm_final):
    bsz, seq, d = x_prompt.shape
    dbsz, tdec, _ = x_sample.shape
    n_meta = meta_tokens.shape[0]
    depth = w_in.shape[0]
    window = cache_k_window.shape[2]
    kv_heads, head_dim = cache_k_window.shape[3], cache_k_window.shape[4]
    gate_rank = w_gate_up.shape[1]
    bqk = w_gate_up.shape[2]
    n_ph, _, n_keys, p_half = peer_sub_keys.shape[1:]
    assert depth == 1 and d == 1024 and window == ATTN_BLOCK and kv_heads == 2 and head_dim == 64
    assert bqk == 256 and state_gla.shape[2:] == (4, 64, 128) and n_meta <= ATTN_BLOCK
    assert seq % ATTN_BLOCK == 0 and tdec <= SAMPLE_PAD and n_keys == 128 and p_half == 64 and n_ph == 8
    rope_dim = head_dim // 4
    meta_pad = ATTN_BLOCK - n_meta
    lp = ATTN_BLOCK + seq
    nblk = lp // ATTN_BLOCK

    w = w_in[0]
    c_lr = 2304
    c_gate = c_lr + gate_rank
    w1 = w[:, :c_lr].astype(BF16)
    wlr = jnp.pad(w[:, c_lr:c_gate], ((0, 0), (0, LANES - gate_rank))).astype(BF16)
    w2 = w[:, c_gate:].astype(BF16)
    wgu = jnp.pad(w_gate_up[0], ((0, LANES - gate_rank), (0, 0))).astype(BF16)
    bg = b_gate[0][None, :]
    gmix = g_norm_mix[0][None, :]
    wa = w_branch_a[0].astype(BF16)
    wb = w_branch_b[0].astype(BF16)
    wo = w_out[0].astype(BF16)
    gffn = g_norm_ffn[0][None, :]
    wq = w_peer_q[0].astype(BF16)
    keys = peer_sub_keys[0].reshape(n_ph * 2, n_keys, p_half).astype(BF16)
    u_tab = _pack_pairs(peer_u[0])
    gfin = g_norm_final[None, :]
    gn = g_gla_norm[0][None, :]
    sinks = attn_sinks[0]
    qk_scale = float(bqk // 4) ** -0.5

    rows_p = jnp.arange(lp)
    tab_p = _rope_table(rows_p - meta_pad, rows_p >= meta_pad, rope_dim, head_dim)
    proj_rows = max(r for r in range(16, PROJ_ROWS + 1, 16) if lp % r == 0)
    nq = nblk - 1
    nchunks = lp // GLA_CHUNK
    skip = ATTN_BLOCK // GLA_CHUNK
    ncq = nchunks - skip
    per_seq = seq // MERGE_ROWS
    gt_map_p = lambda i: ((i // per_seq) * nblk + 1 + (i % per_seq), 0)

    meta = jnp.broadcast_to(meta_tokens[None].astype(x_prompt.dtype), (bsz, n_meta, d))
    xpad = jnp.concatenate([jnp.zeros((bsz, meta_pad, d), x_prompt.dtype), meta, x_prompt], axis=1)
    xpad = xpad.reshape(bsz * lp, d)
    x_rows = x_prompt.reshape(bsz * seq, d)

    def mix_and_route(b0, gb):
        qa, kv, gl, gt = _project(xpad, gmix, tab_p, w1, wlr, wgu, bg, w2, proj_rows, qk_scale, b0 * lp, gb * lp)
        ya = _attention(
            sinks, qa, kv, kv, gb, nq, ATTN_BLOCK,
            lambda b, n: (b * nblk + n + 1, 0), lambda b, n: (b * nblk + n, 0), lambda b, n: (b * nblk + n + 1, 0),
            lambda b, n: (b * nq + n, 0), gb * seq, first_valid_key=meta_pad, block_offset=1)
        s0 = jnp.zeros((gb,) + state_gla.shape[2:], F32)
        yb, s_fin = _gla(gl, s0, gn, gb, nchunks, GLA_CHUNK,
                         lambda b, c: (b * nchunks + c, 0),
                         lambda b, c: (b * ncq + jnp.maximum(c - skip, 0), 0), gb * seq)
        xm, hn, et, wt = _merge_route(x_rows, gt, ya, yb, wa, wb, wo, gffn, wq, keys, gt_map_p, b0 * seq)
        kv_w = kv.reshape(gb, lp, 2, kv_heads, head_dim)[:, lp - window:]
        return (xm, hn, et, wt), kv_w, s_fin

    group = PROMPT_GROUP if (bsz % PROMPT_GROUP == 0 and (PROMPT_GROUP * seq) % GATHER_ROW_QUANTUM == 0) else bsz
    tabs = (u_tab, _pack_pairs(peer_v[0]))
    y_acc, kv_parts, s_parts = jnp.zeros((bsz * seq, d), F32), [], []
    for b0 in range(0, bsz, group):
        routed, kv_w, s_fin = mix_and_route(b0, group)
        y_acc = _peer_tail(*routed, tabs, gfin, y_acc, b0 * seq)
        kv_parts.append(kv_w)
        s_parts.append(s_fin)
    y_prompt = y_acc.reshape(bsz, seq, d)
    kv_p = jnp.concatenate(kv_parts, axis=0)
    s_fin_p = jnp.concatenate(s_parts, axis=0)
    new_k_p = kv_p[:, :, 0][None]
    new_v_p = kv_p[:, :, 1][None]

    sp = SAMPLE_PAD
    xs_pad = jnp.pad(x_sample, ((0, 0), (0, sp - tdec), (0, 0))).reshape(dbsz * sp, d)
    rows_s = jnp.arange(sp)
    reps = 256 // sp
    tab_s = jnp.tile(_rope_table(PAST_LEN + rows_s, rows_s < tdec, rope_dim, head_dim), (reps, 1))
    qa_s, kv_s, gl_s, gt_s = _project(xs_pad, gmix, tab_s, w1, wlr, wgu, bg, w2, 256, qk_scale, 0, dbsz * sp)

    cache_kv = jnp.concatenate([cache_k_window[0].reshape(dbsz * window, kv_heads * head_dim),
                                cache_v_window[0].reshape(dbsz * window, kv_heads * head_dim)], axis=1)
    seq_map = lambda b, n: (b, 0)
    ya_s = _attention(sinks, qa_s, cache_kv, kv_s, dbsz, 1, sp, seq_map, seq_map, seq_map, seq_map,
                      dbsz * sp, first_valid_key=None, block_offset=0)
    yb_s, s_fin_s = _gla(gl_s, state_gla[0], gn, dbsz, 1, sp, seq_map, seq_map, dbsz * sp)

    def real_rows(a):
        return a.reshape(dbsz, sp, a.shape[-1])[:, :tdec].reshape(dbsz * tdec, a.shape[-1])

    xs_rows = x_sample.reshape(dbsz * tdec, d)
    xm_s, hn_s, et_s, wt_s = _merge_route(xs_rows, real_rows(gt_s), real_rows(ya_s), real_rows(yb_s),
                                          wa, wb, wo, gffn, wq, keys, lambda i: (i, 0), 0)
    hn_s, _ = lax.optimization_barrier((hn_s, y_acc))
    y_sample = _peer_tail(xm_s, hn_s, et_s, wt_s, tabs, gfin,
                          jnp.zeros((dbsz * tdec, d), F32), 0).reshape(dbsz, tdec, d)

    kv_new = real_rows(kv_s).reshape(dbsz, tdec, 2, kv_heads, head_dim)
    new_k_s = jnp.concatenate([cache_k_window[0].astype(F32), kv_new[:, :, 0]], axis=1)[:, -window:][None]
    new_v_s = jnp.concatenate([cache_v_window[0].astype(F32), kv_new[:, :, 1]], axis=1)[:, -window:][None]

    return (y_prompt, y_sample, new_k_p, new_v_p, s_fin_p[None], new_k_s, new_v_s, s_fin_s[None])
```

```python
import functools
import math

import jax
import jax.numpy as jnp
from jax import lax
from jax.experimental import pallas as pl
from jax.experimental.pallas import tpu as pltpu
from jax.experimental.pallas import tpu_sc as plsc

F32 = jnp.float32
BF16 = jnp.bfloat16

EPS = 1e-6
NEG_INF = -1e30
PAST_LEN = 16384
ROPE_THETA = 500000.0
GATE_NORMALIZER = 16.0
PEER_TOPK = 16

LANES = 128
SUBLANES = 8
VMEM_LIMIT_BYTES = 56 * 1024 * 1024

ATTN_BLOCK = 128
GLA_CHUNK = 64
SAMPLE_PAD = 16
PROJ_ROWS = 544
MERGE_ROWS = 128
GATHER_WINDOW = 32
GATHER_BUFFERS = 4
DOTS_INDEX_CHUNK = 8192
MIX_INDEX_CHUNK = 4096
GATHER_ROW_QUANTUM = 512
PROMPT_GROUP = 1


def _cparams(sem):
    return pltpu.CompilerParams(dimension_semantics=sem, vmem_limit_bytes=VMEM_LIMIT_BYTES)


def _rms(x, g):
    ms = jnp.mean(x * x, axis=-1, keepdims=True)
    return (x * lax.rsqrt(ms + EPS)) * g


def _proj_kernel(x_ref, g_ref, tab_ref, w1_ref, wlr_ref, wgu_ref, bg_ref, w2_ref,
                 qa_ref, kv_ref, gl_ref, gt_ref, *, period, qk_scale):
    i = pl.program_id(0)
    tr = x_ref.shape[0]
    hb = _rms(x_ref[...], g_ref[...]).astype(BF16)
    z1 = jnp.dot(hb, w1_ref[...], preferred_element_type=F32)

    start = pl.multiple_of((i * tr) % period, SUBLANES)
    tab = tab_ref[pl.ds(start, tr), :]
    cosf = tab[:, 0:LANES]
    sin_lo = tab[:, LANES:2 * LANES]
    sin_hi = tab[:, 2 * LANES:3 * LANES]
    valid = tab[:, 3 * LANES:3 * LANES + 1]

    def rope(xg):
        return xg * cosf + pltpu.roll(xg, 8, 1) * sin_lo + pltpu.roll(xg, LANES - 8, 1) * sin_hi

    for gi in range(4):
        sl = slice(gi * LANES, (gi + 1) * LANES)
        qa_ref[:, sl] = rope(z1[:, sl]).astype(BF16)
    kv_ref[:, 0:LANES] = rope(z1[:, 512:640])
    kv_ref[:, LANES:2 * LANES] = z1[:, 640:768]

    lr = jnp.dot(hb, wlr_ref[...], preferred_element_type=F32)
    pre = jnp.dot(lr.astype(BF16), wgu_ref[...], preferred_element_type=F32) + bg_ref[...]
    log_sig = jnp.minimum(pre, 0.0) - jnp.log1p(jnp.exp(-jnp.abs(pre)))
    ld = jnp.where(valid > 0.5, log_sig / GATE_NORMALIZER, 0.0)

    gl_ref[:, 0:256] = z1[:, 768:1024] * qk_scale
    gl_ref[:, 256:512] = z1[:, 1024:1280]
    gl_ref[:, 512:768] = ld
    gl_ref[:, 768:1792] = z1[:, 1280:2304]
    gt_ref[...] = jnp.dot(hb, w2_ref[...], preferred_element_type=F32)


def _project(x, g, tab, w1, wlr, wgu, bg, w2, rows, qk_scale, first_row, r):
    d = x.shape[1]
    period = tab.shape[0]
    const = lambda i: (0, 0)
    row = lambda i: (i, 0)
    first = first_row // rows
    return pl.pallas_call(
        functools.partial(_proj_kernel, period=period, qk_scale=qk_scale),
        grid=(r // rows,),
        in_specs=[
            pl.BlockSpec((rows, d), lambda i: (i + first, 0)),
            pl.BlockSpec(g.shape, const),
            pl.BlockSpec(tab.shape, const),
            pl.BlockSpec(w1.shape, const),
            pl.BlockSpec(wlr.shape, const),
            pl.BlockSpec(wgu.shape, const),
            pl.BlockSpec(bg.shape, const),
            pl.BlockSpec(w2.shape, const),
        ],
        out_specs=[
            pl.BlockSpec((rows, 512), row),
            pl.BlockSpec((rows, 256), row),
            pl.BlockSpec((rows, 1792), row),
            pl.BlockSpec((rows, 2048), row),
        ],
        out_shape=[
            jax.ShapeDtypeStruct((r, 512), BF16),
            jax.ShapeDtypeStruct((r, 256), F32),
            jax.ShapeDtypeStruct((r, 1792), F32),
            jax.ShapeDtypeStruct((r, 2048), F32),
        ],
        compiler_params=_cparams(("arbitrary",)),
        name="proj",
    )(x, g, tab, w1, wlr, wgu, bg, w2)


def _attn_kernel(sink_ref, q_ref, prev_ref, cur_ref, o_ref, *, first_valid_key, block_offset):
    n = pl.program_id(1)
    qr = q_ref.shape[0]
    kr = cur_ref.shape[0]
    w = prev_ref.shape[0]
    nk = w + kr
    group = 4
    hd = 64

    rows = lax.broadcasted_iota(jnp.int32, (group * qr, nk), 0)
    cols = lax.broadcasted_iota(jnp.int32, (group * qr, nk), 1)
    head_of_row = rows // qr
    diff = (rows - head_of_row * qr) - cols + w
    mask = (diff >= 0) & (diff <= w)
    if first_valid_key is not None:
        blk = n + block_offset
        mask = mask & (cols >= first_valid_key + w - blk * w)

    prev = prev_ref[...]
    cur = cur_ref[...]
    q = q_ref[...]
    row_head = lax.broadcasted_iota(jnp.int32, (group * qr, 1), 0) // qr
    for kh in range(2):
        k = jnp.concatenate([prev[:, kh * hd:(kh + 1) * hd], cur[:, kh * hd:(kh + 1) * hd]], axis=0).astype(BF16)
        v = jnp.concatenate([prev[:, LANES + kh * hd:LANES + (kh + 1) * hd],
                             cur[:, LANES + kh * hd:LANES + (kh + 1) * hd]], axis=0).astype(BF16)
        qs = jnp.concatenate([q[:, (group * kh + g) * hd:(group * kh + g + 1) * hd] for g in range(group)], axis=0)
        s = lax.dot_general(qs, k, (((1,), (1,)), ((), ())), preferred_element_type=F32) * (hd ** -0.5)
        s = jnp.where(mask, s, NEG_INF)
        sink = jnp.zeros((group * qr, 1), F32)
        for g in range(group):
            sink = jnp.where(row_head == g, sink_ref[group * kh + g], sink)
        m = jnp.maximum(jnp.max(s, axis=-1, keepdims=True), sink)
        e = jnp.exp(s - m)
        p = e / (jnp.sum(e, axis=-1, keepdims=True) + jnp.exp(sink - m))
        o = jnp.dot(p.astype(BF16), v, preferred_element_type=F32)
        for g in range(group):
            h = group * kh + g
            o_ref[:, h * hd:(h + 1) * hd] = o[g * qr:(g + 1) * qr].astype(BF16)


def _attention(sinks, q, kv_prev, kv_cur, nb, nblk, qr, q_map, prev_map, cur_map, out_map, out_rows,
               first_valid_key, block_offset):
    w = ATTN_BLOCK
    return pl.pallas_call(
        functools.partial(_attn_kernel, first_valid_key=first_valid_key, block_offset=block_offset),
        grid=(nb, nblk),
        in_specs=[
            pl.BlockSpec(memory_space=pltpu.SMEM),
            pl.BlockSpec((qr, 512), q_map),
            pl.BlockSpec((w, 256), prev_map),
            pl.BlockSpec((qr, 256), cur_map),
        ],
        out_specs=pl.BlockSpec((qr, 512), out_map),
        out_shape=jax.ShapeDtypeStruct((out_rows, 512), BF16),
        compiler_params=_cparams(("arbitrary", "arbitrary")),
        name="swa",
    )(sinks, q, kv_prev, kv_cur)


def _gla_kernel(gl_ref, s0_ref, gn_ref, yb_ref, sfin_ref, st_ref):
    c = pl.program_id(1)
    ch = gl_ref.shape[0]
    nh, dk, dv = 4, 64, 128

    @pl.when(c == 0)
    def _():
        for h in range(nh):
            st_ref[h] = s0_ref[0, h].T

    gl = gl_ref[...]
    q = gl[:, 0:256]
    k = gl[:, 256:512]
    b = gl[:, 512:768]
    row = lax.broadcasted_iota(jnp.int32, (ch, nh * dk), 0)
    sh = 1
    while sh < ch:
        b = b + jnp.where(row >= sh, pltpu.roll(b, sh, 0), 0.0)
        sh *= 2
    b_last = b[ch - 1:ch, :]
    q_t = (q * jnp.exp(b)).astype(BF16)
    k_t = (k * jnp.exp(-b)).astype(BF16)
    k_end = (k * jnp.exp(b_last - b)).astype(BF16)
    decay = jnp.exp(b_last)
    causal = (lax.broadcasted_iota(jnp.int32, (ch, ch), 0) >= lax.broadcasted_iota(jnp.int32, (ch, ch), 1))
    gn = gn_ref[...]
    nt = (((1,), (1,)), ((), ()))
    for h in range(nh):
        ks = slice(h * dk, (h + 1) * dk)
        v = gl[:, 768 + h * dv:768 + (h + 1) * dv]
        vb = v.astype(BF16)
        a = lax.dot_general(q_t[:, ks], k_t[:, ks], nt, preferred_element_type=F32)
        a = jnp.where(causal, a, 0.0)
        s_t = st_ref[h]
        o = jnp.dot(a.astype(BF16), vb, preferred_element_type=F32)
        o = o + lax.dot_general(q_t[:, ks], s_t.astype(BF16), nt, preferred_element_type=F32)
        upd = jnp.dot(v.T.astype(BF16), k_end[:, ks], preferred_element_type=F32)
        st_ref[h] = s_t * decay[:, ks] + upd
        go = gl[:, 1280 + h * dv:1280 + (h + 1) * dv]
        y = _rms(o, gn) * (go * jax.nn.sigmoid(go))
        yb_ref[:, h * dv:(h + 1) * dv] = y.astype(BF16)

    @pl.when(c == pl.num_programs(1) - 1)
    def _():
        for h in range(nh):
            sfin_ref[0, h] = st_ref[h].T


def _gla(gl, s0, gn, nb, nchunks, ch, in_map, out_map, out_rows):
    return pl.pallas_call(
        _gla_kernel,
        grid=(nb, nchunks),
        in_specs=[
            pl.BlockSpec((ch, 1792), in_map),
            pl.BlockSpec((1, 4, 64, 128), lambda b, c: (b, 0, 0, 0)),
            pl.BlockSpec((1, 128), lambda b, c: (0, 0)),
        ],
        out_specs=[
            pl.BlockSpec((ch, 512), out_map),
            pl.BlockSpec((1, 4, 64, 128), lambda b, c: (b, 0, 0, 0)),
        ],
        out_shape=[
            jax.ShapeDtypeStruct((out_rows, 512), BF16),
            jax.ShapeDtypeStruct((nb, 4, 64, 128), F32),
        ],
        scratch_shapes=[pltpu.VMEM((4, 128, 64), F32)],
        compiler_params=_cparams(("arbitrary", "arbitrary")),
        name="gla",
    )(gl, s0, gn)


def _extract_topk(work, nsel, iota0, sentinel, payload=None):
    slabs = work.shape[0] // SUBLANES
    assert slabs & (slabs - 1) == 0
    vals, idxs, picks = [], [], []
    for j in range(nsel):
        v = [work[i * SUBLANES:(i + 1) * SUBLANES] for i in range(slabs)]
        ix = [iota0[i * SUBLANES:(i + 1) * SUBLANES] for i in range(slabs)]
        while len(v) > 1:
            keep = [v[i] >= v[i + 1] for i in range(0, len(v), 2)]
            ix = [jnp.where(k, ix[2 * i], ix[2 * i + 1]) for i, k in enumerate(keep)]
            v = [jnp.where(k, v[2 * i], v[2 * i + 1]) for i, k in enumerate(keep)]
        m = jnp.max(v[0], axis=0, keepdims=True)
        idx = jnp.min(jnp.where(v[0] == m, ix[0], sentinel), axis=0, keepdims=True)
        vals.append(m)
        idxs.append(idx)
        hit = iota0 == idx
        if payload is not None:
            picks.append(jnp.max(jnp.where(hit, payload, -1), axis=0, keepdims=True))
        if j + 1 < nsel:
            work = jnp.where(hit, -jnp.inf, work)
    return vals, idxs, picks


def _merge_kernel(x_ref, gt_ref, ya_ref, yb_ref, wa_ref, wb_ref, wo_ref, gf_ref, wq_ref, keys_ref,
                  xm_ref, hn_ref, et_ref, wt_ref):
    td = x_ref.shape[0]
    nkeys = keys_ref.shape[1]
    half = keys_ref.shape[2]
    nheads = keys_ref.shape[0] // 2
    topk = PEER_TOPK

    gt = gt_ref[...]
    d = x_ref.shape[1]
    ma = jnp.dot(ya_ref[...], wa_ref[...], preferred_element_type=F32)
    mb = jnp.dot(yb_ref[...], wb_ref[...], preferred_element_type=F32)
    m = jax.nn.sigmoid(gt[:, 0:d]) * ma + jax.nn.sigmoid(gt[:, d:2 * d]) * mb
    xm = x_ref[...] + jnp.dot(m.astype(BF16), wo_ref[...], preferred_element_type=F32)
    xm_ref[...] = xm
    hn = _rms(xm, gf_ref[...])
    hn_ref[...] = hn
    q = jnp.dot(hn.astype(BF16), wq_ref[...], preferred_element_type=F32).astype(BF16)

    nt = (((1,), (1,)), ((), ()))
    iota_k = lax.broadcasted_iota(jnp.int32, (nkeys, td), 0)
    pair_rows = [topk // (a + 1) for a in range(topk)]
    ncand = SUBLANES
    while ncand < sum(pair_rows):
        ncand *= 2
    cand_pad = ncand - sum(pair_rows)
    iota_c = lax.broadcasted_iota(jnp.int32, (ncand, td), 0)
    wts, ids = [], []
    for h in range(nheads):
        sv, si = [], []
        for c in range(2):
            gi = 2 * h + c
            s_t = lax.dot_general(keys_ref[gi], q[:, gi * half:(gi + 1) * half], nt,
                                  preferred_element_type=F32)
            vals, idxs, _ = _extract_topk(s_t, topk, iota_k, nkeys)
            sv.append(vals)
            si.append(idxs)
        sv1 = jnp.concatenate(sv[1], axis=0)
        si1 = jnp.concatenate(si[1], axis=0)
        cand = jnp.concatenate([sv[0][a] + sv1[0:nb] for a, nb in enumerate(pair_rows)]
                               + [jnp.full((cand_pad, td), -jnp.inf, F32)], axis=0)
        cidx = jnp.concatenate([si[0][a] * nkeys + si1[0:nb] for a, nb in enumerate(pair_rows)]
                               + [jnp.full((cand_pad, td), -1, jnp.int32)], axis=0)
        fvals, _, eids = _extract_topk(cand, topk, iota_c, ncand, payload=cidx)
        fv = jnp.concatenate(fvals, axis=0)
        e = jnp.exp(fv - fvals[0])
        wts.append(e / jnp.sum(e, axis=0, keepdims=True))
        ids.extend(eids)
    wt_ref[...] = jnp.concatenate(wts, axis=0).T
    et_ref[...] = jnp.concatenate(ids, axis=0).T


def _merge_route(x, gt, ya, yb, wa, wb, wo, gf, wq, keys, gt_map, x_first_row):
    t = ya.shape[0]
    d = x.shape[1]
    td = MERGE_ROWS
    nsel = (keys.shape[0] // 2) * PEER_TOPK
    const2 = lambda i: (0, 0)
    row = lambda i: (i, 0)
    x_first = x_first_row // td
    return pl.pallas_call(
        _merge_kernel,
        grid=(t // td,),
        in_specs=[
            pl.BlockSpec((td, d), lambda i: (i + x_first, 0)),
            pl.BlockSpec((td, 2 * d), gt_map),
            pl.BlockSpec((td, ya.shape[1]), row),
            pl.BlockSpec((td, yb.shape[1]), row),
            pl.BlockSpec(wa.shape, const2),
            pl.BlockSpec(wb.shape, const2),
            pl.BlockSpec(wo.shape, const2),
            pl.BlockSpec(gf.shape, const2),
            pl.BlockSpec(wq.shape, const2),
            pl.BlockSpec(keys.shape, lambda i: (0, 0, 0)),
        ],
        out_specs=[
            pl.BlockSpec((td, d), row),
            pl.BlockSpec((td, d), row),
            pl.BlockSpec((td, nsel), row),
            pl.BlockSpec((td, nsel), row),
        ],
        out_shape=[
            jax.ShapeDtypeStruct((t, d), F32),
            jax.ShapeDtypeStruct((t, d), F32),
            jax.ShapeDtypeStruct((t, nsel), jnp.int32),
            jax.ShapeDtypeStruct((t, nsel), F32),
        ],
        compiler_params=_cparams(("arbitrary",)),
        name="merge_route",
    )(x, gt, ya, yb, wa, wb, wo, gf, wq, keys)


def _coef_kernel(act_ref, wt_ref, o_ref):
    act = act_ref[...]
    gelu = 0.5 * act * (1.0 + lax.erf(act * (2.0 ** -0.5)))
    bits = pltpu.bitcast((wt_ref[...] * gelu).astype(BF16).astype(F32), jnp.int32)
    o_ref[...] = bits | lax.shift_right_logical(bits, 16)


def _expert_coefs(act, wt):
    t, nsel = wt.shape
    rows = math.gcd(t, 512)
    row = lambda i: (i, 0)
    return pl.pallas_call(
        _coef_kernel,
        grid=(t // rows,),
        in_specs=[pl.BlockSpec((rows, nsel), row), pl.BlockSpec((rows, nsel), row)],
        out_specs=pl.BlockSpec((rows, nsel), row),
        out_shape=jax.ShapeDtypeStruct((t, nsel), jnp.int32),
        compiler_params=_cparams(("arbitrary",)),
        name="expert_coefs",
    )(act, wt)


def _finish_kernel(xm_ref, o_ref, gfin_ref, yacc_hbm, y_ref):
    del yacc_hbm
    y_ref[...] = _rms(xm_ref[...] + o_ref[...], gfin_ref[...])


def _finish(xm, o, gfin, y_acc, row_off):
    t, d = xm.shape
    rows = math.gcd(t, 512)
    first = row_off // rows
    row = lambda i: (i, 0)
    return pl.pallas_call(
        _finish_kernel,
        grid=(t // rows,),
        in_specs=[pl.BlockSpec((rows, d), row), pl.BlockSpec((rows, d), row),
                  pl.BlockSpec(gfin.shape, lambda i: (0, 0)), pl.BlockSpec(memory_space=pl.ANY)],
        out_specs=pl.BlockSpec((rows, d), lambda i: (i + first, 0)),
        out_shape=jax.ShapeDtypeStruct(y_acc.shape, F32),
        input_output_aliases={3: 0},
        compiler_params=_cparams(("arbitrary",)),
        name="finish",
    )(xm, o, gfin, y_acc)


def _rope_table(pos, valid, rope_dim, head_dim):
    half = rope_dim // 2
    inv = ROPE_THETA ** (-jnp.arange(0, rope_dim, 2, dtype=F32) / rope_dim)
    ang = pos.astype(F32)[:, None] * inv[None, :]
    cos, sin = jnp.cos(ang), jnp.sin(ang)
    n = pos.shape[0]
    ones = jnp.ones((n, head_dim - rope_dim), F32)
    zeros_h = jnp.zeros((n, half), F32)
    zeros_r = jnp.zeros((n, head_dim - rope_dim), F32)
    reps = LANES // head_dim
    cosf = jnp.tile(jnp.concatenate([cos, cos, ones], axis=1), (1, reps))
    sin_lo = jnp.tile(jnp.concatenate([zeros_h, sin, zeros_r], axis=1), (1, reps))
    sin_hi = jnp.tile(jnp.concatenate([-sin, zeros_h, zeros_r], axis=1), (1, reps))
    vcol = jnp.broadcast_to(valid.astype(F32)[:, None], (n, LANES))
    return jnp.concatenate([cosf, sin_lo, sin_hi, vcol], axis=1)


def _pack_pairs(x):
    half = x.shape[1] // 2
    lo = lax.bitcast_convert_type(x[:, :half].astype(BF16), jnp.uint16).astype(jnp.uint32)
    hi = lax.bitcast_convert_type(x[:, half:].astype(BF16), jnp.uint16).astype(jnp.uint32)
    return lax.bitcast_convert_type(lo | (hi << 16), jnp.int32)


def _expert_dots(table, idx, hn, nsel):
    n = idx.shape[0]
    c = table.shape[1]
    sc = plsc.get_sparse_core_info()
    lanes = sc.num_lanes
    workers = sc.num_cores * sc.num_subcores
    gw = GATHER_WINDOW
    nbuf = GATHER_BUFFERS
    per_worker = n // workers
    ich = min(DOTS_INDEX_CHUNK, per_worker)
    tok = ich // nsel
    assert n % workers == 0 and per_worker % ich == 0 and nsel == nbuf * gw and hn.shape[1] == c and gw % lanes == 0
    mesh = plsc.VectorSubcoreMesh(core_axis_name="c", subcore_axis_name="s")

    @functools.partial(
        pl.kernel, out_type=jax.ShapeDtypeStruct((n,), F32), mesh=mesh, name="expert_dots",
        compiler_params=pltpu.CompilerParams(needs_layout_passes=False),
        scratch_types=[pltpu.VMEM((ich,), jnp.int32), pltpu.VMEM((tok, c), jnp.int32), pltpu.VMEM((ich,), F32),
                       pltpu.VMEM((lanes * lanes,), F32)]
                      + [pltpu.VMEM((gw, c), jnp.int32)] * nbuf + [pltpu.SemaphoreType.DMA] * nbuf)
    def dots(tab_hbm, idx_hbm, hn_hbm, act_hbm, idx_v, h_v, act_v, scr, *bufs_sems):
        rows, sems = bufs_sems[:nbuf], bufs_sems[nbuf:]
        wid = lax.axis_index("s") * sc.num_cores + lax.axis_index("c")
        base = wid * per_worker
        lane = lax.iota(jnp.int32, lanes)

        def gather(win, buf, sem):
            return pltpu.make_async_copy(tab_hbm.at[idx_v.at[pl.ds(win * gw, gw)]], buf, sem)

        def reduce_window(buf, t_loc, out_off):
            for rb in range(gw // lanes):
                def kbody(k2, accs):
                    k0 = 2 * k2 * lanes
                    hw0 = plsc.bitcast(h_v[t_loc, pl.ds(k0, lanes)], BF16)
                    hw1 = plsc.bitcast(h_v[t_loc, pl.ds(k0 + lanes, lanes)], BF16)
                    out = []
                    for r in range(lanes):
                        w0 = plsc.bitcast(buf[rb * lanes + r, pl.ds(k0, lanes)], BF16)
                        w1 = plsc.bitcast(buf[rb * lanes + r, pl.ds(k0 + lanes, lanes)], BF16)
                        p = plsc.bitcast(w0 * hw0 + w1 * hw1, jnp.int32)
                        out.append(accs[r] + lax.bitcast_convert_type(p << 16, F32)
                                   + lax.bitcast_convert_type(p, F32))
                    return tuple(out)

                accs = lax.fori_loop(0, c // (2 * lanes), kbody,
                                     tuple(jnp.zeros((lanes,), F32) for _ in range(lanes)))
                for r in range(lanes):
                    scr[pl.ds(r * lanes, lanes)] = accs[r]
                cols = [plsc.load_gather(scr, [lane * lanes + l]) for l in range(lanes)]
                while len(cols) > 1:
                    cols = [cols[i] + cols[i + 1] for i in range(0, len(cols), 2)]
                act_v[pl.ds(out_off + rb * lanes, lanes)] = cols[0]

        @pl.loop(0, per_worker // ich)
        def _(g):
            cb = base + g * ich
            pltpu.sync_copy(idx_hbm.at[pl.ds(cb, ich)], idx_v)
            tok_base = pl.multiple_of(wid * (per_worker // nsel) + g * tok, tok)
            pltpu.sync_copy(hn_hbm.at[pl.ds(tok_base, tok)], h_v)
            for q in range(nbuf):
                gather(q, rows[q], sems[q]).start()

            @pl.loop(0, tok)
            def _(j):
                for q in range(nbuf):
                    gather(nbuf * j + q, rows[q], sems[q]).wait()
                    reduce_window(rows[q], j, j * nsel + q * gw)

                    @pl.when(j + 1 < tok)
                    def _():
                        gather(nbuf * (j + 1) + q, rows[q], sems[q]).start()

            pltpu.sync_copy(act_v, act_hbm.at[pl.ds(cb, ich)])

    return dots(table, idx, hn)


def _expert_mix(table, idx, coef, nsel):
    n = idx.shape[0]
    c = table.shape[1]
    d = 2 * c
    sc = plsc.get_sparse_core_info()
    lanes = sc.num_lanes
    workers = sc.num_cores * sc.num_subcores
    gw = GATHER_WINDOW
    nbuf = GATHER_BUFFERS
    per_worker = n // workers
    ich = min(MIX_INDEX_CHUNK, per_worker)
    tok = ich // nsel
    kblock = 16
    assert n % workers == 0 and per_worker % ich == 0 and nsel == nbuf * gw and c % (kblock * lanes) == 0
    mesh = plsc.VectorSubcoreMesh(core_axis_name="c", subcore_axis_name="s")

    @functools.partial(
        pl.kernel, out_type=jax.ShapeDtypeStruct((n // nsel, d), F32), mesh=mesh, name="expert_mix",
        compiler_params=pltpu.CompilerParams(needs_layout_passes=False),
        scratch_types=[pltpu.VMEM((ich,), jnp.int32), pltpu.VMEM((ich,), jnp.int32), pltpu.VMEM((tok, d), F32)]
                      + [pltpu.VMEM((gw, c), jnp.int32)] * nbuf + [pltpu.SemaphoreType.DMA] * nbuf)
    def mix(tab_hbm, idx_hbm, coef_hbm, out_hbm, idx_v, coef_v, out_v, *bufs_sems):
        rows, sems = bufs_sems[:nbuf], bufs_sems[nbuf:]
        wid = lax.axis_index("s") * sc.num_cores + lax.axis_index("c")
        base = wid * per_worker
        zero_idx = jnp.zeros((lanes,), jnp.int32)

        def gather(win, buf, sem):
            return pltpu.make_async_copy(tab_hbm.at[idx_v.at[pl.ds(win * gw, gw)]], buf, sem)

        def accumulate_window(buf, t_loc, coef_off, first):
            for kb in range(c // (kblock * lanes)):
                col0 = kb * kblock * lanes
                if first:
                    init = tuple(jnp.zeros((lanes,), F32) for _ in range(2 * kblock))
                else:
                    init = tuple(out_v[t_loc, pl.ds(col0 + i * lanes, lanes)] for i in range(kblock)) + \
                           tuple(out_v[t_loc, pl.ds(c + col0 + i * lanes, lanes)] for i in range(kblock))

                def rbody(r2, accs):
                    accs = list(accs)
                    r = 2 * r2
                    cw0 = plsc.bitcast(plsc.load_gather(coef_v, [zero_idx + (coef_off + r)]), BF16)
                    cw1 = plsc.bitcast(plsc.load_gather(coef_v, [zero_idx + (coef_off + r + 1)]), BF16)
                    for i in range(kblock):
                        w0 = plsc.bitcast(buf[r, pl.ds(col0 + i * lanes, lanes)], BF16)
                        w1 = plsc.bitcast(buf[r + 1, pl.ds(col0 + i * lanes, lanes)], BF16)
                        p = plsc.bitcast(w0 * cw0 + w1 * cw1, jnp.int32)
                        accs[i] = accs[i] + lax.bitcast_convert_type(p << 16, F32)
                        accs[kblock + i] = accs[kblock + i] + lax.bitcast_convert_type(p, F32)
                    return tuple(accs)

                accs = lax.fori_loop(0, gw // 2, rbody, init)
                for i in range(kblock):
                    out_v[t_loc, pl.ds(col0 + i * lanes, lanes)] = accs[i]
                    out_v[t_loc, pl.ds(c + col0 + i * lanes, lanes)] = accs[kblock + i]

        @pl.loop(0, per_worker // ich)
        def _(g):
            cb = base + g * ich
            pltpu.sync_copy(idx_hbm.at[pl.ds(cb, ich)], idx_v)
            pltpu.sync_copy(coef_hbm.at[pl.ds(cb, ich)], coef_v)
            for q in range(nbuf):
                gather(q, rows[q], sems[q]).start()

            @pl.loop(0, tok)
            def _(j):
                for q in range(nbuf):
                    gather(nbuf * j + q, rows[q], sems[q]).wait()
                    accumulate_window(rows[q], j, j * nsel + q * gw, q == 0)

                    @pl.when(j + 1 < tok)
                    def _():
                        gather(nbuf * (j + 1) + q, rows[q], sems[q]).start()

            tok_base = pl.multiple_of(wid * (per_worker // nsel) + g * tok, tok)
            pltpu.sync_copy(out_v, out_hbm.at[pl.ds(tok_base, tok)])

    return mix(table, idx, coef)


def _peer_tail(xm, hn, et, wt, tabs, gfin, y_acc, row_off):
    u_tab, v_tab = tabs
    t, nsel = et.shape
    eidx = et.reshape(t * nsel)
    act = _expert_dots(u_tab, eidx, _pack_pairs(hn), nsel).reshape(t, nsel)
    coef = _expert_coefs(act, wt).reshape(t * nsel)
    mixed = _expert_mix(v_tab, eidx, coef, nsel)
    return _finish(xm, mixed, gfin, y_acc, row_off)


def kernel(x_prompt, x_sample, cache_k_window, cache_v_window, state_gla, meta_tokens, g_norm_mix, w_in,
           w_gate_up, b_gate, attn_sinks, g_gla_norm, w_branch_a, w_branch_b, w_out, g_norm_ffn, w_peer_q,
           peer_sub_keys, peer_u, peer_v, g_norm_final):
    bsz, seq, d = x_prompt.shape
    dbsz, tdec, _ = x_sample.shape
    n_meta = meta_tokens.shape[0]
    depth = w_in.shape[0]
    window = cache_k_window.shape[2]
    kv_heads, head_dim = cache_k_window.shape[3], cache_k_window.shape[4]
    gate_rank = w_gate_up.shape[1]
    bqk = w_gate_up.shape[2]
    n_ph, _, n_keys, p_half = peer_sub_keys.shape[1:]
    assert depth == 1 and d == 1024 and window == ATTN_BLOCK and kv_heads == 2 and head_dim == 64
    assert bqk == 256 and state_gla.shape[2:] == (4, 64, 128) and n_meta <= ATTN_BLOCK
    assert seq % ATTN_BLOCK == 0 and tdec <= SAMPLE_PAD and n_keys == 128 and p_half == 64 and n_ph == 8
    rope_dim = head_dim // 4
    meta_pad = ATTN_BLOCK - n_meta
    lp = ATTN_BLOCK + seq
    nblk = lp // ATTN_BLOCK

    w = w_in[0]
    c_lr = 2304
    c_gate = c_lr + gate_rank
    w1 = w[:, :c_lr].astype(BF16)
    wlr = jnp.pad(w[:, c_lr:c_gate], ((0, 0), (0, LANES - gate_rank))).astype(BF16)
    w2 = w[:, c_gate:].astype(BF16)
    wgu = jnp.pad(w_gate_up[0], ((0, LANES - gate_rank), (0, 0))).astype(BF16)
    bg = b_gate[0][None, :]
    gmix = g_norm_mix[0][None, :]
    wa = w_branch_a[0].astype(BF16)
    wb = w_branch_b[0].astype(BF16)
    wo = w_out[0].astype(BF16)
    gffn = g_norm_ffn[0][None, :]
    wq = w_peer_q[0].astype(BF16)
    keys = peer_sub_keys[0].reshape(n_ph * 2, n_keys, p_half).astype(BF16)
    u_tab = _pack_pairs(peer_u[0])
    gfin = g_norm_final[None, :]
    gn = g_gla_norm[0][None, :]
    sinks = attn_sinks[0]
    qk_scale = float(bqk // 4) ** -0.5

    rows_p = jnp.arange(lp)
    tab_p = _rope_table(rows_p - meta_pad, rows_p >= meta_pad, rope_dim, head_dim)
    proj_rows = max(r for r in range(16, PROJ_ROWS + 1, 16) if lp % r == 0)
    nq = nblk - 1
    nchunks = lp // GLA_CHUNK
    skip = ATTN_BLOCK // GLA_CHUNK
    ncq = nchunks - skip
    per_seq = seq // MERGE_ROWS
    gt_map_p = lambda i: ((i // per_seq) * nblk + 1 + (i % per_seq), 0)

    meta = jnp.broadcast_to(meta_tokens[None].astype(x_prompt.dtype), (bsz, n_meta, d))
    xpad = jnp.concatenate([jnp.zeros((bsz, meta_pad, d), x_prompt.dtype), meta, x_prompt], axis=1)
    xpad = xpad.reshape(bsz * lp, d)
    x_rows = x_prompt.reshape(bsz * seq, d)

    def mix_and_route(b0, gb):
        qa, kv, gl, gt = _project(xpad, gmix, tab_p, w1, wlr, wgu, bg, w2, proj_rows, qk_scale, b0 * lp, gb * lp)
        ya = _attention(
            sinks, qa, kv, kv, gb, nq, ATTN_BLOCK,
            lambda b, n: (b * nblk + n + 1, 0), lambda b, n: (b * nblk + n, 0), lambda b, n: (b * nblk + n + 1, 0),
            lambda b, n: (b * nq + n, 0), gb * seq, first_valid_key=meta_pad, block_offset=1)
        s0 = jnp.zeros((gb,) + state_gla.shape[2:], F32)
        yb, s_fin = _gla(gl, s0, gn, gb, nchunks, GLA_CHUNK,
                         lambda b, c: (b * nchunks + c, 0),
                         lambda b, c: (b * ncq + jnp.maximum(c - skip, 0), 0), gb * seq)
        xm, hn, et, wt = _merge_route(x_rows, gt, ya, yb, wa, wb, wo, gffn, wq, keys, gt_map_p, b0 * seq)
        kv_w = kv.reshape(gb, lp, 2, kv_heads, head_dim)[:, lp - window:]
        return (xm, hn, et, wt), kv_w, s_fin

    group = PROMPT_GROUP if (bsz % PROMPT_GROUP == 0 and (PROMPT_GROUP * seq) % GATHER_ROW_QUANTUM == 0) else bsz
    tabs = (u_tab, _pack_pairs(peer_v[0]))
    y_acc, kv_parts, s_parts = jnp.zeros((bsz * seq, d), F32), [], []
    for b0 in range(0, bsz, group):
        routed, kv_w, s_fin = mix_and_route(b0, group)
        y_acc = _peer_tail(*routed, tabs, gfin, y_acc, b0 * seq)
        kv_parts.append(kv_w)
        s_parts.append(s_fin)
    y_prompt = y_acc.reshape(bsz, seq, d)
    kv_p = jnp.concatenate(kv_parts, axis=0)
    s_fin_p = jnp.concatenate(s_parts, axis=0)
    new_k_p = kv_p[:, :, 0][None]
    new_v_p = kv_p[:, :, 1][None]

    sp = SAMPLE_PAD
    xs_pad = jnp.pad(x_sample, ((0, 0), (0, sp - tdec), (0, 0))).reshape(dbsz * sp, d)
    rows_s = jnp.arange(sp)
    reps = 256 // sp
    tab_s = jnp.tile(_rope_table(PAST_LEN + rows_s, rows_s < tdec, rope_dim, head_dim), (reps, 1))
    qa_s, kv_s, gl_s, gt_s = _project(xs_pad, gmix, tab_s, w1, wlr, wgu, bg, w2, 256, qk_scale, 0, dbsz * sp)

    cache_kv = jnp.concatenate([cache_k_window[0].reshape(dbsz * window, kv_heads * head_dim),
                                cache_v_window[0].reshape(dbsz * window, kv_heads * head_dim)], axis=1)
    seq_map = lambda b, n: (b, 0)
    ya_s = _attention(sinks, qa_s, cache_kv, kv_s, dbsz, 1, sp, seq_map, seq_map, seq_map, seq_map,
                      dbsz * sp, first_valid_key=None, block_offset=0)
    yb_s, s_fin_s = _gla(gl_s, state_gla[0], gn, dbsz, 1, sp, seq_map, seq_map, dbsz * sp)

    def real_rows(a):
        return a.reshape(dbsz, sp, a.shape[-1])[:, :tdec].reshape(dbsz * tdec, a.shape[-1])

    xs_rows = x_sample.reshape(dbsz * tdec, d)
    xm_s, hn_s, et_s, wt_s = _merge_route(xs_rows, real_rows(gt_s), real_rows(ya_s), real_rows(yb_s),
                                          wa, wb, wo, gffn, wq, keys, lambda i: (i, 0), 0)
    hn_s, _ = lax.optimization_barrier((hn_s, y_acc))
    y_sample = _peer_tail(xm_s, hn_s, et_s, wt_s, tabs, gfin,
                          jnp.zeros((dbsz * tdec, d), F32), 0).reshape(dbsz, tdec, d)

    kv_new = real_rows(kv_s).reshape(dbsz, tdec, 2, kv_heads, head_dim)
    new_k_s = jnp.concatenate([cache_k_window[0].astype(F32), kv_new[:, :, 0]], axis=1)[:, -window:][None]
    new_v_s = jnp.concatenate([cache_v_window[0].astype(F32), kv_new[:, :, 1]], axis=1)[:, -window:][None]

    return (y_prompt, y_sample, new_k_p, new_v_p, s_fin_p[None], new_k_s, new_v_s, s_fin_s[None])
```

```python
import functools
import math

import jax
import jax.numpy as jnp
from jax import lax
from jax.experimental import pallas as pl
from jax.experimental.pallas import tpu as pltpu
from jax.experimental.pallas import tpu_sc as plsc

F32 = jnp.float32
BF16 = jnp.bfloat16

EPS = 1e-6
NEG_INF = -1e30
PAST_LEN = 16384
ROPE_THETA = 500000.0
GATE_NORMALIZER = 16.0
PEER_TOPK = 16

LANES = 128
SUBLANES = 8
VMEM_LIMIT_BYTES = 56 * 1024 * 1024

ATTN_BLOCK = 128
GLA_CHUNK = 64
SAMPLE_PAD = 16
PROJ_ROWS = 544
MERGE_ROWS = 128
GATHER_WINDOW = 32
GATHER_BUFFERS = 4
DOTS_INDEX_CHUNK = 8192
MIX_INDEX_CHUNK = 4096
GATHER_ROW_QUANTUM = 512
PROMPT_GROUP = 1


def _cparams(sem):
    return pltpu.CompilerParams(dimension_semantics=sem, vmem_limit_bytes=VMEM_LIMIT_BYTES)


def _rms(x, g):
    ms = jnp.mean(x * x, axis=-1, keepdims=True)
    return (x * lax.rsqrt(ms + EPS)) * g


def _proj_kernel(x_ref, g_ref, tab_ref, w1_ref, wlr_ref, wgu_ref, bg_ref, w2_ref,
                 qa_ref, kv_ref, gl_ref, gt_ref, *, period, qk_scale):
    i = pl.program_id(0)
    tr = x_ref.shape[0]
    hb = _rms(x_ref[...], g_ref[...]).astype(BF16)
    z1 = jnp.dot(hb, w1_ref[...], preferred_element_type=F32)

    start = pl.multiple_of((i * tr) % period, SUBLANES)
    tab = tab_ref[pl.ds(start, tr), :]
    cosf = tab[:, 0:LANES]
    sin_lo = tab[:, LANES:2 * LANES]
    sin_hi = tab[:, 2 * LANES:3 * LANES]
    valid = tab[:, 3 * LANES:3 * LANES + 1]

    def rope(xg):
        return xg * cosf + pltpu.roll(xg, 8, 1) * sin_lo + pltpu.roll(xg, LANES - 8, 1) * sin_hi

    for gi in range(4):
        sl = slice(gi * LANES, (gi + 1) * LANES)
        qa_ref[:, sl] = rope(z1[:, sl]).astype(BF16)
    kv_ref[:, 0:LANES] = rope(z1[:, 512:640])
    kv_ref[:, LANES:2 * LANES] = z1[:, 640:768]

    lr = jnp.dot(hb, wlr_ref[...], preferred_element_type=F32)
    pre = jnp.dot(lr.astype(BF16), wgu_ref[...], preferred_element_type=F32) + bg_ref[...]
    log_sig = jnp.minimum(pre, 0.0) - jnp.log1p(jnp.exp(-jnp.abs(pre)))
    ld = jnp.where(valid > 0.5, log_sig / GATE_NORMALIZER, 0.0)

    gl_ref[:, 0:256] = z1[:, 768:1024] * qk_scale
    gl_ref[:, 256:512] = z1[:, 1024:1280]
    gl_ref[:, 512:768] = ld
    gl_ref[:, 768:1792] = z1[:, 1280:2304]
    gt_ref[...] = jnp.dot(hb, w2_ref[...], preferred_element_type=F32)


def _project(x, g, tab, w1, wlr, wgu, bg, w2, rows, qk_scale, first_row, r):
    d = x.shape[1]
    period = tab.shape[0]
    const = lambda i: (0, 0)
    row = lambda i: (i, 0)
    first = first_row // rows
    return pl.pallas_call(
        functools.partial(_proj_kernel, period=period, qk_scale=qk_scale),
        grid=(r // rows,),
        in_specs=[
            pl.BlockSpec((rows, d), lambda i: (i + first, 0)),
            pl.BlockSpec(g.shape, const),
            pl.BlockSpec(tab.shape, const),
            pl.BlockSpec(w1.shape, const),
            pl.BlockSpec(wlr.shape, const),
            pl.BlockSpec(wgu.shape, const),
            pl.BlockSpec(bg.shape, const),
            pl.BlockSpec(w2.shape, const),
        ],
        out_specs=[
            pl.BlockSpec((rows, 512), row),
            pl.BlockSpec((rows, 256), row),
            pl.BlockSpec((rows, 1792), row),
            pl.BlockSpec((rows, 2048), row),
        ],
        out_shape=[
            jax.ShapeDtypeStruct((r, 512), BF16),
            jax.ShapeDtypeStruct((r, 256), F32),
            jax.ShapeDtypeStruct((r, 1792), F32),
            jax.ShapeDtypeStruct((r, 2048), F32),
        ],
        compiler_params=_cparams(("arbitrary",)),
        name="proj",
    )(x, g, tab, w1, wlr, wgu, bg, w2)


def _attn_kernel(sink_ref, q_ref, prev_ref, cur_ref, o_ref, *, first_valid_key, block_offset):
    n = pl.program_id(1)
    qr = q_ref.shape[0]
    kr = cur_ref.shape[0]
    w = prev_ref.shape[0]
    nk = w + kr
    group = 4
    hd = 64

    rows = lax.broadcasted_iota(jnp.int32, (group * qr, nk), 0)
    cols = lax.broadcasted_iota(jnp.int32, (group * qr, nk), 1)
    head_of_row = rows // qr
    diff = (rows - head_of_row * qr) - cols + w
    mask = (diff >= 0) & (diff <= w)
    if first_valid_key is not None:
        blk = n + block_offset
        mask = mask & (cols >= first_valid_key + w - blk * w)

    prev = prev_ref[...]
    cur = cur_ref[...]
    q = q_ref[...]
    row_head = lax.broadcasted_iota(jnp.int32, (group * qr, 1), 0) // qr
    for kh in range(2):
        k = jnp.concatenate([prev[:, kh * hd:(kh + 1) * hd], cur[:, kh * hd:(kh + 1) * hd]], axis=0).astype(BF16)
        v = jnp.concatenate([prev[:, LANES + kh * hd:LANES + (kh + 1) * hd],
                             cur[:, LANES + kh * hd:LANES + (kh + 1) * hd]], axis=0).astype(BF16)
        qs = jnp.concatenate([q[:, (group * kh + g) * hd:(group * kh + g + 1) * hd] for g in range(group)], axis=0)
        s = lax.dot_general(qs, k, (((1,), (1,)), ((), ())), preferred_element_type=F32) * (hd ** -0.5)
        s = jnp.where(mask, s, NEG_INF)
        sink = jnp.zeros((group * qr, 1), F32)
        for g in range(group):
            sink = jnp.where(row_head == g, sink_ref[group * kh + g], sink)
        m = jnp.maximum(jnp.max(s, axis=-1, keepdims=True), sink)
        e = jnp.exp(s - m)
        p = e / (jnp.sum(e, axis=-1, keepdims=True) + jnp.exp(sink - m))
        o = jnp.dot(p.astype(BF16), v, preferred_element_type=F32)
        for g in range(group):
            h = group * kh + g
            o_ref[:, h * hd:(h + 1) * hd] = o[g * qr:(g + 1) * qr].astype(BF16)


def _attention(sinks, q, kv_prev, kv_cur, nb, nblk, qr, q_map, prev_map, cur_map, out_map, out_rows,
               first_valid_key, block_offset):
    w = ATTN_BLOCK
    return pl.pallas_call(
        functools.partial(_attn_kernel, first_valid_key=first_valid_key, block_offset=block_offset),
        grid=(nb, nblk),
        in_specs=[
            pl.BlockSpec(memory_space=pltpu.SMEM),
            pl.BlockSpec((qr, 512), q_map),
            pl.BlockSpec((w, 256), prev_map),
            pl.BlockSpec((qr, 256), cur_map),
        ],
        out_specs=pl.BlockSpec((qr, 512), out_map),
        out_shape=jax.ShapeDtypeStruct((out_rows, 512), BF16),
        compiler_params=_cparams(("arbitrary", "arbitrary")),
        name="swa",
    )(sinks, q, kv_prev, kv_cur)


def _gla_kernel(gl_ref, s0_ref, gn_ref, yb_ref, sfin_ref, st_ref, *, ch):
    c = pl.program_id(1)
    nh, dk, dv = 4, 64, 128

    @pl.when(c == 0)
    def _():
        for h in range(nh):
            st_ref[h] = s0_ref[0, h].T

    row = lax.broadcasted_iota(jnp.int32, (ch, nh * dk), 0)
    causal = (lax.broadcasted_iota(jnp.int32, (ch, ch), 0) >= lax.broadcasted_iota(jnp.int32, (ch, ch), 1))
    gn = gn_ref[...]
    nt = (((1,), (1,)), ((), ()))
    for sub in range(gl_ref.shape[0] // ch):
        rs = slice(sub * ch, (sub + 1) * ch)
        gl = gl_ref[rs, :]
        q = gl[:, 0:256]
        k = gl[:, 256:512]
        b = gl[:, 512:768]
        sh = 1
        while sh < ch:
            b = b + jnp.where(row >= sh, pltpu.roll(b, sh, 0), 0.0)
            sh *= 2
        b_last = b[ch - 1:ch, :]
        q_t = (q * jnp.exp(b)).astype(BF16)
        k_t = (k * jnp.exp(-b)).astype(BF16)
        k_end = (k * jnp.exp(b_last - b)).astype(BF16)
        decay = jnp.exp(b_last)
        for h in range(nh):
            ks = slice(h * dk, (h + 1) * dk)
            v = gl[:, 768 + h * dv:768 + (h + 1) * dv]
            vb = v.astype(BF16)
            a = lax.dot_general(q_t[:, ks], k_t[:, ks], nt, preferred_element_type=F32)
            a = jnp.where(causal, a, 0.0)
            s_t = st_ref[h]
            o = jnp.dot(a.astype(BF16), vb, preferred_element_type=F32)
            o = o + lax.dot_general(q_t[:, ks], s_t.astype(BF16), nt, preferred_element_type=F32)
            upd = jnp.dot(v.T.astype(BF16), k_end[:, ks], preferred_element_type=F32)
            st_ref[h] = s_t * decay[:, ks] + upd
            go = gl[:, 1280 + h * dv:1280 + (h + 1) * dv]
            y = _rms(o, gn) * (go * jax.nn.sigmoid(go))
            yb_ref[rs, h * dv:(h + 1) * dv] = y.astype(BF16)

    @pl.when(c == pl.num_programs(1) - 1)
    def _():
        for h in range(nh):
            sfin_ref[0, h] = st_ref[h].T


def _gla(gl, s0, gn, nb, nsteps, rows, ch, in_map, out_map, out_rows):
    return pl.pallas_call(
        functools.partial(_gla_kernel, ch=ch),
        grid=(nb, nsteps),
        in_specs=[
            pl.BlockSpec((rows, 1792), in_map),
            pl.BlockSpec((1, 4, 64, 128), lambda b, c: (b, 0, 0, 0)),
            pl.BlockSpec((1, 128), lambda b, c: (0, 0)),
        ],
        out_specs=[
            pl.BlockSpec((rows, 512), out_map),
            pl.BlockSpec((1, 4, 64, 128), lambda b, c: (b, 0, 0, 0)),
        ],
        out_shape=[
            jax.ShapeDtypeStruct((out_rows, 512), BF16),
            jax.ShapeDtypeStruct((nb, 4, 64, 128), F32),
        ],
        scratch_shapes=[pltpu.VMEM((4, 128, 64), F32)],
        compiler_params=_cparams(("arbitrary", "arbitrary")),
        name="gla",
    )(gl, s0, gn)


def _extract_topk(work, nsel, iota0, sentinel, payload=None):
    slabs = work.shape[0] // SUBLANES
    assert slabs & (slabs - 1) == 0
    vals, idxs, picks = [], [], []
    for j in range(nsel):
        v = [work[i * SUBLANES:(i + 1) * SUBLANES] for i in range(slabs)]
        ix = [iota0[i * SUBLANES:(i + 1) * SUBLANES] for i in range(slabs)]
        while len(v) > 1:
            keep = [v[i] >= v[i + 1] for i in range(0, len(v), 2)]
            ix = [jnp.where(k, ix[2 * i], ix[2 * i + 1]) for i, k in enumerate(keep)]
            v = [jnp.where(k, v[2 * i], v[2 * i + 1]) for i, k in enumerate(keep)]
        m = jnp.max(v[0], axis=0, keepdims=True)
        idx = jnp.min(jnp.where(v[0] == m, ix[0], sentinel), axis=0, keepdims=True)
        vals.append(m)
        idxs.append(idx)
        hit = iota0 == idx
        if payload is not None:
            picks.append(jnp.max(jnp.where(hit, payload, -1), axis=0, keepdims=True))
        if j + 1 < nsel:
            work = jnp.where(hit, -jnp.inf, work)
    return vals, idxs, picks


def _merge_kernel(x_ref, gt_ref, ya_ref, yb_ref, wa_ref, wb_ref, wo_ref, gf_ref, wq_ref, keys_ref,
                  xm_ref, hn_ref, et_ref, wt_ref):
    td = x_ref.shape[0]
    nkeys = keys_ref.shape[1]
    half = keys_ref.shape[2]
    nheads = keys_ref.shape[0] // 2
    topk = PEER_TOPK

    gt = gt_ref[...]
    d = x_ref.shape[1]
    ma = jnp.dot(ya_ref[...], wa_ref[...], preferred_element_type=F32)
    mb = jnp.dot(yb_ref[...], wb_ref[...], preferred_element_type=F32)
    m = jax.nn.sigmoid(gt[:, 0:d]) * ma + jax.nn.sigmoid(gt[:, d:2 * d]) * mb
    xm = x_ref[...] + jnp.dot(m.astype(BF16), wo_ref[...], preferred_element_type=F32)
    xm_ref[...] = xm
    hn = _rms(xm, gf_ref[...])
    hn_ref[...] = hn
    q = jnp.dot(hn.astype(BF16), wq_ref[...], preferred_element_type=F32).astype(BF16)

    nt = (((1,), (1,)), ((), ()))
    iota_k = lax.broadcasted_iota(jnp.int32, (nkeys, td), 0)
    pair_rows = [topk // (a + 1) for a in range(topk)]
    ncand = SUBLANES
    while ncand < sum(pair_rows):
        ncand *= 2
    cand_pad = ncand - sum(pair_rows)
    iota_c = lax.broadcasted_iota(jnp.int32, (ncand, td), 0)
    wts, ids = [], []
    for h in range(nheads):
        sv, si = [], []
        for c in range(2):
            gi = 2 * h + c
            s_t = lax.dot_general(keys_ref[gi], q[:, gi * half:(gi + 1) * half], nt,
                                  preferred_element_type=F32)
            vals, idxs, _ = _extract_topk(s_t, topk, iota_k, nkeys)
            sv.append(vals)
            si.append(idxs)
        sv1 = jnp.concatenate(sv[1], axis=0)
        si1 = jnp.concatenate(si[1], axis=0)
        cand = jnp.concatenate([sv[0][a] + sv1[0:nb] for a, nb in enumerate(pair_rows)]
                               + [jnp.full((cand_pad, td), -jnp.inf, F32)], axis=0)
        cidx = jnp.concatenate([si[0][a] * nkeys + si1[0:nb] for a, nb in enumerate(pair_rows)]
                               + [jnp.full((cand_pad, td), -1, jnp.int32)], axis=0)
        fvals, _, eids = _extract_topk(cand, topk, iota_c, ncand, payload=cidx)
        fv = jnp.concatenate(fvals, axis=0)
        e = jnp.exp(fv - fvals[0])
        wts.append(e / jnp.sum(e, axis=0, keepdims=True))
        ids.extend(eids)
    wt_ref[...] = jnp.concatenate(wts, axis=0).T
    et_ref[...] = jnp.concatenate(ids, axis=0).T


def _merge_route(x, gt, ya, yb, wa, wb, wo, gf, wq, keys, gt_map, x_first_row):
    t = ya.shape[0]
    d = x.shape[1]
    td = MERGE_ROWS
    nsel = (keys.shape[0] // 2) * PEER_TOPK
    const2 = lambda i: (0, 0)
    row = lambda i: (i, 0)
    x_first = x_first_row // td
    return pl.pallas_call(
        _merge_kernel,
        grid=(t // td,),
        in_specs=[
            pl.BlockSpec((td, d), lambda i: (i + x_first, 0)),
            pl.BlockSpec((td, 2 * d), gt_map),
            pl.BlockSpec((td, ya.shape[1]), row),
            pl.BlockSpec((td, yb.shape[1]), row),
            pl.BlockSpec(wa.shape, const2),
            pl.BlockSpec(wb.shape, const2),
            pl.BlockSpec(wo.shape, const2),
            pl.BlockSpec(gf.shape, const2),
            pl.BlockSpec(wq.shape, const2),
            pl.BlockSpec(keys.shape, lambda i: (0, 0, 0)),
        ],
        out_specs=[
            pl.BlockSpec((td, d), row),
            pl.BlockSpec((td, d), row),
            pl.BlockSpec((td, nsel), row),
            pl.BlockSpec((td, nsel), row),
        ],
        out_shape=[
            jax.ShapeDtypeStruct((t, d), F32),
            jax.ShapeDtypeStruct((t, d), F32),
            jax.ShapeDtypeStruct((t, nsel), jnp.int32),
            jax.ShapeDtypeStruct((t, nsel), F32),
        ],
        compiler_params=_cparams(("arbitrary",)),
        name="merge_route",
    )(x, gt, ya, yb, wa, wb, wo, gf, wq, keys)


def _coef_kernel(act_ref, wt_ref, o_ref):
    act = act_ref[...]
    gelu = 0.5 * act * (1.0 + lax.erf(act * (2.0 ** -0.5)))
    bits = pltpu.bitcast((wt_ref[...] * gelu).astype(BF16).astype(F32), jnp.int32)
    o_ref[...] = bits | lax.shift_right_logical(bits, 16)


def _expert_coefs(act, wt):
    t, nsel = wt.shape
    rows = math.gcd(t, 512)
    row = lambda i: (i, 0)
    return pl.pallas_call(
        _coef_kernel,
        grid=(t // rows,),
        in_specs=[pl.BlockSpec((rows, nsel), row), pl.BlockSpec((rows, nsel), row)],
        out_specs=pl.BlockSpec((rows, nsel), row),
        out_shape=jax.ShapeDtypeStruct((t, nsel), jnp.int32),
        compiler_params=_cparams(("arbitrary",)),
        name="expert_coefs",
    )(act, wt)


def _finish_kernel(xm_ref, o_ref, gfin_ref, yacc_hbm, y_ref):
    del yacc_hbm
    y_ref[...] = _rms(xm_ref[...] + o_ref[...], gfin_ref[...])


def _finish(xm, o, gfin, y_acc, row_off):
    t, d = xm.shape
    rows = math.gcd(t, 512)
    first = row_off // rows
    row = lambda i: (i, 0)
    return pl.pallas_call(
        _finish_kernel,
        grid=(t // rows,),
        in_specs=[pl.BlockSpec((rows, d), row), pl.BlockSpec((rows, d), row),
                  pl.BlockSpec(gfin.shape, lambda i: (0, 0)), pl.BlockSpec(memory_space=pl.ANY)],
        out_specs=pl.BlockSpec((rows, d), lambda i: (i + first, 0)),
        out_shape=jax.ShapeDtypeStruct(y_acc.shape, F32),
        input_output_aliases={3: 0},
        compiler_params=_cparams(("arbitrary",)),
        name="finish",
    )(xm, o, gfin, y_acc)


def _rope_table(pos, valid, rope_dim, head_dim):
    half = rope_dim // 2
    inv = ROPE_THETA ** (-jnp.arange(0, rope_dim, 2, dtype=F32) / rope_dim)
    ang = pos.astype(F32)[:, None] * inv[None, :]
    cos, sin = jnp.cos(ang), jnp.sin(ang)
    n = pos.shape[0]
    ones = jnp.ones((n, head_dim - rope_dim), F32)
    zeros_h = jnp.zeros((n, half), F32)
    zeros_r = jnp.zeros((n, head_dim - rope_dim), F32)
    reps = LANES // head_dim
    cosf = jnp.tile(jnp.concatenate([cos, cos, ones], axis=1), (1, reps))
    sin_lo = jnp.tile(jnp.concatenate([zeros_h, sin, zeros_r], axis=1), (1, reps))
    sin_hi = jnp.tile(jnp.concatenate([-sin, zeros_h, zeros_r], axis=1), (1, reps))
    vcol = jnp.broadcast_to(valid.astype(F32)[:, None], (n, LANES))
    return jnp.concatenate([cosf, sin_lo, sin_hi, vcol], axis=1)


def _pack_pairs(x):
    half = x.shape[1] // 2
    lo = lax.bitcast_convert_type(x[:, :half].astype(BF16), jnp.uint16).astype(jnp.uint32)
    hi = lax.bitcast_convert_type(x[:, half:].astype(BF16), jnp.uint16).astype(jnp.uint32)
    return lax.bitcast_convert_type(lo | (hi << 16), jnp.int32)


def _expert_dots(table, idx, hn, nsel):
    n = idx.shape[0]
    c = table.shape[1]
    sc = plsc.get_sparse_core_info()
    lanes = sc.num_lanes
    workers = sc.num_cores * sc.num_subcores
    gw = GATHER_WINDOW
    nbuf = GATHER_BUFFERS
    per_worker = n // workers
    ich = min(DOTS_INDEX_CHUNK, per_worker)
    tok = ich // nsel
    assert n % workers == 0 and per_worker % ich == 0 and nsel == nbuf * gw and hn.shape[1] == c and gw % lanes == 0
    mesh = plsc.VectorSubcoreMesh(core_axis_name="c", subcore_axis_name="s")

    @functools.partial(
        pl.kernel, out_type=jax.ShapeDtypeStruct((n,), F32), mesh=mesh, name="expert_dots",
        compiler_params=pltpu.CompilerParams(needs_layout_passes=False),
        scratch_types=[pltpu.VMEM((ich,), jnp.int32), pltpu.VMEM((tok, c), jnp.int32), pltpu.VMEM((ich,), F32),
                       pltpu.VMEM((lanes * lanes,), F32)]
                      + [pltpu.VMEM((gw, c), jnp.int32)] * nbuf + [pltpu.SemaphoreType.DMA] * nbuf)
    def dots(tab_hbm, idx_hbm, hn_hbm, act_hbm, idx_v, h_v, act_v, scr, *bufs_sems):
        rows, sems = bufs_sems[:nbuf], bufs_sems[nbuf:]
        wid = lax.axis_index("s") * sc.num_cores + lax.axis_index("c")
        base = wid * per_worker
        lane = lax.iota(jnp.int32, lanes)

        def gather(win, buf, sem):
            return pltpu.make_async_copy(tab_hbm.at[idx_v.at[pl.ds(win * gw, gw)]], buf, sem)

        def reduce_window(buf, t_loc, out_off):
            for rb in range(gw // lanes):
                def kbody(k2, accs):
                    k0 = 2 * k2 * lanes
                    hw0 = plsc.bitcast(h_v[t_loc, pl.ds(k0, lanes)], BF16)
                    hw1 = plsc.bitcast(h_v[t_loc, pl.ds(k0 + lanes, lanes)], BF16)
                    out = []
                    for r in range(lanes):
                        w0 = plsc.bitcast(buf[rb * lanes + r, pl.ds(k0, lanes)], BF16)
                        w1 = plsc.bitcast(buf[rb * lanes + r, pl.ds(k0 + lanes, lanes)], BF16)
                        p = plsc.bitcast(w0 * hw0 + w1 * hw1, jnp.int32)
                        out.append(accs[r] + lax.bitcast_convert_type(p << 16, F32)
                                   + lax.bitcast_convert_type(p, F32))
                    return tuple(out)

                accs = lax.fori_loop(0, c // (2 * lanes), kbody,
                                     tuple(jnp.zeros((lanes,), F32) for _ in range(lanes)))
                for r in range(lanes):
                    scr[pl.ds(r * lanes, lanes)] = accs[r]
                cols = [plsc.load_gather(scr, [lane * lanes + l]) for l in range(lanes)]
                while len(cols) > 1:
                    cols = [cols[i] + cols[i + 1] for i in range(0, len(cols), 2)]
                act_v[pl.ds(out_off + rb * lanes, lanes)] = cols[0]

        @pl.loop(0, per_worker // ich)
        def _(g):
            cb = base + g * ich
            pltpu.sync_copy(idx_hbm.at[pl.ds(cb, ich)], idx_v)
            tok_base = pl.multiple_of(wid * (per_worker // nsel) + g * tok, tok)
            pltpu.sync_copy(hn_hbm.at[pl.ds(tok_base, tok)], h_v)
            for q in range(nbuf):
                gather(q, rows[q], sems[q]).start()

            @pl.loop(0, tok)
            def _(j):
                for q in range(nbuf):
                    gather(nbuf * j + q, rows[q], sems[q]).wait()
                    reduce_window(rows[q], j, j * nsel + q * gw)

                    @pl.when(j + 1 < tok)
                    def _():
                        gather(nbuf * (j + 1) + q, rows[q], sems[q]).start()

            pltpu.sync_copy(act_v, act_hbm.at[pl.ds(cb, ich)])

    return dots(table, idx, hn)


def _expert_mix(table, idx, coef, nsel):
    n = idx.shape[0]
    c = table.shape[1]
    d = 2 * c
    sc = plsc.get_sparse_core_info()
    lanes = sc.num_lanes
    workers = sc.num_cores * sc.num_subcores
    gw = GATHER_WINDOW
    nbuf = GATHER_BUFFERS
    per_worker = n // workers
    ich = min(MIX_INDEX_CHUNK, per_worker)
    tok = ich // nsel
    kblock = 16
    assert n % workers == 0 and per_worker % ich == 0 and nsel == nbuf * gw and c % (kblock * lanes) == 0
    mesh = plsc.VectorSubcoreMesh(core_axis_name="c", subcore_axis_name="s")

    @functools.partial(
        pl.kernel, out_type=jax.ShapeDtypeStruct((n // nsel, d), F32), mesh=mesh, name="expert_mix",
        compiler_params=pltpu.CompilerParams(needs_layout_passes=False),
        scratch_types=[pltpu.VMEM((ich,), jnp.int32), pltpu.VMEM((ich,), jnp.int32), pltpu.VMEM((tok, d), F32)]
                      + [pltpu.VMEM((gw, c), jnp.int32)] * nbuf + [pltpu.SemaphoreType.DMA] * nbuf)
    def mix(tab_hbm, idx_hbm, coef_hbm, out_hbm, idx_v, coef_v, out_v, *bufs_sems):
        rows, sems = bufs_sems[:nbuf], bufs_sems[nbuf:]
        wid = lax.axis_index("s") * sc.num_cores + lax.axis_index("c")
        base = wid * per_worker
        zero_idx = jnp.zeros((lanes,), jnp.int32)

        def gather(win, buf, sem):
            return pltpu.make_async_copy(tab_hbm.at[idx_v.at[pl.ds(win * gw, gw)]], buf, sem)

        def accumulate_window(buf, t_loc, coef_off, first):
            for kb in range(c // (kblock * lanes)):
                col0 = kb * kblock * lanes
                if first:
                    init = tuple(jnp.zeros((lanes,), F32) for _ in range(2 * kblock))
                else:
                    init = tuple(out_v[t_loc, pl.ds(col0 + i * lanes, lanes)] for i in range(kblock)) + \
                           tuple(out_v[t_loc, pl.ds(c + col0 + i * lanes, lanes)] for i in range(kblock))

                def rbody(r2, accs):
                    accs = list(accs)
                    r = 2 * r2
                    cw0 = plsc.bitcast(plsc.load_gather(coef_v, [zero_idx + (coef_off + r)]), BF16)
                    cw1 = plsc.bitcast(plsc.load_gather(coef_v, [zero_idx + (coef_off + r + 1)]), BF16)
                    for i in range(kblock):
                        w0 = plsc.bitcast(buf[r, pl.ds(col0 + i * lanes, lanes)], BF16)
                        w1 = plsc.bitcast(buf[r + 1, pl.ds(col0 + i * lanes, lanes)], BF16)
                        p = plsc.bitcast(w0 * cw0 + w1 * cw1, jnp.int32)
                        accs[i] = accs[i] + lax.bitcast_convert_type(p << 16, F32)
                        accs[kblock + i] = accs[kblock + i] + lax.bitcast_convert_type(p, F32)
                    return tuple(accs)

                accs = lax.fori_loop(0, gw // 2, rbody, init)
                for i in range(kblock):
                    out_v[t_loc, pl.ds(col0 + i * lanes, lanes)] = accs[i]
                    out_v[t_loc, pl.ds(c + col0 + i * lanes, lanes)] = accs[kblock + i]

        @pl.loop(0, per_worker // ich)
        def _(g):
            cb = base + g * ich
            pltpu.sync_copy(idx_hbm.at[pl.ds(cb, ich)], idx_v)
            pltpu.sync_copy(coef_hbm.at[pl.ds(cb, ich)], coef_v)
            for q in range(nbuf):
                gather(q, rows[q], sems[q]).start()

            @pl.loop(0, tok)
            def _(j):
                for q in range(nbuf):
                    gather(nbuf * j + q, rows[q], sems[q]).wait()
                    accumulate_window(rows[q], j, j * nsel + q * gw, q == 0)

                    @pl.when(j + 1 < tok)
                    def _():
                        gather(nbuf * (j + 1) + q, rows[q], sems[q]).start()

            tok_base = pl.multiple_of(wid * (per_worker // nsel) + g * tok, tok)
            pltpu.sync_copy(out_v, out_hbm.at[pl.ds(tok_base, tok)])

    return mix(table, idx, coef)


def _peer_tail(xm, hn, et, wt, tabs, gfin, y_acc, row_off):
    u_tab, v_tab = tabs
    t, nsel = et.shape
    eidx = et.reshape(t * nsel)
    act = _expert_dots(u_tab, eidx, _pack_pairs(hn), nsel).reshape(t, nsel)
    coef = _expert_coefs(act, wt).reshape(t * nsel)
    mixed = _expert_mix(v_tab, eidx, coef, nsel)
    return _finish(xm, mixed, gfin, y_acc, row_off)


def kernel(x_prompt, x_sample, cache_k_window, cache_v_window, state_gla, meta_tokens, g_norm_mix, w_in,
           w_gate_up, b_gate, attn_sinks, g_gla_norm, w_branch_a, w_branch_b, w_out, g_norm_ffn, w_peer_q,
           peer_sub_keys, peer_u, peer_v, g_norm_final):
    bsz, seq, d = x_prompt.shape
    dbsz, tdec, _ = x_sample.shape
    n_meta = meta_tokens.shape[0]
    depth = w_in.shape[0]
    window = cache_k_window.shape[2]
    kv_heads, head_dim = cache_k_window.shape[3], cache_k_window.shape[4]
    gate_rank = w_gate_up.shape[1]
    bqk = w_gate_up.shape[2]
    n_ph, _, n_keys, p_half = peer_sub_keys.shape[1:]
    assert depth == 1 and d == 1024 and window == ATTN_BLOCK and kv_heads == 2 and head_dim == 64
    assert bqk == 256 and state_gla.shape[2:] == (4, 64, 128) and n_meta <= ATTN_BLOCK
    assert seq % ATTN_BLOCK == 0 and tdec <= SAMPLE_PAD and n_keys == 128 and p_half == 64 and n_ph == 8
    rope_dim = head_dim // 4
    meta_pad = ATTN_BLOCK - n_meta
    lp = ATTN_BLOCK + seq
    nblk = lp // ATTN_BLOCK

    w = w_in[0]
    c_lr = 2304
    c_gate = c_lr + gate_rank
    w1 = w[:, :c_lr].astype(BF16)
    wlr = jnp.pad(w[:, c_lr:c_gate], ((0, 0), (0, LANES - gate_rank))).astype(BF16)
    w2 = w[:, c_gate:].astype(BF16)
    wgu = jnp.pad(w_gate_up[0], ((0, LANES - gate_rank), (0, 0))).astype(BF16)
    bg = b_gate[0][None, :]
    gmix = g_norm_mix[0][None, :]
    wa = w_branch_a[0].astype(BF16)
    wb = w_branch_b[0].astype(BF16)
    wo = w_out[0].astype(BF16)
    gffn = g_norm_ffn[0][None, :]
    wq = w_peer_q[0].astype(BF16)
    keys = peer_sub_keys[0].reshape(n_ph * 2, n_keys, p_half).astype(BF16)
    u_tab = _pack_pairs(peer_u[0])
    gfin = g_norm_final[None, :]
    gn = g_gla_norm[0][None, :]
    sinks = attn_sinks[0]
    qk_scale = float(bqk // 4) ** -0.5

    rows_p = jnp.arange(lp)
    tab_p = _rope_table(rows_p - meta_pad, rows_p >= meta_pad, rope_dim, head_dim)
    proj_rows = max(r for r in range(16, PROJ_ROWS + 1, 16) if lp % r == 0)
    nq = nblk - 1
    per_seq = seq // MERGE_ROWS
    gt_map_p = lambda i: ((i // per_seq) * nblk + 1 + (i % per_seq), 0)

    meta = jnp.broadcast_to(meta_tokens[None].astype(x_prompt.dtype), (bsz, n_meta, d))
    xpad = jnp.concatenate([jnp.zeros((bsz, meta_pad, d), x_prompt.dtype), meta, x_prompt], axis=1)
    xpad = xpad.reshape(bsz * lp, d)
    x_rows = x_prompt.reshape(bsz * seq, d)

    def mix_and_route(b0, gb):
        qa, kv, gl, gt = _project(xpad, gmix, tab_p, w1, wlr, wgu, bg, w2, proj_rows, qk_scale, b0 * lp, gb * lp)
        ya = _attention(
            sinks, qa, kv, kv, gb, nq, ATTN_BLOCK,
            lambda b, n: (b * nblk + n + 1, 0), lambda b, n: (b * nblk + n, 0), lambda b, n: (b * nblk + n + 1, 0),
            lambda b, n: (b * nq + n, 0), gb * seq, first_valid_key=meta_pad, block_offset=1)
        s0 = jnp.zeros((gb,) + state_gla.shape[2:], F32)
        yb, s_fin = _gla(gl, s0, gn, gb, nblk, ATTN_BLOCK, GLA_CHUNK,
                         lambda b, c: (b * nblk + c, 0),
                         lambda b, c: (b * nq + jnp.maximum(c - 1, 0), 0), gb * seq)
        xm, hn, et, wt = _merge_route(x_rows, gt, ya, yb, wa, wb, wo, gffn, wq, keys, gt_map_p, b0 * seq)
        kv_w = kv.reshape(gb, lp, 2, kv_heads, head_dim)[:, lp - window:]
        return (xm, hn, et, wt), kv_w, s_fin

    group = PROMPT_GROUP if (bsz % PROMPT_GROUP == 0 and (PROMPT_GROUP * seq) % GATHER_ROW_QUANTUM == 0) else bsz
    tabs = (u_tab, _pack_pairs(peer_v[0]))
    y_acc, kv_parts, s_parts = jnp.zeros((bsz * seq, d), F32), [], []
    for b0 in range(0, bsz, group):
        routed, kv_w, s_fin = mix_and_route(b0, group)
        y_acc = _peer_tail(*routed, tabs, gfin, y_acc, b0 * seq)
        kv_parts.append(kv_w)
        s_parts.append(s_fin)
    y_prompt = y_acc.reshape(bsz, seq, d)
    kv_p = jnp.concatenate(kv_parts, axis=0)
    s_fin_p = jnp.concatenate(s_parts, axis=0)
    new_k_p = kv_p[:, :, 0][None]
    new_v_p = kv_p[:, :, 1][None]

    sp = SAMPLE_PAD
    xs_pad = jnp.pad(x_sample, ((0, 0), (0, sp - tdec), (0, 0))).reshape(dbsz * sp, d)
    rows_s = jnp.arange(sp)
    reps = 256 // sp
    tab_s = jnp.tile(_rope_table(PAST_LEN + rows_s, rows_s < tdec, rope_dim, head_dim), (reps, 1))
    qa_s, kv_s, gl_s, gt_s = _project(xs_pad, gmix, tab_s, w1, wlr, wgu, bg, w2, 256, qk_scale, 0, dbsz * sp)

    cache_kv = jnp.concatenate([cache_k_window[0].reshape(dbsz * window, kv_heads * head_dim),
                                cache_v_window[0].reshape(dbsz * window, kv_heads * head_dim)], axis=1)
    seq_map = lambda b, n: (b, 0)
    ya_s = _attention(sinks, qa_s, cache_kv, kv_s, dbsz, 1, sp, seq_map, seq_map, seq_map, seq_map,
                      dbsz * sp, first_valid_key=None, block_offset=0)
    yb_s, s_fin_s = _gla(gl_s, state_gla[0], gn, dbsz, 1, sp, sp, seq_map, seq_map, dbsz * sp)

    def real_rows(a):
        return a.reshape(dbsz, sp, a.shape[-1])[:, :tdec].reshape(dbsz * tdec, a.shape[-1])

    xs_rows = x_sample.reshape(dbsz * tdec, d)
    xm_s, hn_s, et_s, wt_s = _merge_route(xs_rows, real_rows(gt_s), real_rows(ya_s), real_rows(yb_s),
                                          wa, wb, wo, gffn, wq, keys, lambda i: (i, 0), 0)
    hn_s, _ = lax.optimization_barrier((hn_s, y_acc))
    y_sample = _peer_tail(xm_s, hn_s, et_s, wt_s, tabs, gfin,
                          jnp.zeros((dbsz * tdec, d), F32), 0).reshape(dbsz, tdec, d)

    kv_new = real_rows(kv_s).reshape(dbsz, tdec, 2, kv_heads, head_dim)
    new_k_s = jnp.concatenate([cache_k_window[0].astype(F32), kv_new[:, :, 0]], axis=1)[:, -window:][None]
    new_v_s = jnp.concatenate([cache_v_window[0].astype(F32), kv_new[:, :, 1]], axis=1)[:, -window:][None]

    return (y_prompt, y_sample, new_k_p, new_v_p, s_fin_p[None], new_k_s, new_v_s, s_fin_s[None])
```

```python
import functools
import math

import jax
import jax.numpy as jnp
from jax import lax
from jax.experimental import pallas as pl
from jax.experimental.pallas import tpu as pltpu
from jax.experimental.pallas import tpu_sc as plsc

F32 = jnp.float32
BF16 = jnp.bfloat16

EPS = 1e-6
NEG_INF = -1e30
PAST_LEN = 16384
ROPE_THETA = 500000.0
GATE_NORMALIZER = 16.0
PEER_TOPK = 16

LANES = 128
SUBLANES = 8
VMEM_LIMIT_BYTES = 56 * 1024 * 1024

ATTN_BLOCK = 128
GLA_CHUNK = 64
SAMPLE_PAD = 16
PROJ_ROWS = 544
MERGE_ROWS = 128
GATHER_WINDOW = 32
GATHER_BUFFERS = 4
DOTS_INDEX_CHUNK = 8192
MIX_INDEX_CHUNK = 4096
GATHER_ROW_QUANTUM = 512
PROMPT_GROUP = 1


def _cparams(sem):
    return pltpu.CompilerParams(dimension_semantics=sem, vmem_limit_bytes=VMEM_LIMIT_BYTES)


def _rms(x, g):
    ms = jnp.mean(x * x, axis=-1, keepdims=True)
    return (x * lax.rsqrt(ms + EPS)) * g


def _proj_kernel(x_ref, g_ref, tab_ref, w1_ref, wlr_ref, wgu_ref, bg_ref, w2_ref,
                 qa_ref, kv_ref, gl_ref, gt_ref, *, period, qk_scale):
    i = pl.program_id(0)
    tr = x_ref.shape[0]
    hb = _rms(x_ref[...], g_ref[...]).astype(BF16)
    z1 = jnp.dot(hb, w1_ref[...], preferred_element_type=F32)

    start = pl.multiple_of((i * tr) % period, SUBLANES)
    tab = tab_ref[pl.ds(start, tr), :]
    cosf = tab[:, 0:LANES]
    sin_lo = tab[:, LANES:2 * LANES]
    sin_hi = tab[:, 2 * LANES:3 * LANES]
    valid = tab[:, 3 * LANES:3 * LANES + 1]

    def rope(xg):
        return xg * cosf + pltpu.roll(xg, 8, 1) * sin_lo + pltpu.roll(xg, LANES - 8, 1) * sin_hi

    for gi in range(4):
        sl = slice(gi * LANES, (gi + 1) * LANES)
        qa_ref[:, sl] = rope(z1[:, sl]).astype(BF16)
    kv_ref[:, 0:LANES] = rope(z1[:, 512:640])
    kv_ref[:, LANES:2 * LANES] = z1[:, 640:768]

    lr = jnp.dot(hb, wlr_ref[...], preferred_element_type=F32)
    pre = jnp.dot(lr.astype(BF16), wgu_ref[...], preferred_element_type=F32) + bg_ref[...]
    log_sig = jnp.minimum(pre, 0.0) - jnp.log1p(jnp.exp(-jnp.abs(pre)))
    ld = jnp.where(valid > 0.5, log_sig / GATE_NORMALIZER, 0.0)

    gl_ref[:, 0:256] = z1[:, 768:1024] * qk_scale
    gl_ref[:, 256:512] = z1[:, 1024:1280]
    gl_ref[:, 512:768] = ld
    gl_ref[:, 768:1792] = z1[:, 1280:2304]
    gt_ref[...] = jnp.dot(hb, w2_ref[...], preferred_element_type=F32)


def _project(x, g, tab, w1, wlr, wgu, bg, w2, rows, qk_scale, first_row, r):
    d = x.shape[1]
    period = tab.shape[0]
    const = lambda i: (0, 0)
    row = lambda i: (i, 0)
    first = first_row // rows
    return pl.pallas_call(
        functools.partial(_proj_kernel, period=period, qk_scale=qk_scale),
        grid=(r // rows,),
        in_specs=[
            pl.BlockSpec((rows, d), lambda i: (i + first, 0)),
            pl.BlockSpec(g.shape, const),
            pl.BlockSpec(tab.shape, const),
            pl.BlockSpec(w1.shape, const),
            pl.BlockSpec(wlr.shape, const),
            pl.BlockSpec(wgu.shape, const),
            pl.BlockSpec(bg.shape, const),
            pl.BlockSpec(w2.shape, const),
        ],
        out_specs=[
            pl.BlockSpec((rows, 512), row),
            pl.BlockSpec((rows, 256), row),
            pl.BlockSpec((rows, 1792), row),
            pl.BlockSpec((rows, 2048), row),
        ],
        out_shape=[
            jax.ShapeDtypeStruct((r, 512), BF16),
            jax.ShapeDtypeStruct((r, 256), F32),
            jax.ShapeDtypeStruct((r, 1792), F32),
            jax.ShapeDtypeStruct((r, 2048), F32),
        ],
        compiler_params=_cparams(("arbitrary",)),
        name="proj",
    )(x, g, tab, w1, wlr, wgu, bg, w2)


def _attn_kernel(sink_ref, q_ref, prev_ref, cur_ref, o_ref, *, first_valid_key, block_offset):
    n = pl.program_id(1)
    qr = q_ref.shape[0]
    kr = cur_ref.shape[0]
    w = prev_ref.shape[0]
    nk = w + kr
    group = 4
    hd = 64

    rows = lax.broadcasted_iota(jnp.int32, (group * qr, nk), 0)
    cols = lax.broadcasted_iota(jnp.int32, (group * qr, nk), 1)
    head_of_row = rows // qr
    diff = (rows - head_of_row * qr) - cols + w
    mask = (diff >= 0) & (diff <= w)
    if first_valid_key is not None:
        blk = n + block_offset
        mask = mask & (cols >= first_valid_key + w - blk * w)

    prev = prev_ref[...]
    cur = cur_ref[...]
    q = q_ref[...]
    row_head = lax.broadcasted_iota(jnp.int32, (group * qr, 1), 0) // qr
    for kh in range(2):
        k = jnp.concatenate([prev[:, kh * hd:(kh + 1) * hd], cur[:, kh * hd:(kh + 1) * hd]], axis=0).astype(BF16)
        v = jnp.concatenate([prev[:, LANES + kh * hd:LANES + (kh + 1) * hd],
                             cur[:, LANES + kh * hd:LANES + (kh + 1) * hd]], axis=0).astype(BF16)
        qs = jnp.concatenate([q[:, (group * kh + g) * hd:(group * kh + g + 1) * hd] for g in range(group)], axis=0)
        s = lax.dot_general(qs, k, (((1,), (1,)), ((), ())), preferred_element_type=F32) * (hd ** -0.5)
        s = jnp.where(mask, s, NEG_INF)
        sink = jnp.zeros((group * qr, 1), F32)
        for g in range(group):
            sink = jnp.where(row_head == g, sink_ref[group * kh + g], sink)
        m = jnp.maximum(jnp.max(s, axis=-1, keepdims=True), sink)
        e = jnp.exp(s - m)
        p = e / (jnp.sum(e, axis=-1, keepdims=True) + jnp.exp(sink - m))
        o = jnp.dot(p.astype(BF16), v, preferred_element_type=F32)
        for g in range(group):
            h = group * kh + g
            o_ref[:, h * hd:(h + 1) * hd] = o[g * qr:(g + 1) * qr].astype(BF16)


def _attention(sinks, q, kv_prev, kv_cur, nb, nblk, qr, q_map, prev_map, cur_map, out_map, out_rows,
               first_valid_key, block_offset):
    w = ATTN_BLOCK
    return pl.pallas_call(
        functools.partial(_attn_kernel, first_valid_key=first_valid_key, block_offset=block_offset),
        grid=(nb, nblk),
        in_specs=[
            pl.BlockSpec(memory_space=pltpu.SMEM),
            pl.BlockSpec((qr, 512), q_map),
            pl.BlockSpec((w, 256), prev_map),
            pl.BlockSpec((qr, 256), cur_map),
        ],
        out_specs=pl.BlockSpec((qr, 512), out_map),
        out_shape=jax.ShapeDtypeStruct((out_rows, 512), BF16),
        compiler_params=_cparams(("arbitrary", "arbitrary")),
        name="swa",
    )(sinks, q, kv_prev, kv_cur)


def _gla_kernel(gl_ref, s0_ref, gn_ref, yb_ref, sfin_ref, st_ref, *, ch):
    c = pl.program_id(1)
    nh, dk, dv = 4, 64, 128

    @pl.when(c == 0)
    def _():
        for h in range(nh):
            st_ref[h] = s0_ref[0, h].T

    row = lax.broadcasted_iota(jnp.int32, (ch, nh * dk), 0)
    causal = (lax.broadcasted_iota(jnp.int32, (ch, ch), 0) >= lax.broadcasted_iota(jnp.int32, (ch, ch), 1))
    gn = gn_ref[...]
    nt = (((1,), (1,)), ((), ()))
    for sub in range(gl_ref.shape[0] // ch):
        rs = slice(sub * ch, (sub + 1) * ch)
        gl = gl_ref[rs, :]
        q = gl[:, 0:256]
        k = gl[:, 256:512]
        b = gl[:, 512:768]
        sh = 1
        while sh < ch:
            b = b + jnp.where(row >= sh, pltpu.roll(b, sh, 0), 0.0)
            sh *= 2
        b_last = b[ch - 1:ch, :]
        q_t = (q * jnp.exp(b)).astype(BF16)
        k_t = (k * jnp.exp(-b)).astype(BF16)
        k_end = (k * jnp.exp(b_last - b)).astype(BF16)
        decay = jnp.exp(b_last)
        for h in range(nh):
            ks = slice(h * dk, (h + 1) * dk)
            v = gl[:, 768 + h * dv:768 + (h + 1) * dv]
            vb = v.astype(BF16)
            a = lax.dot_general(q_t[:, ks], k_t[:, ks], nt, preferred_element_type=F32)
            a = jnp.where(causal, a, 0.0)
            s_t = st_ref[h]
            o = jnp.dot(a.astype(BF16), vb, preferred_element_type=F32)
            o = o + lax.dot_general(q_t[:, ks], s_t.astype(BF16), nt, preferred_element_type=F32)
            upd = jnp.dot(v.T.astype(BF16), k_end[:, ks], preferred_element_type=F32)
            st_ref[h] = s_t * decay[:, ks] + upd
            go = gl[:, 1280 + h * dv:1280 + (h + 1) * dv]
            y = _rms(o, gn) * (go * jax.nn.sigmoid(go))
            yb_ref[rs, h * dv:(h + 1) * dv] = y.astype(BF16)

    @pl.when(c == pl.num_programs(1) - 1)
    def _():
        for h in range(nh):
            sfin_ref[0, h] = st_ref[h].T


def _gla(gl, s0, gn, nb, nsteps, rows, ch, in_map, out_map, out_rows):
    return pl.pallas_call(
        functools.partial(_gla_kernel, ch=ch),
        grid=(nb, nsteps),
        in_specs=[
            pl.BlockSpec((rows, 1792), in_map),
            pl.BlockSpec((1, 4, 64, 128), lambda b, c: (b, 0, 0, 0)),
            pl.BlockSpec((1, 128), lambda b, c: (0, 0)),
        ],
        out_specs=[
            pl.BlockSpec((rows, 512), out_map),
            pl.BlockSpec((1, 4, 64, 128), lambda b, c: (b, 0, 0, 0)),
        ],
        out_shape=[
            jax.ShapeDtypeStruct((out_rows, 512), BF16),
            jax.ShapeDtypeStruct((nb, 4, 64, 128), F32),
        ],
        scratch_shapes=[pltpu.VMEM((4, 128, 64), F32)],
        compiler_params=_cparams(("arbitrary", "arbitrary")),
        name="gla",
    )(gl, s0, gn)


def _extract_topk(work, nsel, iota0, sentinel, payload=None):
    slabs = work.shape[0] // SUBLANES
    assert slabs & (slabs - 1) == 0
    vals, idxs, picks = [], [], []
    for j in range(nsel):
        v = [work[i * SUBLANES:(i + 1) * SUBLANES] for i in range(slabs)]
        ix = [iota0[i * SUBLANES:(i + 1) * SUBLANES] for i in range(slabs)]
        while len(v) > 1:
            keep = [v[i] >= v[i + 1] for i in range(0, len(v), 2)]
            ix = [jnp.where(k, ix[2 * i], ix[2 * i + 1]) for i, k in enumerate(keep)]
            v = [jnp.where(k, v[2 * i], v[2 * i + 1]) for i, k in enumerate(keep)]
        m = jnp.max(v[0], axis=0, keepdims=True)
        idx = jnp.min(jnp.where(v[0] == m, ix[0], sentinel), axis=0, keepdims=True)
        vals.append(m)
        idxs.append(idx)
        hit = iota0 == idx
        if payload is not None:
            picks.append(jnp.max(jnp.where(hit, payload, -1), axis=0, keepdims=True))
        if j + 1 < nsel:
            work = jnp.where(hit, -jnp.inf, work)
    return vals, idxs, picks


def _merge_kernel(x_ref, gt_ref, ya_ref, yb_ref, wa_ref, wb_ref, wo_ref, gf_ref, wq_ref, keys_ref,
                  xm_ref, hn_ref, et_ref, wt_ref):
    td = x_ref.shape[0]
    nkeys = keys_ref.shape[1]
    half = keys_ref.shape[2]
    nheads = keys_ref.shape[0] // 2
    topk = PEER_TOPK

    gt = gt_ref[...]
    d = x_ref.shape[1]
    ma = jnp.dot(ya_ref[...], wa_ref[...], preferred_element_type=F32)
    mb = jnp.dot(yb_ref[...], wb_ref[...], preferred_element_type=F32)
    m = jax.nn.sigmoid(gt[:, 0:d]) * ma + jax.nn.sigmoid(gt[:, d:2 * d]) * mb
    xm = x_ref[...] + jnp.dot(m.astype(BF16), wo_ref[...], preferred_element_type=F32)
    xm_ref[...] = xm
    hn = _rms(xm, gf_ref[...])
    hn_ref[...] = hn
    q = jnp.dot(hn.astype(BF16), wq_ref[...], preferred_element_type=F32).astype(BF16)

    nt = (((1,), (1,)), ((), ()))
    iota_k = lax.broadcasted_iota(jnp.int32, (nkeys, td), 0)
    pair_rows = [topk // (a + 1) for a in range(topk)]
    ncand = SUBLANES
    while ncand < sum(pair_rows):
        ncand *= 2
    cand_pad = ncand - sum(pair_rows)
    iota_c = lax.broadcasted_iota(jnp.int32, (ncand, td), 0)
    wts, ids = [], []
    for h in range(nheads):
        sv, si = [], []
        for c in range(2):
            gi = 2 * h + c
            s_t = lax.dot_general(keys_ref[gi], q[:, gi * half:(gi + 1) * half], nt,
                                  preferred_element_type=F32)
            vals, idxs, _ = _extract_topk(s_t, topk, iota_k, nkeys)
            sv.append(vals)
            si.append(idxs)
        sv1 = jnp.concatenate(sv[1], axis=0)
        si1 = jnp.concatenate(si[1], axis=0)
        cand = jnp.concatenate([sv[0][a] + sv1[0:nb] for a, nb in enumerate(pair_rows)]
                               + [jnp.full((cand_pad, td), -jnp.inf, F32)], axis=0)
        cidx = jnp.concatenate([si[0][a] * nkeys + si1[0:nb] for a, nb in enumerate(pair_rows)]
                               + [jnp.full((cand_pad, td), -1, jnp.int32)], axis=0)
        fvals, _, eids = _extract_topk(cand, topk, iota_c, ncand, payload=cidx)
        fv = jnp.concatenate(fvals, axis=0)
        e = jnp.exp(fv - fvals[0])
        wts.append(e / jnp.sum(e, axis=0, keepdims=True))
        ids.extend(eids)
    wt_ref[...] = jnp.concatenate(wts, axis=0).T
    et_ref[...] = jnp.concatenate(ids, axis=0).T


def _merge_route(x, gt, ya, yb, wa, wb, wo, gf, wq, keys, gt_map, x_first_row):
    t = ya.shape[0]
    d = x.shape[1]
    td = MERGE_ROWS
    nsel = (keys.shape[0] // 2) * PEER_TOPK
    const2 = lambda i: (0, 0)
    row = lambda i: (i, 0)
    x_first = x_first_row // td
    return pl.pallas_call(
        _merge_kernel,
        grid=(t // td,),
        in_specs=[
            pl.BlockSpec((td, d), lambda i: (i + x_first, 0)),
            pl.BlockSpec((td, 2 * d), gt_map),
            pl.BlockSpec((td, ya.shape[1]), row),
            pl.BlockSpec((td, yb.shape[1]), row),
            pl.BlockSpec(wa.shape, const2),
            pl.BlockSpec(wb.shape, const2),
            pl.BlockSpec(wo.shape, const2),
            pl.BlockSpec(gf.shape, const2),
            pl.BlockSpec(wq.shape, const2),
            pl.BlockSpec(keys.shape, lambda i: (0, 0, 0)),
        ],
        out_specs=[
            pl.BlockSpec((td, d), row),
            pl.BlockSpec((td, d), row),
            pl.BlockSpec((td, nsel), row),
            pl.BlockSpec((td, nsel), row),
        ],
        out_shape=[
            jax.ShapeDtypeStruct((t, d), F32),
            jax.ShapeDtypeStruct((t, d), F32),
            jax.ShapeDtypeStruct((t, nsel), jnp.int32),
            jax.ShapeDtypeStruct((t, nsel), F32),
        ],
        compiler_params=_cparams(("arbitrary",)),
        name="merge_route",
    )(x, gt, ya, yb, wa, wb, wo, gf, wq, keys)


def _coef_kernel(act_ref, wt_ref, o_ref):
    act = act_ref[...]
    gelu = 0.5 * act * (1.0 + lax.erf(act * (2.0 ** -0.5)))
    bits = pltpu.bitcast((wt_ref[...] * gelu).astype(BF16).astype(F32), jnp.int32)
    o_ref[...] = bits | lax.shift_right_logical(bits, 16)


def _expert_coefs(act, wt):
    t, nsel = wt.shape
    rows = math.gcd(t, 512)
    row = lambda i: (i, 0)
    return pl.pallas_call(
        _coef_kernel,
        grid=(t // rows,),
        in_specs=[pl.BlockSpec((rows, nsel), row), pl.BlockSpec((rows, nsel), row)],
        out_specs=pl.BlockSpec((rows, nsel), row),
        out_shape=jax.ShapeDtypeStruct((t, nsel), jnp.int32),
        compiler_params=_cparams(("arbitrary",)),
        name="expert_coefs",
    )(act, wt)


def _finish_kernel(xm_ref, o_ref, gfin_ref, yacc_hbm, y_ref):
    del yacc_hbm
    y_ref[...] = _rms(xm_ref[...] + o_ref[...], gfin_ref[...])


def _finish(xm, o, gfin, y_acc, row_off):
    t, d = xm.shape
    rows = math.gcd(t, 512)
    first = row_off // rows
    row = lambda i: (i, 0)
    return pl.pallas_call(
        _finish_kernel,
        grid=(t // rows,),
        in_specs=[pl.BlockSpec((rows, d), row), pl.BlockSpec((rows, d), row),
                  pl.BlockSpec(gfin.shape, lambda i: (0, 0)), pl.BlockSpec(memory_space=pl.ANY)],
        out_specs=pl.BlockSpec((rows, d), lambda i: (i + first, 0)),
        out_shape=jax.ShapeDtypeStruct(y_acc.shape, F32),
        input_output_aliases={3: 0},
        compiler_params=_cparams(("arbitrary",)),
        name="finish",
    )(xm, o, gfin, y_acc)


def _rope_table(pos, valid, rope_dim, head_dim):
    half = rope_dim // 2
    inv = ROPE_THETA ** (-jnp.arange(0, rope_dim, 2, dtype=F32) / rope_dim)
    ang = pos.astype(F32)[:, None] * inv[None, :]
    cos, sin = jnp.cos(ang), jnp.sin(ang)
    n = pos.shape[0]
    ones = jnp.ones((n, head_dim - rope_dim), F32)
    zeros_h = jnp.zeros((n, half), F32)
    zeros_r = jnp.zeros((n, head_dim - rope_dim), F32)
    reps = LANES // head_dim
    cosf = jnp.tile(jnp.concatenate([cos, cos, ones], axis=1), (1, reps))
    sin_lo = jnp.tile(jnp.concatenate([zeros_h, sin, zeros_r], axis=1), (1, reps))
    sin_hi = jnp.tile(jnp.concatenate([-sin, zeros_h, zeros_r], axis=1), (1, reps))
    vcol = jnp.broadcast_to(valid.astype(F32)[:, None], (n, LANES))
    return jnp.concatenate([cosf, sin_lo, sin_hi, vcol], axis=1)


def _pack_pairs(x):
    half = x.shape[1] // 2
    lo = lax.bitcast_convert_type(x[:, :half].astype(BF16), jnp.uint16).astype(jnp.uint32)
    hi = lax.bitcast_convert_type(x[:, half:].astype(BF16), jnp.uint16).astype(jnp.uint32)
    return lax.bitcast_convert_type(lo | (hi << 16), jnp.int32)


def _expert_dots(table, idx, hn, nsel):
    n = idx.shape[0]
    c = table.shape[1]
    sc = plsc.get_sparse_core_info()
    lanes = sc.num_lanes
    workers = sc.num_cores * sc.num_subcores
    gw = GATHER_WINDOW
    nbuf = GATHER_BUFFERS
    per_worker = n // workers
    ich = min(DOTS_INDEX_CHUNK, per_worker)
    tok = ich // nsel
    assert n % workers == 0 and per_worker % ich == 0 and nsel == nbuf * gw and hn.shape[1] == c and gw % lanes == 0
    mesh = plsc.VectorSubcoreMesh(core_axis_name="c", subcore_axis_name="s")

    @functools.partial(
        pl.kernel, out_type=jax.ShapeDtypeStruct((n,), F32), mesh=mesh, name="expert_dots",
        compiler_params=pltpu.CompilerParams(needs_layout_passes=False),
        scratch_types=[pltpu.VMEM((ich,), jnp.int32), pltpu.VMEM((tok, c), jnp.int32), pltpu.VMEM((ich,), F32),
                       pltpu.VMEM((nsel * lanes,), F32)]
                      + [pltpu.VMEM((gw, c), jnp.int32)] * nbuf + [pltpu.SemaphoreType.DMA] * nbuf)
    def dots(tab_hbm, idx_hbm, hn_hbm, act_hbm, idx_v, h_v, act_v, scr, *bufs_sems):
        rows, sems = bufs_sems[:nbuf], bufs_sems[nbuf:]
        wid = lax.axis_index("s") * sc.num_cores + lax.axis_index("c")
        base = wid * per_worker
        lane = lax.iota(jnp.int32, lanes)

        def gather(win, buf, sem):
            return pltpu.make_async_copy(tab_hbm.at[idx_v.at[pl.ds(win * gw, gw)]], buf, sem)

        def reduce_window(buf, t_loc, win_off):
            for rb in range(gw // lanes):
                def kbody(k2, accs):
                    k0 = 2 * k2 * lanes
                    hw0 = plsc.bitcast(h_v[t_loc, pl.ds(k0, lanes)], BF16)
                    hw1 = plsc.bitcast(h_v[t_loc, pl.ds(k0 + lanes, lanes)], BF16)
                    out = []
                    for r in range(lanes):
                        w0 = plsc.bitcast(buf[rb * lanes + r, pl.ds(k0, lanes)], BF16)
                        w1 = plsc.bitcast(buf[rb * lanes + r, pl.ds(k0 + lanes, lanes)], BF16)
                        p = plsc.bitcast(w0 * hw0 + w1 * hw1, jnp.int32)
                        out.append(accs[r] + lax.bitcast_convert_type(p << 16, F32)
                                   + lax.bitcast_convert_type(p, F32))
                    return tuple(out)

                accs = lax.fori_loop(0, c // (2 * lanes), kbody,
                                     tuple(jnp.zeros((lanes,), F32) for _ in range(lanes)))
                for r in range(lanes):
                    scr[pl.ds((win_off + rb * lanes + r) * lanes, lanes)] = accs[r]

        def finish_token(out_off):
            for blk in range(nsel // lanes):
                cols = [plsc.load_gather(scr, [(lane + blk * lanes) * lanes + l]) for l in range(lanes)]
                while len(cols) > 1:
                    cols = [cols[i] + cols[i + 1] for i in range(0, len(cols), 2)]
                act_v[pl.ds(out_off + blk * lanes, lanes)] = cols[0]

        @pl.loop(0, per_worker // ich)
        def _(g):
            cb = base + g * ich
            pltpu.sync_copy(idx_hbm.at[pl.ds(cb, ich)], idx_v)
            tok_base = pl.multiple_of(wid * (per_worker // nsel) + g * tok, tok)
            pltpu.sync_copy(hn_hbm.at[pl.ds(tok_base, tok)], h_v)
            for q in range(nbuf):
                gather(q, rows[q], sems[q]).start()

            @pl.loop(0, tok)
            def _(j):
                for q in range(nbuf):
                    gather(nbuf * j + q, rows[q], sems[q]).wait()
                    reduce_window(rows[q], j, q * gw)

                    @pl.when(j + 1 < tok)
                    def _():
                        gather(nbuf * (j + 1) + q, rows[q], sems[q]).start()

                finish_token(j * nsel)

            pltpu.sync_copy(act_v, act_hbm.at[pl.ds(cb, ich)])

    return dots(table, idx, hn)


def _expert_mix(table, idx, coef, nsel):
    n = idx.shape[0]
    c = table.shape[1]
    d = 2 * c
    sc = plsc.get_sparse_core_info()
    lanes = sc.num_lanes
    workers = sc.num_cores * sc.num_subcores
    gw = GATHER_WINDOW
    nbuf = GATHER_BUFFERS
    per_worker = n // workers
    ich = min(MIX_INDEX_CHUNK, per_worker)
    tok = ich // nsel
    kblock = 16
    assert n % workers == 0 and per_worker % ich == 0 and nsel == nbuf * gw and c % (kblock * lanes) == 0
    mesh = plsc.VectorSubcoreMesh(core_axis_name="c", subcore_axis_name="s")

    @functools.partial(
        pl.kernel, out_type=jax.ShapeDtypeStruct((n // nsel, d), F32), mesh=mesh, name="expert_mix",
        compiler_params=pltpu.CompilerParams(needs_layout_passes=False),
        scratch_types=[pltpu.VMEM((ich,), jnp.int32), pltpu.VMEM((ich,), jnp.int32), pltpu.VMEM((tok, d), F32)]
                      + [pltpu.VMEM((gw, c), jnp.int32)] * nbuf + [pltpu.SemaphoreType.DMA] * nbuf)
    def mix(tab_hbm, idx_hbm, coef_hbm, out_hbm, idx_v, coef_v, out_v, *bufs_sems):
        rows, sems = bufs_sems[:nbuf], bufs_sems[nbuf:]
        wid = lax.axis_index("s") * sc.num_cores + lax.axis_index("c")
        base = wid * per_worker
        zero_idx = jnp.zeros((lanes,), jnp.int32)

        def gather(win, buf, sem):
            return pltpu.make_async_copy(tab_hbm.at[idx_v.at[pl.ds(win * gw, gw)]], buf, sem)

        def accumulate_window(buf, t_loc, coef_off, first):
            for kb in range(c // (kblock * lanes)):
                col0 = kb * kblock * lanes
                if first:
                    init = tuple(jnp.zeros((lanes,), F32) for _ in range(2 * kblock))
                else:
                    init = tuple(out_v[t_loc, pl.ds(col0 + i * lanes, lanes)] for i in range(kblock)) + \
                           tuple(out_v[t_loc, pl.ds(c + col0 + i * lanes, lanes)] for i in range(kblock))

                def rbody(r2, accs):
                    accs = list(accs)
                    r = 2 * r2
                    cw0 = plsc.bitcast(plsc.load_gather(coef_v, [zero_idx + (coef_off + r)]), BF16)
                    cw1 = plsc.bitcast(plsc.load_gather(coef_v, [zero_idx + (coef_off + r + 1)]), BF16)
                    for i in range(kblock):
                        w0 = plsc.bitcast(buf[r, pl.ds(col0 + i * lanes, lanes)], BF16)
                        w1 = plsc.bitcast(buf[r + 1, pl.ds(col0 + i * lanes, lanes)], BF16)
                        p = plsc.bitcast(w0 * cw0 + w1 * cw1, jnp.int32)
                        accs[i] = accs[i] + lax.bitcast_convert_type(p << 16, F32)
                        accs[kblock + i] = accs[kblock + i] + lax.bitcast_convert_type(p, F32)
                    return tuple(accs)

                accs = lax.fori_loop(0, gw // 2, rbody, init)
                for i in range(kblock):
                    out_v[t_loc, pl.ds(col0 + i * lanes, lanes)] = accs[i]
                    out_v[t_loc, pl.ds(c + col0 + i * lanes, lanes)] = accs[kblock + i]

        @pl.loop(0, per_worker // ich)
        def _(g):
            cb = base + g * ich
            pltpu.sync_copy(idx_hbm.at[pl.ds(cb, ich)], idx_v)
            pltpu.sync_copy(coef_hbm.at[pl.ds(cb, ich)], coef_v)
            for q in range(nbuf):
                gather(q, rows[q], sems[q]).start()

            @pl.loop(0, tok)
            def _(j):
                for q in range(nbuf):
                    gather(nbuf * j + q, rows[q], sems[q]).wait()
                    accumulate_window(rows[q], j, j * nsel + q * gw, q == 0)

                    @pl.when(j + 1 < tok)
                    def _():
                        gather(nbuf * (j + 1) + q, rows[q], sems[q]).start()

            tok_base = pl.multiple_of(wid * (per_worker // nsel) + g * tok, tok)
            pltpu.sync_copy(out_v, out_hbm.at[pl.ds(tok_base, tok)])

    return mix(table, idx, coef)


def _peer_tail(xm, hn, et, wt, tabs, gfin, y_acc, row_off):
    u_tab, v_tab = tabs
    t, nsel = et.shape
    eidx = et.reshape(t * nsel)
    act = _expert_dots(u_tab, eidx, _pack_pairs(hn), nsel).reshape(t, nsel)
    coef = _expert_coefs(act, wt).reshape(t * nsel)
    mixed = _expert_mix(v_tab, eidx, coef, nsel)
    return _finish(xm, mixed, gfin, y_acc, row_off)


def kernel(x_prompt, x_sample, cache_k_window, cache_v_window, state_gla, meta_tokens, g_norm_mix, w_in,
           w_gate_up, b_gate, attn_sinks, g_gla_norm, w_branch_a, w_branch_b, w_out, g_norm_ffn, w_peer_q,
           peer_sub_keys, peer_u, peer_v, g_norm_final):
    bsz, seq, d = x_prompt.shape
    dbsz, tdec, _ = x_sample.shape
    n_meta = meta_tokens.shape[0]
    depth = w_in.shape[0]
    window = cache_k_window.shape[2]
    kv_heads, head_dim = cache_k_window.shape[3], cache_k_window.shape[4]
    gate_rank = w_gate_up.shape[1]
    bqk = w_gate_up.shape[2]
    n_ph, _, n_keys, p_half = peer_sub_keys.shape[1:]
    assert depth == 1 and d == 1024 and window == ATTN_BLOCK and kv_heads == 2 and head_dim == 64
    assert bqk == 256 and state_gla.shape[2:] == (4, 64, 128) and n_meta <= ATTN_BLOCK
    assert seq % ATTN_BLOCK == 0 and tdec <= SAMPLE_PAD and n_keys == 128 and p_half == 64 and n_ph == 8
    rope_dim = head_dim // 4
    meta_pad = ATTN_BLOCK - n_meta
    lp = ATTN_BLOCK + seq
    nblk = lp // ATTN_BLOCK

    w = w_in[0]
    c_lr = 2304
    c_gate = c_lr + gate_rank
    w1 = w[:, :c_lr].astype(BF16)
    wlr = jnp.pad(w[:, c_lr:c_gate], ((0, 0), (0, LANES - gate_rank))).astype(BF16)
    w2 = w[:, c_gate:].astype(BF16)
    wgu = jnp.pad(w_gate_up[0], ((0, LANES - gate_rank), (0, 0))).astype(BF16)
    bg = b_gate[0][None, :]
    gmix = g_norm_mix[0][None, :]
    wa = w_branch_a[0].astype(BF16)
    wb = w_branch_b[0].astype(BF16)
    wo = w_out[0].astype(BF16)
    gffn = g_norm_ffn[0][None, :]
    wq = w_peer_q[0].astype(BF16)
    keys = peer_sub_keys[0].reshape(n_ph * 2, n_keys, p_half).astype(BF16)
    u_tab = _pack_pairs(peer_u[0])
    gfin = g_norm_final[None, :]
    gn = g_gla_norm[0][None, :]
    sinks = attn_sinks[0]
    qk_scale = float(bqk // 4) ** -0.5

    rows_p = jnp.arange(lp)
    tab_p = _rope_table(rows_p - meta_pad, rows_p >= meta_pad, rope_dim, head_dim)
    proj_rows = max(r for r in range(16, PROJ_ROWS + 1, 16) if lp % r == 0)
    nq = nblk - 1
    per_seq = seq // MERGE_ROWS
    gt_map_p = lambda i: ((i // per_seq) * nblk + 1 + (i % per_seq), 0)

    meta = jnp.broadcast_to(meta_tokens[None].astype(x_prompt.dtype), (bsz, n_meta, d))
    xpad = jnp.concatenate([jnp.zeros((bsz, meta_pad, d), x_prompt.dtype), meta, x_prompt], axis=1)
    xpad = xpad.reshape(bsz * lp, d)
    x_rows = x_prompt.reshape(bsz * seq, d)

    def mix_and_route(b0, gb):
        qa, kv, gl, gt = _project(xpad, gmix, tab_p, w1, wlr, wgu, bg, w2, proj_rows, qk_scale, b0 * lp, gb * lp)
        ya = _attention(
            sinks, qa, kv, kv, gb, nq, ATTN_BLOCK,
            lambda b, n: (b * nblk + n + 1, 0), lambda b, n: (b * nblk + n, 0), lambda b, n: (b * nblk + n + 1, 0),
            lambda b, n: (b * nq + n, 0), gb * seq, first_valid_key=meta_pad, block_offset=1)
        s0 = jnp.zeros((gb,) + state_gla.shape[2:], F32)
        yb, s_fin = _gla(gl, s0, gn, gb, nblk, ATTN_BLOCK, GLA_CHUNK,
                         lambda b, c: (b * nblk + c, 0),
                         lambda b, c: (b * nq + jnp.maximum(c - 1, 0), 0), gb * seq)
        xm, hn, et, wt = _merge_route(x_rows, gt, ya, yb, wa, wb, wo, gffn, wq, keys, gt_map_p, b0 * seq)
        kv_w = kv.reshape(gb, lp, 2, kv_heads, head_dim)[:, lp - window:]
        return (xm, hn, et, wt), kv_w, s_fin

    group = PROMPT_GROUP if (bsz % PROMPT_GROUP == 0 and (PROMPT_GROUP * seq) % GATHER_ROW_QUANTUM == 0) else bsz
    tabs = (u_tab, _pack_pairs(peer_v[0]))
    y_acc, kv_parts, s_parts = jnp.zeros((bsz * seq, d), F32), [], []
    for b0 in range(0, bsz, group):
        routed, kv_w, s_fin = mix_and_route(b0, group)
        y_acc = _peer_tail(*routed, tabs, gfin, y_acc, b0 * seq)
        kv_parts.append(kv_w)
        s_parts.append(s_fin)
    y_prompt = y_acc.reshape(bsz, seq, d)
    kv_p = jnp.concatenate(kv_parts, axis=0)
    s_fin_p = jnp.concatenate(s_parts, axis=0)
    new_k_p = kv_p[:, :, 0][None]
    new_v_p = kv_p[:, :, 1][None]

    sp = SAMPLE_PAD
    xs_pad = jnp.pad(x_sample, ((0, 0), (0, sp - tdec), (0, 0))).reshape(dbsz * sp, d)
    rows_s = jnp.arange(sp)
    reps = 256 // sp
    tab_s = jnp.tile(_rope_table(PAST_LEN + rows_s, rows_s < tdec, rope_dim, head_dim), (reps, 1))
    qa_s, kv_s, gl_s, gt_s = _project(xs_pad, gmix, tab_s, w1, wlr, wgu, bg, w2, 256, qk_scale, 0, dbsz * sp)

    cache_kv = jnp.concatenate([cache_k_window[0].reshape(dbsz * window, kv_heads * head_dim),
                                cache_v_window[0].reshape(dbsz * window, kv_heads * head_dim)], axis=1)
    seq_map = lambda b, n: (b, 0)
    ya_s = _attention(sinks, qa_s, cache_kv, kv_s, dbsz, 1, sp, seq_map, seq_map, seq_map, seq_map,
                      dbsz * sp, first_valid_key=None, block_offset=0)
    yb_s, s_fin_s = _gla(gl_s, state_gla[0], gn, dbsz, 1, sp, sp, seq_map, seq_map, dbsz * sp)

    def real_rows(a):
        return a.reshape(dbsz, sp, a.shape[-1])[:, :tdec].reshape(dbsz * tdec, a.shape[-1])

    xs_rows = x_sample.reshape(dbsz * tdec, d)
    xm_s, hn_s, et_s, wt_s = _merge_route(xs_rows, real_rows(gt_s), real_rows(ya_s), real_rows(yb_s),
                                          wa, wb, wo, gffn, wq, keys, lambda i: (i, 0), 0)
    hn_s, _ = lax.optimization_barrier((hn_s, y_acc))
    y_sample = _peer_tail(xm_s, hn_s, et_s, wt_s, tabs, gfin,
                          jnp.zeros((dbsz * tdec, d), F32), 0).reshape(dbsz, tdec, d)

    kv_new = real_rows(kv_s).reshape(dbsz, tdec, 2, kv_heads, head_dim)
    new_k_s = jnp.concatenate([cache_k_window[0].astype(F32), kv_new[:, :, 0]], axis=1)[:, -window:][None]
    new_v_s = jnp.concatenate([cache_v_window[0].astype(F32), kv_new[:, :, 1]], axis=1)[:, -window:][None]

    return (y_prompt, y_sample, new_k_p, new_v_p, s_fin_p[None], new_k_s, new_v_s, s_fin_s[None])
```
